```python
import jax, jax.numpy as jnp
from jax import lax
import numpy as np

D_MODEL = 1024
BATCH = 4
SEQ = 4096
DEPTH = 1
DEC_BATCH = 8
DEC_SEQ = 4096
PAST_LEN = 128

D_MIX = D_MODEL
D_CONV = D_MIX // 2
D_SG = D_MIX - D_CONV
N_SG_HEADS = 8
SG_HEAD_DIM = D_SG // N_SG_HEADS
CHUNK = 128
CONV_WIDTH = 3
D_IN = 3 * D_CONV + 2 * D_SG
N_EXPERT_GROUPS = 4
EXPERTS_PER_GROUP = 8
N_EXPERTS = N_EXPERT_GROUPS * EXPERTS_PER_GROUP
TOP_K_FINE = 2
D_EXPERT = D_MODEL // 2
ALPHA = (2.0 * DEPTH) ** 0.25
BETA = (8.0 * DEPTH) ** -0.25
LN_EPS = 1e-5

kernel_name = "hybrid_conv_sgu_hmoe_encoder"


def layer_norm(x, g, b):
    xf = x.astype(jnp.float32)
    mu = jnp.mean(xf, axis=-1, keepdims=True)
    xc = xf - mu
    var = jnp.mean(xc * xc, axis=-1, keepdims=True)
    return (xc * lax.rsqrt(var + LN_EPS) * g.astype(jnp.float32) + b.astype(jnp.float32)).astype(x.dtype)


def short_conv(h, w, b):
    hp = jnp.pad(h, ((0, 0), (1, 1), (0, 0)))
    return hp[:, :-2] * w[0] + hp[:, 1:-1] * w[1] + hp[:, 2:] * w[2] + b


def spatial_gate(u, v, w_s, b_s, g, bb):
    v = layer_norm(v, g, bb)
    nb, s, _ = v.shape
    vc = v.reshape(nb, s // CHUNK, CHUNK, N_SG_HEADS, SG_HEAD_DIM)
    mixed = jnp.einsum('hpq,bnqhc->bnphc', w_s, vc) + b_s.T[None, None, :, :, None]
    return u * mixed.reshape(nb, s, D_SG)


def hier_moe(x, w_rc, b_rc, w_rf, b_rf, w_gate, w_up, w_down):
    nb, s, d = x.shape
    t = x.reshape(-1, d)
    lc = (t @ w_rc + b_rc).astype(jnp.float32)
    pc = jax.nn.softmax(lc, axis=-1)
    grp = jnp.argmax(lc, axis=-1)
    p_grp = jnp.take_along_axis(pc, grp[:, None], axis=-1)
    lf = (t @ w_rf + b_rf).astype(jnp.float32).reshape(-1, N_EXPERT_GROUPS, EXPERTS_PER_GROUP)
    lf_sel = jnp.take_along_axis(lf, grp[:, None, None], axis=1)[:, 0]
    top_v, top_i = lax.top_k(lf_sel, TOP_K_FINE)
    top_w = jax.nn.softmax(top_v, axis=-1)
    fine = jnp.sum(jax.nn.one_hot(top_i, EXPERTS_PER_GROUP, dtype=jnp.float32) * top_w[..., None], axis=1)
    gate = (jax.nn.one_hot(grp, N_EXPERT_GROUPS, dtype=jnp.float32)[:, :, None]
            * (p_grp * fine)[:, None, :]).reshape(-1, N_EXPERTS).astype(x.dtype)
    out = jnp.zeros_like(t)
    for e in range(N_EXPERTS):
        hdn = jax.nn.silu(t @ w_gate[e]) * (t @ w_up[e])
        out = out + gate[:, e:e + 1] * (hdn @ w_down[e])
    return out.reshape(nb, s, d)


def encoder_layer(x, w_in, b_in, conv_w, conv_b, sg_ln_g, sg_ln_b, w_s, b_s, w_o, b_o,
                  ln1_g, ln1_b, w_rc, b_rc, w_rf, b_rf, w_gate, w_up, w_down, ln2_g, ln2_b):
    h = x @ w_in + b_in
    c_gate = h[..., :D_CONV]
    b_gate = h[..., D_CONV:2 * D_CONV]
    hv = h[..., 2 * D_CONV:3 * D_CONV]
    u = jax.nn.gelu(h[..., 3 * D_CONV:3 * D_CONV + D_SG])
    v = jax.nn.gelu(h[..., 3 * D_CONV + D_SG:])
    y_a = b_gate * short_conv(c_gate * hv, conv_w, conv_b)
    y_b = spatial_gate(u, v, w_s, b_s, sg_ln_g, sg_ln_b)
    mix = jnp.concatenate([y_a, y_b], axis=-1) @ w_o + b_o
    x = layer_norm(ALPHA * x + mix, ln1_g, ln1_b)
    x = layer_norm(ALPHA * x + hier_moe(x, w_rc, b_rc, w_rf, b_rf, w_gate, w_up, w_down), ln2_g, ln2_b)
    return x


def setup_inputs(seed: int = 0) -> dict:
    key = jax.random.key(seed)
    ks = jax.random.split(key, 24)
    f32 = jnp.float32
    nrm = lambda k, shape, scale: jax.random.normal(k, shape, f32) * scale
    L = DEPTH
    return {
        "x_prompt": nrm(ks[0], (BATCH, SEQ, D_MODEL), 1.0),
        "x_sample": nrm(ks[1], (DEC_BATCH, DEC_SEQ, D_MODEL), 1.0),
        "w_in": nrm(ks[2], (L, D_MODEL, D_IN), D_MODEL ** -0.5),
        "b_in": nrm(ks[3], (L, D_IN), 0.02),
        "conv_w": nrm(ks[4], (L, CONV_WIDTH, D_CONV), CONV_WIDTH ** -0.5),
        "conv_b": nrm(ks[5], (L, D_CONV), 0.02),
        "sg_ln_g": 1.0 + nrm(ks[6], (L, D_SG), 0.02),
        "sg_ln_b": nrm(ks[7], (L, D_SG), 0.02),
        "w_s": nrm(ks[8], (L, N_SG_HEADS, CHUNK, CHUNK), CHUNK ** -0.5),
        "b_s": 1.0 + nrm(ks[9], (L, N_SG_HEADS, CHUNK), 0.02),
        "w_o": nrm(ks[10], (L, D_MIX, D_MODEL), BETA * D_MIX ** -0.5),
        "b_o": nrm(ks[11], (L, D_MODEL), 0.02),
        "ln1_g": 1.0 + nrm(ks[12], (L, D_MODEL), 0.02),
        "ln1_b": nrm(ks[13], (L, D_MODEL), 0.02),
        "w_rc": nrm(ks[14], (L, D_MODEL, N_EXPERT_GROUPS), D_MODEL ** -0.5),
        "b_rc": nrm(ks[15], (L, N_EXPERT_GROUPS), 0.01),
        "w_rf": nrm(ks[16], (L, D_MODEL, N_EXPERTS), D_MODEL ** -0.5),
        "b_rf": nrm(ks[17], (L, N_EXPERTS), 0.01),
        "w_gate": nrm(ks[18], (L, N_EXPERTS, D_MODEL, D_EXPERT), D_MODEL ** -0.5),
        "w_up": nrm(ks[19], (L, N_EXPERTS, D_MODEL, D_EXPERT), D_MODEL ** -0.5),
        "w_down": nrm(ks[20], (L, N_EXPERTS, D_EXPERT, D_MODEL), BETA * D_EXPERT ** -0.5),
        "ln2_g": 1.0 + nrm(ks[21], (L, D_MODEL), 0.02),
        "ln2_b": nrm(ks[22], (L, D_MODEL), 0.02),
    }


def reference(x_prompt, x_sample, w_in, b_in, conv_w, conv_b, sg_ln_g, sg_ln_b, w_s, b_s, w_o, b_o,
              ln1_g, ln1_b, w_rc, b_rc, w_rf, b_rf, w_gate, w_up, w_down, ln2_g, ln2_b):
    y_prompt = x_prompt
    y_sample = x_sample
    for l in range(DEPTH):
        p = (w_in[l], b_in[l], conv_w[l], conv_b[l], sg_ln_g[l], sg_ln_b[l], w_s[l], b_s[l], w_o[l], b_o[l],
             ln1_g[l], ln1_b[l], w_rc[l], b_rc[l], w_rf[l], b_rf[l], w_gate[l], w_up[l], w_down[l],
             ln2_g[l], ln2_b[l])
        y_prompt = encoder_layer(y_prompt, *p)
        y_sample = encoder_layer(y_sample, *p)
    return (y_prompt, y_sample)
```

```python
import functools

import jax
import jax.numpy as jnp
from jax import lax
from jax.experimental import pallas as pl
from jax.experimental.pallas import tpu as pltpu

D_MODEL = 1024
D_CONV = 512
D_SG = 512
N_SG_HEADS = 8
SG_HEAD_DIM = D_SG // N_SG_HEADS
CHUNK = 128
N_EXPERT_GROUPS = 4
EXPERTS_PER_GROUP = 8
N_EXPERTS = N_EXPERT_GROUPS * EXPERTS_PER_GROUP
D_EXPERT = 512
LN_EPS = 1e-5

LANES = 128
HALO = 16
SEQ_TILE = 512
EXPERT_TILE = 256
TOKEN_TILE = 256
VMEM_LIMIT_BYTES = 56 * 1024 * 1024

COL_E1, COL_E2, COL_W1, COL_W2, COL_R1, COL_R2 = 0, 1, 2, 3, 4, 5
COARSE_OFF = N_EXPERTS


def _dot(a, b):
    return jnp.dot(a, b, preferred_element_type=jnp.float32)


def _gelu_tanh(x):
    return 0.5 * x * (1.0 + jnp.tanh(0.7978845608028654 * (x + 0.044715 * (x * x * x))))


def _layer_norm(x, g, b):
    mu = jnp.mean(x, axis=-1, keepdims=True)
    xc = x - mu
    var = jnp.mean(xc * xc, axis=-1, keepdims=True)
    return xc * lax.rsqrt(var + LN_EPS) * g + b


def _mixer_kernel(x_ref, xprev_ref, xnext_ref, w_in_ref, b_in_ref, conv_w_ref, conv_b_ref,
                  sg_g_ref, sg_b_ref, ws_ref, bs_ref, w_o_ref, b_o_ref, ln1_g_ref, ln1_b_ref,
                  w_r_ref, b_r_ref,
                  x1_ref, route_ref, counts_ref,
                  xb_ref, ymix_ref, carry_ref, *, ts, nt, alpha):
    b = pl.program_id(0)
    j = pl.program_id(1)

    @pl.when((b == 0) & (j == 0))
    def _():
        carry_ref[...] = jnp.zeros_like(carry_ref)

    bf16 = jnp.bfloat16
    x = x_ref[...]
    xb_ref[0:HALO, :] = xprev_ref[...].astype(bf16)
    xb_ref[HALO:HALO + ts, :] = x.astype(bf16)
    xb_ref[HALO + ts:HALO + ts + HALO, :] = xnext_ref[...].astype(bf16)
    xe = xb_ref[...]
    xm = xb_ref[HALO:HALO + ts, :]

    def proj(lhs, lo, hi):
        return _dot(lhs, w_in_ref[:, lo:hi]) + b_in_ref[:, lo:hi]

    g_e = proj(xe, 0, D_CONV) * proj(xe, 2 * D_CONV, 3 * D_CONV)
    row = lax.broadcasted_iota(jnp.int32, (ts + 2 * HALO, 1), 0)
    has_prev = jnp.where(j > 0, 1.0, 0.0)
    has_next = jnp.where(j < nt - 1, 1.0, 0.0)
    in_seq = jnp.where(row < HALO, has_prev, jnp.where(row >= HALO + ts, has_next, 1.0))
    g_e = g_e * in_seq
    g_prev = g_e[HALO - 1:HALO - 1 + ts, :]
    g_mid = g_e[HALO:HALO + ts, :]
    g_next = g_e[HALO + 1:HALO + 1 + ts, :]
    conv = (g_prev * conv_w_ref[0:1, :] + g_mid * conv_w_ref[1:2, :] + g_next * conv_w_ref[2:3, :]
            + conv_b_ref[...])
    y_a = proj(xm, D_CONV, 2 * D_CONV) * conv
    ymix_ref[:, 0:D_CONV] = y_a.astype(bf16)

    u = _gelu_tanh(proj(xm, 3 * D_CONV, 3 * D_CONV + D_SG))
    v = _gelu_tanh(proj(xm, 3 * D_CONV + D_SG, 3 * D_CONV + 2 * D_SG))
    v_ln = _layer_norm(v, sg_g_ref[...], sg_b_ref[...]).astype(bf16)
    lane_c = lax.broadcasted_iota(jnp.int32, (CHUNK, LANES), 1)
    first_head = lane_c < SG_HEAD_DIM
    for cp in range(ts // (2 * CHUNK)):
        r0 = cp * 2 * CHUNK
        r1 = r0 + CHUNK
        for hp in range(N_SG_HEADS // 2):
            c0 = hp * LANES
            rhs = jnp.concatenate([v_ln[r0:r0 + CHUNK, c0:c0 + LANES],
                                   v_ln[r1:r1 + CHUNK, c0:c0 + LANES]], axis=1)
            res = _dot(ws_ref[hp], rhs)
            bias = bs_ref[:, c0:c0 + LANES]
            m0 = jnp.where(first_head, res[0:CHUNK, 0:LANES], res[CHUNK:2 * CHUNK, 0:LANES]) + bias
            m1 = jnp.where(first_head, res[0:CHUNK, LANES:2 * LANES], res[CHUNK:2 * CHUNK, LANES:2 * LANES]) + bias
            ymix_ref[r0:r0 + CHUNK, D_CONV + c0:D_CONV + c0 + LANES] = (u[r0:r0 + CHUNK, c0:c0 + LANES] * m0).astype(bf16)
            ymix_ref[r1:r1 + CHUNK, D_CONV + c0:D_CONV + c0 + LANES] = (u[r1:r1 + CHUNK, c0:c0 + LANES] * m1).astype(bf16)

    mix = _dot(ymix_ref[...], w_o_ref[...]) + b_o_ref[...]
    x1 = _layer_norm(alpha * x + mix, ln1_g_ref[...], ln1_b_ref[...])
    x1_ref[...] = x1

    logits = _dot(x1.astype(bf16), w_r_ref[...]) + b_r_ref[...]
    lane = lax.broadcasted_iota(jnp.int32, (ts, LANES), 1)
    lane_f = lane.astype(jnp.float32)
    neg = jnp.float32(-jnp.inf)
    big = jnp.float32(1e9)
    is_c = (lane >= COARSE_OFF) & (lane < COARSE_OFF + N_EXPERT_GROUPS)
    lc = jnp.where(is_c, logits, neg)
    mx = jnp.max(lc, axis=1, keepdims=True)
    grp = jnp.min(jnp.where(lc == mx, lane_f - COARSE_OFF, big), axis=1, keepdims=True)
    p_grp = 1.0 / jnp.sum(jnp.where(is_c, jnp.exp(logits - mx), 0.0), axis=1, keepdims=True)
    grp_lo = grp * EXPERTS_PER_GROUP
    in_grp = (lane_f >= grp_lo) & (lane_f < grp_lo + EXPERTS_PER_GROUP)
    lf = jnp.where(in_grp, logits, neg)
    v1 = jnp.max(lf, axis=1, keepdims=True)
    e1 = jnp.min(jnp.where(lf == v1, lane_f, big), axis=1, keepdims=True)
    lf2 = jnp.where(lane_f == e1, neg, lf)
    v2 = jnp.max(lf2, axis=1, keepdims=True)
    e2 = jnp.min(jnp.where(lf2 == v2, lane_f, big), axis=1, keepdims=True)
    a = jnp.exp(v2 - v1)
    w1 = p_grp / (1.0 + a)
    w2 = p_grp * a / (1.0 + a)

    hit1 = lane_f == e1
    hit2 = lane_f == e2
    onehot = jnp.where(hit1 | hit2, 1.0, 0.0)
    ri = lax.broadcasted_iota(jnp.int32, (ts, ts), 0)
    ci = lax.broadcasted_iota(jnp.int32, (ts, ts), 1)
    tri = jnp.where(ri > ci, 1.0, 0.0).astype(bf16)
    before = _dot(tri, onehot.astype(bf16)) + carry_ref[...]
    r1 = jnp.sum(jnp.where(hit1, before, 0.0), axis=1, keepdims=True)
    r2 = jnp.sum(jnp.where(hit2, before, 0.0), axis=1, keepdims=True)
    carry = carry_ref[...] + jnp.sum(onehot, axis=0, keepdims=True)
    carry_ref[...] = carry
    counts_ref[...] = jnp.broadcast_to(carry, counts_ref.shape)

    slab = jnp.where(lane == COL_E1, e1, 0.0)
    slab = jnp.where(lane == COL_E2, e2, slab)
    slab = jnp.where(lane == COL_W1, w1, slab)
    slab = jnp.where(lane == COL_W2, w2, slab)
    slab = jnp.where(lane == COL_R1, r1, slab)
    slab = jnp.where(lane == COL_R2, r2, slab)
    route_ref[...] = slab


def _mixer(x3, p, alpha):
    nb, s, d = x3.shape
    ts = SEQ_TILE
    nt = s // ts
    hb = ts // HALO
    const2 = lambda b, j: (0, 0)
    const3 = lambda b, j: (0, 0, 0)
    full = lambda a: pl.BlockSpec(a.shape, const2 if a.ndim == 2 else const3)
    in_specs = [
        pl.BlockSpec((None, ts, d), lambda b, j: (b, j, 0)),
        pl.BlockSpec((None, HALO, d), lambda b, j: (b, jnp.maximum(j * hb - 1, 0), 0)),
        pl.BlockSpec((None, HALO, d), lambda b, j: (b, jnp.minimum((j + 1) * hb, s // HALO - 1), 0)),
    ] + [full(a) for a in p]
    out_shape = (jax.ShapeDtypeStruct((nb * s, d), jnp.float32),
                 jax.ShapeDtypeStruct((nb * s, LANES), jnp.float32),
                 jax.ShapeDtypeStruct((8, LANES), jnp.float32))
    out_specs = (pl.BlockSpec((ts, d), lambda b, j: (b * nt + j, 0)),
                 pl.BlockSpec((ts, LANES), lambda b, j: (b * nt + j, 0)),
                 pl.BlockSpec((8, LANES), const2))
    return pl.pallas_call(
        functools.partial(_mixer_kernel, ts=ts, nt=nt, alpha=alpha),
        grid=(nb, nt),
        in_specs=in_specs,
        out_specs=out_specs,
        out_shape=out_shape,
        scratch_shapes=[pltpu.VMEM((ts + 2 * HALO, d), jnp.bfloat16),
                        pltpu.VMEM((ts, d), jnp.bfloat16),
                        pltpu.VMEM((1, LANES), jnp.float32)],
        compiler_params=pltpu.CompilerParams(dimension_semantics=("arbitrary", "arbitrary"),
                                             vmem_limit_bytes=VMEM_LIMIT_BYTES),
        name="mixer",
    )(x3, x3, x3, *p)


def _row_copy(src_ref, src_row, dst_ref, dst_row, sem):
    return pltpu.make_async_copy(src_ref.at[pl.ds(src_row, 1)], dst_ref.at[pl.ds(dst_row, 1)], sem)


def _dispatch_kernel(pos_ref, x_ref, xs_in_ref, xs_ref, sem, *, tt):
    del xs_in_ref

    def start(r, c):
        _row_copy(x_ref, r, xs_ref, pos_ref[0, r], sem).start()
        _row_copy(x_ref, r, xs_ref, pos_ref[1, r], sem).start()
        return c

    lax.fori_loop(0, tt, start, 0)

    def wait(r, c):
        _row_copy(x_ref, 0, xs_ref, 0, sem).wait()
        _row_copy(x_ref, 0, xs_ref, 0, sem).wait()
        return c

    lax.fori_loop(0, tt, wait, 0)


def _dispatch(pos, x1, n_rows):
    n, d = x1.shape
    tt = TOKEN_TILE
    xs0 = jnp.zeros((n_rows, d), x1.dtype)
    return pl.pallas_call(
        functools.partial(_dispatch_kernel, tt=tt),
        grid=(n // tt,),
        in_specs=[pl.BlockSpec((None, 2, tt), lambda i: (i, 0, 0), memory_space=pltpu.SMEM),
                  pl.BlockSpec((tt, d), lambda i: (i, 0)),
                  pl.BlockSpec(memory_space=pl.ANY)],
        out_specs=pl.BlockSpec(memory_space=pl.ANY),
        out_shape=jax.ShapeDtypeStruct((n_rows, d), x1.dtype),
        scratch_shapes=[pltpu.SemaphoreType.DMA],
        input_output_aliases={2: 0},
        compiler_params=pltpu.CompilerParams(dimension_semantics=("arbitrary",)),
        name="dispatch",
    )(pos, x1, xs0)


def _expert_kernel(te_ref, nused_ref, xs_ref, wgu_ref, wd_ref, ys_ref):
    i = pl.program_id(0)

    @pl.when(i < nused_ref[0])
    def _():
        x = xs_ref[...].astype(jnp.bfloat16)
        gu = _dot(x, wgu_ref[...])
        g = gu[:, 0:D_EXPERT]
        h = g * jax.nn.sigmoid(g) * gu[:, D_EXPERT:2 * D_EXPERT]
        ys_ref[...] = _dot(h.astype(jnp.bfloat16), wd_ref[...])

    @pl.when(i >= nused_ref[0])
    def _():
        ys_ref[...] = jnp.zeros_like(ys_ref)


def _experts(tile_expert, n_used, xs, w_gu, w_d):
    n_rows, d = xs.shape
    tm = EXPERT_TILE
    grid_spec = pltpu.PrefetchScalarGridSpec(
        num_scalar_prefetch=2,
        grid=(n_rows // tm,),
        in_specs=[pl.BlockSpec((tm, d), lambda i, te, nu: (i, 0)),
                  pl.BlockSpec((None, d, 2 * D_EXPERT), lambda i, te, nu: (te[i], 0, 0)),
                  pl.BlockSpec((None, D_EXPERT, d), lambda i, te, nu: (te[i], 0, 0))],
        out_specs=pl.BlockSpec((tm, d), lambda i, te, nu: (i, 0)),
    )
    return pl.pallas_call(
        _expert_kernel,
        grid_spec=grid_spec,
        out_shape=jax.ShapeDtypeStruct((n_rows, d), jnp.float32),
        compiler_params=pltpu.CompilerParams(dimension_semantics=("arbitrary",),
                                             vmem_limit_bytes=VMEM_LIMIT_BYTES),
        name="experts",
    )(tile_expert, n_used, xs, w_gu, w_d)


def _combine_kernel(pos_ref, x1_ref, route_ref, ys_ref, g_ref, b_ref, out_ref, ybuf_ref, sem, *, tt, alpha):
    def start(r, c):
        _row_copy(ys_ref, pos_ref[0, r], ybuf_ref.at[0], r, sem).start()
        _row_copy(ys_ref, pos_ref[1, r], ybuf_ref.at[1], r, sem).start()
        return c

    lax.fori_loop(0, tt, start, 0)

    def wait(r, c):
        _row_copy(ys_ref, 0, ybuf_ref.at[0], 0, sem).wait()
        _row_copy(ys_ref, 0, ybuf_ref.at[0], 0, sem).wait()
        return c

    lax.fori_loop(0, tt, wait, 0)

    route = route_ref[...]
    w1 = route[:, COL_W1:COL_W1 + 1]
    w2 = route[:, COL_W2:COL_W2 + 1]
    moe = w1 * ybuf_ref[0] + w2 * ybuf_ref[1]
    out_ref[...] = _layer_norm(alpha * x1_ref[...] + moe, g_ref[...], b_ref[...])


def _combine(pos, x1, route, ys, g, b, alpha):
    n, d = x1.shape
    tt = TOKEN_TILE
    return pl.pallas_call(
        functools.partial(_combine_kernel, tt=tt, alpha=alpha),
        grid=(n // tt,),
        in_specs=[pl.BlockSpec((None, 2, tt), lambda i: (i, 0, 0), memory_space=pltpu.SMEM),
                  pl.BlockSpec((tt, d), lambda i: (i, 0)),
                  pl.BlockSpec((tt, LANES), lambda i: (i, 0)),
                  pl.BlockSpec(memory_space=pl.ANY),
                  pl.BlockSpec((1, d), lambda i: (0, 0)),
                  pl.BlockSpec((1, d), lambda i: (0, 0))],
        out_specs=pl.BlockSpec((tt, d), lambda i: (i, 0)),
        out_shape=jax.ShapeDtypeStruct((n, d), jnp.float32),
        scratch_shapes=[pltpu.VMEM((2, tt, d), jnp.float32), pltpu.SemaphoreType.DMA],
        compiler_params=pltpu.CompilerParams(dimension_semantics=("arbitrary",),
                                             vmem_limit_bytes=VMEM_LIMIT_BYTES),
        name="combine",
    )(pos, x1, route, ys, g, b)


def _encoder_layer(x3, w_in, b_in, conv_w, conv_b, sg_ln_g, sg_ln_b, w_s, b_s, w_o, b_o,
                   ln1_g, ln1_b, w_rc, b_rc, w_rf, b_rf, w_gate, w_up, w_down, ln2_g, ln2_b, alpha):
    nb, s, d = x3.shape
    n = nb * s
    bf16 = jnp.bfloat16
    row = lambda a: a.reshape(1, -1)
    ws_pairs = w_s.reshape(N_SG_HEADS // 2, 2 * CHUNK, CHUNK).astype(bf16)
    bs_full = jnp.repeat(b_s.T, SG_HEAD_DIM, axis=1)
    w_r = jnp.zeros((d, LANES), jnp.float32).at[:, 0:N_EXPERTS].set(w_rf)
    w_r = w_r.at[:, COARSE_OFF:COARSE_OFF + N_EXPERT_GROUPS].set(w_rc).astype(bf16)
    b_r = jnp.zeros((1, LANES), jnp.float32).at[0, 0:N_EXPERTS].set(b_rf)
    b_r = b_r.at[0, COARSE_OFF:COARSE_OFF + N_EXPERT_GROUPS].set(b_rc)
    params = (w_in.astype(bf16), row(b_in), conv_w, row(conv_b), row(sg_ln_g), row(sg_ln_b),
              ws_pairs, bs_full, w_o.astype(bf16), row(b_o), row(ln1_g), row(ln1_b), w_r, b_r)
    x1, route, counts = _mixer(x3, params, alpha)

    tm = EXPERT_TILE
    n_tiles = 2 * n // tm + N_EXPERTS
    cnt = counts[0, 0:N_EXPERTS].astype(jnp.int32)
    tiles_e = (cnt + tm - 1) // tm
    tile_end = jnp.cumsum(tiles_e)
    row_off = (tile_end - tiles_e) * tm
    e1 = route[:, COL_E1].astype(jnp.int32)
    e2 = route[:, COL_E2].astype(jnp.int32)
    pos1 = row_off[e1] + route[:, COL_R1].astype(jnp.int32)
    pos2 = row_off[e2] + route[:, COL_R2].astype(jnp.int32)
    tt = TOKEN_TILE
    pos = jnp.stack([pos1.reshape(n // tt, tt), pos2.reshape(n // tt, tt)], axis=1)
    tile_ids = jnp.arange(n_tiles, dtype=jnp.int32)
    tile_expert = jnp.sum((tile_end[None, :] <= tile_ids[:, None]).astype(jnp.int32), axis=1)
    tile_expert = jnp.minimum(tile_expert, N_EXPERTS - 1)
    n_used = tile_end[N_EXPERTS - 1:N_EXPERTS].astype(jnp.int32)

    xs = _dispatch(pos, x1, n_tiles * tm)
    w_gu = jnp.concatenate([w_gate, w_up], axis=2).astype(bf16)
    ys = _experts(tile_expert, n_used, xs, w_gu, w_down.astype(bf16))
    out = _combine(pos, x1, route, ys, row(ln2_g), row(ln2_b), alpha)
    return out.reshape(nb, s, d)


def kernel(x_prompt, x_sample, w_in, b_in, conv_w, conv_b, sg_ln_g, sg_ln_b, w_s, b_s, w_o, b_o, ln1_g, ln1_b, w_rc, b_rc, w_rf, b_rf, w_gate, w_up, w_down, ln2_g, ln2_b):
    depth = w_in.shape[0]
    alpha = (2.0 * depth) ** 0.25
    nbp = x_prompt.shape[0]
    x = jnp.concatenate([x_prompt, x_sample], axis=0)
    for l in range(depth):
        x = _encoder_layer(x, w_in[l], b_in[l], conv_w[l], conv_b[l], sg_ln_g[l], sg_ln_b[l], w_s[l], b_s[l],
                           w_o[l], b_o[l], ln1_g[l], ln1_b[l], w_rc[l], b_rc[l], w_rf[l], b_rf[l],
                           w_gate[l], w_up[l], w_down[l], ln2_g[l], ln2_b[l], alpha)
    return (x[:nbp], x[nbp:])
```

```python
import functools

import jax
import jax.numpy as jnp
from jax import lax
from jax.experimental import pallas as pl
from jax.experimental.pallas import tpu as pltpu

D_MODEL = 1024
D_CONV = 512
D_SG = 512
N_SG_HEADS = 8
SG_HEAD_DIM = D_SG // N_SG_HEADS
CHUNK = 128
N_EXPERT_GROUPS = 4
EXPERTS_PER_GROUP = 8
N_EXPERTS = N_EXPERT_GROUPS * EXPERTS_PER_GROUP
D_EXPERT = 512
LN_EPS = 1e-5
D_PACK = D_MODEL // 2

LANES = 128
SUBLANES = 8
HALO = 16
SEQ_TILE = 512
EXPERT_TILE = 256
TOKEN_TILE = 256
VMEM_LIMIT_BYTES = 56 * 1024 * 1024

COL_E1, COL_E2, COL_W1, COL_W2, COL_R1, COL_R2 = 0, 1, 2, 3, 4, 5
COARSE_OFF = N_EXPERTS


def _dot(a, b):
    return jnp.dot(a, b, preferred_element_type=jnp.float32)


def _gelu_tanh(x):
    return 0.5 * x * (1.0 + jnp.tanh(0.7978845608028654 * (x + 0.044715 * (x * x * x))))


def _layer_norm(x, g, b):
    mu = jnp.mean(x, axis=-1, keepdims=True)
    xc = x - mu
    var = jnp.mean(xc * xc, axis=-1, keepdims=True)
    return xc * lax.rsqrt(var + LN_EPS) * g + b


def _pack_halves(x):
    u32 = jnp.uint32
    lo = lax.bitcast_convert_type(x[:, 0:D_PACK].astype(jnp.bfloat16).astype(jnp.float32), u32)
    hi = lax.bitcast_convert_type(x[:, D_PACK:D_MODEL].astype(jnp.bfloat16).astype(jnp.float32), u32)
    return (hi & u32(0xFFFF0000)) | (lo >> u32(16))


def _unpack_halves(w):
    u32 = jnp.uint32
    lo = lax.bitcast_convert_type(w << u32(16), jnp.float32)
    hi = lax.bitcast_convert_type(w & u32(0xFFFF0000), jnp.float32)
    return lo, hi


def _mixer_kernel(xa_ref, xa_prev_ref, xa_next_ref, xb_in_ref, xb_prev_ref, xb_next_ref,
                  w_in_ref, b_in_ref, conv_w_ref, conv_b_ref,
                  sg_g_ref, sg_b_ref, ws_ref, bs_ref, w_o_ref, b_o_ref, ln1_g_ref, ln1_b_ref,
                  w_r_ref, b_r_ref,
                  x1_ref, xp_ref, slab_ref, route_ref, counts_ref,
                  xb_ref, ymix_ref, carry_ref, *, ts, nt, n_tiles_a, alpha):
    t = pl.program_id(0)
    j = t % nt
    from_a = t < n_tiles_a

    @pl.when(t == 0)
    def _():
        carry_ref[...] = jnp.zeros_like(carry_ref)

    bf16 = jnp.bfloat16
    x = jnp.where(from_a, xa_ref[...], xb_in_ref[...])
    xb_ref[0:HALO, :] = jnp.where(from_a, xa_prev_ref[...], xb_prev_ref[...]).astype(bf16)
    xb_ref[HALO:HALO + ts, :] = x.astype(bf16)
    xb_ref[HALO + ts:HALO + ts + HALO, :] = jnp.where(from_a, xa_next_ref[...], xb_next_ref[...]).astype(bf16)
    xe = xb_ref[...]
    xm = xb_ref[HALO:HALO + ts, :]

    def proj(lhs, lo, hi):
        return _dot(lhs, w_in_ref[:, lo:hi]) + b_in_ref[:, lo:hi]

    g_e = proj(xe, 0, D_CONV) * proj(xe, 2 * D_CONV, 3 * D_CONV)
    row = lax.broadcasted_iota(jnp.int32, (ts + 2 * HALO, 1), 0)
    has_prev = jnp.where(j > 0, 1.0, 0.0)
    has_next = jnp.where(j < nt - 1, 1.0, 0.0)
    in_seq = jnp.where(row < HALO, has_prev, jnp.where(row >= HALO + ts, has_next, 1.0))
    g_e = g_e * in_seq
    g_prev = g_e[HALO - 1:HALO - 1 + ts, :]
    g_mid = g_e[HALO:HALO + ts, :]
    g_next = g_e[HALO + 1:HALO + 1 + ts, :]
    conv = (g_prev * conv_w_ref[0:1, :] + g_mid * conv_w_ref[1:2, :] + g_next * conv_w_ref[2:3, :]
            + conv_b_ref[...])
    y_a = proj(xm, D_CONV, 2 * D_CONV) * conv
    ymix_ref[:, 0:D_CONV] = y_a.astype(bf16)

    u = _gelu_tanh(proj(xm, 3 * D_CONV, 3 * D_CONV + D_SG))
    v = _gelu_tanh(proj(xm, 3 * D_CONV + D_SG, 3 * D_CONV + 2 * D_SG))
    v_ln = _layer_norm(v, sg_g_ref[...], sg_b_ref[...]).astype(bf16)
    lane_c = lax.broadcasted_iota(jnp.int32, (CHUNK, LANES), 1)
    first_head = lane_c < SG_HEAD_DIM
    for cp in range(ts // (2 * CHUNK)):
        r0 = cp * 2 * CHUNK
        r1 = r0 + CHUNK
        for hp in range(N_SG_HEADS // 2):
            c0 = hp * LANES
            rhs = jnp.concatenate([v_ln[r0:r0 + CHUNK, c0:c0 + LANES],
                                   v_ln[r1:r1 + CHUNK, c0:c0 + LANES]], axis=1)
            res = _dot(ws_ref[hp], rhs)
            bias = bs_ref[:, c0:c0 + LANES]
            m0 = jnp.where(first_head, res[0:CHUNK, 0:LANES], res[CHUNK:2 * CHUNK, 0:LANES]) + bias
            m1 = jnp.where(first_head, res[0:CHUNK, LANES:2 * LANES], res[CHUNK:2 * CHUNK, LANES:2 * LANES]) + bias
            ymix_ref[r0:r0 + CHUNK, D_CONV + c0:D_CONV + c0 + LANES] = (u[r0:r0 + CHUNK, c0:c0 + LANES] * m0).astype(bf16)
            ymix_ref[r1:r1 + CHUNK, D_CONV + c0:D_CONV + c0 + LANES] = (u[r1:r1 + CHUNK, c0:c0 + LANES] * m1).astype(bf16)

    mix = _dot(ymix_ref[...], w_o_ref[...]) + b_o_ref[...]
    x1 = _layer_norm(alpha * x + mix, ln1_g_ref[...], ln1_b_ref[...])
    x1_ref[...] = x1
    xp_ref[...] = _pack_halves(x1)

    logits = _dot(x1.astype(bf16), w_r_ref[...]) + b_r_ref[...]
    lane = lax.broadcasted_iota(jnp.int32, (ts, LANES), 1)
    lane_f = lane.astype(jnp.float32)
    neg = jnp.float32(-jnp.inf)
    big = jnp.float32(1e9)
    is_c = (lane >= COARSE_OFF) & (lane < COARSE_OFF + N_EXPERT_GROUPS)
    lc = jnp.where(is_c, logits, neg)
    mx = jnp.max(lc, axis=1, keepdims=True)
    grp = jnp.min(jnp.where(lc == mx, lane_f - COARSE_OFF, big), axis=1, keepdims=True)
    p_grp = 1.0 / jnp.sum(jnp.where(is_c, jnp.exp(logits - mx), 0.0), axis=1, keepdims=True)
    grp_lo = grp * EXPERTS_PER_GROUP
    in_grp = (lane_f >= grp_lo) & (lane_f < grp_lo + EXPERTS_PER_GROUP)
    lf = jnp.where(in_grp, logits, neg)
    v1 = jnp.max(lf, axis=1, keepdims=True)
    e1 = jnp.min(jnp.where(lf == v1, lane_f, big), axis=1, keepdims=True)
    lf2 = jnp.where(lane_f == e1, neg, lf)
    v2 = jnp.max(lf2, axis=1, keepdims=True)
    e2 = jnp.min(jnp.where(lf2 == v2, lane_f, big), axis=1, keepdims=True)
    a = jnp.exp(v2 - v1)
    w1 = p_grp / (1.0 + a)
    w2 = p_grp * a / (1.0 + a)

    hit1 = lane_f == e1
    hit2 = lane_f == e2
    onehot = jnp.where(hit1 | hit2, 1.0, 0.0)
    ri = lax.broadcasted_iota(jnp.int32, (ts, ts), 0)
    ci = lax.broadcasted_iota(jnp.int32, (ts, ts), 1)
    tri = jnp.where(ri > ci, 1.0, 0.0).astype(bf16)
    before = _dot(tri, onehot.astype(bf16)) + carry_ref[...]
    r1 = jnp.sum(jnp.where(hit1, before, 0.0), axis=1, keepdims=True)
    r2 = jnp.sum(jnp.where(hit2, before, 0.0), axis=1, keepdims=True)
    carry = carry_ref[...] + jnp.sum(onehot, axis=0, keepdims=True)
    carry_ref[...] = carry
    counts_ref[...] = jnp.broadcast_to(carry, counts_ref.shape)

    slab = jnp.where(lane == COL_E1, e1, 0.0)
    slab = jnp.where(lane == COL_E2, e2, slab)
    slab = jnp.where(lane == COL_W1, w1, slab)
    slab = jnp.where(lane == COL_W2, w2, slab)
    slab = jnp.where(lane == COL_R1, r1, slab)
    slab = jnp.where(lane == COL_R2, r2, slab)
    slab_ref[...] = slab
    route_ref[...] = slab.T[0:SUBLANES, :]


def _mixer(xa, xb, p, alpha):
    nba, s, d = xa.shape
    nbb = xb.shape[0]
    ts = SEQ_TILE
    nt = s // ts
    hb = ts // HALO
    n_tiles_a = nba * nt
    n_tiles = n_tiles_a + nbb * nt
    n_total = n_tiles * ts

    tile_a = lambda t: jnp.minimum(t, n_tiles_a - 1)
    tile_b = lambda t: jnp.maximum(t - n_tiles_a, 0)

    def x_specs(tile):
        cur = lambda t: (tile(t) // nt, tile(t) % nt, 0)
        prv = lambda t: (tile(t) // nt, jnp.maximum((tile(t) % nt) * hb - 1, 0), 0)
        nxt = lambda t: (tile(t) // nt, jnp.minimum((tile(t) % nt + 1) * hb, s // HALO - 1), 0)
        return [pl.BlockSpec((None, ts, d), cur), pl.BlockSpec((None, HALO, d), prv),
                pl.BlockSpec((None, HALO, d), nxt)]

    full = lambda a: pl.BlockSpec(a.shape, lambda t: (0,) * a.ndim)
    in_specs = x_specs(tile_a) + x_specs(tile_b) + [full(a) for a in p]
    out_shape = (jax.ShapeDtypeStruct((n_total, d), jnp.float32),
                 jax.ShapeDtypeStruct((n_total, D_PACK), jnp.uint32),
                 jax.ShapeDtypeStruct((n_total, LANES), jnp.float32),
                 jax.ShapeDtypeStruct((SUBLANES, n_total), jnp.float32),
                 jax.ShapeDtypeStruct((SUBLANES, LANES), jnp.float32))
    row_blk = lambda t: (t, 0)
    out_specs = (pl.BlockSpec((ts, d), row_blk),
                 pl.BlockSpec((ts, D_PACK), row_blk),
                 pl.BlockSpec((ts, LANES), row_blk),
                 pl.BlockSpec((SUBLANES, ts), lambda t: (0, t)),
                 pl.BlockSpec((SUBLANES, LANES), lambda t: (0, 0)))
    return pl.pallas_call(
        functools.partial(_mixer_kernel, ts=ts, nt=nt, n_tiles_a=n_tiles_a, alpha=alpha),
        grid=(n_tiles,),
        in_specs=in_specs,
        out_specs=out_specs,
        out_shape=out_shape,
        scratch_shapes=[pltpu.VMEM((ts + 2 * HALO, d), jnp.bfloat16),
                        pltpu.VMEM((ts, d), jnp.bfloat16),
                        pltpu.VMEM((1, LANES), jnp.float32)],
        compiler_params=pltpu.CompilerParams(dimension_semantics=("arbitrary",),
                                             vmem_limit_bytes=VMEM_LIMIT_BYTES),
        name="mixer",
    )(xa, xa, xa, xb, xb, xb, *p)


def _row_copy(src_ref, src_row, dst_ref, dst_row, sem):
    return pltpu.make_async_copy(src_ref.at[pl.ds(src_row, 1)], dst_ref.at[pl.ds(dst_row, 1)], sem)


def _dispatch_kernel(pos_ref, x_ref, xs_in_ref, xs_ref, sem, *, tt):
    del xs_in_ref

    def start(r, c):
        _row_copy(x_ref, r, xs_ref, pos_ref[0, r], sem).start()
        _row_copy(x_ref, r, xs_ref, pos_ref[1, r], sem).start()
        return c

    lax.fori_loop(0, tt, start, 0, unroll=8)
    pltpu.make_async_copy(x_ref, xs_ref.at[pl.ds(0, tt)], sem).wait()
    pltpu.make_async_copy(x_ref, xs_ref.at[pl.ds(0, tt)], sem).wait()


def _dispatch(pos, xp, n_rows):
    n, dp = xp.shape
    tt = TOKEN_TILE
    xs0 = jnp.zeros((n_rows, dp), xp.dtype)
    return pl.pallas_call(
        functools.partial(_dispatch_kernel, tt=tt),
        grid=(n // tt,),
        in_specs=[pl.BlockSpec((None, 2, tt), lambda i: (i, 0, 0), memory_space=pltpu.SMEM),
                  pl.BlockSpec((tt, dp), lambda i: (i, 0)),
                  pl.BlockSpec(memory_space=pl.ANY)],
        out_specs=pl.BlockSpec(memory_space=pl.ANY),
        out_shape=jax.ShapeDtypeStruct((n_rows, dp), xp.dtype),
        scratch_shapes=[pltpu.SemaphoreType.DMA],
        input_output_aliases={2: 0},
        compiler_params=pltpu.CompilerParams(dimension_semantics=("arbitrary",)),
        name="dispatch",
    )(pos, xp, xs0)


def _expert_kernel(te_ref, first_ref, nused_ref, xs_ref, wg_ref, wu_ref, wd_ref, ys_ref, wgu_bf_ref, wd_bf_ref):
    i = pl.program_id(0)
    bf16 = jnp.bfloat16

    @pl.when(first_ref[i] == 1)
    def _():
        wgu_bf_ref[:, 0:D_EXPERT] = wg_ref[...].astype(bf16)
        wgu_bf_ref[:, D_EXPERT:2 * D_EXPERT] = wu_ref[...].astype(bf16)
        wd_bf_ref[...] = wd_ref[...].astype(bf16)

    @pl.when(i < nused_ref[0])
    def _():
        lo, hi = _unpack_halves(xs_ref[...])
        gu = (_dot(lo.astype(bf16), wgu_bf_ref[0:D_PACK, :])
              + _dot(hi.astype(bf16), wgu_bf_ref[D_PACK:D_MODEL, :]))
        g = gu[:, 0:D_EXPERT]
        h = g * jax.nn.sigmoid(g) * gu[:, D_EXPERT:2 * D_EXPERT]
        ys_ref[...] = _pack_halves(_dot(h.astype(bf16), wd_bf_ref[...]))

    @pl.when(i >= nused_ref[0])
    def _():
        ys_ref[...] = jnp.zeros_like(ys_ref)


def _experts(tile_expert, tile_first, n_used, xs, w_gate, w_up, w_down):
    n_rows, dp = xs.shape
    d = D_MODEL
    tm = EXPERT_TILE
    wmap = lambda i, te, fi, nu: (te[i], 0, 0)
    grid_spec = pltpu.PrefetchScalarGridSpec(
        num_scalar_prefetch=3,
        grid=(n_rows // tm,),
        in_specs=[pl.BlockSpec((tm, dp), lambda i, te, fi, nu: (i, 0)),
                  pl.BlockSpec((None, d, D_EXPERT), wmap),
                  pl.BlockSpec((None, d, D_EXPERT), wmap),
                  pl.BlockSpec((None, D_EXPERT, d), wmap)],
        out_specs=pl.BlockSpec((tm, dp), lambda i, te, fi, nu: (i, 0)),
        scratch_shapes=[pltpu.VMEM((d, 2 * D_EXPERT), jnp.bfloat16),
                        pltpu.VMEM((D_EXPERT, d), jnp.bfloat16)],
    )
    return pl.pallas_call(
        _expert_kernel,
        grid_spec=grid_spec,
        out_shape=jax.ShapeDtypeStruct((n_rows, dp), jnp.uint32),
        compiler_params=pltpu.CompilerParams(dimension_semantics=("arbitrary",),
                                             vmem_limit_bytes=VMEM_LIMIT_BYTES),
        name="experts",
    )(tile_expert, tile_first, n_used, xs, w_gate, w_up, w_down)


def _combine_kernel(pos_ref, x1_ref, slab_ref, ys_ref, g_ref, b_ref, out_a_ref, out_b_ref, ybuf_ref, sem,
                    *, tt, n_tiles_a, alpha):
    def start(r, c):
        _row_copy(ys_ref, pos_ref[0, r], ybuf_ref.at[0], r, sem).start()
        _row_copy(ys_ref, pos_ref[1, r], ybuf_ref.at[1], r, sem).start()
        return c

    lax.fori_loop(0, tt, start, 0, unroll=8)
    pltpu.make_async_copy(ys_ref.at[pl.ds(0, tt)], ybuf_ref.at[0], sem).wait()
    pltpu.make_async_copy(ys_ref.at[pl.ds(0, tt)], ybuf_ref.at[1], sem).wait()

    slab = slab_ref[...]
    w1 = slab[:, COL_W1:COL_W1 + 1]
    w2 = slab[:, COL_W2:COL_W2 + 1]
    lo1, hi1 = _unpack_halves(ybuf_ref[0])
    lo2, hi2 = _unpack_halves(ybuf_ref[1])
    moe = jnp.concatenate([w1 * lo1 + w2 * lo2, w1 * hi1 + w2 * hi2], axis=1)
    out = _layer_norm(alpha * x1_ref[...] + moe, g_ref[...], b_ref[...])
    from_a = pl.program_id(0) < n_tiles_a

    @pl.when(from_a)
    def _():
        out_a_ref[...] = out

    @pl.when(jnp.logical_not(from_a))
    def _():
        out_b_ref[...] = out


def _combine(pos, x1, slab, ys, g, b, n_a, alpha):
    n, d = x1.shape
    tt = TOKEN_TILE
    n_tiles_a = n_a // tt
    return pl.pallas_call(
        functools.partial(_combine_kernel, tt=tt, n_tiles_a=n_tiles_a, alpha=alpha),
        grid=(n // tt,),
        in_specs=[pl.BlockSpec((None, 2, tt), lambda i: (i, 0, 0), memory_space=pltpu.SMEM),
                  pl.BlockSpec((tt, d), lambda i: (i, 0)),
                  pl.BlockSpec((tt, LANES), lambda i: (i, 0)),
                  pl.BlockSpec(memory_space=pl.ANY),
                  pl.BlockSpec((1, d), lambda i: (0, 0)),
                  pl.BlockSpec((1, d), lambda i: (0, 0))],
        out_specs=(pl.BlockSpec((tt, d), lambda i: (jnp.minimum(i, n_tiles_a - 1), 0)),
                   pl.BlockSpec((tt, d), lambda i: (jnp.maximum(i - n_tiles_a, 0), 0))),
        out_shape=(jax.ShapeDtypeStruct((n_a, d), jnp.float32),
                   jax.ShapeDtypeStruct((n - n_a, d), jnp.float32)),
        scratch_shapes=[pltpu.VMEM((2, tt, D_PACK), jnp.uint32), pltpu.SemaphoreType.DMA],
        compiler_params=pltpu.CompilerParams(dimension_semantics=("arbitrary",),
                                             vmem_limit_bytes=VMEM_LIMIT_BYTES),
        name="combine",
    )(pos, x1, slab, ys, g, b)


def _encoder_layer(xa, xb, w_in, b_in, conv_w, conv_b, sg_ln_g, sg_ln_b, w_s, b_s, w_o, b_o,
                   ln1_g, ln1_b, w_rc, b_rc, w_rf, b_rf, w_gate, w_up, w_down, ln2_g, ln2_b, alpha):
    d = D_MODEL
    n_a = xa.shape[0] * xa.shape[1]
    n = n_a + xb.shape[0] * xb.shape[1]
    bf16 = jnp.bfloat16
    row = lambda a: a.reshape(1, -1)
    ws_pairs = w_s.reshape(N_SG_HEADS // 2, 2 * CHUNK, CHUNK).astype(bf16)
    bs_full = jnp.repeat(b_s.T, SG_HEAD_DIM, axis=1)
    w_r = jnp.zeros((d, LANES), jnp.float32).at[:, 0:N_EXPERTS].set(w_rf)
    w_r = w_r.at[:, COARSE_OFF:COARSE_OFF + N_EXPERT_GROUPS].set(w_rc).astype(bf16)
    b_r = jnp.zeros((1, LANES), jnp.float32).at[0, 0:N_EXPERTS].set(b_rf)
    b_r = b_r.at[0, COARSE_OFF:COARSE_OFF + N_EXPERT_GROUPS].set(b_rc)
    params = (w_in.astype(bf16), row(b_in), conv_w, row(conv_b), row(sg_ln_g), row(sg_ln_b),
              ws_pairs, bs_full, w_o.astype(bf16), row(b_o), row(ln1_g), row(ln1_b), w_r, b_r)

    x1, xp, slab, route, carry = _mixer(xa, xb, params, alpha)

    tm = EXPERT_TILE
    n_tiles = 2 * n // tm + N_EXPERTS
    cnt = carry[0, 0:N_EXPERTS].astype(jnp.int32)
    tiles_e = (cnt + tm - 1) // tm
    tile_end = jnp.cumsum(tiles_e)
    tile_start = tile_end - tiles_e
    row_off = tile_start * tm
    experts = jnp.arange(N_EXPERTS, dtype=jnp.int32)[:, None]

    def sorted_pos(e_row, r_row):
        e = route[e_row].astype(jnp.int32)
        return jnp.sum(jnp.where(e[None, :] == experts, row_off[:, None], 0), axis=0) + route[r_row].astype(jnp.int32)

    tt = TOKEN_TILE
    pos = jnp.stack([sorted_pos(COL_E1, COL_R1).reshape(n // tt, tt),
                     sorted_pos(COL_E2, COL_R2).reshape(n // tt, tt)], axis=1)
    tile_ids = jnp.arange(n_tiles, dtype=jnp.int32)
    n_used = tile_end[N_EXPERTS - 1:N_EXPERTS]
    tile_expert = jnp.sum((tile_end[None, :] <= tile_ids[:, None]).astype(jnp.int32), axis=1)
    last_used = jnp.sum((tile_end <= n_used[0] - 1).astype(jnp.int32))
    tile_expert = jnp.minimum(tile_expert, last_used)
    tile_first = jnp.any(tile_ids[:, None] == tile_start[None, :], axis=1) & (tile_ids < n_used[0])
    tile_first = tile_first.astype(jnp.int32).at[0].set(1)

    xs = _dispatch(pos, xp, n_tiles * tm)
    ys = _experts(tile_expert, tile_first, n_used, xs, w_gate, w_up, w_down)
    out_a, out_b = _combine(pos, x1, slab, ys, row(ln2_g), row(ln2_b), n_a, alpha)
    return out_a.reshape(xa.shape), out_b.reshape(xb.shape)


def kernel(x_prompt, x_sample, w_in, b_in, conv_w, conv_b, sg_ln_g, sg_ln_b, w_s, b_s, w_o, b_o, ln1_g, ln1_b, w_rc, b_rc, w_rf, b_rf, w_gate, w_up, w_down, ln2_g, ln2_b):
    depth = w_in.shape[0]
    alpha = (2.0 * depth) ** 0.25
    xs = (x_prompt, x_sample)
    for l in range(depth):
        xs = _encoder_layer(*xs, w_in[l], b_in[l], conv_w[l], conv_b[l], sg_ln_g[l], sg_ln_b[l], w_s[l], b_s[l],
                            w_o[l], b_o[l], ln1_g[l], ln1_b[l], w_rc[l], b_rc[l], w_rf[l], b_rf[l],
                            w_gate[l], w_up[l], w_down[l], ln2_g[l], ln2_b[l], alpha)
    return (xs[0], xs[1])
```

```python
import functools

import jax
import jax.numpy as jnp
from jax import lax
from jax.experimental import pallas as pl
from jax.experimental.pallas import tpu as pltpu
from jax.experimental.pallas import tpu_sc as plsc

D_MODEL = 1024
D_CONV = 512
D_SG = 512
N_SG_HEADS = 8
SG_HEAD_DIM = D_SG // N_SG_HEADS
CHUNK = 128
N_EXPERT_GROUPS = 4
EXPERTS_PER_GROUP = 8
N_EXPERTS = N_EXPERT_GROUPS * EXPERTS_PER_GROUP
D_EXPERT = 512
LN_EPS = 1e-5
D_PACK = D_MODEL // 2

LANES = 128
SUBLANES = 8
HALO = 16
SEQ_TILE = 512
EXPERT_TILE = 256
TOKEN_TILE = 256
SC_CORES = 2
SC_SUBCORES = 16
SC_WORKERS = SC_CORES * SC_SUBCORES
SC_ROWS = 64
VMEM_LIMIT_BYTES = 56 * 1024 * 1024

COL_E1, COL_E2, COL_W1, COL_W2, COL_R1, COL_R2 = 0, 1, 2, 3, 4, 5
COARSE_OFF = N_EXPERTS


def _dot(a, b):
    return jnp.dot(a, b, preferred_element_type=jnp.float32)


def _gelu_tanh(x):
    return 0.5 * x * (1.0 + jnp.tanh(0.7978845608028654 * (x + 0.044715 * (x * x * x))))


def _layer_norm(x, g, b):
    mu = jnp.mean(x, axis=-1, keepdims=True)
    xc = x - mu
    var = jnp.mean(xc * xc, axis=-1, keepdims=True)
    return xc * lax.rsqrt(var + LN_EPS) * g + b


def _pack_halves(x):
    u32 = jnp.uint32
    lo = lax.bitcast_convert_type(x[:, 0:D_PACK].astype(jnp.bfloat16).astype(jnp.float32), u32)
    hi = lax.bitcast_convert_type(x[:, D_PACK:D_MODEL].astype(jnp.bfloat16).astype(jnp.float32), u32)
    return (hi & u32(0xFFFF0000)) | (lo >> u32(16))


def _unpack_halves(w):
    u32 = jnp.uint32
    lo = lax.bitcast_convert_type(w << u32(16), jnp.float32)
    hi = lax.bitcast_convert_type(w & u32(0xFFFF0000), jnp.float32)
    return lo, hi


def _mixer_kernel(xa_ref, xa_prev_ref, xa_next_ref, xb_in_ref, xb_prev_ref, xb_next_ref,
                  w_in_ref, b_in_ref, conv_w_ref, conv_b_ref,
                  sg_g_ref, sg_b_ref, ws_ref, bs_ref, w_o_ref, b_o_ref, ln1_g_ref, ln1_b_ref,
                  w_r_ref, b_r_ref,
                  x1_ref, xp_ref, slab_ref, route_ref, counts_ref,
                  xb_ref, ymix_ref, carry_ref, *, ts, nt, n_tiles_a, alpha):
    t = pl.program_id(0)
    j = t % nt
    from_a = t < n_tiles_a

    @pl.when(t == 0)
    def _():
        carry_ref[...] = jnp.zeros_like(carry_ref)

    bf16 = jnp.bfloat16
    x = jnp.where(from_a, xa_ref[...], xb_in_ref[...])
    xb_ref[0:HALO, :] = jnp.where(from_a, xa_prev_ref[...], xb_prev_ref[...]).astype(bf16)
    xb_ref[HALO:HALO + ts, :] = x.astype(bf16)
    xb_ref[HALO + ts:HALO + ts + HALO, :] = jnp.where(from_a, xa_next_ref[...], xb_next_ref[...]).astype(bf16)
    xe = xb_ref[...]
    xm = xb_ref[HALO:HALO + ts, :]

    def proj(lhs, lo, hi):
        return _dot(lhs, w_in_ref[:, lo:hi]) + b_in_ref[:, lo:hi]

    g_e = proj(xe, 0, D_CONV) * proj(xe, 2 * D_CONV, 3 * D_CONV)
    row = lax.broadcasted_iota(jnp.int32, (ts + 2 * HALO, 1), 0)
    has_prev = jnp.where(j > 0, 1.0, 0.0)
    has_next = jnp.where(j < nt - 1, 1.0, 0.0)
    in_seq = jnp.where(row < HALO, has_prev, jnp.where(row >= HALO + ts, has_next, 1.0))
    g_e = g_e * in_seq
    g_prev = g_e[HALO - 1:HALO - 1 + ts, :]
    g_mid = g_e[HALO:HALO + ts, :]
    g_next = g_e[HALO + 1:HALO + 1 + ts, :]
    conv = (g_prev * conv_w_ref[0:1, :] + g_mid * conv_w_ref[1:2, :] + g_next * conv_w_ref[2:3, :]
            + conv_b_ref[...])
    y_a = proj(xm, D_CONV, 2 * D_CONV) * conv
    ymix_ref[:, 0:D_CONV] = y_a.astype(bf16)

    u = _gelu_tanh(proj(xm, 3 * D_CONV, 3 * D_CONV + D_SG))
    v = _gelu_tanh(proj(xm, 3 * D_CONV + D_SG, 3 * D_CONV + 2 * D_SG))
    v_ln = _layer_norm(v, sg_g_ref[...], sg_b_ref[...]).astype(bf16)
    lane_c = lax.broadcasted_iota(jnp.int32, (CHUNK, LANES), 1)
    first_head = lane_c < SG_HEAD_DIM
    for cp in range(ts // (2 * CHUNK)):
        r0 = cp * 2 * CHUNK
        r1 = r0 + CHUNK
        for hp in range(N_SG_HEADS // 2):
            c0 = hp * LANES
            rhs = jnp.concatenate([v_ln[r0:r0 + CHUNK, c0:c0 + LANES],
                                   v_ln[r1:r1 + CHUNK, c0:c0 + LANES]], axis=1)
            res = _dot(ws_ref[hp], rhs)
            bias = bs_ref[:, c0:c0 + LANES]
            m0 = jnp.where(first_head, res[0:CHUNK, 0:LANES], res[CHUNK:2 * CHUNK, 0:LANES]) + bias
            m1 = jnp.where(first_head, res[0:CHUNK, LANES:2 * LANES], res[CHUNK:2 * CHUNK, LANES:2 * LANES]) + bias
            ymix_ref[r0:r0 + CHUNK, D_CONV + c0:D_CONV + c0 + LANES] = (u[r0:r0 + CHUNK, c0:c0 + LANES] * m0).astype(bf16)
            ymix_ref[r1:r1 + CHUNK, D_CONV + c0:D_CONV + c0 + LANES] = (u[r1:r1 + CHUNK, c0:c0 + LANES] * m1).astype(bf16)

    mix = _dot(ymix_ref[...], w_o_ref[...]) + b_o_ref[...]
    x1 = _layer_norm(alpha * x + mix, ln1_g_ref[...], ln1_b_ref[...])
    x1_ref[...] = x1
    xp_ref[...] = _pack_halves(x1)

    logits = _dot(x1.astype(bf16), w_r_ref[...]) + b_r_ref[...]
    lane = lax.broadcasted_iota(jnp.int32, (ts, LANES), 1)
    lane_f = lane.astype(jnp.float32)
    neg = jnp.float32(-jnp.inf)
    big = jnp.float32(1e9)
    is_c = (lane >= COARSE_OFF) & (lane < COARSE_OFF + N_EXPERT_GROUPS)
    lc = jnp.where(is_c, logits, neg)
    mx = jnp.max(lc, axis=1, keepdims=True)
    grp = jnp.min(jnp.where(lc == mx, lane_f - COARSE_OFF, big), axis=1, keepdims=True)
    p_grp = 1.0 / jnp.sum(jnp.where(is_c, jnp.exp(logits - mx), 0.0), axis=1, keepdims=True)
    grp_lo = grp * EXPERTS_PER_GROUP
    in_grp = (lane_f >= grp_lo) & (lane_f < grp_lo + EXPERTS_PER_GROUP)
    lf = jnp.where(in_grp, logits, neg)
    v1 = jnp.max(lf, axis=1, keepdims=True)
    e1 = jnp.min(jnp.where(lf == v1, lane_f, big), axis=1, keepdims=True)
    lf2 = jnp.where(lane_f == e1, neg, lf)
    v2 = jnp.max(lf2, axis=1, keepdims=True)
    e2 = jnp.min(jnp.where(lf2 == v2, lane_f, big), axis=1, keepdims=True)
    a = jnp.exp(v2 - v1)
    w1 = p_grp / (1.0 + a)
    w2 = p_grp * a / (1.0 + a)

    hit1 = lane_f == e1
    hit2 = lane_f == e2
    onehot = jnp.where(hit1 | hit2, 1.0, 0.0)
    ri = lax.broadcasted_iota(jnp.int32, (ts, ts), 0)
    ci = lax.broadcasted_iota(jnp.int32, (ts, ts), 1)
    tri = jnp.where(ri > ci, 1.0, 0.0).astype(bf16)
    before = _dot(tri, onehot.astype(bf16)) + carry_ref[...]
    r1 = jnp.sum(jnp.where(hit1, before, 0.0), axis=1, keepdims=True)
    r2 = jnp.sum(jnp.where(hit2, before, 0.0), axis=1, keepdims=True)
    carry = carry_ref[...] + jnp.sum(onehot, axis=0, keepdims=True)
    carry_ref[...] = carry
    counts_ref[...] = jnp.broadcast_to(carry, counts_ref.shape)

    slab = jnp.where(lane == COL_E1, e1, 0.0)
    slab = jnp.where(lane == COL_E2, e2, slab)
    slab = jnp.where(lane == COL_W1, w1, slab)
    slab = jnp.where(lane == COL_W2, w2, slab)
    slab = jnp.where(lane == COL_R1, r1, slab)
    slab = jnp.where(lane == COL_R2, r2, slab)
    slab_ref[...] = slab
    route_ref[...] = slab.T[0:SUBLANES, :]


def _mixer(xa, xb, p, alpha):
    nba, s, d = xa.shape
    nbb = xb.shape[0]
    ts = SEQ_TILE
    nt = s // ts
    hb = ts // HALO
    n_tiles_a = nba * nt
    n_tiles = n_tiles_a + nbb * nt
    n_total = n_tiles * ts

    tile_a = lambda t: jnp.minimum(t, n_tiles_a - 1)
    tile_b = lambda t: jnp.maximum(t - n_tiles_a, 0)

    def x_specs(tile):
        cur = lambda t: (tile(t) // nt, tile(t) % nt, 0)
        prv = lambda t: (tile(t) // nt, jnp.maximum((tile(t) % nt) * hb - 1, 0), 0)
        nxt = lambda t: (tile(t) // nt, jnp.minimum((tile(t) % nt + 1) * hb, s // HALO - 1), 0)
        return [pl.BlockSpec((None, ts, d), cur), pl.BlockSpec((None, HALO, d), prv),
                pl.BlockSpec((None, HALO, d), nxt)]

    full = lambda a: pl.BlockSpec(a.shape, lambda t: (0,) * a.ndim)
    in_specs = x_specs(tile_a) + x_specs(tile_b) + [full(a) for a in p]
    out_shape = (jax.ShapeDtypeStruct((n_total, d), jnp.float32),
                 jax.ShapeDtypeStruct((n_total, D_PACK), jnp.uint32),
                 jax.ShapeDtypeStruct((n_total, LANES), jnp.float32),
                 jax.ShapeDtypeStruct((SUBLANES, n_total), jnp.float32),
                 jax.ShapeDtypeStruct((SUBLANES, LANES), jnp.float32))
    row_blk = lambda t: (t, 0)
    out_specs = (pl.BlockSpec((ts, d), row_blk),
                 pl.BlockSpec((ts, D_PACK), row_blk),
                 pl.BlockSpec((ts, LANES), row_blk),
                 pl.BlockSpec((SUBLANES, ts), lambda t: (0, t)),
                 pl.BlockSpec((SUBLANES, LANES), lambda t: (0, 0)))
    return pl.pallas_call(
        functools.partial(_mixer_kernel, ts=ts, nt=nt, n_tiles_a=n_tiles_a, alpha=alpha),
        grid=(n_tiles,),
        in_specs=in_specs,
        out_specs=out_specs,
        out_shape=out_shape,
        scratch_shapes=[pltpu.VMEM((ts + 2 * HALO, d), jnp.bfloat16),
                        pltpu.VMEM((ts, d), jnp.bfloat16),
                        pltpu.VMEM((1, LANES), jnp.float32)],
        compiler_params=pltpu.CompilerParams(dimension_semantics=("arbitrary",),
                                             vmem_limit_bytes=VMEM_LIMIT_BYTES),
        name="mixer",
    )(xa, xa, xa, xb, xb, xb, *p)


def _sc_mesh():
    return plsc.VectorSubcoreMesh(core_axis_name="c", subcore_axis_name="s",
                                  num_cores=SC_CORES, num_subcores=SC_SUBCORES)


def _sc_worker_base(per_worker):
    return (lax.axis_index("s") * SC_CORES + lax.axis_index("c")) * per_worker


def _sc_dispatch(pos1, pos2, xp, n_rows):
    n, dp = xp.shape
    per_worker = n // SC_WORKERS
    steps = per_worker // SC_ROWS

    def body(pos1_hbm, pos2_hbm, xp_hbm, xs_hbm, idx1_v, idx2_v, rows_v, sem):
        base = _sc_worker_base(per_worker)

        @pl.loop(0, steps)
        def _(k):
            off = pl.multiple_of(base + k * SC_ROWS, SC_ROWS)
            pltpu.sync_copy(pos1_hbm.at[pl.ds(off, SC_ROWS)], idx1_v)
            pltpu.sync_copy(pos2_hbm.at[pl.ds(off, SC_ROWS)], idx2_v)
            pltpu.sync_copy(xp_hbm.at[pl.ds(off, SC_ROWS)], rows_v)
            c1 = pltpu.async_copy(rows_v, xs_hbm.at[idx1_v], sem)
            c2 = pltpu.async_copy(rows_v, xs_hbm.at[idx2_v], sem)
            c1.wait()
            c2.wait()

    return pl.kernel(
        body,
        out_type=jax.ShapeDtypeStruct((n_rows, dp), xp.dtype),
        mesh=_sc_mesh(),
        scratch_types=[pltpu.VMEM((SC_ROWS,), jnp.int32), pltpu.VMEM((SC_ROWS,), jnp.int32),
                       pltpu.VMEM((SC_ROWS, dp), xp.dtype), pltpu.SemaphoreType.DMA],
        compiler_params=pltpu.CompilerParams(use_tc_tiling_on_sc=True),
        name="sc_dispatch",
    )(pos1, pos2, xp)


def _sc_gather(pos1, pos2, ys):
    n = pos1.shape[0]
    dp = ys.shape[1]
    per_worker = n // SC_WORKERS
    steps = per_worker // SC_ROWS

    def body(pos1_hbm, pos2_hbm, ys_hbm, out_hbm, idx_v, rows_v, sem):
        base = _sc_worker_base(per_worker)

        @pl.loop(0, steps)
        def _(k):
            off = pl.multiple_of(base + k * SC_ROWS, SC_ROWS)
            for slot, pos_hbm in enumerate((pos1_hbm, pos2_hbm)):
                pltpu.sync_copy(pos_hbm.at[pl.ds(off, SC_ROWS)], idx_v)
                pltpu.async_copy(ys_hbm.at[idx_v], rows_v, sem).wait()
                pltpu.sync_copy(rows_v, out_hbm.at[slot, pl.ds(off, SC_ROWS)])

    return pl.kernel(
        body,
        out_type=jax.ShapeDtypeStruct((2, n, dp), ys.dtype),
        mesh=_sc_mesh(),
        scratch_types=[pltpu.VMEM((SC_ROWS,), jnp.int32), pltpu.VMEM((SC_ROWS, dp), ys.dtype),
                       pltpu.SemaphoreType.DMA],
        compiler_params=pltpu.CompilerParams(use_tc_tiling_on_sc=True),
        name="sc_gather",
    )(pos1, pos2, ys)


def _expert_kernel(te_ref, first_ref, nused_ref, xs_ref, wg_ref, wu_ref, wd_ref, ys_ref, wgu_bf_ref, wd_bf_ref):
    i = pl.program_id(0)
    bf16 = jnp.bfloat16

    @pl.when(first_ref[i] == 1)
    def _():
        wgu_bf_ref[:, 0:D_EXPERT] = wg_ref[...].astype(bf16)
        wgu_bf_ref[:, D_EXPERT:2 * D_EXPERT] = wu_ref[...].astype(bf16)
        wd_bf_ref[...] = wd_ref[...].astype(bf16)

    @pl.when(i < nused_ref[0])
    def _():
        lo, hi = _unpack_halves(xs_ref[...])
        gu = (_dot(lo.astype(bf16), wgu_bf_ref[0:D_PACK, :])
              + _dot(hi.astype(bf16), wgu_bf_ref[D_PACK:D_MODEL, :]))
        g = gu[:, 0:D_EXPERT]
        h = g * jax.nn.sigmoid(g) * gu[:, D_EXPERT:2 * D_EXPERT]
        ys_ref[...] = _pack_halves(_dot(h.astype(bf16), wd_bf_ref[...]))

    @pl.when(i >= nused_ref[0])
    def _():
        ys_ref[...] = jnp.zeros_like(ys_ref)


def _experts(tile_expert, tile_first, n_used, xs, w_gate, w_up, w_down):
    n_rows, dp = xs.shape
    d = D_MODEL
    tm = EXPERT_TILE
    wmap = lambda i, te, fi, nu: (te[i], 0, 0)
    grid_spec = pltpu.PrefetchScalarGridSpec(
        num_scalar_prefetch=3,
        grid=(n_rows // tm,),
        in_specs=[pl.BlockSpec((tm, dp), lambda i, te, fi, nu: (i, 0)),
                  pl.BlockSpec((None, d, D_EXPERT), wmap),
                  pl.BlockSpec((None, d, D_EXPERT), wmap),
                  pl.BlockSpec((None, D_EXPERT, d), wmap)],
        out_specs=pl.BlockSpec((tm, dp), lambda i, te, fi, nu: (i, 0)),
        scratch_shapes=[pltpu.VMEM((d, 2 * D_EXPERT), jnp.bfloat16),
                        pltpu.VMEM((D_EXPERT, d), jnp.bfloat16)],
    )
    return pl.pallas_call(
        _expert_kernel,
        grid_spec=grid_spec,
        out_shape=jax.ShapeDtypeStruct((n_rows, dp), jnp.uint32),
        compiler_params=pltpu.CompilerParams(dimension_semantics=("arbitrary",),
                                             vmem_limit_bytes=VMEM_LIMIT_BYTES),
        name="experts",
    )(tile_expert, tile_first, n_used, xs, w_gate, w_up, w_down)


def _combine_kernel(x1_ref, slab_ref, y1_ref, y2_ref, g_ref, b_ref, out_a_ref, out_b_ref, *, n_tiles_a, alpha):
    slab = slab_ref[...]
    w1 = slab[:, COL_W1:COL_W1 + 1]
    w2 = slab[:, COL_W2:COL_W2 + 1]
    lo1, hi1 = _unpack_halves(y1_ref[...])
    lo2, hi2 = _unpack_halves(y2_ref[...])
    moe = jnp.concatenate([w1 * lo1 + w2 * lo2, w1 * hi1 + w2 * hi2], axis=1)
    out = _layer_norm(alpha * x1_ref[...] + moe, g_ref[...], b_ref[...])
    from_a = pl.program_id(0) < n_tiles_a

    @pl.when(from_a)
    def _():
        out_a_ref[...] = out

    @pl.when(jnp.logical_not(from_a))
    def _():
        out_b_ref[...] = out


def _combine(x1, slab, yg, g, b, n_a, alpha):
    n, d = x1.shape
    tt = TOKEN_TILE
    n_tiles_a = n_a // tt
    return pl.pallas_call(
        functools.partial(_combine_kernel, n_tiles_a=n_tiles_a, alpha=alpha),
        grid=(n // tt,),
        in_specs=[pl.BlockSpec((tt, d), lambda i: (i, 0)),
                  pl.BlockSpec((tt, LANES), lambda i: (i, 0)),
                  pl.BlockSpec((None, tt, D_PACK), lambda i: (0, i, 0)),
                  pl.BlockSpec((None, tt, D_PACK), lambda i: (1, i, 0)),
                  pl.BlockSpec((1, d), lambda i: (0, 0)),
                  pl.BlockSpec((1, d), lambda i: (0, 0))],
        out_specs=(pl.BlockSpec((tt, d), lambda i: (jnp.minimum(i, n_tiles_a - 1), 0)),
                   pl.BlockSpec((tt, d), lambda i: (jnp.maximum(i - n_tiles_a, 0), 0))),
        out_shape=(jax.ShapeDtypeStruct((n_a, d), jnp.float32),
                   jax.ShapeDtypeStruct((n - n_a, d), jnp.float32)),
        compiler_params=pltpu.CompilerParams(dimension_semantics=("arbitrary",),
                                             vmem_limit_bytes=VMEM_LIMIT_BYTES),
        name="combine",
    )(x1, slab, yg, yg, g, b)


def _encoder_layer(xa, xb, w_in, b_in, conv_w, conv_b, sg_ln_g, sg_ln_b, w_s, b_s, w_o, b_o,
                   ln1_g, ln1_b, w_rc, b_rc, w_rf, b_rf, w_gate, w_up, w_down, ln2_g, ln2_b, alpha):
    d = D_MODEL
    n_a = xa.shape[0] * xa.shape[1]
    n = n_a + xb.shape[0] * xb.shape[1]
    bf16 = jnp.bfloat16
    row = lambda a: a.reshape(1, -1)
    ws_pairs = w_s.reshape(N_SG_HEADS // 2, 2 * CHUNK, CHUNK).astype(bf16)
    bs_full = jnp.repeat(b_s.T, SG_HEAD_DIM, axis=1)
    w_r = jnp.zeros((d, LANES), jnp.float32).at[:, 0:N_EXPERTS].set(w_rf)
    w_r = w_r.at[:, COARSE_OFF:COARSE_OFF + N_EXPERT_GROUPS].set(w_rc).astype(bf16)
    b_r = jnp.zeros((1, LANES), jnp.float32).at[0, 0:N_EXPERTS].set(b_rf)
    b_r = b_r.at[0, COARSE_OFF:COARSE_OFF + N_EXPERT_GROUPS].set(b_rc)
    params = (w_in.astype(bf16), row(b_in), conv_w, row(conv_b), row(sg_ln_g), row(sg_ln_b),
              ws_pairs, bs_full, w_o.astype(bf16), row(b_o), row(ln1_g), row(ln1_b), w_r, b_r)

    x1, xp, slab, route, carry = _mixer(xa, xb, params, alpha)

    tm = EXPERT_TILE
    n_tiles = 2 * n // tm + N_EXPERTS
    cnt = carry[0, 0:N_EXPERTS].astype(jnp.int32)
    tiles_e = (cnt + tm - 1) // tm
    tile_end = jnp.cumsum(tiles_e)
    tile_start = tile_end - tiles_e
    row_off = tile_start * tm
    experts = jnp.arange(N_EXPERTS, dtype=jnp.int32)[:, None]

    def sorted_pos(e_row, r_row):
        e = route[e_row].astype(jnp.int32)
        return jnp.sum(jnp.where(e[None, :] == experts, row_off[:, None], 0), axis=0) + route[r_row].astype(jnp.int32)

    pos1 = sorted_pos(COL_E1, COL_R1)
    pos2 = sorted_pos(COL_E2, COL_R2)
    tile_ids = jnp.arange(n_tiles, dtype=jnp.int32)
    n_used = tile_end[N_EXPERTS - 1:N_EXPERTS]
    tile_expert = jnp.sum((tile_end[None, :] <= tile_ids[:, None]).astype(jnp.int32), axis=1)
    last_used = jnp.sum((tile_end <= n_used[0] - 1).astype(jnp.int32))
    tile_expert = jnp.minimum(tile_expert, last_used)
    tile_first = jnp.any(tile_ids[:, None] == tile_start[None, :], axis=1) & (tile_ids < n_used[0])
    tile_first = tile_first.astype(jnp.int32).at[0].set(1)

    xs = _sc_dispatch(pos1, pos2, xp, n_tiles * tm)
    ys = _experts(tile_expert, tile_first, n_used, xs, w_gate, w_up, w_down)
    yg = _sc_gather(pos1, pos2, ys)
    out_a, out_b = _combine(x1, slab, yg, row(ln2_g), row(ln2_b), n_a, alpha)
    return out_a.reshape(xa.shape), out_b.reshape(xb.shape)


def kernel(x_prompt, x_sample, w_in, b_in, conv_w, conv_b, sg_ln_g, sg_ln_b, w_s, b_s, w_o, b_o, ln1_g, ln1_b, w_rc, b_rc, w_rf, b_rf, w_gate, w_up, w_down, ln2_g, ln2_b):
    depth = w_in.shape[0]
    alpha = (2.0 * depth) ** 0.25
    xs = (x_prompt, x_sample)
    for l in range(depth):
        xs = _encoder_layer(*xs, w_in[l], b_in[l], conv_w[l], conv_b[l], sg_ln_g[l], sg_ln_b[l], w_s[l], b_s[l],
                            w_o[l], b_o[l], ln1_g[l], ln1_b[l], w_rc[l], b_rc[l], w_rf[l], b_rf[l],
                            w_gate[l], w_up[l], w_down[l], ln2_g[l], ln2_b[l], alpha)
    return (xs[0], xs[1])
```

```python
import functools

import jax
import jax.numpy as jnp
from jax import lax
from jax.experimental import pallas as pl
from jax.experimental.pallas import tpu as pltpu
from jax.experimental.pallas import tpu_sc as plsc

D_MODEL = 1024
D_CONV = 512
D_SG = 512
N_SG_HEADS = 8
SG_HEAD_DIM = D_SG // N_SG_HEADS
CHUNK = 128
N_EXPERT_GROUPS = 4
EXPERTS_PER_GROUP = 8
N_EXPERTS = N_EXPERT_GROUPS * EXPERTS_PER_GROUP
D_EXPERT = 512
LN_EPS = 1e-5
D_PACK = D_MODEL // 2

LANES = 128
SUBLANES = 8
HALO = 16
SEQ_TILE = 512
SEQ_SUBTILE = 512
EXPERT_TILE = 512
EXPERT_SUBTILE = 256
TOKEN_TILE = 256
SC_CORES = 2
SC_SUBCORES = 16
SC_WORKERS = SC_CORES * SC_SUBCORES
SC_ROWS = 64
VMEM_LIMIT_BYTES = 56 * 1024 * 1024

COL_E1, COL_E2, COL_W1, COL_W2, COL_R1, COL_R2 = 0, 1, 2, 3, 4, 5
COARSE_OFF = N_EXPERTS


def _dot(a, b):
    return jnp.dot(a, b, preferred_element_type=jnp.float32)


def _gelu_tanh(x):
    return 0.5 * x * (1.0 + jnp.tanh(0.7978845608028654 * (x + 0.044715 * (x * x * x))))


def _layer_norm(x, g, b):
    mu = jnp.mean(x, axis=-1, keepdims=True)
    xc = x - mu
    var = jnp.mean(xc * xc, axis=-1, keepdims=True)
    return xc * lax.rsqrt(var + LN_EPS) * g + b


def _pack_halves(x):
    u32 = jnp.uint32
    lo = lax.bitcast_convert_type(x[:, 0:D_PACK].astype(jnp.bfloat16).astype(jnp.float32), u32)
    hi = lax.bitcast_convert_type(x[:, D_PACK:D_MODEL].astype(jnp.bfloat16).astype(jnp.float32), u32)
    return (hi & u32(0xFFFF0000)) | (lo >> u32(16))


def _unpack_halves(w):
    u32 = jnp.uint32
    lo = lax.bitcast_convert_type(w << u32(16), jnp.float32)
    hi = lax.bitcast_convert_type(w & u32(0xFFFF0000), jnp.float32)
    return lo, hi


def _mixer_kernel(xa_ref, xa_prev_ref, xa_next_ref, xb_in_ref, xb_prev_ref, xb_next_ref,
                  w_in_ref, b_in_ref, conv_w_ref, conv_b_ref,
                  sg_g_ref, sg_b_ref, ws_ref, bs_ref, w_o_ref, b_o_ref, ln1_g_ref, ln1_b_ref,
                  w_r_ref, b_r_ref,
                  x1_ref, xp_ref, slab_ref, route_ref, counts_ref,
                  xb_ref, ymix_ref, tri_ref, carry_ref, *, ts, nt, n_tiles_a, alpha):
    t = pl.program_id(0)
    j = t % nt
    from_a = t < n_tiles_a
    bf16 = jnp.bfloat16
    sub = SEQ_SUBTILE

    @pl.when(t == 0)
    def _():
        carry_ref[...] = jnp.zeros_like(carry_ref)
        ri = lax.broadcasted_iota(jnp.int32, (sub, sub), 0)
        ci = lax.broadcasted_iota(jnp.int32, (sub, sub), 1)
        tri_ref[...] = jnp.where(ri > ci, 1.0, 0.0).astype(bf16)

    xb_ref[0:HALO, :] = jnp.where(from_a, xa_prev_ref[...], xb_prev_ref[...]).astype(bf16)
    xb_ref[HALO:HALO + ts, :] = jnp.where(from_a, xa_ref[...], xb_in_ref[...]).astype(bf16)
    xb_ref[HALO + ts:HALO + ts + HALO, :] = jnp.where(from_a, xa_next_ref[...], xb_next_ref[...]).astype(bf16)

    def proj(lhs, lo, hi):
        return _dot(lhs, w_in_ref[:, lo:hi]) + b_in_ref[:, lo:hi]

    has_prev = jnp.where(j > 0, 1.0, 0.0)
    has_next = jnp.where(j < nt - 1, 1.0, 0.0)
    row_e = lax.broadcasted_iota(jnp.int32, (sub + 2 * HALO, 1), 0)
    lane_c = lax.broadcasted_iota(jnp.int32, (CHUNK, LANES), 1)
    first_head = lane_c < SG_HEAD_DIM
    lane = lax.broadcasted_iota(jnp.int32, (sub, LANES), 1)
    lane_f = lane.astype(jnp.float32)
    neg = jnp.float32(-jnp.inf)
    big = jnp.float32(1e9)
    is_c = (lane >= COARSE_OFF) & (lane < COARSE_OFF + N_EXPERT_GROUPS)
    carry = carry_ref[...]

    for r0 in range(0, ts, sub):
        rows = pl.ds(r0, sub)
        x = jnp.where(from_a, xa_ref[rows, :], xb_in_ref[rows, :])
        xe = xb_ref[r0:r0 + sub + 2 * HALO, :]
        xm = xb_ref[HALO + r0:HALO + r0 + sub, :]

        g_e = proj(xe, 0, D_CONV) * proj(xe, 2 * D_CONV, 3 * D_CONV)
        if r0 == 0:
            g_e = g_e * jnp.where(row_e < HALO, has_prev, 1.0)
        if r0 + sub == ts:
            g_e = g_e * jnp.where(row_e >= HALO + sub, has_next, 1.0)
        g_prev = g_e[HALO - 1:HALO - 1 + sub, :]
        g_mid = g_e[HALO:HALO + sub, :]
        g_next = g_e[HALO + 1:HALO + 1 + sub, :]
        conv = (g_prev * conv_w_ref[0:1, :] + g_mid * conv_w_ref[1:2, :] + g_next * conv_w_ref[2:3, :]
                + conv_b_ref[...])
        y_a = proj(xm, D_CONV, 2 * D_CONV) * conv
        ymix_ref[rows, 0:D_CONV] = y_a.astype(bf16)

        u = _gelu_tanh(proj(xm, 3 * D_CONV, 3 * D_CONV + D_SG))
        v = _gelu_tanh(proj(xm, 3 * D_CONV + D_SG, 3 * D_CONV + 2 * D_SG))
        v_ln = _layer_norm(v, sg_g_ref[...], sg_b_ref[...]).astype(bf16)
        for q0 in range(0, sub, 2 * CHUNK):
            q1 = q0 + CHUNK
            for hp in range(N_SG_HEADS // 2):
                c0 = hp * LANES
                rhs = jnp.concatenate([v_ln[q0:q0 + CHUNK, c0:c0 + LANES], v_ln[q1:q1 + CHUNK, c0:c0 + LANES]], axis=1)
                res = _dot(ws_ref[hp], rhs)
                bias = bs_ref[:, c0:c0 + LANES]
                m0 = jnp.where(first_head, res[0:CHUNK, 0:LANES], res[CHUNK:2 * CHUNK, 0:LANES]) + bias
                m1 = jnp.where(first_head, res[0:CHUNK, LANES:2 * LANES], res[CHUNK:2 * CHUNK, LANES:2 * LANES]) + bias
                ymix_ref[r0 + q0:r0 + q0 + CHUNK, D_CONV + c0:D_CONV + c0 + LANES] = (
                    u[q0:q0 + CHUNK, c0:c0 + LANES] * m0).astype(bf16)
                ymix_ref[r0 + q1:r0 + q1 + CHUNK, D_CONV + c0:D_CONV + c0 + LANES] = (
                    u[q1:q1 + CHUNK, c0:c0 + LANES] * m1).astype(bf16)

        mix = _dot(ymix_ref[rows, :], w_o_ref[...]) + b_o_ref[...]
        x1 = _layer_norm(alpha * x + mix, ln1_g_ref[...], ln1_b_ref[...])
        x1_ref[rows, :] = x1
        xp_ref[rows, :] = _pack_halves(x1)

        logits = _dot(x1.astype(bf16), w_r_ref[...]) + b_r_ref[...]
        lc = jnp.where(is_c, logits, neg)
        mx = jnp.max(lc, axis=1, keepdims=True)
        grp = jnp.min(jnp.where(lc == mx, lane_f - COARSE_OFF, big), axis=1, keepdims=True)
        p_grp = 1.0 / jnp.sum(jnp.where(is_c, jnp.exp(logits - mx), 0.0), axis=1, keepdims=True)
        grp_lo = grp * EXPERTS_PER_GROUP
        in_grp = (lane_f >= grp_lo) & (lane_f < grp_lo + EXPERTS_PER_GROUP)
        lf = jnp.where(in_grp, logits, neg)
        v1 = jnp.max(lf, axis=1, keepdims=True)
        e1 = jnp.min(jnp.where(lf == v1, lane_f, big), axis=1, keepdims=True)
        lf2 = jnp.where(lane_f == e1, neg, lf)
        v2 = jnp.max(lf2, axis=1, keepdims=True)
        e2 = jnp.min(jnp.where(lf2 == v2, lane_f, big), axis=1, keepdims=True)
        a = jnp.exp(v2 - v1)
        w1 = p_grp / (1.0 + a)
        w2 = p_grp * a / (1.0 + a)

        hit1 = lane_f == e1
        hit2 = lane_f == e2
        onehot = jnp.where(hit1 | hit2, 1.0, 0.0)
        before = _dot(tri_ref[...], onehot.astype(bf16)) + carry
        r1 = jnp.sum(jnp.where(hit1, before, 0.0), axis=1, keepdims=True)
        r2 = jnp.sum(jnp.where(hit2, before, 0.0), axis=1, keepdims=True)
        carry = carry + jnp.sum(onehot, axis=0, keepdims=True)

        slab = jnp.where(lane == COL_E1, e1, 0.0)
        slab = jnp.where(lane == COL_E2, e2, slab)
        slab = jnp.where(lane == COL_W1, w1, slab)
        slab = jnp.where(lane == COL_W2, w2, slab)
        slab = jnp.where(lane == COL_R1, r1, slab)
        slab = jnp.where(lane == COL_R2, r2, slab)
        slab_ref[rows, :] = slab
        route_ref[:, rows] = slab.T[0:SUBLANES, :]

    carry_ref[...] = carry
    counts_ref[...] = jnp.broadcast_to(carry, counts_ref.shape)


def _mixer(xa, xb, p, alpha):
    nba, s, d = xa.shape
    nbb = xb.shape[0]
    ts = SEQ_TILE
    nt = s // ts
    hb = ts // HALO
    n_tiles_a = nba * nt
    n_tiles = n_tiles_a + nbb * nt
    n_total = n_tiles * ts

    tile_a = lambda t: jnp.minimum(t, n_tiles_a - 1)
    tile_b = lambda t: jnp.maximum(t - n_tiles_a, 0)

    def x_specs(tile):
        cur = lambda t: (tile(t) // nt, tile(t) % nt, 0)
        prv = lambda t: (tile(t) // nt, jnp.maximum((tile(t) % nt) * hb - 1, 0), 0)
        nxt = lambda t: (tile(t) // nt, jnp.minimum((tile(t) % nt + 1) * hb, s // HALO - 1), 0)
        return [pl.BlockSpec((None, ts, d), cur), pl.BlockSpec((None, HALO, d), prv),
                pl.BlockSpec((None, HALO, d), nxt)]

    full = lambda a: pl.BlockSpec(a.shape, lambda t: (0,) * a.ndim)
    in_specs = x_specs(tile_a) + x_specs(tile_b) + [full(a) for a in p]
    out_shape = (jax.ShapeDtypeStruct((n_total, d), jnp.float32),
                 jax.ShapeDtypeStruct((n_total, D_PACK), jnp.uint32),
                 jax.ShapeDtypeStruct((n_total, LANES), jnp.float32),
                 jax.ShapeDtypeStruct((SUBLANES, n_total), jnp.float32),
                 jax.ShapeDtypeStruct((SUBLANES, LANES), jnp.float32))
    row_blk = lambda t: (t, 0)
    out_specs = (pl.BlockSpec((ts, d), row_blk),
                 pl.BlockSpec((ts, D_PACK), row_blk),
                 pl.BlockSpec((ts, LANES), row_blk),
                 pl.BlockSpec((SUBLANES, ts), lambda t: (0, t)),
                 pl.BlockSpec((SUBLANES, LANES), lambda t: (0, 0)))
    return pl.pallas_call(
        functools.partial(_mixer_kernel, ts=ts, nt=nt, n_tiles_a=n_tiles_a, alpha=alpha),
        grid=(n_tiles,),
        in_specs=in_specs,
        out_specs=out_specs,
        out_shape=out_shape,
        scratch_shapes=[pltpu.VMEM((ts + 2 * HALO, d), jnp.bfloat16),
                        pltpu.VMEM((ts, d), jnp.bfloat16),
                        pltpu.VMEM((SEQ_SUBTILE, SEQ_SUBTILE), jnp.bfloat16),
                        pltpu.VMEM((1, LANES), jnp.float32)],
        compiler_params=pltpu.CompilerParams(dimension_semantics=("arbitrary",),
                                             vmem_limit_bytes=VMEM_LIMIT_BYTES),
        name="mixer",
    )(xa, xa, xa, xb, xb, xb, *p)


def _sc_mesh():
    return plsc.VectorSubcoreMesh(core_axis_name="c", subcore_axis_name="s",
                                  num_cores=SC_CORES, num_subcores=SC_SUBCORES)


def _sc_worker_base(per_worker):
    return (lax.axis_index("s") * SC_CORES + lax.axis_index("c")) * per_worker


def _sc_dispatch(pos1, pos2, xp, n_rows):
    n, dp = xp.shape
    per_worker = n // SC_WORKERS
    steps = per_worker // SC_ROWS

    def body(pos1_hbm, pos2_hbm, xp_hbm, xs_hbm, idx1_v, idx2_v, rows_v, sem):
        base = _sc_worker_base(per_worker)

        @pl.loop(0, steps)
        def _(k):
            off = pl.multiple_of(base + k * SC_ROWS, SC_ROWS)
            pltpu.sync_copy(pos1_hbm.at[pl.ds(off, SC_ROWS)], idx1_v)
            pltpu.sync_copy(pos2_hbm.at[pl.ds(off, SC_ROWS)], idx2_v)
            pltpu.sync_copy(xp_hbm.at[pl.ds(off, SC_ROWS)], rows_v)
            c1 = pltpu.async_copy(rows_v, xs_hbm.at[idx1_v], sem)
            c2 = pltpu.async_copy(rows_v, xs_hbm.at[idx2_v], sem)
            c1.wait()
            c2.wait()

    return pl.kernel(
        body,
        out_type=jax.ShapeDtypeStruct((n_rows, dp), xp.dtype),
        mesh=_sc_mesh(),
        scratch_types=[pltpu.VMEM((SC_ROWS,), jnp.int32), pltpu.VMEM((SC_ROWS,), jnp.int32),
                       pltpu.VMEM((SC_ROWS, dp), xp.dtype), pltpu.SemaphoreType.DMA],
        compiler_params=pltpu.CompilerParams(use_tc_tiling_on_sc=True),
        name="sc_dispatch",
    )(pos1, pos2, xp)


def _sc_gather(pos1, pos2, ys):
    n = pos1.shape[0]
    dp = ys.shape[1]
    per_worker = n // SC_WORKERS
    steps = per_worker // SC_ROWS

    def body(pos1_hbm, pos2_hbm, ys_hbm, out_hbm, idx_v, rows_v, sem):
        base = _sc_worker_base(per_worker)

        @pl.loop(0, steps)
        def _(k):
            off = pl.multiple_of(base + k * SC_ROWS, SC_ROWS)
            for slot, pos_hbm in enumerate((pos1_hbm, pos2_hbm)):
                pltpu.sync_copy(pos_hbm.at[pl.ds(off, SC_ROWS)], idx_v)
                pltpu.async_copy(ys_hbm.at[idx_v], rows_v, sem).wait()
                pltpu.sync_copy(rows_v, out_hbm.at[slot, pl.ds(off, SC_ROWS)])

    return pl.kernel(
        body,
        out_type=jax.ShapeDtypeStruct((2, n, dp), ys.dtype),
        mesh=_sc_mesh(),
        scratch_types=[pltpu.VMEM((SC_ROWS,), jnp.int32), pltpu.VMEM((SC_ROWS, dp), ys.dtype),
                       pltpu.SemaphoreType.DMA],
        compiler_params=pltpu.CompilerParams(use_tc_tiling_on_sc=True),
        name="sc_gather",
    )(pos1, pos2, ys)


def _expert_kernel(te_ref, first_ref, nused_ref, xs_ref, wg_ref, wu_ref, wd_ref, ys_ref, wgu_bf_ref, wd_bf_ref):
    i = pl.program_id(0)
    bf16 = jnp.bfloat16

    @pl.when(first_ref[i] == 1)
    def _():
        wgu_bf_ref[:, 0:D_EXPERT] = wg_ref[...].astype(bf16)
        wgu_bf_ref[:, D_EXPERT:2 * D_EXPERT] = wu_ref[...].astype(bf16)
        wd_bf_ref[...] = wd_ref[...].astype(bf16)

    @pl.when(i < nused_ref[0])
    def _():
        for r0 in range(0, xs_ref.shape[0], EXPERT_SUBTILE):
            rows = pl.ds(r0, EXPERT_SUBTILE)
            lo, hi = _unpack_halves(xs_ref[rows, :])
            gu = (_dot(lo.astype(bf16), wgu_bf_ref[0:D_PACK, :])
                  + _dot(hi.astype(bf16), wgu_bf_ref[D_PACK:D_MODEL, :]))
            g = gu[:, 0:D_EXPERT]
            h = g * jax.nn.sigmoid(g) * gu[:, D_EXPERT:2 * D_EXPERT]
            ys_ref[rows, :] = _pack_halves(_dot(h.astype(bf16), wd_bf_ref[...]))

    @pl.when(i >= nused_ref[0])
    def _():
        ys_ref[...] = jnp.zeros_like(ys_ref)


def _experts(tile_expert, tile_first, n_used, xs, w_gate, w_up, w_down):
    n_rows, dp = xs.shape
    d = D_MODEL
    tm = EXPERT_TILE
    wmap = lambda i, te, fi, nu: (te[i], 0, 0)
    grid_spec = pltpu.PrefetchScalarGridSpec(
        num_scalar_prefetch=3,
        grid=(n_rows // tm,),
        in_specs=[pl.BlockSpec((tm, dp), lambda i, te, fi, nu: (i, 0)),
                  pl.BlockSpec((None, d, D_EXPERT), wmap),
                  pl.BlockSpec((None, d, D_EXPERT), wmap),
                  pl.BlockSpec((None, D_EXPERT, d), wmap)],
        out_specs=pl.BlockSpec((tm, dp), lambda i, te, fi, nu: (i, 0)),
        scratch_shapes=[pltpu.VMEM((d, 2 * D_EXPERT), jnp.bfloat16),
                        pltpu.VMEM((D_EXPERT, d), jnp.bfloat16)],
    )
    return pl.pallas_call(
        _expert_kernel,
        grid_spec=grid_spec,
        out_shape=jax.ShapeDtypeStruct((n_rows, dp), jnp.uint32),
        compiler_params=pltpu.CompilerParams(dimension_semantics=("arbitrary",),
                                             vmem_limit_bytes=VMEM_LIMIT_BYTES),
        name="experts",
    )(tile_expert, tile_first, n_used, xs, w_gate, w_up, w_down)


def _combine_kernel(x1_ref, slab_ref, y1_ref, y2_ref, g_ref, b_ref, out_a_ref, out_b_ref, *, n_tiles_a, alpha):
    slab = slab_ref[...]
    w1 = slab[:, COL_W1:COL_W1 + 1]
    w2 = slab[:, COL_W2:COL_W2 + 1]
    lo1, hi1 = _unpack_halves(y1_ref[...])
    lo2, hi2 = _unpack_halves(y2_ref[...])
    moe = jnp.concatenate([w1 * lo1 + w2 * lo2, w1 * hi1 + w2 * hi2], axis=1)
    out = _layer_norm(alpha * x1_ref[...] + moe, g_ref[...], b_ref[...])
    from_a = pl.program_id(0) < n_tiles_a

    @pl.when(from_a)
    def _():
        out_a_ref[...] = out

    @pl.when(jnp.logical_not(from_a))
    def _():
        out_b_ref[...] = out


def _combine(x1, slab, yg, g, b, n_a, alpha):
    n, d = x1.shape
    tt = TOKEN_TILE
    n_tiles_a = n_a // tt
    return pl.pallas_call(
        functools.partial(_combine_kernel, n_tiles_a=n_tiles_a, alpha=alpha),
        grid=(n // tt,),
        in_specs=[pl.BlockSpec((tt, d), lambda i: (i, 0)),
                  pl.BlockSpec((tt, LANES), lambda i: (i, 0)),
                  pl.BlockSpec((None, tt, D_PACK), lambda i: (0, i, 0)),
                  pl.BlockSpec((None, tt, D_PACK), lambda i: (1, i, 0)),
                  pl.BlockSpec((1, d), lambda i: (0, 0)),
                  pl.BlockSpec((1, d), lambda i: (0, 0))],
        out_specs=(pl.BlockSpec((tt, d), lambda i: (jnp.minimum(i, n_tiles_a - 1), 0)),
                   pl.BlockSpec((tt, d), lambda i: (jnp.maximum(i - n_tiles_a, 0), 0))),
        out_shape=(jax.ShapeDtypeStruct((n_a, d), jnp.float32),
                   jax.ShapeDtypeStruct((n - n_a, d), jnp.float32)),
        compiler_params=pltpu.CompilerParams(dimension_semantics=("arbitrary",),
                                             vmem_limit_bytes=VMEM_LIMIT_BYTES),
        name="combine",
    )(x1, slab, yg, yg, g, b)


def _encoder_layer(xa, xb, w_in, b_in, conv_w, conv_b, sg_ln_g, sg_ln_b, w_s, b_s, w_o, b_o,
                   ln1_g, ln1_b, w_rc, b_rc, w_rf, b_rf, w_gate, w_up, w_down, ln2_g, ln2_b, alpha):
    d = D_MODEL
    n_a = xa.shape[0] * xa.shape[1]
    n = n_a + xb.shape[0] * xb.shape[1]
    bf16 = jnp.bfloat16
    row = lambda a: a.reshape(1, -1)
    ws_pairs = w_s.reshape(N_SG_HEADS // 2, 2 * CHUNK, CHUNK).astype(bf16)
    bs_full = jnp.repeat(b_s.T, SG_HEAD_DIM, axis=1)
    w_r = jnp.zeros((d, LANES), jnp.float32).at[:, 0:N_EXPERTS].set(w_rf)
    w_r = w_r.at[:, COARSE_OFF:COARSE_OFF + N_EXPERT_GROUPS].set(w_rc).astype(bf16)
    b_r = jnp.zeros((1, LANES), jnp.float32).at[0, 0:N_EXPERTS].set(b_rf)
    b_r = b_r.at[0, COARSE_OFF:COARSE_OFF + N_EXPERT_GROUPS].set(b_rc)
    params = (w_in.astype(bf16), row(b_in), conv_w, row(conv_b), row(sg_ln_g), row(sg_ln_b),
              ws_pairs, bs_full, w_o.astype(bf16), row(b_o), row(ln1_g), row(ln1_b), w_r, b_r)

    x1, xp, slab, route, carry = _mixer(xa, xb, params, alpha)

    tm = EXPERT_TILE
    n_tiles = 2 * n // tm + N_EXPERTS
    cnt = carry[0, 0:N_EXPERTS].astype(jnp.int32)
    tiles_e = (cnt + tm - 1) // tm
    tile_end = jnp.cumsum(tiles_e)
    tile_start = tile_end - tiles_e
    row_off = tile_start * tm
    experts = jnp.arange(N_EXPERTS, dtype=jnp.int32)[:, None]

    def sorted_pos(e_row, r_row):
        e = route[e_row].astype(jnp.int32)
        return jnp.sum(jnp.where(e[None, :] == experts, row_off[:, None], 0), axis=0) + route[r_row].astype(jnp.int32)

    pos1 = sorted_pos(COL_E1, COL_R1)
    pos2 = sorted_pos(COL_E2, COL_R2)
    tile_ids = jnp.arange(n_tiles, dtype=jnp.int32)
    n_used = tile_end[N_EXPERTS - 1:N_EXPERTS]
    tile_expert = jnp.sum((tile_end[None, :] <= tile_ids[:, None]).astype(jnp.int32), axis=1)
    last_used = jnp.sum((tile_end <= n_used[0] - 1).astype(jnp.int32))
    tile_expert = jnp.minimum(tile_expert, last_used)
    tile_first = jnp.any(tile_ids[:, None] == tile_start[None, :], axis=1) & (tile_ids < n_used[0])
    tile_first = tile_first.astype(jnp.int32).at[0].set(1)

    xs = _sc_dispatch(pos1, pos2, xp, n_tiles * tm)
    ys = _experts(tile_expert, tile_first, n_used, xs, w_gate, w_up, w_down)
    yg = _sc_gather(pos1, pos2, ys)
    out_a, out_b = _combine(x1, slab, yg, row(ln2_g), row(ln2_b), n_a, alpha)
    return out_a.reshape(xa.shape), out_b.reshape(xb.shape)


def kernel(x_prompt, x_sample, w_in, b_in, conv_w, conv_b, sg_ln_g, sg_ln_b, w_s, b_s, w_o, b_o, ln1_g, ln1_b, w_rc, b_rc, w_rf, b_rf, w_gate, w_up, w_down, ln2_g, ln2_b):
    depth = w_in.shape[0]
    alpha = (2.0 * depth) ** 0.25
    xs = (x_prompt, x_sample)
    for l in range(depth):
        xs = _encoder_layer(*xs, w_in[l], b_in[l], conv_w[l], conv_b[l], sg_ln_g[l], sg_ln_b[l], w_s[l], b_s[l],
                            w_o[l], b_o[l], ln1_g[l], ln1_b[l], w_rc[l], b_rc[l], w_rf[l], b_rf[l],
                            w_gate[l], w_up[l], w_down[l], ln2_g[l], ln2_b[l], alpha)
    return (xs[0], xs[1])
```

```python
import functools

import jax
import jax.numpy as jnp
from jax import lax
from jax.experimental import pallas as pl
from jax.experimental.pallas import tpu as pltpu
from jax.experimental.pallas import tpu_sc as plsc

D_MODEL = 1024
D_CONV = 512
D_SG = 512
N_SG_HEADS = 8
SG_HEAD_DIM = D_SG // N_SG_HEADS
CHUNK = 128
N_EXPERT_GROUPS = 4
EXPERTS_PER_GROUP = 8
N_EXPERTS = N_EXPERT_GROUPS * EXPERTS_PER_GROUP
D_EXPERT = 512
LN_EPS = 1e-5
D_PACK = D_MODEL // 2

LANES = 128
SUBLANES = 8
HALO = 16
SEQ_TILE = 512
SEQ_SUBTILE = 512
EXPERT_TILE = 512
EXPERT_SUBTILE = 256
TOKEN_TILE = 256
SC_CORES = 2
SC_SUBCORES = 16
SC_WORKERS = SC_CORES * SC_SUBCORES
SC_ROWS = 64
VMEM_LIMIT_BYTES = 56 * 1024 * 1024

COL_E1, COL_E2, COL_W1, COL_W2, COL_R1, COL_R2 = 0, 1, 2, 3, 4, 5
COARSE_OFF = N_EXPERTS


def _dot(a, b):
    return jnp.dot(a, b, preferred_element_type=jnp.float32)


def _gelu_tanh(x):
    return 0.5 * x * (1.0 + jnp.tanh(0.7978845608028654 * (x + 0.044715 * (x * x * x))))


def _layer_norm(x, g, b):
    mu = jnp.mean(x, axis=-1, keepdims=True)
    xc = x - mu
    var = jnp.mean(xc * xc, axis=-1, keepdims=True)
    return xc * lax.rsqrt(var + LN_EPS) * g + b


def _pack_halves(x):
    u32 = jnp.uint32
    lo = lax.bitcast_convert_type(x[:, 0:D_PACK].astype(jnp.bfloat16).astype(jnp.float32), u32)
    hi = lax.bitcast_convert_type(x[:, D_PACK:D_MODEL].astype(jnp.bfloat16).astype(jnp.float32), u32)
    return (hi & u32(0xFFFF0000)) | (lo >> u32(16))


def _unpack_halves(w):
    u32 = jnp.uint32
    lo = lax.bitcast_convert_type(w << u32(16), jnp.float32)
    hi = lax.bitcast_convert_type(w & u32(0xFFFF0000), jnp.float32)
    return lo, hi


def _mixer_kernel(x_ref, x_prev_ref, x_next_ref, w_in_ref, b_in_ref, conv_w_ref, conv_b_ref,
                  sg_g_ref, sg_b_ref, ws_ref, bs_ref, w_o_ref, b_o_ref, ln1_g_ref, ln1_b_ref,
                  w_r_ref, b_r_ref,
                  x1_ref, xp_ref, slab_ref, route_ref, counts_ref,
                  xb_ref, ymix_ref, tri_ref, carry_ref, *, ts, nt, alpha):
    t = pl.program_id(0)
    j = t % nt
    bf16 = jnp.bfloat16
    sub = SEQ_SUBTILE

    @pl.when(t == 0)
    def _():
        carry_ref[...] = jnp.zeros_like(carry_ref)
        ri = lax.broadcasted_iota(jnp.int32, (sub, sub), 0)
        ci = lax.broadcasted_iota(jnp.int32, (sub, sub), 1)
        tri_ref[...] = jnp.where(ri > ci, 1.0, 0.0).astype(bf16)

    xb_ref[0:HALO, :] = x_prev_ref[...].astype(bf16)
    xb_ref[HALO:HALO + ts, :] = x_ref[...].astype(bf16)
    xb_ref[HALO + ts:HALO + ts + HALO, :] = x_next_ref[...].astype(bf16)

    def proj(lhs, lo, hi):
        return _dot(lhs, w_in_ref[:, lo:hi]) + b_in_ref[:, lo:hi]

    has_prev = jnp.where(j > 0, 1.0, 0.0)
    has_next = jnp.where(j < nt - 1, 1.0, 0.0)
    row_e = lax.broadcasted_iota(jnp.int32, (sub + 2 * HALO, 1), 0)
    lane_c = lax.broadcasted_iota(jnp.int32, (CHUNK, LANES), 1)
    first_head = lane_c < SG_HEAD_DIM
    lane = lax.broadcasted_iota(jnp.int32, (sub, LANES), 1)
    lane_f = lane.astype(jnp.float32)
    neg = jnp.float32(-jnp.inf)
    big = jnp.float32(1e9)
    is_c = (lane >= COARSE_OFF) & (lane < COARSE_OFF + N_EXPERT_GROUPS)
    carry = carry_ref[...]

    for r0 in range(0, ts, sub):
        rows = pl.ds(r0, sub)
        x = x_ref[rows, :]
        xe = xb_ref[r0:r0 + sub + 2 * HALO, :]
        xm = xb_ref[HALO + r0:HALO + r0 + sub, :]

        g_e = proj(xe, 0, D_CONV) * proj(xe, 2 * D_CONV, 3 * D_CONV)
        if r0 == 0:
            g_e = g_e * jnp.where(row_e < HALO, has_prev, 1.0)
        if r0 + sub == ts:
            g_e = g_e * jnp.where(row_e >= HALO + sub, has_next, 1.0)
        g_prev = g_e[HALO - 1:HALO - 1 + sub, :]
        g_mid = g_e[HALO:HALO + sub, :]
        g_next = g_e[HALO + 1:HALO + 1 + sub, :]
        conv = (g_prev * conv_w_ref[0:1, :] + g_mid * conv_w_ref[1:2, :] + g_next * conv_w_ref[2:3, :]
                + conv_b_ref[...])
        y_a = proj(xm, D_CONV, 2 * D_CONV) * conv
        ymix_ref[rows, 0:D_CONV] = y_a.astype(bf16)

        u = _gelu_tanh(proj(xm, 3 * D_CONV, 3 * D_CONV + D_SG))
        v = _gelu_tanh(proj(xm, 3 * D_CONV + D_SG, 3 * D_CONV + 2 * D_SG))
        v_ln = _layer_norm(v, sg_g_ref[...], sg_b_ref[...]).astype(bf16)
        for q0 in range(0, sub, 2 * CHUNK):
            q1 = q0 + CHUNK
            for hp in range(N_SG_HEADS // 2):
                c0 = hp * LANES
                rhs = jnp.concatenate([v_ln[q0:q0 + CHUNK, c0:c0 + LANES], v_ln[q1:q1 + CHUNK, c0:c0 + LANES]], axis=1)
                res = _dot(ws_ref[hp], rhs)
                bias = bs_ref[:, c0:c0 + LANES]
                m0 = jnp.where(first_head, res[0:CHUNK, 0:LANES], res[CHUNK:2 * CHUNK, 0:LANES]) + bias
                m1 = jnp.where(first_head, res[0:CHUNK, LANES:2 * LANES], res[CHUNK:2 * CHUNK, LANES:2 * LANES]) + bias
                ymix_ref[r0 + q0:r0 + q0 + CHUNK, D_CONV + c0:D_CONV + c0 + LANES] = (
                    u[q0:q0 + CHUNK, c0:c0 + LANES] * m0).astype(bf16)
                ymix_ref[r0 + q1:r0 + q1 + CHUNK, D_CONV + c0:D_CONV + c0 + LANES] = (
                    u[q1:q1 + CHUNK, c0:c0 + LANES] * m1).astype(bf16)

        mix = _dot(ymix_ref[rows, :], w_o_ref[...]) + b_o_ref[...]
        x1 = _layer_norm(alpha * x + mix, ln1_g_ref[...], ln1_b_ref[...])
        x1_ref[rows, :] = x1
        xp_ref[rows, :] = _pack_halves(x1)

        logits = _dot(x1.astype(bf16), w_r_ref[...]) + b_r_ref[...]
        lc = jnp.where(is_c, logits, neg)
        mx = jnp.max(lc, axis=1, keepdims=True)
        grp = jnp.min(jnp.where(lc == mx, lane_f - COARSE_OFF, big), axis=1, keepdims=True)
        p_grp = 1.0 / jnp.sum(jnp.where(is_c, jnp.exp(logits - mx), 0.0), axis=1, keepdims=True)
        grp_lo = grp * EXPERTS_PER_GROUP
        in_grp = (lane_f >= grp_lo) & (lane_f < grp_lo + EXPERTS_PER_GROUP)
        lf = jnp.where(in_grp, logits, neg)
        v1 = jnp.max(lf, axis=1, keepdims=True)
        e1 = jnp.min(jnp.where(lf == v1, lane_f, big), axis=1, keepdims=True)
        lf2 = jnp.where(lane_f == e1, neg, lf)
        v2 = jnp.max(lf2, axis=1, keepdims=True)
        e2 = jnp.min(jnp.where(lf2 == v2, lane_f, big), axis=1, keepdims=True)
        a = jnp.exp(v2 - v1)
        w1 = p_grp / (1.0 + a)
        w2 = p_grp * a / (1.0 + a)

        hit1 = lane_f == e1
        hit2 = lane_f == e2
        onehot = jnp.where(hit1 | hit2, 1.0, 0.0)
        before = _dot(tri_ref[...], onehot.astype(bf16)) + carry
        r1 = jnp.sum(jnp.where(hit1, before, 0.0), axis=1, keepdims=True)
        r2 = jnp.sum(jnp.where(hit2, before, 0.0), axis=1, keepdims=True)
        carry = carry + jnp.sum(onehot, axis=0, keepdims=True)

        slab = jnp.where(lane == COL_E1, e1, 0.0)
        slab = jnp.where(lane == COL_E2, e2, slab)
        slab = jnp.where(lane == COL_W1, w1, slab)
        slab = jnp.where(lane == COL_W2, w2, slab)
        slab = jnp.where(lane == COL_R1, r1, slab)
        slab = jnp.where(lane == COL_R2, r2, slab)
        slab_ref[rows, :] = slab
        route_ref[:, rows] = slab.T[0:SUBLANES, :]

    carry_ref[...] = carry
    counts_ref[...] = jnp.broadcast_to(carry, counts_ref.shape)


def _mixer(x, p, alpha):
    nb, s, d = x.shape
    ts = SEQ_TILE
    nt = s // ts
    hb = ts // HALO
    n_tiles = nb * nt
    n_total = n_tiles * ts

    cur = lambda t: (t // nt, t % nt, 0)
    prv = lambda t: (t // nt, jnp.maximum((t % nt) * hb - 1, 0), 0)
    nxt = lambda t: (t // nt, jnp.minimum((t % nt + 1) * hb, s // HALO - 1), 0)
    full = lambda a: pl.BlockSpec(a.shape, lambda t: (0,) * a.ndim)
    in_specs = [pl.BlockSpec((None, ts, d), cur), pl.BlockSpec((None, HALO, d), prv),
                pl.BlockSpec((None, HALO, d), nxt)] + [full(a) for a in p]
    out_shape = (jax.ShapeDtypeStruct((n_total, d), jnp.float32),
                 jax.ShapeDtypeStruct((n_total, D_PACK), jnp.uint32),
                 jax.ShapeDtypeStruct((n_total, LANES), jnp.float32),
                 jax.ShapeDtypeStruct((SUBLANES, n_total), jnp.float32),
                 jax.ShapeDtypeStruct((SUBLANES, LANES), jnp.float32))
    row_blk = lambda t: (t, 0)
    out_specs = (pl.BlockSpec((ts, d), row_blk),
                 pl.BlockSpec((ts, D_PACK), row_blk),
                 pl.BlockSpec((ts, LANES), row_blk),
                 pl.BlockSpec((SUBLANES, ts), lambda t: (0, t)),
                 pl.BlockSpec((SUBLANES, LANES), lambda t: (0, 0)))
    return pl.pallas_call(
        functools.partial(_mixer_kernel, ts=ts, nt=nt, alpha=alpha),
        grid=(n_tiles,),
        in_specs=in_specs,
        out_specs=out_specs,
        out_shape=out_shape,
        scratch_shapes=[pltpu.VMEM((ts + 2 * HALO, d), jnp.bfloat16),
                        pltpu.VMEM((ts, d), jnp.bfloat16),
                        pltpu.VMEM((SEQ_SUBTILE, SEQ_SUBTILE), jnp.bfloat16),
                        pltpu.VMEM((1, LANES), jnp.float32)],
        compiler_params=pltpu.CompilerParams(dimension_semantics=("arbitrary",),
                                             vmem_limit_bytes=VMEM_LIMIT_BYTES),
        name="mixer",
    )(x, x, x, *p)


def _sc_mesh():
    return plsc.VectorSubcoreMesh(core_axis_name="c", subcore_axis_name="s",
                                  num_cores=SC_CORES, num_subcores=SC_SUBCORES)


def _sc_worker_base(per_worker):
    return (lax.axis_index("s") * SC_CORES + lax.axis_index("c")) * per_worker


def _sc_dispatch(pos1, pos2, xp, n_rows):
    n, dp = xp.shape
    per_worker = n // SC_WORKERS
    steps = per_worker // SC_ROWS

    def body(pos1_hbm, pos2_hbm, xp_hbm, xs_hbm, idx1_v, idx2_v, rows_v, sem):
        base = _sc_worker_base(per_worker)

        @pl.loop(0, steps)
        def _(k):
            off = pl.multiple_of(base + k * SC_ROWS, SC_ROWS)
            pltpu.sync_copy(pos1_hbm.at[pl.ds(off, SC_ROWS)], idx1_v)
            pltpu.sync_copy(pos2_hbm.at[pl.ds(off, SC_ROWS)], idx2_v)
            pltpu.sync_copy(xp_hbm.at[pl.ds(off, SC_ROWS)], rows_v)
            c1 = pltpu.async_copy(rows_v, xs_hbm.at[idx1_v], sem)
            c2 = pltpu.async_copy(rows_v, xs_hbm.at[idx2_v], sem)
            c1.wait()
            c2.wait()

    return pl.kernel(
        body,
        out_type=jax.ShapeDtypeStruct((n_rows, dp), xp.dtype),
        mesh=_sc_mesh(),
        scratch_types=[pltpu.VMEM((SC_ROWS,), jnp.int32), pltpu.VMEM((SC_ROWS,), jnp.int32),
                       pltpu.VMEM((SC_ROWS, dp), xp.dtype), pltpu.SemaphoreType.DMA],
        compiler_params=pltpu.CompilerParams(use_tc_tiling_on_sc=True),
        name="sc_dispatch",
    )(pos1, pos2, xp)


def _sc_gather(pos1, pos2, ys):
    n = pos1.shape[0]
    dp = ys.shape[1]
    per_worker = n // SC_WORKERS
    steps = per_worker // SC_ROWS

    def body(pos1_hbm, pos2_hbm, ys_hbm, out_hbm, idx_v, rows_v, sem):
        base = _sc_worker_base(per_worker)

        @pl.loop(0, steps)
        def _(k):
            off = pl.multiple_of(base + k * SC_ROWS, SC_ROWS)
            for slot, pos_hbm in enumerate((pos1_hbm, pos2_hbm)):
                pltpu.sync_copy(pos_hbm.at[pl.ds(off, SC_ROWS)], idx_v)
                pltpu.async_copy(ys_hbm.at[idx_v], rows_v, sem).wait()
                pltpu.sync_copy(rows_v, out_hbm.at[slot, pl.ds(off, SC_ROWS)])

    return pl.kernel(
        body,
        out_type=jax.ShapeDtypeStruct((2, n, dp), ys.dtype),
        mesh=_sc_mesh(),
        scratch_types=[pltpu.VMEM((SC_ROWS,), jnp.int32), pltpu.VMEM((SC_ROWS, dp), ys.dtype),
                       pltpu.SemaphoreType.DMA],
        compiler_params=pltpu.CompilerParams(use_tc_tiling_on_sc=True),
        name="sc_gather",
    )(pos1, pos2, ys)


def _expert_kernel(te_ref, first_ref, nused_ref, xs_ref, wg_ref, wu_ref, wd_ref, ys_ref, wgu_bf_ref, wd_bf_ref):
    i = pl.program_id(0)
    bf16 = jnp.bfloat16

    @pl.when(first_ref[i] == 1)
    def _():
        wgu_bf_ref[:, 0:D_EXPERT] = wg_ref[...].astype(bf16)
        wgu_bf_ref[:, D_EXPERT:2 * D_EXPERT] = wu_ref[...].astype(bf16)
        wd_bf_ref[...] = wd_ref[...].astype(bf16)

    @pl.when(i < nused_ref[0])
    def _():
        for r0 in range(0, xs_ref.shape[0], EXPERT_SUBTILE):
            rows = pl.ds(r0, EXPERT_SUBTILE)
            lo, hi = _unpack_halves(xs_ref[rows, :])
            gu = (_dot(lo.astype(bf16), wgu_bf_ref[0:D_PACK, :])
                  + _dot(hi.astype(bf16), wgu_bf_ref[D_PACK:D_MODEL, :]))
            g = gu[:, 0:D_EXPERT]
            h = g * jax.nn.sigmoid(g) * gu[:, D_EXPERT:2 * D_EXPERT]
            ys_ref[rows, :] = _pack_halves(_dot(h.astype(bf16), wd_bf_ref[...]))

    @pl.when(i >= nused_ref[0])
    def _():
        ys_ref[...] = jnp.zeros_like(ys_ref)


def _experts(tile_expert, tile_first, n_used, xs, w_gate, w_up, w_down):
    n_rows, dp = xs.shape
    d = D_MODEL
    tm = EXPERT_TILE
    wmap = lambda i, te, fi, nu: (te[i], 0, 0)
    grid_spec = pltpu.PrefetchScalarGridSpec(
        num_scalar_prefetch=3,
        grid=(n_rows // tm,),
        in_specs=[pl.BlockSpec((tm, dp), lambda i, te, fi, nu: (i, 0)),
                  pl.BlockSpec((None, d, D_EXPERT), wmap),
                  pl.BlockSpec((None, d, D_EXPERT), wmap),
                  pl.BlockSpec((None, D_EXPERT, d), wmap)],
        out_specs=pl.BlockSpec((tm, dp), lambda i, te, fi, nu: (i, 0)),
        scratch_shapes=[pltpu.VMEM((d, 2 * D_EXPERT), jnp.bfloat16),
                        pltpu.VMEM((D_EXPERT, d), jnp.bfloat16)],
    )
    return pl.pallas_call(
        _expert_kernel,
        grid_spec=grid_spec,
        out_shape=jax.ShapeDtypeStruct((n_rows, dp), jnp.uint32),
        compiler_params=pltpu.CompilerParams(dimension_semantics=("arbitrary",),
                                             vmem_limit_bytes=VMEM_LIMIT_BYTES),
        name="experts",
    )(tile_expert, tile_first, n_used, xs, w_gate, w_up, w_down)


def _combine_kernel(x1_ref, slab_ref, y1_ref, y2_ref, g_ref, b_ref, out_ref, *, alpha):
    slab = slab_ref[...]
    w1 = slab[:, COL_W1:COL_W1 + 1]
    w2 = slab[:, COL_W2:COL_W2 + 1]
    lo1, hi1 = _unpack_halves(y1_ref[...])
    lo2, hi2 = _unpack_halves(y2_ref[...])
    moe = jnp.concatenate([w1 * lo1 + w2 * lo2, w1 * hi1 + w2 * hi2], axis=1)
    out_ref[...] = _layer_norm(alpha * x1_ref[...] + moe, g_ref[...], b_ref[...])


def _combine(x1, slab, yg, g, b, alpha):
    n, d = x1.shape
    tt = TOKEN_TILE
    return pl.pallas_call(
        functools.partial(_combine_kernel, alpha=alpha),
        grid=(n // tt,),
        in_specs=[pl.BlockSpec((tt, d), lambda i: (i, 0)),
                  pl.BlockSpec((tt, LANES), lambda i: (i, 0)),
                  pl.BlockSpec((None, tt, D_PACK), lambda i: (0, i, 0)),
                  pl.BlockSpec((None, tt, D_PACK), lambda i: (1, i, 0)),
                  pl.BlockSpec((1, d), lambda i: (0, 0)),
                  pl.BlockSpec((1, d), lambda i: (0, 0))],
        out_specs=pl.BlockSpec((tt, d), lambda i: (i, 0)),
        out_shape=jax.ShapeDtypeStruct((n, d), jnp.float32),
        compiler_params=pltpu.CompilerParams(dimension_semantics=("arbitrary",),
                                             vmem_limit_bytes=VMEM_LIMIT_BYTES),
        name="combine",
    )(x1, slab, yg, yg, g, b)


def _row(a):
    return a.reshape(1, -1)


def _mixer_params(w_in, b_in, conv_w, conv_b, sg_ln_g, sg_ln_b, w_s, b_s, w_o, b_o, ln1_g, ln1_b,
                  w_rc, b_rc, w_rf, b_rf):
    d = D_MODEL
    bf16 = jnp.bfloat16
    row = _row
    ws_pairs = w_s.reshape(N_SG_HEADS // 2, 2 * CHUNK, CHUNK).astype(bf16)
    bs_full = jnp.repeat(b_s.T, SG_HEAD_DIM, axis=1)
    w_r = jnp.zeros((d, LANES), jnp.float32).at[:, 0:N_EXPERTS].set(w_rf)
    w_r = w_r.at[:, COARSE_OFF:COARSE_OFF + N_EXPERT_GROUPS].set(w_rc).astype(bf16)
    b_r = jnp.zeros((1, LANES), jnp.float32).at[0, 0:N_EXPERTS].set(b_rf)
    b_r = b_r.at[0, COARSE_OFF:COARSE_OFF + N_EXPERT_GROUPS].set(b_rc)
    return (w_in.astype(bf16), row(b_in), conv_w, row(conv_b), row(sg_ln_g), row(sg_ln_b),
            ws_pairs, bs_full, w_o.astype(bf16), row(b_o), row(ln1_g), row(ln1_b), w_r, b_r)


def _encoder_layer(x, mixer_params, w_gate, w_up, w_down, ln2_g, ln2_b, alpha):
    n = x.shape[0] * x.shape[1]
    x1, xp, slab, route, carry = _mixer(x, mixer_params, alpha)

    tm = EXPERT_TILE
    n_tiles = 2 * n // tm + N_EXPERTS
    cnt = carry[0, 0:N_EXPERTS].astype(jnp.int32)
    tiles_e = (cnt + tm - 1) // tm
    tile_end = jnp.cumsum(tiles_e)
    tile_start = tile_end - tiles_e
    row_off = tile_start * tm
    experts = jnp.arange(N_EXPERTS, dtype=jnp.int32)[:, None]

    def sorted_pos(e_row, r_row):
        e = route[e_row].astype(jnp.int32)
        return jnp.sum(jnp.where(e[None, :] == experts, row_off[:, None], 0), axis=0) + route[r_row].astype(jnp.int32)

    pos1 = sorted_pos(COL_E1, COL_R1)
    pos2 = sorted_pos(COL_E2, COL_R2)
    tile_ids = jnp.arange(n_tiles, dtype=jnp.int32)
    n_used = tile_end[N_EXPERTS - 1:N_EXPERTS]
    tile_expert = jnp.sum((tile_end[None, :] <= tile_ids[:, None]).astype(jnp.int32), axis=1)
    last_used = jnp.sum((tile_end <= n_used[0] - 1).astype(jnp.int32))
    tile_expert = jnp.minimum(tile_expert, last_used)
    tile_first = jnp.any(tile_ids[:, None] == tile_start[None, :], axis=1) & (tile_ids < n_used[0])
    tile_first = tile_first.astype(jnp.int32).at[0].set(1)

    xs = _sc_dispatch(pos1, pos2, xp, n_tiles * tm)
    ys = _experts(tile_expert, tile_first, n_used, xs, w_gate, w_up, w_down)
    yg = _sc_gather(pos1, pos2, ys)
    return _combine(x1, slab, yg, _row(ln2_g), _row(ln2_b), alpha).reshape(x.shape)


def kernel(x_prompt, x_sample, w_in, b_in, conv_w, conv_b, sg_ln_g, sg_ln_b, w_s, b_s, w_o, b_o, ln1_g, ln1_b, w_rc, b_rc, w_rf, b_rf, w_gate, w_up, w_down, ln2_g, ln2_b):
    depth = w_in.shape[0]
    alpha = (2.0 * depth) ** 0.25
    xs = (x_prompt, x_sample)
    for l in range(depth):
        mixer_params = _mixer_params(w_in[l], b_in[l], conv_w[l], conv_b[l], sg_ln_g[l], sg_ln_b[l], w_s[l], b_s[l],
                                     w_o[l], b_o[l], ln1_g[l], ln1_b[l], w_rc[l], b_rc[l], w_rf[l], b_rf[l])
        xs = tuple(_encoder_layer(x, mixer_params, w_gate[l], w_up[l], w_down[l], ln2_g[l], ln2_b[l], alpha)
                   for x in xs)
    return xs
```

```python
import functools

import jax
import jax.numpy as jnp
from jax import lax
from jax.experimental import pallas as pl
from jax.experimental.pallas import tpu as pltpu
from jax.experimental.pallas import tpu_sc as plsc

D_MODEL = 1024
D_CONV = 512
D_SG = 512
N_SG_HEADS = 8
SG_HEAD_DIM = D_SG // N_SG_HEADS
CHUNK = 128
N_EXPERT_GROUPS = 4
EXPERTS_PER_GROUP = 8
N_EXPERTS = N_EXPERT_GROUPS * EXPERTS_PER_GROUP
D_EXPERT = 512
LN_EPS = 1e-5
D_PACK = D_MODEL // 2

LANES = 128
SUBLANES = 8
HALO = 16
SEQ_TILE = 512
SEQ_SUBTILE = 512
EXPERT_TILE = 1024
EXPERT_SUBTILE = 256
TOKEN_TILE = 1024
SC_CORES = 2
SC_SUBCORES = 16
SC_WORKERS = SC_CORES * SC_SUBCORES
SC_ROWS = 64
VMEM_LIMIT_BYTES = 56 * 1024 * 1024

COL_E1, COL_E2, COL_W1, COL_W2, COL_R1, COL_R2 = 0, 1, 2, 3, 4, 5
COARSE_OFF = N_EXPERTS


def _dot(a, b):
    return jnp.dot(a, b, preferred_element_type=jnp.float32)


def _gelu_tanh(x):
    return 0.5 * x * (1.0 + jnp.tanh(0.7978845608028654 * (x + 0.044715 * (x * x * x))))


def _layer_norm(x, g, b):
    mu = jnp.mean(x, axis=-1, keepdims=True)
    xc = x - mu
    var = jnp.mean(xc * xc, axis=-1, keepdims=True)
    return xc * lax.rsqrt(var + LN_EPS) * g + b


def _pack_halves(x):
    u32 = jnp.uint32
    lo = lax.bitcast_convert_type(x[:, 0:D_PACK].astype(jnp.bfloat16).astype(jnp.float32), u32)
    hi = lax.bitcast_convert_type(x[:, D_PACK:D_MODEL].astype(jnp.bfloat16).astype(jnp.float32), u32)
    return (hi & u32(0xFFFF0000)) | (lo >> u32(16))


def _unpack_halves(w):
    u32 = jnp.uint32
    lo = lax.bitcast_convert_type(w << u32(16), jnp.float32)
    hi = lax.bitcast_convert_type(w & u32(0xFFFF0000), jnp.float32)
    return lo, hi


def _mixer_kernel(x_ref, x_prev_ref, x_next_ref, w_in_ref, b_in_ref, conv_w_ref, conv_b_ref,
                  sg_g_ref, sg_b_ref, ws_ref, bs_ref, w_o_ref, b_o_ref, ln1_g_ref, ln1_b_ref,
                  w_r_ref, b_r_ref,
                  x1_ref, xp_ref, slab_ref, route_ref, counts_ref,
                  xb_ref, ymix_ref, tri_ref, carry_ref, *, ts, nt, alpha):
    t = pl.program_id(0)
    j = t % nt
    bf16 = jnp.bfloat16
    sub = SEQ_SUBTILE

    @pl.when(t == 0)
    def _():
        carry_ref[...] = jnp.zeros_like(carry_ref)
        ri = lax.broadcasted_iota(jnp.int32, (sub, sub), 0)
        ci = lax.broadcasted_iota(jnp.int32, (sub, sub), 1)
        tri_ref[...] = jnp.where(ri > ci, 1.0, 0.0).astype(bf16)

    xb_ref[0:HALO, :] = x_prev_ref[...].astype(bf16)
    xb_ref[HALO:HALO + ts, :] = x_ref[...].astype(bf16)
    xb_ref[HALO + ts:HALO + ts + HALO, :] = x_next_ref[...].astype(bf16)

    def proj(lhs, lo, hi):
        return _dot(lhs, w_in_ref[:, lo:hi]) + b_in_ref[:, lo:hi]

    has_prev = jnp.where(j > 0, 1.0, 0.0)
    has_next = jnp.where(j < nt - 1, 1.0, 0.0)
    row_e = lax.broadcasted_iota(jnp.int32, (sub + 2 * HALO, 1), 0)
    lane_c = lax.broadcasted_iota(jnp.int32, (CHUNK, LANES), 1)
    first_head = lane_c < SG_HEAD_DIM
    lane = lax.broadcasted_iota(jnp.int32, (sub, LANES), 1)
    lane_f = lane.astype(jnp.float32)
    neg = jnp.float32(-jnp.inf)
    big = jnp.float32(1e9)
    is_c = (lane >= COARSE_OFF) & (lane < COARSE_OFF + N_EXPERT_GROUPS)
    carry = carry_ref[...]

    for r0 in range(0, ts, sub):
        rows = pl.ds(r0, sub)
        x = x_ref[rows, :]
        xe = xb_ref[r0:r0 + sub + 2 * HALO, :]
        xm = xb_ref[HALO + r0:HALO + r0 + sub, :]

        g_e = proj(xe, 0, D_CONV) * proj(xe, 2 * D_CONV, 3 * D_CONV)
        if r0 == 0:
            g_e = g_e * jnp.where(row_e < HALO, has_prev, 1.0)
        if r0 + sub == ts:
            g_e = g_e * jnp.where(row_e >= HALO + sub, has_next, 1.0)
        g_prev = g_e[HALO - 1:HALO - 1 + sub, :]
        g_mid = g_e[HALO:HALO + sub, :]
        g_next = g_e[HALO + 1:HALO + 1 + sub, :]
        conv = (g_prev * conv_w_ref[0:1, :] + g_mid * conv_w_ref[1:2, :] + g_next * conv_w_ref[2:3, :]
                + conv_b_ref[...])
        y_a = proj(xm, D_CONV, 2 * D_CONV) * conv
        ymix_ref[rows, 0:D_CONV] = y_a.astype(bf16)

        u = _gelu_tanh(proj(xm, 3 * D_CONV, 3 * D_CONV + D_SG))
        v = _gelu_tanh(proj(xm, 3 * D_CONV + D_SG, 3 * D_CONV + 2 * D_SG))
        v_ln = _layer_norm(v, sg_g_ref[...], sg_b_ref[...]).astype(bf16)
        for q0 in range(0, sub, 2 * CHUNK):
            q1 = q0 + CHUNK
            for hp in range(N_SG_HEADS // 2):
                c0 = hp * LANES
                rhs = jnp.concatenate([v_ln[q0:q0 + CHUNK, c0:c0 + LANES], v_ln[q1:q1 + CHUNK, c0:c0 + LANES]], axis=1)
                res = _dot(ws_ref[hp], rhs)
                bias = bs_ref[:, c0:c0 + LANES]
                m0 = jnp.where(first_head, res[0:CHUNK, 0:LANES], res[CHUNK:2 * CHUNK, 0:LANES]) + bias
                m1 = jnp.where(first_head, res[0:CHUNK, LANES:2 * LANES], res[CHUNK:2 * CHUNK, LANES:2 * LANES]) + bias
                ymix_ref[r0 + q0:r0 + q0 + CHUNK, D_CONV + c0:D_CONV + c0 + LANES] = (
                    u[q0:q0 + CHUNK, c0:c0 + LANES] * m0).astype(bf16)
                ymix_ref[r0 + q1:r0 + q1 + CHUNK, D_CONV + c0:D_CONV + c0 + LANES] = (
                    u[q1:q1 + CHUNK, c0:c0 + LANES] * m1).astype(bf16)

        mix = _dot(ymix_ref[rows, :], w_o_ref[...]) + b_o_ref[...]
        x1 = _layer_norm(alpha * x + mix, ln1_g_ref[...], ln1_b_ref[...])
        x1_ref[rows, :] = x1
        xp_ref[rows, :] = _pack_halves(x1)

        logits = _dot(x1.astype(bf16), w_r_ref[...]) + b_r_ref[...]
        lc = jnp.where(is_c, logits, neg)
        mx = jnp.max(lc, axis=1, keepdims=True)
        grp = jnp.min(jnp.where(lc == mx, lane_f - COARSE_OFF, big), axis=1, keepdims=True)
        p_grp = 1.0 / jnp.sum(jnp.where(is_c, jnp.exp(logits - mx), 0.0), axis=1, keepdims=True)
        grp_lo = grp * EXPERTS_PER_GROUP
        in_grp = (lane_f >= grp_lo) & (lane_f < grp_lo + EXPERTS_PER_GROUP)
        lf = jnp.where(in_grp, logits, neg)
        v1 = jnp.max(lf, axis=1, keepdims=True)
        e1 = jnp.min(jnp.where(lf == v1, lane_f, big), axis=1, keepdims=True)
        lf2 = jnp.where(lane_f == e1, neg, lf)
        v2 = jnp.max(lf2, axis=1, keepdims=True)
        e2 = jnp.min(jnp.where(lf2 == v2, lane_f, big), axis=1, keepdims=True)
        a = jnp.exp(v2 - v1)
        w1 = p_grp / (1.0 + a)
        w2 = p_grp * a / (1.0 + a)

        hit1 = lane_f == e1
        hit2 = lane_f == e2
        onehot = jnp.where(hit1 | hit2, 1.0, 0.0)
        before = _dot(tri_ref[...], onehot.astype(bf16)) + carry
        r1 = jnp.sum(jnp.where(hit1, before, 0.0), axis=1, keepdims=True)
        r2 = jnp.sum(jnp.where(hit2, before, 0.0), axis=1, keepdims=True)
        carry = carry + jnp.sum(onehot, axis=0, keepdims=True)

        slab = jnp.where(lane == COL_E1, e1, 0.0)
        slab = jnp.where(lane == COL_E2, e2, slab)
        slab = jnp.where(lane == COL_W1, w1, slab)
        slab = jnp.where(lane == COL_W2, w2, slab)
        slab = jnp.where(lane == COL_R1, r1, slab)
        slab = jnp.where(lane == COL_R2, r2, slab)
        slab_ref[rows, :] = slab
        route_ref[:, rows] = slab.T[0:SUBLANES, :]

    carry_ref[...] = carry
    counts_ref[...] = jnp.broadcast_to(carry, counts_ref.shape)


def _mixer(x, p, alpha):
    nb, s, d = x.shape
    ts = SEQ_TILE
    nt = s // ts
    hb = ts // HALO
    n_tiles = nb * nt
    n_total = n_tiles * ts

    cur = lambda t: (t // nt, t % nt, 0)
    prv = lambda t: (t // nt, jnp.maximum((t % nt) * hb - 1, 0), 0)
    nxt = lambda t: (t // nt, jnp.minimum((t % nt + 1) * hb, s // HALO - 1), 0)
    full = lambda a: pl.BlockSpec(a.shape, lambda t: (0,) * a.ndim)
    in_specs = [pl.BlockSpec((None, ts, d), cur), pl.BlockSpec((None, HALO, d), prv),
                pl.BlockSpec((None, HALO, d), nxt)] + [full(a) for a in p]
    out_shape = (jax.ShapeDtypeStruct((n_total, d), jnp.float32),
                 jax.ShapeDtypeStruct((n_total, D_PACK), jnp.uint32),
                 jax.ShapeDtypeStruct((n_total, LANES), jnp.float32),
                 jax.ShapeDtypeStruct((SUBLANES, n_total), jnp.float32),
                 jax.ShapeDtypeStruct((SUBLANES, LANES), jnp.float32))
    row_blk = lambda t: (t, 0)
    out_specs = (pl.BlockSpec((ts, d), row_blk),
                 pl.BlockSpec((ts, D_PACK), row_blk),
                 pl.BlockSpec((ts, LANES), row_blk),
                 pl.BlockSpec((SUBLANES, ts), lambda t: (0, t)),
                 pl.BlockSpec((SUBLANES, LANES), lambda t: (0, 0)))
    return pl.pallas_call(
        functools.partial(_mixer_kernel, ts=ts, nt=nt, alpha=alpha),
        grid=(n_tiles,),
        in_specs=in_specs,
        out_specs=out_specs,
        out_shape=out_shape,
        scratch_shapes=[pltpu.VMEM((ts + 2 * HALO, d), jnp.bfloat16),
                        pltpu.VMEM((ts, d), jnp.bfloat16),
                        pltpu.VMEM((SEQ_SUBTILE, SEQ_SUBTILE), jnp.bfloat16),
                        pltpu.VMEM((1, LANES), jnp.float32)],
        compiler_params=pltpu.CompilerParams(dimension_semantics=("arbitrary",),
                                             vmem_limit_bytes=VMEM_LIMIT_BYTES),
        name="mixer",
    )(x, x, x, *p)


def _sc_mesh():
    return plsc.VectorSubcoreMesh(core_axis_name="c", subcore_axis_name="s",
                                  num_cores=SC_CORES, num_subcores=SC_SUBCORES)


def _sc_worker_base(per_worker):
    return (lax.axis_index("s") * SC_CORES + lax.axis_index("c")) * per_worker


def _sc_dispatch(pos1, pos2, xp, n_rows):
    n, dp = xp.shape
    per_worker = n // SC_WORKERS
    steps = per_worker // SC_ROWS

    def body(pos1_hbm, pos2_hbm, xp_hbm, xs_hbm, idx1_v, idx2_v, rows_v, sem):
        base = _sc_worker_base(per_worker)

        @pl.loop(0, steps)
        def _(k):
            off = pl.multiple_of(base + k * SC_ROWS, SC_ROWS)
            pltpu.sync_copy(pos1_hbm.at[pl.ds(off, SC_ROWS)], idx1_v)
            pltpu.sync_copy(pos2_hbm.at[pl.ds(off, SC_ROWS)], idx2_v)
            pltpu.sync_copy(xp_hbm.at[pl.ds(off, SC_ROWS)], rows_v)
            c1 = pltpu.async_copy(rows_v, xs_hbm.at[idx1_v], sem)
            c2 = pltpu.async_copy(rows_v, xs_hbm.at[idx2_v], sem)
            c1.wait()
            c2.wait()

    return pl.kernel(
        body,
        out_type=jax.ShapeDtypeStruct((n_rows, dp), xp.dtype),
        mesh=_sc_mesh(),
        scratch_types=[pltpu.VMEM((SC_ROWS,), jnp.int32), pltpu.VMEM((SC_ROWS,), jnp.int32),
                       pltpu.VMEM((SC_ROWS, dp), xp.dtype), pltpu.SemaphoreType.DMA],
        compiler_params=pltpu.CompilerParams(use_tc_tiling_on_sc=True),
        name="sc_dispatch",
    )(pos1, pos2, xp)


def _sc_gather(pos1, pos2, ys):
    n = pos1.shape[0]
    dp = ys.shape[1]
    per_worker = n // SC_WORKERS
    steps = per_worker // SC_ROWS

    def body(pos1_hbm, pos2_hbm, ys_hbm, out_hbm, idx_v, rows_v, sem):
        base = _sc_worker_base(per_worker)

        @pl.loop(0, steps)
        def _(k):
            off = pl.multiple_of(base + k * SC_ROWS, SC_ROWS)
            for slot, pos_hbm in enumerate((pos1_hbm, pos2_hbm)):
                pltpu.sync_copy(pos_hbm.at[pl.ds(off, SC_ROWS)], idx_v)
                pltpu.async_copy(ys_hbm.at[idx_v], rows_v, sem).wait()
                pltpu.sync_copy(rows_v, out_hbm.at[slot, pl.ds(off, SC_ROWS)])

    return pl.kernel(
        body,
        out_type=jax.ShapeDtypeStruct((2, n, dp), ys.dtype),
        mesh=_sc_mesh(),
        scratch_types=[pltpu.VMEM((SC_ROWS,), jnp.int32), pltpu.VMEM((SC_ROWS, dp), ys.dtype),
                       pltpu.SemaphoreType.DMA],
        compiler_params=pltpu.CompilerParams(use_tc_tiling_on_sc=True),
        name="sc_gather",
    )(pos1, pos2, ys)


def _expert_kernel(te_ref, first_ref, nsub_ref, xs_ref, wg_ref, wu_ref, wd_ref, ys_ref, wgu_bf_ref, wd_bf_ref):
    i = pl.program_id(0)
    bf16 = jnp.bfloat16
    sub = EXPERT_SUBTILE

    @pl.when(first_ref[i] == 1)
    def _():
        wgu_bf_ref[:, 0:D_EXPERT] = wg_ref[...].astype(bf16)
        wgu_bf_ref[:, D_EXPERT:2 * D_EXPERT] = wu_ref[...].astype(bf16)
        wd_bf_ref[...] = wd_ref[...].astype(bf16)

    def swiglu(r0):
        rows = pl.ds(r0, sub)
        lo, hi = _unpack_halves(xs_ref[rows, :])
        gu = (_dot(lo.astype(bf16), wgu_bf_ref[0:D_PACK, :])
              + _dot(hi.astype(bf16), wgu_bf_ref[D_PACK:D_MODEL, :]))
        g = gu[:, 0:D_EXPERT]
        h = g * jax.nn.sigmoid(g) * gu[:, D_EXPERT:2 * D_EXPERT]
        ys_ref[rows, :] = _pack_halves(_dot(h.astype(bf16), wd_bf_ref[...]))

    n_sub = nsub_ref[i]

    def pair(k, c):
        r0 = pl.multiple_of(k * (2 * sub), 2 * sub)
        swiglu(r0)
        swiglu(r0 + sub)
        return c

    lax.fori_loop(0, n_sub // 2, pair, 0)

    @pl.when(n_sub % 2 == 1)
    def _():
        swiglu(pl.multiple_of((n_sub - 1) * sub, sub))

    def clear(k, c):
        ys_ref[pl.ds(pl.multiple_of(k * sub, sub), sub), :] = jnp.zeros((sub, D_PACK), ys_ref.dtype)
        return c

    lax.fori_loop(n_sub, ys_ref.shape[0] // sub, clear, 0)


def _experts(tile_expert, tile_first, tile_nsub, xs, w_gate, w_up, w_down):
    n_rows, dp = xs.shape
    d = D_MODEL
    tm = EXPERT_TILE
    wmap = lambda i, te, fi, ns: (te[i], 0, 0)
    grid_spec = pltpu.PrefetchScalarGridSpec(
        num_scalar_prefetch=3,
        grid=(n_rows // tm,),
        in_specs=[pl.BlockSpec((tm, dp), lambda i, te, fi, ns: (i, 0)),
                  pl.BlockSpec((None, d, D_EXPERT), wmap),
                  pl.BlockSpec((None, d, D_EXPERT), wmap),
                  pl.BlockSpec((None, D_EXPERT, d), wmap)],
        out_specs=pl.BlockSpec((tm, dp), lambda i, te, fi, ns: (i, 0)),
        scratch_shapes=[pltpu.VMEM((d, 2 * D_EXPERT), jnp.bfloat16),
                        pltpu.VMEM((D_EXPERT, d), jnp.bfloat16)],
    )
    return pl.pallas_call(
        _expert_kernel,
        grid_spec=grid_spec,
        out_shape=jax.ShapeDtypeStruct((n_rows, dp), jnp.uint32),
        compiler_params=pltpu.CompilerParams(dimension_semantics=("arbitrary",),
                                             vmem_limit_bytes=VMEM_LIMIT_BYTES),
        name="experts",
    )(tile_expert, tile_first, tile_nsub, xs, w_gate, w_up, w_down)


def _combine_kernel(x1_ref, slab_ref, y1_ref, y2_ref, g_ref, b_ref, out_ref, *, alpha):
    slab = slab_ref[...]
    w1 = slab[:, COL_W1:COL_W1 + 1]
    w2 = slab[:, COL_W2:COL_W2 + 1]
    lo1, hi1 = _unpack_halves(y1_ref[...])
    lo2, hi2 = _unpack_halves(y2_ref[...])
    moe = jnp.concatenate([w1 * lo1 + w2 * lo2, w1 * hi1 + w2 * hi2], axis=1)
    out_ref[...] = _layer_norm(alpha * x1_ref[...] + moe, g_ref[...], b_ref[...])


def _combine(x1, slab, yg, g, b, alpha):
    n, d = x1.shape
    tt = TOKEN_TILE
    return pl.pallas_call(
        functools.partial(_combine_kernel, alpha=alpha),
        grid=(n // tt,),
        in_specs=[pl.BlockSpec((tt, d), lambda i: (i, 0)),
                  pl.BlockSpec((tt, LANES), lambda i: (i, 0)),
                  pl.BlockSpec((None, tt, D_PACK), lambda i: (0, i, 0)),
                  pl.BlockSpec((None, tt, D_PACK), lambda i: (1, i, 0)),
                  pl.BlockSpec((1, d), lambda i: (0, 0)),
                  pl.BlockSpec((1, d), lambda i: (0, 0))],
        out_specs=pl.BlockSpec((tt, d), lambda i: (i, 0)),
        out_shape=jax.ShapeDtypeStruct((n, d), jnp.float32),
        compiler_params=pltpu.CompilerParams(dimension_semantics=("arbitrary",),
                                             vmem_limit_bytes=VMEM_LIMIT_BYTES),
        name="combine",
    )(x1, slab, yg, yg, g, b)


def _row(a):
    return a.reshape(1, -1)


def _mixer_params(w_in, b_in, conv_w, conv_b, sg_ln_g, sg_ln_b, w_s, b_s, w_o, b_o, ln1_g, ln1_b,
                  w_rc, b_rc, w_rf, b_rf):
    d = D_MODEL
    bf16 = jnp.bfloat16
    row = _row
    ws_pairs = w_s.reshape(N_SG_HEADS // 2, 2 * CHUNK, CHUNK).astype(bf16)
    bs_full = jnp.repeat(b_s.T, SG_HEAD_DIM, axis=1)
    w_r = jnp.zeros((d, LANES), jnp.float32).at[:, 0:N_EXPERTS].set(w_rf)
    w_r = w_r.at[:, COARSE_OFF:COARSE_OFF + N_EXPERT_GROUPS].set(w_rc).astype(bf16)
    b_r = jnp.zeros((1, LANES), jnp.float32).at[0, 0:N_EXPERTS].set(b_rf)
    b_r = b_r.at[0, COARSE_OFF:COARSE_OFF + N_EXPERT_GROUPS].set(b_rc)
    return (w_in.astype(bf16), row(b_in), conv_w, row(conv_b), row(sg_ln_g), row(sg_ln_b),
            ws_pairs, bs_full, w_o.astype(bf16), row(b_o), row(ln1_g), row(ln1_b), w_r, b_r)


def _encoder_layer(x, mixer_params, w_gate, w_up, w_down, ln2_g, ln2_b, alpha):
    n = x.shape[0] * x.shape[1]
    x1, xp, slab, route, carry = _mixer(x, mixer_params, alpha)

    tm = EXPERT_TILE
    n_tiles = 2 * n // tm + N_EXPERTS
    cnt = carry[0, 0:N_EXPERTS].astype(jnp.int32)
    tiles_e = (cnt + tm - 1) // tm
    tile_end = jnp.cumsum(tiles_e)
    tile_start = tile_end - tiles_e
    row_off = tile_start * tm
    experts = jnp.arange(N_EXPERTS, dtype=jnp.int32)[:, None]

    def sorted_pos(e_row, r_row):
        e = route[e_row].astype(jnp.int32)
        return jnp.sum(jnp.where(e[None, :] == experts, row_off[:, None], 0), axis=0) + route[r_row].astype(jnp.int32)

    pos1 = sorted_pos(COL_E1, COL_R1)
    pos2 = sorted_pos(COL_E2, COL_R2)
    tile_ids = jnp.arange(n_tiles, dtype=jnp.int32)
    n_used = tile_end[N_EXPERTS - 1:N_EXPERTS]
    tile_expert = jnp.sum((tile_end[None, :] <= tile_ids[:, None]).astype(jnp.int32), axis=1)
    last_used = jnp.sum((tile_end <= n_used[0] - 1).astype(jnp.int32))
    tile_expert = jnp.minimum(tile_expert, last_used)
    tile_first = jnp.any(tile_ids[:, None] == tile_start[None, :], axis=1) & (tile_ids < n_used[0])
    tile_first = tile_first.astype(jnp.int32).at[0].set(1)
    rows_left = jnp.sum(jnp.where(tile_expert[:, None] == experts.T, (row_off + cnt)[None, :], 0), axis=1) - tile_ids * tm
    tile_nsub = jnp.where(tile_ids < n_used[0], (jnp.clip(rows_left, 0, tm) + EXPERT_SUBTILE - 1) // EXPERT_SUBTILE, 0)

    xs = _sc_dispatch(pos1, pos2, xp, n_tiles * tm)
    ys = _experts(tile_expert, tile_first, tile_nsub.astype(jnp.int32), xs, w_gate, w_up, w_down)
    yg = _sc_gather(pos1, pos2, ys)
    return _combine(x1, slab, yg, _row(ln2_g), _row(ln2_b), alpha).reshape(x.shape)


def kernel(x_prompt, x_sample, w_in, b_in, conv_w, conv_b, sg_ln_g, sg_ln_b, w_s, b_s, w_o, b_o, ln1_g, ln1_b, w_rc, b_rc, w_rf, b_rf, w_gate, w_up, w_down, ln2_g, ln2_b):
    depth = w_in.shape[0]
    alpha = (2.0 * depth) ** 0.25
    xs = (x_prompt, x_sample)
    for l in range(depth):
        mixer_params = _mixer_params(w_in[l], b_in[l], conv_w[l], conv_b[l], sg_ln_g[l], sg_ln_b[l], w_s[l], b_s[l],
                                     w_o[l], b_o[l], ln1_g[l], ln1_b[l], w_rc[l], b_rc[l], w_rf[l], b_rf[l])
        xs = tuple(_encoder_layer(x, mixer_params, w_gate[l], w_up[l], w_down[l], ln2_g[l], ln2_b[l], alpha)
                   for x in xs)
    return xs
```

```python
import functools

import jax
import jax.numpy as jnp
from jax import lax
from jax.experimental import pallas as pl
from jax.experimental.pallas import tpu as pltpu
from jax.experimental.pallas import tpu_sc as plsc

D_MODEL = 1024
D_CONV = 512
D_SG = 512
N_SG_HEADS = 8
SG_HEAD_DIM = D_SG // N_SG_HEADS
CHUNK = 128
N_EXPERT_GROUPS = 4
EXPERTS_PER_GROUP = 8
N_EXPERTS = N_EXPERT_GROUPS * EXPERTS_PER_GROUP
D_EXPERT = 512
LN_EPS = 1e-5
D_PACK = D_MODEL // 2

LANES = 128
SUBLANES = 8
HALO = 16
SEQ_TILE = 512
SEQ_SUBTILE = 512
EXPERT_TILE = 1024
EXPERT_SUBTILE = 256
TOKEN_TILE = 1024
SC_CORES = 2
SC_SUBCORES = 16
SC_WORKERS = SC_CORES * SC_SUBCORES
SC_ROWS = 64
VMEM_LIMIT_BYTES = 56 * 1024 * 1024

COL_E1, COL_E2, COL_W1, COL_W2, COL_R1, COL_R2 = 0, 1, 2, 3, 4, 5
COARSE_OFF = N_EXPERTS


def _dot(a, b):
    return jnp.dot(a, b, preferred_element_type=jnp.float32)


def _gelu_tanh(x):
    return 0.5 * x * (1.0 + jnp.tanh(0.7978845608028654 * (x + 0.044715 * (x * x * x))))


def _layer_norm(x, g, b):
    mu = jnp.mean(x, axis=-1, keepdims=True)
    xc = x - mu
    var = jnp.mean(xc * xc, axis=-1, keepdims=True)
    return xc * lax.rsqrt(var + LN_EPS) * g + b


def _pack_halves(x):
    u32 = jnp.uint32
    lo = lax.bitcast_convert_type(x[:, 0:D_PACK].astype(jnp.bfloat16).astype(jnp.float32), u32)
    hi = lax.bitcast_convert_type(x[:, D_PACK:D_MODEL].astype(jnp.bfloat16).astype(jnp.float32), u32)
    return (hi & u32(0xFFFF0000)) | (lo >> u32(16))


def _unpack_halves(w):
    u32 = jnp.uint32
    lo = lax.bitcast_convert_type(w << u32(16), jnp.float32)
    hi = lax.bitcast_convert_type(w & u32(0xFFFF0000), jnp.float32)
    return lo, hi


def _mixer_kernel(x_ref, x_prev_ref, x_next_ref, w_in_ref, b_in_ref, conv_w_ref, conv_b_ref,
                  sg_g_ref, sg_b_ref, ws_ref, bs_ref, w_o_ref, b_o_ref, ln1_g_ref, ln1_b_ref,
                  w_r_ref, b_r_ref,
                  x1_ref, xp_ref, slab_ref, route_ref, counts_ref,
                  xb_ref, ymix_ref, tri_ref, carry_ref, *, ts, nt, alpha):
    t = pl.program_id(0)
    j = t % nt
    bf16 = jnp.bfloat16
    sub = SEQ_SUBTILE

    @pl.when(t == 0)
    def _():
        carry_ref[...] = jnp.zeros_like(carry_ref)
        ri = lax.broadcasted_iota(jnp.int32, (sub, sub), 0)
        ci = lax.broadcasted_iota(jnp.int32, (sub, sub), 1)
        tri_ref[...] = jnp.where(ri > ci, 1.0, 0.0).astype(bf16)

    xb_ref[0:HALO, :] = x_prev_ref[...].astype(bf16)
    xb_ref[HALO:HALO + ts, :] = x_ref[...].astype(bf16)
    xb_ref[HALO + ts:HALO + ts + HALO, :] = x_next_ref[...].astype(bf16)

    def proj(lhs, lo, hi):
        return _dot(lhs, w_in_ref[:, lo:hi]) + b_in_ref[:, lo:hi]

    has_prev = jnp.where(j > 0, 1.0, 0.0)
    has_next = jnp.where(j < nt - 1, 1.0, 0.0)
    row_e = lax.broadcasted_iota(jnp.int32, (sub + 2 * HALO, 1), 0)
    lane_c = lax.broadcasted_iota(jnp.int32, (CHUNK, LANES), 1)
    first_head = lane_c < SG_HEAD_DIM
    lane = lax.broadcasted_iota(jnp.int32, (sub, LANES), 1)
    lane_f = lane.astype(jnp.float32)
    neg = jnp.float32(-jnp.inf)
    big = jnp.float32(1e9)
    is_c = (lane >= COARSE_OFF) & (lane < COARSE_OFF + N_EXPERT_GROUPS)
    carry = carry_ref[...]

    for r0 in range(0, ts, sub):
        rows = pl.ds(r0, sub)
        x = x_ref[rows, :]
        xe = xb_ref[r0:r0 + sub + 2 * HALO, :]
        xm = xb_ref[HALO + r0:HALO + r0 + sub, :]

        g_e = proj(xe, 0, D_CONV) * proj(xe, 2 * D_CONV, 3 * D_CONV)
        if r0 == 0:
            g_e = g_e * jnp.where(row_e < HALO, has_prev, 1.0)
        if r0 + sub == ts:
            g_e = g_e * jnp.where(row_e >= HALO + sub, has_next, 1.0)
        g_prev = g_e[HALO - 1:HALO - 1 + sub, :]
        g_mid = g_e[HALO:HALO + sub, :]
        g_next = g_e[HALO + 1:HALO + 1 + sub, :]
        conv = (g_prev * conv_w_ref[0:1, :] + g_mid * conv_w_ref[1:2, :] + g_next * conv_w_ref[2:3, :]
                + conv_b_ref[...])
        y_a = proj(xm, D_CONV, 2 * D_CONV) * conv
        ymix_ref[rows, 0:D_CONV] = y_a.astype(bf16)

        u = _gelu_tanh(proj(xm, 3 * D_CONV, 3 * D_CONV + D_SG))
        v = _gelu_tanh(proj(xm, 3 * D_CONV + D_SG, 3 * D_CONV + 2 * D_SG))
        v_ln = _layer_norm(v, sg_g_ref[...], sg_b_ref[...]).astype(bf16)
        for q0 in range(0, sub, 2 * CHUNK):
            q1 = q0 + CHUNK
            for hp in range(N_SG_HEADS // 2):
                c0 = hp * LANES
                rhs = jnp.concatenate([v_ln[q0:q0 + CHUNK, c0:c0 + LANES], v_ln[q1:q1 + CHUNK, c0:c0 + LANES]], axis=1)
                res = _dot(ws_ref[hp], rhs)
                bias = bs_ref[:, c0:c0 + LANES]
                m0 = jnp.where(first_head, res[0:CHUNK, 0:LANES], res[CHUNK:2 * CHUNK, 0:LANES]) + bias
                m1 = jnp.where(first_head, res[0:CHUNK, LANES:2 * LANES], res[CHUNK:2 * CHUNK, LANES:2 * LANES]) + bias
                ymix_ref[r0 + q0:r0 + q0 + CHUNK, D_CONV + c0:D_CONV + c0 + LANES] = (
                    u[q0:q0 + CHUNK, c0:c0 + LANES] * m0).astype(bf16)
                ymix_ref[r0 + q1:r0 + q1 + CHUNK, D_CONV + c0:D_CONV + c0 + LANES] = (
                    u[q1:q1 + CHUNK, c0:c0 + LANES] * m1).astype(bf16)

        mix = _dot(ymix_ref[rows, :], w_o_ref[...]) + b_o_ref[...]
        x1 = _layer_norm(alpha * x + mix, ln1_g_ref[...], ln1_b_ref[...])
        x1_ref[rows, :] = x1
        xp_ref[rows, :] = _pack_halves(x1)

        logits = _dot(x1.astype(bf16), w_r_ref[...]) + b_r_ref[...]
        lc = jnp.where(is_c, logits, neg)
        mx = jnp.max(lc, axis=1, keepdims=True)
        grp = jnp.min(jnp.where(lc == mx, lane_f - COARSE_OFF, big), axis=1, keepdims=True)
        p_grp = 1.0 / jnp.sum(jnp.where(is_c, jnp.exp(logits - mx), 0.0), axis=1, keepdims=True)
        grp_lo = grp * EXPERTS_PER_GROUP
        in_grp = (lane_f >= grp_lo) & (lane_f < grp_lo + EXPERTS_PER_GROUP)
        lf = jnp.where(in_grp, logits, neg)
        v1 = jnp.max(lf, axis=1, keepdims=True)
        e1 = jnp.min(jnp.where(lf == v1, lane_f, big), axis=1, keepdims=True)
        lf2 = jnp.where(lane_f == e1, neg, lf)
        v2 = jnp.max(lf2, axis=1, keepdims=True)
        e2 = jnp.min(jnp.where(lf2 == v2, lane_f, big), axis=1, keepdims=True)
        a = jnp.exp(v2 - v1)
        w1 = p_grp / (1.0 + a)
        w2 = p_grp * a / (1.0 + a)

        hit1 = lane_f == e1
        hit2 = lane_f == e2
        onehot = jnp.where(hit1 | hit2, 1.0, 0.0)
        before = _dot(tri_ref[...], onehot.astype(bf16)) + carry
        r1 = jnp.sum(jnp.where(hit1, before, 0.0), axis=1, keepdims=True)
        r2 = jnp.sum(jnp.where(hit2, before, 0.0), axis=1, keepdims=True)
        carry = carry + jnp.sum(onehot, axis=0, keepdims=True)

        slab = jnp.where(lane == COL_E1, e1, 0.0)
        slab = jnp.where(lane == COL_E2, e2, slab)
        slab = jnp.where(lane == COL_W1, w1, slab)
        slab = jnp.where(lane == COL_W2, w2, slab)
        slab = jnp.where(lane == COL_R1, r1, slab)
        slab = jnp.where(lane == COL_R2, r2, slab)
        slab_ref[rows, :] = slab
        route_ref[:, rows] = slab.T[0:SUBLANES, :]

    carry_ref[...] = carry
    counts_ref[...] = jnp.broadcast_to(carry, counts_ref.shape)


def _mixer(x, p, alpha):
    nb, s, d = x.shape
    ts = SEQ_TILE
    nt = s // ts
    hb = ts // HALO
    n_tiles = nb * nt
    n_total = n_tiles * ts

    cur = lambda t: (t // nt, t % nt, 0)
    prv = lambda t: (t // nt, jnp.maximum((t % nt) * hb - 1, 0), 0)
    nxt = lambda t: (t // nt, jnp.minimum((t % nt + 1) * hb, s // HALO - 1), 0)
    full = lambda a: pl.BlockSpec(a.shape, lambda t: (0,) * a.ndim)
    in_specs = [pl.BlockSpec((None, ts, d), cur), pl.BlockSpec((None, HALO, d), prv),
                pl.BlockSpec((None, HALO, d), nxt)] + [full(a) for a in p]
    out_shape = (jax.ShapeDtypeStruct((n_total, d), jnp.float32),
                 jax.ShapeDtypeStruct((n_total, D_PACK), jnp.uint32),
                 jax.ShapeDtypeStruct((n_total, LANES), jnp.float32),
                 jax.ShapeDtypeStruct((SUBLANES, n_total), jnp.float32),
                 jax.ShapeDtypeStruct((SUBLANES, LANES), jnp.float32))
    row_blk = lambda t: (t, 0)
    out_specs = (pl.BlockSpec((ts, d), row_blk),
                 pl.BlockSpec((ts, D_PACK), row_blk),
                 pl.BlockSpec((ts, LANES), row_blk),
                 pl.BlockSpec((SUBLANES, ts), lambda t: (0, t)),
                 pl.BlockSpec((SUBLANES, LANES), lambda t: (0, 0)))
    return pl.pallas_call(
        functools.partial(_mixer_kernel, ts=ts, nt=nt, alpha=alpha),
        grid=(n_tiles,),
        in_specs=in_specs,
        out_specs=out_specs,
        out_shape=out_shape,
        scratch_shapes=[pltpu.VMEM((ts + 2 * HALO, d), jnp.bfloat16),
                        pltpu.VMEM((ts, d), jnp.bfloat16),
                        pltpu.VMEM((SEQ_SUBTILE, SEQ_SUBTILE), jnp.bfloat16),
                        pltpu.VMEM((1, LANES), jnp.float32)],
        compiler_params=pltpu.CompilerParams(dimension_semantics=("arbitrary",),
                                             vmem_limit_bytes=VMEM_LIMIT_BYTES),
        name="mixer",
    )(x, x, x, *p)


def _sc_mesh():
    return plsc.VectorSubcoreMesh(core_axis_name="c", subcore_axis_name="s",
                                  num_cores=SC_CORES, num_subcores=SC_SUBCORES)


def _sc_worker_base(per_worker):
    return (lax.axis_index("s") * SC_CORES + lax.axis_index("c")) * per_worker


def _sc_dispatch(pos1, pos2, xp, n_rows):
    n, dp = xp.shape
    per_worker = n // SC_WORKERS
    steps = per_worker // SC_ROWS

    def body(pos1_hbm, pos2_hbm, xp_hbm, xs_hbm, idx1_v, idx2_v, rows_v, sem):
        base = _sc_worker_base(per_worker)

        @pl.loop(0, steps)
        def _(k):
            off = pl.multiple_of(base + k * SC_ROWS, SC_ROWS)
            pltpu.sync_copy(pos1_hbm.at[pl.ds(off, SC_ROWS)], idx1_v)
            pltpu.sync_copy(pos2_hbm.at[pl.ds(off, SC_ROWS)], idx2_v)
            pltpu.sync_copy(xp_hbm.at[pl.ds(off, SC_ROWS)], rows_v)
            c1 = pltpu.async_copy(rows_v, xs_hbm.at[idx1_v], sem)
            c2 = pltpu.async_copy(rows_v, xs_hbm.at[idx2_v], sem)
            c1.wait()
            c2.wait()

    return pl.kernel(
        body,
        out_type=jax.ShapeDtypeStruct((n_rows, dp), xp.dtype),
        mesh=_sc_mesh(),
        scratch_types=[pltpu.VMEM((SC_ROWS,), jnp.int32), pltpu.VMEM((SC_ROWS,), jnp.int32),
                       pltpu.VMEM((SC_ROWS, dp), xp.dtype), pltpu.SemaphoreType.DMA],
        compiler_params=pltpu.CompilerParams(use_tc_tiling_on_sc=True),
        name="sc_dispatch",
    )(pos1, pos2, xp)


def _sc_gather(pos1, pos2, ys):
    n = pos1.shape[0]
    dp = ys.shape[1]
    per_worker = n // SC_WORKERS
    steps = per_worker // SC_ROWS

    def body(pos1_hbm, pos2_hbm, ys_hbm, out_hbm, idx_v, rows_v, sem):
        base = _sc_worker_base(per_worker)

        @pl.loop(0, steps)
        def _(k):
            off = pl.multiple_of(base + k * SC_ROWS, SC_ROWS)
            for slot, pos_hbm in enumerate((pos1_hbm, pos2_hbm)):
                pltpu.sync_copy(pos_hbm.at[pl.ds(off, SC_ROWS)], idx_v)
                pltpu.async_copy(ys_hbm.at[idx_v], rows_v, sem).wait()
                pltpu.sync_copy(rows_v, out_hbm.at[slot, pl.ds(off, SC_ROWS)])

    return pl.kernel(
        body,
        out_type=jax.ShapeDtypeStruct((2, n, dp), ys.dtype),
        mesh=_sc_mesh(),
        scratch_types=[pltpu.VMEM((SC_ROWS,), jnp.int32), pltpu.VMEM((SC_ROWS, dp), ys.dtype),
                       pltpu.SemaphoreType.DMA],
        compiler_params=pltpu.CompilerParams(use_tc_tiling_on_sc=True),
        name="sc_gather",
    )(pos1, pos2, ys)


def _expert_kernel(te_ref, first_ref, nsub_ref, slot_ref, next_ref, xs_ref, wg_hbm, wu_hbm, wd_hbm, ys_ref,
                   wg_buf, wu_buf, wd_buf, wgu_bf_ref, wd_bf_ref, sem):
    i = pl.program_id(0)
    bf16 = jnp.bfloat16
    sub = EXPERT_SUBTILE

    def weight_copies(expert, slot):
        return (pltpu.make_async_copy(wg_hbm.at[expert], wg_buf.at[slot], sem.at[slot]),
                pltpu.make_async_copy(wu_hbm.at[expert], wu_buf.at[slot], sem.at[slot]),
                pltpu.make_async_copy(wd_hbm.at[expert], wd_buf.at[slot], sem.at[slot]))

    @pl.when(first_ref[i] == 1)
    def _():
        slot = slot_ref[i]

        @pl.when(i == 0)
        def _():
            for c in weight_copies(te_ref[i], slot):
                c.start()

        for c in weight_copies(te_ref[i], slot):
            c.wait()

        @pl.when(next_ref[i] >= 0)
        def _():
            for c in weight_copies(next_ref[i], 1 - slot):
                c.start()

        wgu_bf_ref[:, 0:D_EXPERT] = wg_buf[slot].astype(bf16)
        wgu_bf_ref[:, D_EXPERT:2 * D_EXPERT] = wu_buf[slot].astype(bf16)
        wd_bf_ref[...] = wd_buf[slot].astype(bf16)

    def swiglu(r0):
        rows = pl.ds(r0, sub)
        lo, hi = _unpack_halves(xs_ref[rows, :])
        gu = (_dot(lo.astype(bf16), wgu_bf_ref[0:D_PACK, :])
              + _dot(hi.astype(bf16), wgu_bf_ref[D_PACK:D_MODEL, :]))
        g = gu[:, 0:D_EXPERT]
        h = g * jax.nn.sigmoid(g) * gu[:, D_EXPERT:2 * D_EXPERT]
        ys_ref[rows, :] = _pack_halves(_dot(h.astype(bf16), wd_bf_ref[...]))

    n_sub = nsub_ref[i]

    def pair(k, c):
        r0 = pl.multiple_of(k * (2 * sub), 2 * sub)
        swiglu(r0)
        swiglu(r0 + sub)
        return c

    lax.fori_loop(0, n_sub // 2, pair, 0)

    @pl.when(n_sub % 2 == 1)
    def _():
        swiglu(pl.multiple_of((n_sub - 1) * sub, sub))

    def clear(k, c):
        ys_ref[pl.ds(pl.multiple_of(k * sub, sub), sub), :] = jnp.zeros((sub, D_PACK), ys_ref.dtype)
        return c

    lax.fori_loop(n_sub, ys_ref.shape[0] // sub, clear, 0)


def _experts(tile_expert, tile_first, tile_nsub, tile_slot, tile_next, xs, w_gate, w_up, w_down):
    n_rows, dp = xs.shape
    d = D_MODEL
    tm = EXPERT_TILE
    grid_spec = pltpu.PrefetchScalarGridSpec(
        num_scalar_prefetch=5,
        grid=(n_rows // tm,),
        in_specs=[pl.BlockSpec((tm, dp), lambda i, *_: (i, 0)),
                  pl.BlockSpec(memory_space=pl.ANY),
                  pl.BlockSpec(memory_space=pl.ANY),
                  pl.BlockSpec(memory_space=pl.ANY)],
        out_specs=pl.BlockSpec((tm, dp), lambda i, *_: (i, 0)),
        scratch_shapes=[pltpu.VMEM((2, d, D_EXPERT), jnp.float32),
                        pltpu.VMEM((2, d, D_EXPERT), jnp.float32),
                        pltpu.VMEM((2, D_EXPERT, d), jnp.float32),
                        pltpu.VMEM((d, 2 * D_EXPERT), jnp.bfloat16),
                        pltpu.VMEM((D_EXPERT, d), jnp.bfloat16),
                        pltpu.SemaphoreType.DMA((2,))],
    )
    return pl.pallas_call(
        _expert_kernel,
        grid_spec=grid_spec,
        out_shape=jax.ShapeDtypeStruct((n_rows, dp), jnp.uint32),
        compiler_params=pltpu.CompilerParams(dimension_semantics=("arbitrary",),
                                             vmem_limit_bytes=VMEM_LIMIT_BYTES),
        name="experts",
    )(tile_expert, tile_first, tile_nsub, tile_slot, tile_next, xs, w_gate, w_up, w_down)


def _combine_kernel(x1_ref, slab_ref, y1_ref, y2_ref, g_ref, b_ref, out_ref, *, alpha):
    slab = slab_ref[...]
    w1 = slab[:, COL_W1:COL_W1 + 1]
    w2 = slab[:, COL_W2:COL_W2 + 1]
    lo1, hi1 = _unpack_halves(y1_ref[...])
    lo2, hi2 = _unpack_halves(y2_ref[...])
    moe = jnp.concatenate([w1 * lo1 + w2 * lo2, w1 * hi1 + w2 * hi2], axis=1)
    out_ref[...] = _layer_norm(alpha * x1_ref[...] + moe, g_ref[...], b_ref[...])


def _combine(x1, slab, yg, g, b, alpha):
    n, d = x1.shape
    tt = TOKEN_TILE
    return pl.pallas_call(
        functools.partial(_combine_kernel, alpha=alpha),
        grid=(n // tt,),
        in_specs=[pl.BlockSpec((tt, d), lambda i: (i, 0)),
                  pl.BlockSpec((tt, LANES), lambda i: (i, 0)),
                  pl.BlockSpec((None, tt, D_PACK), lambda i: (0, i, 0)),
                  pl.BlockSpec((None, tt, D_PACK), lambda i: (1, i, 0)),
                  pl.BlockSpec((1, d), lambda i: (0, 0)),
                  pl.BlockSpec((1, d), lambda i: (0, 0))],
        out_specs=pl.BlockSpec((tt, d), lambda i: (i, 0)),
        out_shape=jax.ShapeDtypeStruct((n, d), jnp.float32),
        compiler_params=pltpu.CompilerParams(dimension_semantics=("arbitrary",),
                                             vmem_limit_bytes=VMEM_LIMIT_BYTES),
        name="combine",
    )(x1, slab, yg, yg, g, b)


def _row(a):
    return a.reshape(1, -1)


def _mixer_params(w_in, b_in, conv_w, conv_b, sg_ln_g, sg_ln_b, w_s, b_s, w_o, b_o, ln1_g, ln1_b,
                  w_rc, b_rc, w_rf, b_rf):
    d = D_MODEL
    bf16 = jnp.bfloat16
    row = _row
    ws_pairs = w_s.reshape(N_SG_HEADS // 2, 2 * CHUNK, CHUNK).astype(bf16)
    bs_full = jnp.repeat(b_s.T, SG_HEAD_DIM, axis=1)
    w_r = jnp.zeros((d, LANES), jnp.float32).at[:, 0:N_EXPERTS].set(w_rf)
    w_r = w_r.at[:, COARSE_OFF:COARSE_OFF + N_EXPERT_GROUPS].set(w_rc).astype(bf16)
    b_r = jnp.zeros((1, LANES), jnp.float32).at[0, 0:N_EXPERTS].set(b_rf)
    b_r = b_r.at[0, COARSE_OFF:COARSE_OFF + N_EXPERT_GROUPS].set(b_rc)
    return (w_in.astype(bf16), row(b_in), conv_w, row(conv_b), row(sg_ln_g), row(sg_ln_b),
            ws_pairs, bs_full, w_o.astype(bf16), row(b_o), row(ln1_g), row(ln1_b), w_r, b_r)


def _encoder_layer(x, mixer_params, w_gate, w_up, w_down, ln2_g, ln2_b, alpha):
    n = x.shape[0] * x.shape[1]
    x1, xp, slab, route, carry = _mixer(x, mixer_params, alpha)

    tm = EXPERT_TILE
    n_tiles = 2 * n // tm + N_EXPERTS
    cnt = carry[0, 0:N_EXPERTS].astype(jnp.int32)
    tiles_e = (cnt + tm - 1) // tm
    tile_end = jnp.cumsum(tiles_e)
    tile_start = tile_end - tiles_e
    row_off = tile_start * tm
    experts = jnp.arange(N_EXPERTS, dtype=jnp.int32)[:, None]

    def sorted_pos(e_row, r_row):
        e = route[e_row].astype(jnp.int32)
        return jnp.sum(jnp.where(e[None, :] == experts, row_off[:, None], 0), axis=0) + route[r_row].astype(jnp.int32)

    pos1 = sorted_pos(COL_E1, COL_R1)
    pos2 = sorted_pos(COL_E2, COL_R2)
    tile_ids = jnp.arange(n_tiles, dtype=jnp.int32)
    n_used = tile_end[N_EXPERTS - 1:N_EXPERTS]
    tile_expert = jnp.sum((tile_end[None, :] <= tile_ids[:, None]).astype(jnp.int32), axis=1)
    last_used = jnp.sum((tile_end <= n_used[0] - 1).astype(jnp.int32))
    tile_expert = jnp.minimum(tile_expert, last_used)
    tile_first = jnp.any(tile_ids[:, None] == tile_start[None, :], axis=1) & (tile_ids < n_used[0])
    tile_first = tile_first.astype(jnp.int32).at[0].set(1)
    rows_left = jnp.sum(jnp.where(tile_expert[:, None] == experts.T, (row_off + cnt)[None, :], 0), axis=1) - tile_ids * tm
    tile_nsub = jnp.where(tile_ids < n_used[0], (jnp.clip(rows_left, 0, tm) + EXPERT_SUBTILE - 1) // EXPERT_SUBTILE, 0)

    used = tiles_e > 0
    ordinal = jnp.cumsum(used.astype(jnp.int32)) - 1
    e_ids = experts[:, 0]
    later_used = used[None, :] & (e_ids[None, :] > e_ids[:, None])
    next_used = jnp.min(jnp.where(later_used, e_ids[None, :], N_EXPERTS), axis=1)
    next_used = jnp.where(next_used < N_EXPERTS, next_used, -1)
    tile_slot = ordinal[tile_expert] % 2
    tile_next = next_used[tile_expert]

    xs = _sc_dispatch(pos1, pos2, xp, n_tiles * tm)
    ys = _experts(tile_expert, tile_first, tile_nsub.astype(jnp.int32), tile_slot.astype(jnp.int32),
                  tile_next.astype(jnp.int32), xs, w_gate, w_up, w_down)
    yg = _sc_gather(pos1, pos2, ys)
    return _combine(x1, slab, yg, _row(ln2_g), _row(ln2_b), alpha).reshape(x.shape)


def kernel(x_prompt, x_sample, w_in, b_in, conv_w, conv_b, sg_ln_g, sg_ln_b, w_s, b_s, w_o, b_o, ln1_g, ln1_b, w_rc, b_rc, w_rf, b_rf, w_gate, w_up, w_down, ln2_g, ln2_b):
    depth = w_in.shape[0]
    alpha = (2.0 * depth) ** 0.25
    xs = (x_prompt, x_sample)
    for l in range(depth):
        mixer_params = _mixer_params(w_in[l], b_in[l], conv_w[l], conv_b[l], sg_ln_g[l], sg_ln_b[l], w_s[l], b_s[l],
                                     w_o[l], b_o[l], ln1_g[l], ln1_b[l], w_rc[l], b_rc[l], w_rf[l], b_rf[l])
        xs = tuple(_encoder_layer(x, mixer_params, w_gate[l], w_up[l], w_down[l], ln2_g[l], ln2_b[l], alpha)
                   for x in xs)
    return xs
```

```python
import functools

import jax
import jax.numpy as jnp
from jax import lax
from jax.experimental import pallas as pl
from jax.experimental.pallas import tpu as pltpu
from jax.experimental.pallas import tpu_sc as plsc

D_MODEL = 1024
D_CONV = 512
D_SG = 512
N_SG_HEADS = 8
SG_HEAD_DIM = D_SG // N_SG_HEADS
CHUNK = 128
N_EXPERT_GROUPS = 4
EXPERTS_PER_GROUP = 8
N_EXPERTS = N_EXPERT_GROUPS * EXPERTS_PER_GROUP
D_EXPERT = 512
LN_EPS = 1e-5
D_PACK = D_MODEL // 2

LANES = 128
SUBLANES = 8
HALO = 16
SEQ_TILE = 512
EXPERT_TILE = 1024
EXPERT_SUBTILE = 256
TOKEN_TILE = 1024
SC_CORES = 2
SC_SUBCORES = 16
SC_WORKERS = SC_CORES * SC_SUBCORES
SC_ROWS = 64
VMEM_LIMIT_BYTES = 56 * 1024 * 1024

COL_E1, COL_E2, COL_W1, COL_W2, COL_R1, COL_R2 = 0, 1, 2, 3, 4, 5
COARSE_OFF = N_EXPERTS


def _dot(a, b):
    return jnp.dot(a, b, preferred_element_type=jnp.float32)


def _gelu_tanh(x):
    return 0.5 * x * (1.0 + jnp.tanh(0.7978845608028654 * (x + 0.044715 * (x * x * x))))


def _layer_norm(x, g, b):
    mu = jnp.mean(x, axis=-1, keepdims=True)
    xc = x - mu
    var = jnp.mean(xc * xc, axis=-1, keepdims=True)
    return xc * lax.rsqrt(var + LN_EPS) * g + b


def _pack_halves(x):
    u32 = jnp.uint32
    lo = lax.bitcast_convert_type(x[:, 0:D_PACK].astype(jnp.bfloat16).astype(jnp.float32), u32)
    hi = lax.bitcast_convert_type(x[:, D_PACK:D_MODEL].astype(jnp.bfloat16).astype(jnp.float32), u32)
    return (hi & u32(0xFFFF0000)) | (lo >> u32(16))


def _unpack_halves(w):
    u32 = jnp.uint32
    lo = lax.bitcast_convert_type(w << u32(16), jnp.float32)
    hi = lax.bitcast_convert_type(w & u32(0xFFFF0000), jnp.float32)
    return lo, hi


def _mixer_kernel(x_ref, x_prev_ref, x_next_ref, w_in_ref, b_in_ref, conv_w_ref, conv_b_ref,
                  sg_g_ref, sg_b_ref, ws_ref, bs_ref, w_o_ref, b_o_ref, ln1_g_ref, ln1_b_ref,
                  w_r_ref, b_r_ref,
                  x1_ref, xp_ref, slab_ref, route_ref, counts_ref,
                  xb_ref, ymix_ref, tri_ref, resid_ref, carry_ref, *, ts, nt, n_tiles, alpha):
    t = pl.program_id(0)
    j = jnp.minimum(t, n_tiles - 1) % nt
    bf16 = jnp.bfloat16

    @pl.when(t == 0)
    def _():
        carry_ref[...] = jnp.zeros_like(carry_ref)
        resid_ref[...] = jnp.zeros_like(resid_ref)
        ri = lax.broadcasted_iota(jnp.int32, (ts, ts), 0)
        ci = lax.broadcasted_iota(jnp.int32, (ts, ts), 1)
        tri_ref[...] = jnp.where(ri > ci, 1.0, 0.0).astype(bf16)

    lane = lax.broadcasted_iota(jnp.int32, (ts, LANES), 1)
    lane_f = lane.astype(jnp.float32)
    neg = jnp.float32(-jnp.inf)
    big = jnp.float32(1e9)
    is_c = (lane >= COARSE_OFF) & (lane < COARSE_OFF + N_EXPERT_GROUPS)

    def proj(lhs, lo, hi):
        return _dot(lhs, w_in_ref[:, lo:hi]) + b_in_ref[:, lo:hi]

    def back_norm():
        x1 = _layer_norm(resid_ref[...], ln1_g_ref[...], ln1_b_ref[...])
        x1_ref[...] = x1
        xp_ref[...] = _pack_halves(x1)
        return x1.astype(bf16)

    def back_route(x1b):
        logits = _dot(x1b, w_r_ref[...]) + b_r_ref[...]
        lc = jnp.where(is_c, logits, neg)
        mx = jnp.max(lc, axis=1, keepdims=True)
        grp = jnp.min(jnp.where(lc == mx, lane_f - COARSE_OFF, big), axis=1, keepdims=True)
        p_grp = 1.0 / jnp.sum(jnp.where(is_c, jnp.exp(logits - mx), 0.0), axis=1, keepdims=True)
        grp_lo = grp * EXPERTS_PER_GROUP
        in_grp = (lane_f >= grp_lo) & (lane_f < grp_lo + EXPERTS_PER_GROUP)
        lf = jnp.where(in_grp, logits, neg)
        v1 = jnp.max(lf, axis=1, keepdims=True)
        e1 = jnp.min(jnp.where(lf == v1, lane_f, big), axis=1, keepdims=True)
        lf2 = jnp.where(lane_f == e1, neg, lf)
        v2 = jnp.max(lf2, axis=1, keepdims=True)
        e2 = jnp.min(jnp.where(lf2 == v2, lane_f, big), axis=1, keepdims=True)
        a = jnp.exp(v2 - v1)
        return e1, e2, p_grp / (1.0 + a), p_grp * a / (1.0 + a)

    def back_rank(e1, e2, w1, w2):
        hit1 = lane_f == e1
        hit2 = lane_f == e2
        onehot = jnp.where(hit1 | hit2, 1.0, 0.0)
        carry = carry_ref[...]
        before = _dot(tri_ref[...], onehot.astype(bf16)) + carry
        r1 = jnp.sum(jnp.where(hit1, before, 0.0), axis=1, keepdims=True)
        r2 = jnp.sum(jnp.where(hit2, before, 0.0), axis=1, keepdims=True)
        carry = jnp.where(t > 0, carry + jnp.sum(onehot, axis=0, keepdims=True), carry)
        carry_ref[...] = carry
        counts_ref[...] = jnp.broadcast_to(carry, counts_ref.shape)
        slab = jnp.where(lane == COL_E1, e1, 0.0)
        slab = jnp.where(lane == COL_E2, e2, slab)
        slab = jnp.where(lane == COL_W1, w1, slab)
        slab = jnp.where(lane == COL_W2, w2, slab)
        slab = jnp.where(lane == COL_R1, r1, slab)
        slab = jnp.where(lane == COL_R2, r2, slab)
        slab_ref[...] = slab
        route_ref[...] = slab.T[0:SUBLANES, :]

    def front_load():
        xb_ref[0:HALO, :] = x_prev_ref[...].astype(bf16)
        xb_ref[HALO:HALO + ts, :] = x_ref[...].astype(bf16)
        xb_ref[HALO + ts:HALO + ts + HALO, :] = x_next_ref[...].astype(bf16)

    def front_conv():
        xe = xb_ref[...]
        g_e = proj(xe, 0, D_CONV) * proj(xe, 2 * D_CONV, 3 * D_CONV)
        row_e = lax.broadcasted_iota(jnp.int32, (ts + 2 * HALO, 1), 0)
        has_prev = jnp.where(j > 0, 1.0, 0.0)
        has_next = jnp.where(j < nt - 1, 1.0, 0.0)
        g_e = g_e * jnp.where(row_e < HALO, has_prev, jnp.where(row_e >= HALO + ts, has_next, 1.0))
        conv = (g_e[HALO - 1:HALO - 1 + ts, :] * conv_w_ref[0:1, :] + g_e[HALO:HALO + ts, :] * conv_w_ref[1:2, :]
                + g_e[HALO + 1:HALO + 1 + ts, :] * conv_w_ref[2:3, :] + conv_b_ref[...])
        y_a = proj(xb_ref[HALO:HALO + ts, :], D_CONV, 2 * D_CONV) * conv
        ymix_ref[:, 0:D_CONV] = y_a.astype(bf16)

    def front_gate_proj():
        xm = xb_ref[HALO:HALO + ts, :]
        return proj(xm, 3 * D_CONV, 3 * D_CONV + D_SG), proj(xm, 3 * D_CONV + D_SG, 3 * D_CONV + 2 * D_SG)

    def front_gate(u_pre, v_pre):
        u = _gelu_tanh(u_pre)
        v = _gelu_tanh(v_pre)
        v_ln = _layer_norm(v, sg_g_ref[...], sg_b_ref[...]).astype(bf16)
        first_head = lax.broadcasted_iota(jnp.int32, (CHUNK, LANES), 1) < SG_HEAD_DIM
        for q0 in range(0, ts, 2 * CHUNK):
            q1 = q0 + CHUNK
            for hp in range(N_SG_HEADS // 2):
                c0 = hp * LANES
                rhs = jnp.concatenate([v_ln[q0:q0 + CHUNK, c0:c0 + LANES], v_ln[q1:q1 + CHUNK, c0:c0 + LANES]], axis=1)
                res = _dot(ws_ref[hp], rhs)
                bias = bs_ref[:, c0:c0 + LANES]
                m0 = jnp.where(first_head, res[0:CHUNK, 0:LANES], res[CHUNK:2 * CHUNK, 0:LANES]) + bias
                m1 = jnp.where(first_head, res[0:CHUNK, LANES:2 * LANES], res[CHUNK:2 * CHUNK, LANES:2 * LANES]) + bias
                ymix_ref[q0:q0 + CHUNK, D_CONV + c0:D_CONV + c0 + LANES] = (u[q0:q0 + CHUNK, c0:c0 + LANES] * m0).astype(bf16)
                ymix_ref[q1:q1 + CHUNK, D_CONV + c0:D_CONV + c0 + LANES] = (u[q1:q1 + CHUNK, c0:c0 + LANES] * m1).astype(bf16)

    def front_out_a():
        resid_ref[...] = alpha * x_ref[...] + b_o_ref[...] + _dot(ymix_ref[:, 0:D_CONV], w_o_ref[0:D_CONV, :])

    def front_out_b():
        resid_ref[...] += _dot(ymix_ref[:, D_CONV:D_MODEL], w_o_ref[D_CONV:D_MODEL, :])

    front_load()
    x1b = back_norm()
    routing = back_route(x1b)
    front_conv()
    back_rank(*routing)
    gate_pre = front_gate_proj()
    front_out_a()
    front_gate(*gate_pre)
    front_out_b()


def _mixer(x, p, alpha):
    nb, s, d = x.shape
    ts = SEQ_TILE
    nt = s // ts
    hb = ts // HALO
    n_tiles = nb * nt
    n_total = n_tiles * ts

    front = lambda t: jnp.minimum(t, n_tiles - 1)
    back = lambda t: jnp.maximum(t - 1, 0)
    cur = lambda t: (front(t) // nt, front(t) % nt, 0)
    prv = lambda t: (front(t) // nt, jnp.maximum((front(t) % nt) * hb - 1, 0), 0)
    nxt = lambda t: (front(t) // nt, jnp.minimum((front(t) % nt + 1) * hb, s // HALO - 1), 0)
    full = lambda a: pl.BlockSpec(a.shape, lambda t: (0,) * a.ndim)
    in_specs = [pl.BlockSpec((None, ts, d), cur), pl.BlockSpec((None, HALO, d), prv),
                pl.BlockSpec((None, HALO, d), nxt)] + [full(a) for a in p]
    out_shape = (jax.ShapeDtypeStruct((n_total, d), jnp.float32),
                 jax.ShapeDtypeStruct((n_total, D_PACK), jnp.uint32),
                 jax.ShapeDtypeStruct((n_total, LANES), jnp.float32),
                 jax.ShapeDtypeStruct((SUBLANES, n_total), jnp.float32),
                 jax.ShapeDtypeStruct((SUBLANES, LANES), jnp.float32))
    row_blk = lambda t: (back(t), 0)
    out_specs = (pl.BlockSpec((ts, d), row_blk),
                 pl.BlockSpec((ts, D_PACK), row_blk),
                 pl.BlockSpec((ts, LANES), row_blk),
                 pl.BlockSpec((SUBLANES, ts), lambda t: (0, back(t))),
                 pl.BlockSpec((SUBLANES, LANES), lambda t: (0, 0)))
    return pl.pallas_call(
        functools.partial(_mixer_kernel, ts=ts, nt=nt, n_tiles=n_tiles, alpha=alpha),
        grid=(n_tiles + 1,),
        in_specs=in_specs,
        out_specs=out_specs,
        out_shape=out_shape,
        scratch_shapes=[pltpu.VMEM((ts + 2 * HALO, d), jnp.bfloat16),
                        pltpu.VMEM((ts, d), jnp.bfloat16),
                        pltpu.VMEM((ts, ts), jnp.bfloat16),
                        pltpu.VMEM((ts, d), jnp.float32),
                        pltpu.VMEM((1, LANES), jnp.float32)],
        compiler_params=pltpu.CompilerParams(dimension_semantics=("arbitrary",),
                                             vmem_limit_bytes=VMEM_LIMIT_BYTES),
        name="mixer",
    )(x, x, x, *p)


def _sc_mesh():
    return plsc.VectorSubcoreMesh(core_axis_name="c", subcore_axis_name="s",
                                  num_cores=SC_CORES, num_subcores=SC_SUBCORES)


def _sc_worker_base(per_worker):
    return (lax.axis_index("s") * SC_CORES + lax.axis_index("c")) * per_worker


def _sc_dispatch(pos1, pos2, xp, n_rows):
    n, dp = xp.shape
    per_worker = n // SC_WORKERS
    steps = per_worker // SC_ROWS

    def body(pos1_hbm, pos2_hbm, xp_hbm, xs_hbm, idx1_v, idx2_v, rows_v, sem):
        base = _sc_worker_base(per_worker)

        @pl.loop(0, steps)
        def _(k):
            off = pl.multiple_of(base + k * SC_ROWS, SC_ROWS)
            pltpu.sync_copy(pos1_hbm.at[pl.ds(off, SC_ROWS)], idx1_v)
            pltpu.sync_copy(pos2_hbm.at[pl.ds(off, SC_ROWS)], idx2_v)
            pltpu.sync_copy(xp_hbm.at[pl.ds(off, SC_ROWS)], rows_v)
            c1 = pltpu.async_copy(rows_v, xs_hbm.at[idx1_v], sem)
            c2 = pltpu.async_copy(rows_v, xs_hbm.at[idx2_v], sem)
            c1.wait()
            c2.wait()

    return pl.kernel(
        body,
        out_type=jax.ShapeDtypeStruct((n_rows, dp), xp.dtype),
        mesh=_sc_mesh(),
        scratch_types=[pltpu.VMEM((SC_ROWS,), jnp.int32), pltpu.VMEM((SC_ROWS,), jnp.int32),
                       pltpu.VMEM((SC_ROWS, dp), xp.dtype), pltpu.SemaphoreType.DMA],
        compiler_params=pltpu.CompilerParams(use_tc_tiling_on_sc=True),
        name="sc_dispatch",
    )(pos1, pos2, xp)


def _sc_gather(pos1, pos2, ys):
    n = pos1.shape[0]
    dp = ys.shape[1]
    per_worker = n // SC_WORKERS
    steps = per_worker // SC_ROWS

    def body(pos1_hbm, pos2_hbm, ys_hbm, out_hbm, idx_v, rows_v, sem):
        base = _sc_worker_base(per_worker)

        @pl.loop(0, steps)
        def _(k):
            off = pl.multiple_of(base + k * SC_ROWS, SC_ROWS)
            for slot, pos_hbm in enumerate((pos1_hbm, pos2_hbm)):
                pltpu.sync_copy(pos_hbm.at[pl.ds(off, SC_ROWS)], idx_v)
                pltpu.async_copy(ys_hbm.at[idx_v], rows_v, sem).wait()
                pltpu.sync_copy(rows_v, out_hbm.at[slot, pl.ds(off, SC_ROWS)])

    return pl.kernel(
        body,
        out_type=jax.ShapeDtypeStruct((2, n, dp), ys.dtype),
        mesh=_sc_mesh(),
        scratch_types=[pltpu.VMEM((SC_ROWS,), jnp.int32), pltpu.VMEM((SC_ROWS, dp), ys.dtype),
                       pltpu.SemaphoreType.DMA],
        compiler_params=pltpu.CompilerParams(use_tc_tiling_on_sc=True),
        name="sc_gather",
    )(pos1, pos2, ys)


def _expert_kernel(te_ref, first_ref, nsub_ref, slot_ref, next_ref, xs_ref, wg_hbm, wu_hbm, wd_hbm, ys_ref,
                   wg_buf, wu_buf, wd_buf, wgu_bf_ref, wd_bf_ref, sem):
    i = pl.program_id(0)
    bf16 = jnp.bfloat16
    sub = EXPERT_SUBTILE

    def weight_copies(expert, slot):
        return (pltpu.make_async_copy(wg_hbm.at[expert], wg_buf.at[slot], sem.at[slot]),
                pltpu.make_async_copy(wu_hbm.at[expert], wu_buf.at[slot], sem.at[slot]),
                pltpu.make_async_copy(wd_hbm.at[expert], wd_buf.at[slot], sem.at[slot]))

    @pl.when(first_ref[i] == 1)
    def _():
        slot = slot_ref[i]

        @pl.when(i == 0)
        def _():
            for c in weight_copies(te_ref[i], slot):
                c.start()

        for c in weight_copies(te_ref[i], slot):
            c.wait()

        @pl.when(next_ref[i] >= 0)
        def _():
            for c in weight_copies(next_ref[i], 1 - slot):
                c.start()

        wgu_bf_ref[:, 0:D_EXPERT] = wg_buf[slot].astype(bf16)
        wgu_bf_ref[:, D_EXPERT:2 * D_EXPERT] = wu_buf[slot].astype(bf16)
        wd_bf_ref[...] = wd_buf[slot].astype(bf16)

    def swiglu(r0):
        rows = pl.ds(r0, sub)
        lo, hi = _unpack_halves(xs_ref[rows, :])
        gu = (_dot(lo.astype(bf16), wgu_bf_ref[0:D_PACK, :])
              + _dot(hi.astype(bf16), wgu_bf_ref[D_PACK:D_MODEL, :]))
        g = gu[:, 0:D_EXPERT]
        h = g * jax.nn.sigmoid(g) * gu[:, D_EXPERT:2 * D_EXPERT]
        ys_ref[rows, :] = _pack_halves(_dot(h.astype(bf16), wd_bf_ref[...]))

    n_sub = nsub_ref[i]

    def pair(k, c):
        r0 = pl.multiple_of(k * (2 * sub), 2 * sub)
        swiglu(r0)
        swiglu(r0 + sub)
        return c

    lax.fori_loop(0, n_sub // 2, pair, 0)

    @pl.when(n_sub % 2 == 1)
    def _():
        swiglu(pl.multiple_of((n_sub - 1) * sub, sub))

    def clear(k, c):
        ys_ref[pl.ds(pl.multiple_of(k * sub, sub), sub), :] = jnp.zeros((sub, D_PACK), ys_ref.dtype)
        return c

    lax.fori_loop(n_sub, ys_ref.shape[0] // sub, clear, 0)


def _experts(tile_expert, tile_first, tile_nsub, tile_slot, tile_next, xs, w_gate, w_up, w_down):
    n_rows, dp = xs.shape
    d = D_MODEL
    tm = EXPERT_TILE
    grid_spec = pltpu.PrefetchScalarGridSpec(
        num_scalar_prefetch=5,
        grid=(n_rows // tm,),
        in_specs=[pl.BlockSpec((tm, dp), lambda i, *_: (i, 0)),
                  pl.BlockSpec(memory_space=pl.ANY),
                  pl.BlockSpec(memory_space=pl.ANY),
                  pl.BlockSpec(memory_space=pl.ANY)],
        out_specs=pl.BlockSpec((tm, dp), lambda i, *_: (i, 0)),
        scratch_shapes=[pltpu.VMEM((2, d, D_EXPERT), jnp.float32),
                        pltpu.VMEM((2, d, D_EXPERT), jnp.float32),
                        pltpu.VMEM((2, D_EXPERT, d), jnp.float32),
                        pltpu.VMEM((d, 2 * D_EXPERT), jnp.bfloat16),
                        pltpu.VMEM((D_EXPERT, d), jnp.bfloat16),
                        pltpu.SemaphoreType.DMA((2,))],
    )
    return pl.pallas_call(
        _expert_kernel,
        grid_spec=grid_spec,
        out_shape=jax.ShapeDtypeStruct((n_rows, dp), jnp.uint32),
        compiler_params=pltpu.CompilerParams(dimension_semantics=("arbitrary",),
                                             vmem_limit_bytes=VMEM_LIMIT_BYTES),
        name="experts",
    )(tile_expert, tile_first, tile_nsub, tile_slot, tile_next, xs, w_gate, w_up, w_down)


def _combine_kernel(x1_ref, slab_ref, y1_ref, y2_ref, g_ref, b_ref, out_ref, *, alpha):
    slab = slab_ref[...]
    w1 = slab[:, COL_W1:COL_W1 + 1]
    w2 = slab[:, COL_W2:COL_W2 + 1]
    lo1, hi1 = _unpack_halves(y1_ref[...])
    lo2, hi2 = _unpack_halves(y2_ref[...])
    moe = jnp.concatenate([w1 * lo1 + w2 * lo2, w1 * hi1 + w2 * hi2], axis=1)
    out_ref[...] = _layer_norm(alpha * x1_ref[...] + moe, g_ref[...], b_ref[...])


def _combine(x1, slab, yg, g, b, alpha):
    n, d = x1.shape
    tt = TOKEN_TILE
    return pl.pallas_call(
        functools.partial(_combine_kernel, alpha=alpha),
        grid=(n // tt,),
        in_specs=[pl.BlockSpec((tt, d), lambda i: (i, 0)),
                  pl.BlockSpec((tt, LANES), lambda i: (i, 0)),
                  pl.BlockSpec((None, tt, D_PACK), lambda i: (0, i, 0)),
                  pl.BlockSpec((None, tt, D_PACK), lambda i: (1, i, 0)),
                  pl.BlockSpec((1, d), lambda i: (0, 0)),
                  pl.BlockSpec((1, d), lambda i: (0, 0))],
        out_specs=pl.BlockSpec((tt, d), lambda i: (i, 0)),
        out_shape=jax.ShapeDtypeStruct((n, d), jnp.float32),
        compiler_params=pltpu.CompilerParams(dimension_semantics=("arbitrary",),
                                             vmem_limit_bytes=VMEM_LIMIT_BYTES),
        name="combine",
    )(x1, slab, yg, yg, g, b)


def _row(a):
    return a.reshape(1, -1)


def _mixer_params(w_in, b_in, conv_w, conv_b, sg_ln_g, sg_ln_b, w_s, b_s, w_o, b_o, ln1_g, ln1_b,
                  w_rc, b_rc, w_rf, b_rf):
    d = D_MODEL
    bf16 = jnp.bfloat16
    row = _row
    ws_pairs = w_s.reshape(N_SG_HEADS // 2, 2 * CHUNK, CHUNK).astype(bf16)
    bs_full = jnp.repeat(b_s.T, SG_HEAD_DIM, axis=1)
    w_r = jnp.zeros((d, LANES), jnp.float32).at[:, 0:N_EXPERTS].set(w_rf)
    w_r = w_r.at[:, COARSE_OFF:COARSE_OFF + N_EXPERT_GROUPS].set(w_rc).astype(bf16)
    b_r = jnp.zeros((1, LANES), jnp.float32).at[0, 0:N_EXPERTS].set(b_rf)
    b_r = b_r.at[0, COARSE_OFF:COARSE_OFF + N_EXPERT_GROUPS].set(b_rc)
    return (w_in.astype(bf16), row(b_in), conv_w, row(conv_b), row(sg_ln_g), row(sg_ln_b),
            ws_pairs, bs_full, w_o.astype(bf16), row(b_o), row(ln1_g), row(ln1_b), w_r, b_r)


def _encoder_layer(x, mixer_params, w_gate, w_up, w_down, ln2_g, ln2_b, alpha):
    n = x.shape[0] * x.shape[1]
    x1, xp, slab, route, carry = _mixer(x, mixer_params, alpha)

    tm = EXPERT_TILE
    n_tiles = 2 * n // tm + N_EXPERTS
    cnt = carry[0, 0:N_EXPERTS].astype(jnp.int32)
    tiles_e = (cnt + tm - 1) // tm
    tile_end = jnp.cumsum(tiles_e)
    tile_start = tile_end - tiles_e
    row_off = tile_start * tm
    experts = jnp.arange(N_EXPERTS, dtype=jnp.int32)[:, None]

    def sorted_pos(e_row, r_row):
        e = route[e_row].astype(jnp.int32)
        return jnp.sum(jnp.where(e[None, :] == experts, row_off[:, None], 0), axis=0) + route[r_row].astype(jnp.int32)

    pos1 = sorted_pos(COL_E1, COL_R1)
    pos2 = sorted_pos(COL_E2, COL_R2)
    tile_ids = jnp.arange(n_tiles, dtype=jnp.int32)
    n_used = tile_end[N_EXPERTS - 1:N_EXPERTS]
    tile_expert = jnp.sum((tile_end[None, :] <= tile_ids[:, None]).astype(jnp.int32), axis=1)
    last_used = jnp.sum((tile_end <= n_used[0] - 1).astype(jnp.int32))
    tile_expert = jnp.minimum(tile_expert, last_used)
    tile_first = jnp.any(tile_ids[:, None] == tile_start[None, :], axis=1) & (tile_ids < n_used[0])
    tile_first = tile_first.astype(jnp.int32).at[0].set(1)
    rows_left = jnp.sum(jnp.where(tile_expert[:, None] == experts.T, (row_off + cnt)[None, :], 0), axis=1) - tile_ids * tm
    tile_nsub = jnp.where(tile_ids < n_used[0], (jnp.clip(rows_left, 0, tm) + EXPERT_SUBTILE - 1) // EXPERT_SUBTILE, 0)

    used = tiles_e > 0
    ordinal = jnp.cumsum(used.astype(jnp.int32)) - 1
    e_ids = experts[:, 0]
    later_used = used[None, :] & (e_ids[None, :] > e_ids[:, None])
    next_used = jnp.min(jnp.where(later_used, e_ids[None, :], N_EXPERTS), axis=1)
    next_used = jnp.where(next_used < N_EXPERTS, next_used, -1)
    tile_slot = ordinal[tile_expert] % 2
    tile_next = next_used[tile_expert]

    xs = _sc_dispatch(pos1, pos2, xp, n_tiles * tm)
    ys = _experts(tile_expert, tile_first, tile_nsub.astype(jnp.int32), tile_slot.astype(jnp.int32),
                  tile_next.astype(jnp.int32), xs, w_gate, w_up, w_down)
    yg = _sc_gather(pos1, pos2, ys)
    return _combine(x1, slab, yg, _row(ln2_g), _row(ln2_b), alpha).reshape(x.shape)


def kernel(x_prompt, x_sample, w_in, b_in, conv_w, conv_b, sg_ln_g, sg_ln_b, w_s, b_s, w_o, b_o, ln1_g, ln1_b, w_rc, b_rc, w_rf, b_rf, w_gate, w_up, w_down, ln2_g, ln2_b):
    depth = w_in.shape[0]
    alpha = (2.0 * depth) ** 0.25
    xs = (x_prompt, x_sample)
    for l in range(depth):
        mixer_params = _mixer_params(w_in[l], b_in[l], conv_w[l], conv_b[l], sg_ln_g[l], sg_ln_b[l], w_s[l], b_s[l],
                                     w_o[l], b_o[l], ln1_g[l], ln1_b[l], w_rc[l], b_rc[l], w_rf[l], b_rf[l])
        xs = tuple(_encoder_layer(x, mixer_params, w_gate[l], w_up[l], w_down[l], ln2_g[l], ln2_b[l], alpha)
                   for x in xs)
    return xs
```

```python
import functools

import jax
import jax.numpy as jnp
from jax import lax
from jax.experimental import pallas as pl
from jax.experimental.pallas import tpu as pltpu
from jax.experimental.pallas import tpu_sc as plsc

D_MODEL = 1024
D_CONV = 512
D_SG = 512
N_SG_HEADS = 8
SG_HEAD_DIM = D_SG // N_SG_HEADS
CHUNK = 128
N_EXPERT_GROUPS = 4
EXPERTS_PER_GROUP = 8
N_EXPERTS = N_EXPERT_GROUPS * EXPERTS_PER_GROUP
D_EXPERT = 512
LN_EPS = 1e-5
D_PACK = D_MODEL // 2

LANES = 128
SUBLANES = 8
HALO = 16
SEQ_TILE = 512
EXPERT_TILE = 512
EXPERT_SUBTILE = 256
TOKEN_TILE = 1024
SC_CORES = 2
SC_SUBCORES = 16
SC_WORKERS = SC_CORES * SC_SUBCORES
SC_ROWS = 64
VMEM_LIMIT_BYTES = 56 * 1024 * 1024

COL_E1, COL_E2, COL_W1, COL_W2, COL_R1, COL_R2 = 0, 1, 2, 3, 4, 5
COARSE_OFF = N_EXPERTS


def _dot(a, b):
    return jnp.dot(a, b, preferred_element_type=jnp.float32)


def _gelu_tanh(x):
    return 0.5 * x * (1.0 + jnp.tanh(0.7978845608028654 * (x + 0.044715 * (x * x * x))))


def _layer_norm(x, g, b):
    mu = jnp.mean(x, axis=-1, keepdims=True)
    xc = x - mu
    var = jnp.mean(xc * xc, axis=-1, keepdims=True)
    return xc * lax.rsqrt(var + LN_EPS) * g + b


def _pack_halves(x):
    u32 = jnp.uint32
    lo = lax.bitcast_convert_type(x[:, 0:D_PACK].astype(jnp.bfloat16).astype(jnp.float32), u32)
    hi = lax.bitcast_convert_type(x[:, D_PACK:D_MODEL].astype(jnp.bfloat16).astype(jnp.float32), u32)
    return (hi & u32(0xFFFF0000)) | (lo >> u32(16))


def _unpack_halves(w):
    u32 = jnp.uint32
    lo = lax.bitcast_convert_type(w << u32(16), jnp.float32)
    hi = lax.bitcast_convert_type(w & u32(0xFFFF0000), jnp.float32)
    return lo, hi


def _mixer_kernel(x_ref, x_prev_ref, x_next_ref, w_in_ref, b_in_ref, conv_w_ref, conv_b_ref,
                  sg_g_ref, sg_b_ref, ws_ref, bs_ref, w_o_ref, b_o_ref, ln1_g_ref, ln1_b_ref,
                  w_r_ref, b_r_ref,
                  x1_ref, xp_ref, slab_ref, route_ref, counts_ref,
                  xb_ref, ymix_ref, tri_ref, resid_ref, carry_ref, *, ts, nt, n_tiles, alpha):
    t = pl.program_id(0)
    j = jnp.minimum(t, n_tiles - 1) % nt
    bf16 = jnp.bfloat16

    @pl.when(t == 0)
    def _():
        carry_ref[...] = jnp.zeros_like(carry_ref)
        resid_ref[...] = jnp.zeros_like(resid_ref)
        ri = lax.broadcasted_iota(jnp.int32, (ts, ts), 0)
        ci = lax.broadcasted_iota(jnp.int32, (ts, ts), 1)
        tri_ref[...] = jnp.where(ri > ci, 1.0, 0.0).astype(bf16)

    lane = lax.broadcasted_iota(jnp.int32, (ts, LANES), 1)
    lane_f = lane.astype(jnp.float32)
    neg = jnp.float32(-jnp.inf)
    big = jnp.float32(1e9)
    is_c = (lane >= COARSE_OFF) & (lane < COARSE_OFF + N_EXPERT_GROUPS)

    def proj(lhs, lo, hi):
        return _dot(lhs, w_in_ref[:, lo:hi]) + b_in_ref[:, lo:hi]

    def back_norm():
        x1 = _layer_norm(resid_ref[...], ln1_g_ref[...], ln1_b_ref[...])
        x1_ref[...] = x1
        xp_ref[...] = _pack_halves(x1)
        return x1.astype(bf16)

    def back_route(x1b):
        logits = _dot(x1b, w_r_ref[...]) + b_r_ref[...]
        lc = jnp.where(is_c, logits, neg)
        mx = jnp.max(lc, axis=1, keepdims=True)
        grp = jnp.min(jnp.where(lc == mx, lane_f - COARSE_OFF, big), axis=1, keepdims=True)
        p_grp = 1.0 / jnp.sum(jnp.where(is_c, jnp.exp(logits - mx), 0.0), axis=1, keepdims=True)
        grp_lo = grp * EXPERTS_PER_GROUP
        in_grp = (lane_f >= grp_lo) & (lane_f < grp_lo + EXPERTS_PER_GROUP)
        lf = jnp.where(in_grp, logits, neg)
        v1 = jnp.max(lf, axis=1, keepdims=True)
        e1 = jnp.min(jnp.where(lf == v1, lane_f, big), axis=1, keepdims=True)
        lf2 = jnp.where(lane_f == e1, neg, lf)
        v2 = jnp.max(lf2, axis=1, keepdims=True)
        e2 = jnp.min(jnp.where(lf2 == v2, lane_f, big), axis=1, keepdims=True)
        a = jnp.exp(v2 - v1)
        return e1, e2, p_grp / (1.0 + a), p_grp * a / (1.0 + a)

    def back_rank(e1, e2, w1, w2):
        hit1 = lane_f == e1
        hit2 = lane_f == e2
        onehot = jnp.where(hit1 | hit2, 1.0, 0.0)
        carry = carry_ref[...]
        before = _dot(tri_ref[...], onehot.astype(bf16)) + carry
        r1 = jnp.sum(jnp.where(hit1, before, 0.0), axis=1, keepdims=True)
        r2 = jnp.sum(jnp.where(hit2, before, 0.0), axis=1, keepdims=True)
        carry = jnp.where(t > 0, carry + jnp.sum(onehot, axis=0, keepdims=True), carry)
        carry_ref[...] = carry
        counts_ref[...] = jnp.broadcast_to(carry, counts_ref.shape)
        slab = jnp.where(lane == COL_E1, e1, 0.0)
        slab = jnp.where(lane == COL_E2, e2, slab)
        slab = jnp.where(lane == COL_W1, w1, slab)
        slab = jnp.where(lane == COL_W2, w2, slab)
        slab = jnp.where(lane == COL_R1, r1, slab)
        slab = jnp.where(lane == COL_R2, r2, slab)
        slab_ref[...] = slab
        route_ref[...] = slab.T[0:SUBLANES, :]

    def front_load():
        xb_ref[0:HALO, :] = x_prev_ref[...].astype(bf16)
        xb_ref[HALO:HALO + ts, :] = x_ref[...].astype(bf16)
        xb_ref[HALO + ts:HALO + ts + HALO, :] = x_next_ref[...].astype(bf16)

    def front_conv():
        xe = xb_ref[...]
        g_e = proj(xe, 0, D_CONV) * proj(xe, 2 * D_CONV, 3 * D_CONV)
        row_e = lax.broadcasted_iota(jnp.int32, (ts + 2 * HALO, 1), 0)
        has_prev = jnp.where(j > 0, 1.0, 0.0)
        has_next = jnp.where(j < nt - 1, 1.0, 0.0)
        g_e = g_e * jnp.where(row_e < HALO, has_prev, jnp.where(row_e >= HALO + ts, has_next, 1.0))
        conv = (g_e[HALO - 1:HALO - 1 + ts, :] * conv_w_ref[0:1, :] + g_e[HALO:HALO + ts, :] * conv_w_ref[1:2, :]
                + g_e[HALO + 1:HALO + 1 + ts, :] * conv_w_ref[2:3, :] + conv_b_ref[...])
        y_a = proj(xb_ref[HALO:HALO + ts, :], D_CONV, 2 * D_CONV) * conv
        ymix_ref[:, 0:D_CONV] = y_a.astype(bf16)

    def front_gate_proj():
        xm = xb_ref[HALO:HALO + ts, :]
        return proj(xm, 3 * D_CONV, 3 * D_CONV + D_SG), proj(xm, 3 * D_CONV + D_SG, 3 * D_CONV + 2 * D_SG)

    def front_gate(u_pre, v_pre):
        u = _gelu_tanh(u_pre)
        v = _gelu_tanh(v_pre)
        v_ln = _layer_norm(v, sg_g_ref[...], sg_b_ref[...]).astype(bf16)
        first_head = lax.broadcasted_iota(jnp.int32, (CHUNK, LANES), 1) < SG_HEAD_DIM
        for q0 in range(0, ts, 2 * CHUNK):
            q1 = q0 + CHUNK
            for hp in range(N_SG_HEADS // 2):
                c0 = hp * LANES
                rhs = jnp.concatenate([v_ln[q0:q0 + CHUNK, c0:c0 + LANES], v_ln[q1:q1 + CHUNK, c0:c0 + LANES]], axis=1)
                res = _dot(ws_ref[hp], rhs)
                bias = bs_ref[:, c0:c0 + LANES]
                m0 = jnp.where(first_head, res[0:CHUNK, 0:LANES], res[CHUNK:2 * CHUNK, 0:LANES]) + bias
                m1 = jnp.where(first_head, res[0:CHUNK, LANES:2 * LANES], res[CHUNK:2 * CHUNK, LANES:2 * LANES]) + bias
                ymix_ref[q0:q0 + CHUNK, D_CONV + c0:D_CONV + c0 + LANES] = (u[q0:q0 + CHUNK, c0:c0 + LANES] * m0).astype(bf16)
                ymix_ref[q1:q1 + CHUNK, D_CONV + c0:D_CONV + c0 + LANES] = (u[q1:q1 + CHUNK, c0:c0 + LANES] * m1).astype(bf16)

    def front_out_a():
        resid_ref[...] = alpha * x_ref[...] + b_o_ref[...] + _dot(ymix_ref[:, 0:D_CONV], w_o_ref[0:D_CONV, :])

    def front_out_b():
        resid_ref[...] += _dot(ymix_ref[:, D_CONV:D_MODEL], w_o_ref[D_CONV:D_MODEL, :])

    front_load()
    x1b = back_norm()
    routing = back_route(x1b)
    front_conv()
    back_rank(*routing)
    gate_pre = front_gate_proj()
    front_out_a()
    front_gate(*gate_pre)
    front_out_b()


def _mixer(x, p, alpha):
    nb, s, d = x.shape
    ts = SEQ_TILE
    nt = s // ts
    hb = ts // HALO
    n_tiles = nb * nt
    n_total = n_tiles * ts

    front = lambda t: jnp.minimum(t, n_tiles - 1)
    back = lambda t: jnp.maximum(t - 1, 0)
    cur = lambda t: (front(t) // nt, front(t) % nt, 0)
    prv = lambda t: (front(t) // nt, jnp.maximum((front(t) % nt) * hb - 1, 0), 0)
    nxt = lambda t: (front(t) // nt, jnp.minimum((front(t) % nt + 1) * hb, s // HALO - 1), 0)
    full = lambda a: pl.BlockSpec(a.shape, lambda t: (0,) * a.ndim)
    in_specs = [pl.BlockSpec((None, ts, d), cur), pl.BlockSpec((None, HALO, d), prv),
                pl.BlockSpec((None, HALO, d), nxt)] + [full(a) for a in p]
    out_shape = (jax.ShapeDtypeStruct((n_total, d), jnp.float32),
                 jax.ShapeDtypeStruct((n_total, D_PACK), jnp.uint32),
                 jax.ShapeDtypeStruct((n_total, LANES), jnp.float32),
                 jax.ShapeDtypeStruct((SUBLANES, n_total), jnp.float32),
                 jax.ShapeDtypeStruct((SUBLANES, LANES), jnp.float32))
    row_blk = lambda t: (back(t), 0)
    out_specs = (pl.BlockSpec((ts, d), row_blk),
                 pl.BlockSpec((ts, D_PACK), row_blk),
                 pl.BlockSpec((ts, LANES), row_blk),
                 pl.BlockSpec((SUBLANES, ts), lambda t: (0, back(t))),
                 pl.BlockSpec((SUBLANES, LANES), lambda t: (0, 0)))
    return pl.pallas_call(
        functools.partial(_mixer_kernel, ts=ts, nt=nt, n_tiles=n_tiles, alpha=alpha),
        grid=(n_tiles + 1,),
        in_specs=in_specs,
        out_specs=out_specs,
        out_shape=out_shape,
        scratch_shapes=[pltpu.VMEM((ts + 2 * HALO, d), jnp.bfloat16),
                        pltpu.VMEM((ts, d), jnp.bfloat16),
                        pltpu.VMEM((ts, ts), jnp.bfloat16),
                        pltpu.VMEM((ts, d), jnp.float32),
                        pltpu.VMEM((1, LANES), jnp.float32)],
        compiler_params=pltpu.CompilerParams(dimension_semantics=("arbitrary",),
                                             vmem_limit_bytes=VMEM_LIMIT_BYTES),
        name="mixer",
    )(x, x, x, *p)


def _sc_mesh():
    return plsc.VectorSubcoreMesh(core_axis_name="c", subcore_axis_name="s",
                                  num_cores=SC_CORES, num_subcores=SC_SUBCORES)


def _sc_worker_base(per_worker):
    return (lax.axis_index("s") * SC_CORES + lax.axis_index("c")) * per_worker


def _sc_dispatch(pos1, pos2, xp, n_rows):
    n, dp = xp.shape
    per_worker = n // SC_WORKERS
    steps = per_worker // SC_ROWS

    def body(pos1_hbm, pos2_hbm, xp_hbm, xs_hbm, idx1_v, idx2_v, rows_v, sem):
        base = _sc_worker_base(per_worker)

        @pl.loop(0, steps)
        def _(k):
            off = pl.multiple_of(base + k * SC_ROWS, SC_ROWS)
            pltpu.sync_copy(pos1_hbm.at[pl.ds(off, SC_ROWS)], idx1_v)
            pltpu.sync_copy(pos2_hbm.at[pl.ds(off, SC_ROWS)], idx2_v)
            pltpu.sync_copy(xp_hbm.at[pl.ds(off, SC_ROWS)], rows_v)
            c1 = pltpu.async_copy(rows_v, xs_hbm.at[idx1_v], sem)
            c2 = pltpu.async_copy(rows_v, xs_hbm.at[idx2_v], sem)
            c1.wait()
            c2.wait()

    return pl.kernel(
        body,
        out_type=jax.ShapeDtypeStruct((n_rows, dp), xp.dtype),
        mesh=_sc_mesh(),
        scratch_types=[pltpu.VMEM((SC_ROWS,), jnp.int32), pltpu.VMEM((SC_ROWS,), jnp.int32),
                       pltpu.VMEM((SC_ROWS, dp), xp.dtype), pltpu.SemaphoreType.DMA],
        compiler_params=pltpu.CompilerParams(use_tc_tiling_on_sc=True),
        name="sc_dispatch",
    )(pos1, pos2, xp)


def _sc_gather(pos1, pos2, ys):
    n = pos1.shape[0]
    dp = ys.shape[1]
    per_worker = n // SC_WORKERS
    steps = per_worker // SC_ROWS

    def body(pos1_hbm, pos2_hbm, ys_hbm, out_hbm, idx_v, rows_v, sem):
        base = _sc_worker_base(per_worker)

        @pl.loop(0, steps)
        def _(k):
            off = pl.multiple_of(base + k * SC_ROWS, SC_ROWS)
            for slot, pos_hbm in enumerate((pos1_hbm, pos2_hbm)):
                pltpu.sync_copy(pos_hbm.at[pl.ds(off, SC_ROWS)], idx_v)
                pltpu.async_copy(ys_hbm.at[idx_v], rows_v, sem).wait()
                pltpu.sync_copy(rows_v, out_hbm.at[slot, pl.ds(off, SC_ROWS)])

    return pl.kernel(
        body,
        out_type=jax.ShapeDtypeStruct((2, n, dp), ys.dtype),
        mesh=_sc_mesh(),
        scratch_types=[pltpu.VMEM((SC_ROWS,), jnp.int32), pltpu.VMEM((SC_ROWS, dp), ys.dtype),
                       pltpu.SemaphoreType.DMA],
        compiler_params=pltpu.CompilerParams(use_tc_tiling_on_sc=True),
        name="sc_gather",
    )(pos1, pos2, ys)


def _expert_kernel(te_ref, first_ref, nsub_ref, slot_ref, next_ref, xs_ref, wg_hbm, wu_hbm, wd_hbm, ys_ref,
                   wg_buf, wu_buf, wd_buf, wgu_bf_ref, wd_bf_ref, sem):
    i = pl.program_id(0)
    bf16 = jnp.bfloat16
    sub = EXPERT_SUBTILE

    def weight_copies(expert, slot):
        return (pltpu.make_async_copy(wg_hbm.at[expert], wg_buf.at[slot], sem.at[slot]),
                pltpu.make_async_copy(wu_hbm.at[expert], wu_buf.at[slot], sem.at[slot]),
                pltpu.make_async_copy(wd_hbm.at[expert], wd_buf.at[slot], sem.at[slot]))

    @pl.when(first_ref[i] == 1)
    def _():
        slot = slot_ref[i]

        @pl.when(i == 0)
        def _():
            for c in weight_copies(te_ref[i], slot):
                c.start()

        for c in weight_copies(te_ref[i], slot):
            c.wait()

        @pl.when(next_ref[i] >= 0)
        def _():
            for c in weight_copies(next_ref[i], 1 - slot):
                c.start()

        wgu_bf_ref[:, 0:D_EXPERT] = wg_buf[slot].astype(bf16)
        wgu_bf_ref[:, D_EXPERT:2 * D_EXPERT] = wu_buf[slot].astype(bf16)
        wd_bf_ref[...] = wd_buf[slot].astype(bf16)

    def swiglu(r0):
        rows = pl.ds(r0, sub)
        lo, hi = _unpack_halves(xs_ref[rows, :])
        gu = (_dot(lo.astype(bf16), wgu_bf_ref[0:D_PACK, :])
              + _dot(hi.astype(bf16), wgu_bf_ref[D_PACK:D_MODEL, :]))
        g = gu[:, 0:D_EXPERT]
        h = g * jax.nn.sigmoid(g) * gu[:, D_EXPERT:2 * D_EXPERT]
        ys_ref[rows, :] = _pack_halves(_dot(h.astype(bf16), wd_bf_ref[...]))

    n_sub = nsub_ref[i]

    def pair(k, c):
        r0 = pl.multiple_of(k * (2 * sub), 2 * sub)
        swiglu(r0)
        swiglu(r0 + sub)
        return c

    lax.fori_loop(0, n_sub // 2, pair, 0)

    @pl.when(n_sub % 2 == 1)
    def _():
        swiglu(pl.multiple_of((n_sub - 1) * sub, sub))

    def clear(k, c):
        ys_ref[pl.ds(pl.multiple_of(k * sub, sub), sub), :] = jnp.zeros((sub, D_PACK), ys_ref.dtype)
        return c

    lax.fori_loop(n_sub, ys_ref.shape[0] // sub, clear, 0)


def _experts(tile_expert, tile_first, tile_nsub, tile_slot, tile_next, xs, w_gate, w_up, w_down):
    n_rows, dp = xs.shape
    d = D_MODEL
    tm = EXPERT_TILE
    grid_spec = pltpu.PrefetchScalarGridSpec(
        num_scalar_prefetch=5,
        grid=(n_rows // tm,),
        in_specs=[pl.BlockSpec((tm, dp), lambda i, *_: (i, 0)),
                  pl.BlockSpec(memory_space=pl.ANY),
                  pl.BlockSpec(memory_space=pl.ANY),
                  pl.BlockSpec(memory_space=pl.ANY)],
        out_specs=pl.BlockSpec((tm, dp), lambda i, *_: (i, 0)),
        scratch_shapes=[pltpu.VMEM((2, d, D_EXPERT), jnp.float32),
                        pltpu.VMEM((2, d, D_EXPERT), jnp.float32),
                        pltpu.VMEM((2, D_EXPERT, d), jnp.float32),
                        pltpu.VMEM((d, 2 * D_EXPERT), jnp.bfloat16),
                        pltpu.VMEM((D_EXPERT, d), jnp.bfloat16),
                        pltpu.SemaphoreType.DMA((2,))],
    )
    return pl.pallas_call(
        _expert_kernel,
        grid_spec=grid_spec,
        out_shape=jax.ShapeDtypeStruct((n_rows, dp), jnp.uint32),
        compiler_params=pltpu.CompilerParams(dimension_semantics=("arbitrary",),
                                             vmem_limit_bytes=VMEM_LIMIT_BYTES),
        name="experts",
    )(tile_expert, tile_first, tile_nsub, tile_slot, tile_next, xs, w_gate, w_up, w_down)


def _combine_kernel(x1_ref, slab_ref, y1_ref, y2_ref, g_ref, b_ref, out_ref, *, alpha):
    slab = slab_ref[...]
    w1 = slab[:, COL_W1:COL_W1 + 1]
    w2 = slab[:, COL_W2:COL_W2 + 1]
    lo1, hi1 = _unpack_halves(y1_ref[...])
    lo2, hi2 = _unpack_halves(y2_ref[...])
    moe = jnp.concatenate([w1 * lo1 + w2 * lo2, w1 * hi1 + w2 * hi2], axis=1)
    out_ref[...] = _layer_norm(alpha * x1_ref[...] + moe, g_ref[...], b_ref[...])


def _combine(x1, slab, yg, g, b, alpha):
    n, d = x1.shape
    tt = TOKEN_TILE
    return pl.pallas_call(
        functools.partial(_combine_kernel, alpha=alpha),
        grid=(n // tt,),
        in_specs=[pl.BlockSpec((tt, d), lambda i: (i, 0)),
                  pl.BlockSpec((tt, LANES), lambda i: (i, 0)),
                  pl.BlockSpec((None, tt, D_PACK), lambda i: (0, i, 0)),
                  pl.BlockSpec((None, tt, D_PACK), lambda i: (1, i, 0)),
                  pl.BlockSpec((1, d), lambda i: (0, 0)),
                  pl.BlockSpec((1, d), lambda i: (0, 0))],
        out_specs=pl.BlockSpec((tt, d), lambda i: (i, 0)),
        out_shape=jax.ShapeDtypeStruct((n, d), jnp.float32),
        compiler_params=pltpu.CompilerParams(dimension_semantics=("arbitrary",),
                                             vmem_limit_bytes=VMEM_LIMIT_BYTES),
        name="combine",
    )(x1, slab, yg, yg, g, b)


def _row(a):
    return a.reshape(1, -1)


def _mixer_params(w_in, b_in, conv_w, conv_b, sg_ln_g, sg_ln_b, w_s, b_s, w_o, b_o, ln1_g, ln1_b,
                  w_rc, b_rc, w_rf, b_rf):
    d = D_MODEL
    bf16 = jnp.bfloat16
    row = _row
    ws_pairs = w_s.reshape(N_SG_HEADS // 2, 2 * CHUNK, CHUNK).astype(bf16)
    bs_full = jnp.repeat(b_s.T, SG_HEAD_DIM, axis=1)
    w_r = jnp.zeros((d, LANES), jnp.float32).at[:, 0:N_EXPERTS].set(w_rf)
    w_r = w_r.at[:, COARSE_OFF:COARSE_OFF + N_EXPERT_GROUPS].set(w_rc).astype(bf16)
    b_r = jnp.zeros((1, LANES), jnp.float32).at[0, 0:N_EXPERTS].set(b_rf)
    b_r = b_r.at[0, COARSE_OFF:COARSE_OFF + N_EXPERT_GROUPS].set(b_rc)
    return (w_in.astype(bf16), row(b_in), conv_w, row(conv_b), row(sg_ln_g), row(sg_ln_b),
            ws_pairs, bs_full, w_o.astype(bf16), row(b_o), row(ln1_g), row(ln1_b), w_r, b_r)


def _encoder_layer(x, mixer_params, w_gate, w_up, w_down, ln2_g, ln2_b, alpha):
    n = x.shape[0] * x.shape[1]
    x1, xp, slab, route, carry = _mixer(x, mixer_params, alpha)

    tm = EXPERT_TILE
    n_tiles = 2 * n // tm + N_EXPERTS
    cnt = carry[0, 0:N_EXPERTS].astype(jnp.int32)
    tiles_e = (cnt + tm - 1) // tm
    tile_end = jnp.cumsum(tiles_e)
    tile_start = tile_end - tiles_e
    row_off = tile_start * tm
    experts = jnp.arange(N_EXPERTS, dtype=jnp.int32)[:, None]

    def sorted_pos(e_row, r_row):
        e = route[e_row].astype(jnp.int32)
        return jnp.sum(jnp.where(e[None, :] == experts, row_off[:, None], 0), axis=0) + route[r_row].astype(jnp.int32)

    pos1 = sorted_pos(COL_E1, COL_R1)
    pos2 = sorted_pos(COL_E2, COL_R2)
    tile_ids = jnp.arange(n_tiles, dtype=jnp.int32)
    n_used = tile_end[N_EXPERTS - 1:N_EXPERTS]
    tile_expert = jnp.sum((tile_end[None, :] <= tile_ids[:, None]).astype(jnp.int32), axis=1)
    last_used = jnp.sum((tile_end <= n_used[0] - 1).astype(jnp.int32))
    tile_expert = jnp.minimum(tile_expert, last_used)
    tile_first = jnp.any(tile_ids[:, None] == tile_start[None, :], axis=1) & (tile_ids < n_used[0])
    tile_first = tile_first.astype(jnp.int32).at[0].set(1)
    rows_left = jnp.sum(jnp.where(tile_expert[:, None] == experts.T, (row_off + cnt)[None, :], 0), axis=1) - tile_ids * tm
    tile_nsub = jnp.where(tile_ids < n_used[0], (jnp.clip(rows_left, 0, tm) + EXPERT_SUBTILE - 1) // EXPERT_SUBTILE, 0)

    used = tiles_e > 0
    ordinal = jnp.cumsum(used.astype(jnp.int32)) - 1
    e_ids = experts[:, 0]
    later_used = used[None, :] & (e_ids[None, :] > e_ids[:, None])
    next_used = jnp.min(jnp.where(later_used, e_ids[None, :], N_EXPERTS), axis=1)
    next_used = jnp.where(next_used < N_EXPERTS, next_used, -1)
    tile_slot = ordinal[tile_expert] % 2
    tile_next = next_used[tile_expert]

    xs = _sc_dispatch(pos1, pos2, xp, n_tiles * tm)
    ys = _experts(tile_expert, tile_first, tile_nsub.astype(jnp.int32), tile_slot.astype(jnp.int32),
                  tile_next.astype(jnp.int32), xs, w_gate, w_up, w_down)
    yg = _sc_gather(pos1, pos2, ys)
    return _combine(x1, slab, yg, _row(ln2_g), _row(ln2_b), alpha).reshape(x.shape)


def kernel(x_prompt, x_sample, w_in, b_in, conv_w, conv_b, sg_ln_g, sg_ln_b, w_s, b_s, w_o, b_o, ln1_g, ln1_b, w_rc, b_rc, w_rf, b_rf, w_gate, w_up, w_down, ln2_g, ln2_b):
    depth = w_in.shape[0]
    alpha = (2.0 * depth) ** 0.25
    xs = (x_prompt, x_sample)
    for l in range(depth):
        mixer_params = _mixer_params(w_in[l], b_in[l], conv_w[l], conv_b[l], sg_ln_g[l], sg_ln_b[l], w_s[l], b_s[l],
                                     w_o[l], b_o[l], ln1_g[l], ln1_b[l], w_rc[l], b_rc[l], w_rf[l], b_rf[l])
        xs = tuple(_encoder_layer(x, mixer_params, w_gate[l], w_up[l], w_down[l], ln2_g[l], ln2_b[l], alpha)
                   for x in xs)
    return xs
```

```python
import functools

import jax
import jax.numpy as jnp
from jax import lax
from jax.experimental import pallas as pl
from jax.experimental.pallas import tpu as pltpu
from jax.experimental.pallas import tpu_sc as plsc

D_MODEL = 1024
D_CONV = 512
D_SG = 512
N_SG_HEADS = 8
SG_HEAD_DIM = D_SG // N_SG_HEADS
CHUNK = 128
N_EXPERT_GROUPS = 4
EXPERTS_PER_GROUP = 8
N_EXPERTS = N_EXPERT_GROUPS * EXPERTS_PER_GROUP
D_EXPERT = 512
LN_EPS = 1e-5
D_PACK = D_MODEL // 2

LANES = 128
SUBLANES = 8
HALO = 16
SEQ_TILE = 512
EXPERT_TILE = 1024
EXPERT_SUBTILE = 256
TOKEN_TILE = 1024
SC_CORES = 2
SC_SUBCORES = 16
SC_WORKERS = SC_CORES * SC_SUBCORES
SC_ROWS = 64
VMEM_LIMIT_BYTES = 56 * 1024 * 1024

COL_E1, COL_E2, COL_W1, COL_W2, COL_R1, COL_R2 = 0, 1, 2, 3, 4, 5
COARSE_OFF = N_EXPERTS


def _dot(a, b):
    return jnp.dot(a, b, preferred_element_type=jnp.float32)


def _gelu_tanh(x):
    return 0.5 * x * (1.0 + jnp.tanh(0.7978845608028654 * (x + 0.044715 * (x * x * x))))


def _layer_norm(x, g, b):
    mu = jnp.mean(x, axis=-1, keepdims=True)
    xc = x - mu
    var = jnp.mean(xc * xc, axis=-1, keepdims=True)
    return xc * lax.rsqrt(var + LN_EPS) * g + b


def _pack_halves(x):
    u32 = jnp.uint32
    lo = lax.bitcast_convert_type(x[:, 0:D_PACK].astype(jnp.bfloat16).astype(jnp.float32), u32)
    hi = lax.bitcast_convert_type(x[:, D_PACK:D_MODEL].astype(jnp.bfloat16).astype(jnp.float32), u32)
    return (hi & u32(0xFFFF0000)) | (lo >> u32(16))


def _unpack_halves(w):
    u32 = jnp.uint32
    lo = lax.bitcast_convert_type(w << u32(16), jnp.float32)
    hi = lax.bitcast_convert_type(w & u32(0xFFFF0000), jnp.float32)
    return lo, hi


def _mixer_kernel(x_ref, x_prev_ref, x_next_ref, w_in_ref, b_in_ref, conv_w_ref, conv_b_ref,
                  sg_g_ref, sg_b_ref, ws_ref, bs_ref, w_o_ref, b_o_ref, ln1_g_ref, ln1_b_ref,
                  w_r_ref, b_r_ref,
                  x1_ref, xp_ref, slab_ref, route_ref, counts_ref,
                  xb_ref, ymix_ref, tri_ref, resid_ref, carry_ref, *, ts, nt, n_tiles, alpha):
    t = pl.program_id(0)
    j = jnp.minimum(t, n_tiles - 1) % nt
    bf16 = jnp.bfloat16

    @pl.when(t == 0)
    def _():
        carry_ref[...] = jnp.zeros_like(carry_ref)
        resid_ref[...] = jnp.zeros_like(resid_ref)
        ri = lax.broadcasted_iota(jnp.int32, (ts, ts), 0)
        ci = lax.broadcasted_iota(jnp.int32, (ts, ts), 1)
        tri_ref[...] = jnp.where(ri > ci, 1.0, 0.0).astype(bf16)

    lane = lax.broadcasted_iota(jnp.int32, (ts, LANES), 1)
    lane_f = lane.astype(jnp.float32)
    neg = jnp.float32(-jnp.inf)
    big = jnp.float32(1e9)
    is_c = (lane >= COARSE_OFF) & (lane < COARSE_OFF + N_EXPERT_GROUPS)

    def proj(lhs, lo, hi):
        return _dot(lhs, w_in_ref[:, lo:hi]) + b_in_ref[:, lo:hi]

    def back_norm():
        x1 = _layer_norm(resid_ref[...], ln1_g_ref[...], ln1_b_ref[...])
        x1_ref[...] = x1
        xp_ref[...] = _pack_halves(x1)
        return x1.astype(bf16)

    def back_route(x1b):
        logits = _dot(x1b, w_r_ref[...]) + b_r_ref[...]
        lc = jnp.where(is_c, logits, neg)
        mx = jnp.max(lc, axis=1, keepdims=True)
        grp = jnp.min(jnp.where(lc == mx, lane_f - COARSE_OFF, big), axis=1, keepdims=True)
        p_grp = 1.0 / jnp.sum(jnp.where(is_c, jnp.exp(logits - mx), 0.0), axis=1, keepdims=True)
        grp_lo = grp * EXPERTS_PER_GROUP
        in_grp = (lane_f >= grp_lo) & (lane_f < grp_lo + EXPERTS_PER_GROUP)
        lf = jnp.where(in_grp, logits, neg)
        v1 = jnp.max(lf, axis=1, keepdims=True)
        e1 = jnp.min(jnp.where(lf == v1, lane_f, big), axis=1, keepdims=True)
        lf2 = jnp.where(lane_f == e1, neg, lf)
        v2 = jnp.max(lf2, axis=1, keepdims=True)
        e2 = jnp.min(jnp.where(lf2 == v2, lane_f, big), axis=1, keepdims=True)
        a = jnp.exp(v2 - v1)
        return e1, e2, p_grp / (1.0 + a), p_grp * a / (1.0 + a)

    def back_rank(e1, e2, w1, w2):
        hit1 = lane_f == e1
        hit2 = lane_f == e2
        onehot = jnp.where(hit1 | hit2, 1.0, 0.0)
        carry = carry_ref[...]
        before = _dot(tri_ref[...], onehot.astype(bf16)) + carry
        r1 = jnp.sum(jnp.where(hit1, before, 0.0), axis=1, keepdims=True)
        r2 = jnp.sum(jnp.where(hit2, before, 0.0), axis=1, keepdims=True)
        carry = jnp.where(t > 0, carry + jnp.sum(onehot, axis=0, keepdims=True), carry)
        carry_ref[...] = carry
        counts_ref[...] = jnp.broadcast_to(carry, counts_ref.shape)
        slab = jnp.where(lane == COL_E1, e1, 0.0)
        slab = jnp.where(lane == COL_E2, e2, slab)
        slab = jnp.where(lane == COL_W1, w1, slab)
        slab = jnp.where(lane == COL_W2, w2, slab)
        slab = jnp.where(lane == COL_R1, r1, slab)
        slab = jnp.where(lane == COL_R2, r2, slab)
        slab_ref[...] = slab
        route_ref[...] = slab.T[0:SUBLANES, :]

    def front_load():
        xb_ref[0:HALO, :] = x_prev_ref[...].astype(bf16)
        xb_ref[HALO:HALO + ts, :] = x_ref[...].astype(bf16)
        xb_ref[HALO + ts:HALO + ts + HALO, :] = x_next_ref[...].astype(bf16)

    def front_conv():
        xe = xb_ref[...]
        g_e = proj(xe, 0, D_CONV) * proj(xe, 2 * D_CONV, 3 * D_CONV)
        row_e = lax.broadcasted_iota(jnp.int32, (ts + 2 * HALO, 1), 0)
        has_prev = jnp.where(j > 0, 1.0, 0.0)
        has_next = jnp.where(j < nt - 1, 1.0, 0.0)
        g_e = g_e * jnp.where(row_e < HALO, has_prev, jnp.where(row_e >= HALO + ts, has_next, 1.0))
        conv = (g_e[HALO - 1:HALO - 1 + ts, :] * conv_w_ref[0:1, :] + g_e[HALO:HALO + ts, :] * conv_w_ref[1:2, :]
                + g_e[HALO + 1:HALO + 1 + ts, :] * conv_w_ref[2:3, :] + conv_b_ref[...])
        y_a = proj(xb_ref[HALO:HALO + ts, :], D_CONV, 2 * D_CONV) * conv
        ymix_ref[:, 0:D_CONV] = y_a.astype(bf16)

    def front_gate_proj():
        xm = xb_ref[HALO:HALO + ts, :]
        return proj(xm, 3 * D_CONV, 3 * D_CONV + D_SG), proj(xm, 3 * D_CONV + D_SG, 3 * D_CONV + 2 * D_SG)

    def front_gate(u_pre, v_pre):
        u = _gelu_tanh(u_pre)
        v = _gelu_tanh(v_pre)
        v_ln = _layer_norm(v, sg_g_ref[...], sg_b_ref[...]).astype(bf16)
        first_head = lax.broadcasted_iota(jnp.int32, (CHUNK, LANES), 1) < SG_HEAD_DIM
        for q0 in range(0, ts, 2 * CHUNK):
            q1 = q0 + CHUNK
            for hp in range(N_SG_HEADS // 2):
                c0 = hp * LANES
                rhs = jnp.concatenate([v_ln[q0:q0 + CHUNK, c0:c0 + LANES], v_ln[q1:q1 + CHUNK, c0:c0 + LANES]], axis=1)
                res = _dot(ws_ref[hp], rhs)
                bias = bs_ref[:, c0:c0 + LANES]
                m0 = jnp.where(first_head, res[0:CHUNK, 0:LANES], res[CHUNK:2 * CHUNK, 0:LANES]) + bias
                m1 = jnp.where(first_head, res[0:CHUNK, LANES:2 * LANES], res[CHUNK:2 * CHUNK, LANES:2 * LANES]) + bias
                ymix_ref[q0:q0 + CHUNK, D_CONV + c0:D_CONV + c0 + LANES] = (u[q0:q0 + CHUNK, c0:c0 + LANES] * m0).astype(bf16)
                ymix_ref[q1:q1 + CHUNK, D_CONV + c0:D_CONV + c0 + LANES] = (u[q1:q1 + CHUNK, c0:c0 + LANES] * m1).astype(bf16)

    def front_out_a():
        resid_ref[...] = alpha * x_ref[...] + b_o_ref[...] + _dot(ymix_ref[:, 0:D_CONV], w_o_ref[0:D_CONV, :])

    def front_out_b():
        resid_ref[...] += _dot(ymix_ref[:, D_CONV:D_MODEL], w_o_ref[D_CONV:D_MODEL, :])

    front_load()
    x1b = back_norm()
    routing = back_route(x1b)
    front_conv()
    back_rank(*routing)
    gate_pre = front_gate_proj()
    front_out_a()
    front_gate(*gate_pre)
    front_out_b()


def _mixer(x, p, alpha):
    nb, s, d = x.shape
    ts = SEQ_TILE
    nt = s // ts
    hb = ts // HALO
    n_tiles = nb * nt
    n_total = n_tiles * ts

    front = lambda t: jnp.minimum(t, n_tiles - 1)
    back = lambda t: jnp.maximum(t - 1, 0)
    cur = lambda t: (front(t) // nt, front(t) % nt, 0)
    prv = lambda t: (front(t) // nt, jnp.maximum((front(t) % nt) * hb - 1, 0), 0)
    nxt = lambda t: (front(t) // nt, jnp.minimum((front(t) % nt + 1) * hb, s // HALO - 1), 0)
    full = lambda a: pl.BlockSpec(a.shape, lambda t: (0,) * a.ndim)
    in_specs = [pl.BlockSpec((None, ts, d), cur), pl.BlockSpec((None, HALO, d), prv),
                pl.BlockSpec((None, HALO, d), nxt)] + [full(a) for a in p]
    out_shape = (jax.ShapeDtypeStruct((n_total, d), jnp.float32),
                 jax.ShapeDtypeStruct((n_total, D_PACK), jnp.uint32),
                 jax.ShapeDtypeStruct((n_total, LANES), jnp.float32),
                 jax.ShapeDtypeStruct((SUBLANES, n_total), jnp.float32),
                 jax.ShapeDtypeStruct((SUBLANES, LANES), jnp.float32))
    row_blk = lambda t: (back(t), 0)
    out_specs = (pl.BlockSpec((ts, d), row_blk),
                 pl.BlockSpec((ts, D_PACK), row_blk),
                 pl.BlockSpec((ts, LANES), row_blk),
                 pl.BlockSpec((SUBLANES, ts), lambda t: (0, back(t))),
                 pl.BlockSpec((SUBLANES, LANES), lambda t: (0, 0)))
    return pl.pallas_call(
        functools.partial(_mixer_kernel, ts=ts, nt=nt, n_tiles=n_tiles, alpha=alpha),
        grid=(n_tiles + 1,),
        in_specs=in_specs,
        out_specs=out_specs,
        out_shape=out_shape,
        scratch_shapes=[pltpu.VMEM((ts + 2 * HALO, d), jnp.bfloat16),
                        pltpu.VMEM((ts, d), jnp.bfloat16),
                        pltpu.VMEM((ts, ts), jnp.bfloat16),
                        pltpu.VMEM((ts, d), jnp.float32),
                        pltpu.VMEM((1, LANES), jnp.float32)],
        compiler_params=pltpu.CompilerParams(dimension_semantics=("arbitrary",),
                                             vmem_limit_bytes=VMEM_LIMIT_BYTES),
        cost_estimate=pl.CostEstimate(
            flops=2 * n_total * (d * (3 * D_CONV + 2 * D_SG) + CHUNK * D_SG + d * d + d * LANES + ts * LANES),
            transcendentals=n_total * (2 * D_SG + 2 * LANES),
            bytes_accessed=n_total * (2 * d * 4 + D_PACK * 4 + LANES * 4 + SUBLANES * 4)),
        name="mixer",
    )(x, x, x, *p)


def _sc_mesh():
    return plsc.VectorSubcoreMesh(core_axis_name="c", subcore_axis_name="s",
                                  num_cores=SC_CORES, num_subcores=SC_SUBCORES)


def _sc_worker_base(per_worker):
    return (lax.axis_index("s") * SC_CORES + lax.axis_index("c")) * per_worker


def _sc_dispatch(pos1, pos2, xp, n_rows):
    n, dp = xp.shape
    per_worker = n // SC_WORKERS
    steps = per_worker // SC_ROWS

    def body(pos1_hbm, pos2_hbm, xp_hbm, xs_hbm, idx1_v, idx2_v, rows_v, sem):
        base = _sc_worker_base(per_worker)

        @pl.loop(0, steps)
        def _(k):
            off = pl.multiple_of(base + k * SC_ROWS, SC_ROWS)
            pltpu.sync_copy(pos1_hbm.at[pl.ds(off, SC_ROWS)], idx1_v)
            pltpu.sync_copy(pos2_hbm.at[pl.ds(off, SC_ROWS)], idx2_v)
            pltpu.sync_copy(xp_hbm.at[pl.ds(off, SC_ROWS)], rows_v)
            c1 = pltpu.async_copy(rows_v, xs_hbm.at[idx1_v], sem)
            c2 = pltpu.async_copy(rows_v, xs_hbm.at[idx2_v], sem)
            c1.wait()
            c2.wait()

    return pl.kernel(
        body,
        out_type=jax.ShapeDtypeStruct((n_rows, dp), xp.dtype),
        mesh=_sc_mesh(),
        scratch_types=[pltpu.VMEM((SC_ROWS,), jnp.int32), pltpu.VMEM((SC_ROWS,), jnp.int32),
                       pltpu.VMEM((SC_ROWS, dp), xp.dtype), pltpu.SemaphoreType.DMA],
        compiler_params=pltpu.CompilerParams(use_tc_tiling_on_sc=True),
        cost_estimate=pl.CostEstimate(flops=0, transcendentals=0, bytes_accessed=3 * n * dp * 4 + 2 * n * 4),
        name="sc_dispatch",
    )(pos1, pos2, xp)


def _sc_gather(pos1, pos2, ys):
    n = pos1.shape[0]
    dp = ys.shape[1]
    per_worker = n // SC_WORKERS
    steps = per_worker // SC_ROWS

    def body(pos1_hbm, pos2_hbm, ys_hbm, out_hbm, idx_v, rows_v, sem):
        base = _sc_worker_base(per_worker)

        @pl.loop(0, steps)
        def _(k):
            off = pl.multiple_of(base + k * SC_ROWS, SC_ROWS)
            for slot, pos_hbm in enumerate((pos1_hbm, pos2_hbm)):
                pltpu.sync_copy(pos_hbm.at[pl.ds(off, SC_ROWS)], idx_v)
                pltpu.async_copy(ys_hbm.at[idx_v], rows_v, sem).wait()
                pltpu.sync_copy(rows_v, out_hbm.at[slot, pl.ds(off, SC_ROWS)])

    return pl.kernel(
        body,
        out_type=jax.ShapeDtypeStruct((2, n, dp), ys.dtype),
        mesh=_sc_mesh(),
        scratch_types=[pltpu.VMEM((SC_ROWS,), jnp.int32), pltpu.VMEM((SC_ROWS, dp), ys.dtype),
                       pltpu.SemaphoreType.DMA],
        compiler_params=pltpu.CompilerParams(use_tc_tiling_on_sc=True),
        cost_estimate=pl.CostEstimate(flops=0, transcendentals=0, bytes_accessed=4 * n * dp * 4 + 2 * n * 4),
        name="sc_gather",
    )(pos1, pos2, ys)


def _expert_kernel(te_ref, first_ref, nsub_ref, slot_ref, next_ref, xs_ref, wg_hbm, wu_hbm, wd_hbm, ys_ref,
                   wg_buf, wu_buf, wd_buf, wgu_bf_ref, wd_bf_ref, sem):
    i = pl.program_id(0)
    bf16 = jnp.bfloat16
    sub = EXPERT_SUBTILE

    def weight_copies(expert, slot):
        return (pltpu.make_async_copy(wg_hbm.at[expert], wg_buf.at[slot], sem.at[slot]),
                pltpu.make_async_copy(wu_hbm.at[expert], wu_buf.at[slot], sem.at[slot]),
                pltpu.make_async_copy(wd_hbm.at[expert], wd_buf.at[slot], sem.at[slot]))

    @pl.when(first_ref[i] == 1)
    def _():
        slot = slot_ref[i]

        @pl.when(i == 0)
        def _():
            for c in weight_copies(te_ref[i], slot):
                c.start()

        for c in weight_copies(te_ref[i], slot):
            c.wait()

        @pl.when(next_ref[i] >= 0)
        def _():
            for c in weight_copies(next_ref[i], 1 - slot):
                c.start()

        wgu_bf_ref[:, 0:D_EXPERT] = wg_buf[slot].astype(bf16)
        wgu_bf_ref[:, D_EXPERT:2 * D_EXPERT] = wu_buf[slot].astype(bf16)
        wd_bf_ref[...] = wd_buf[slot].astype(bf16)

    def swiglu(r0):
        rows = pl.ds(r0, sub)
        lo, hi = _unpack_halves(xs_ref[rows, :])
        gu = (_dot(lo.astype(bf16), wgu_bf_ref[0:D_PACK, :])
              + _dot(hi.astype(bf16), wgu_bf_ref[D_PACK:D_MODEL, :]))
        g = gu[:, 0:D_EXPERT]
        h = g * jax.nn.sigmoid(g) * gu[:, D_EXPERT:2 * D_EXPERT]
        ys_ref[rows, :] = _pack_halves(_dot(h.astype(bf16), wd_bf_ref[...]))

    n_sub = nsub_ref[i]

    def pair(k, c):
        r0 = pl.multiple_of(k * (2 * sub), 2 * sub)
        swiglu(r0)
        swiglu(r0 + sub)
        return c

    lax.fori_loop(0, n_sub // 2, pair, 0)

    @pl.when(n_sub % 2 == 1)
    def _():
        swiglu(pl.multiple_of((n_sub - 1) * sub, sub))

    def clear(k, c):
        ys_ref[pl.ds(pl.multiple_of(k * sub, sub), sub), :] = jnp.zeros((sub, D_PACK), ys_ref.dtype)
        return c

    lax.fori_loop(n_sub, ys_ref.shape[0] // sub, clear, 0)


def _experts(tile_expert, tile_first, tile_nsub, tile_slot, tile_next, xs, w_gate, w_up, w_down):
    n_rows, dp = xs.shape
    d = D_MODEL
    tm = EXPERT_TILE
    grid_spec = pltpu.PrefetchScalarGridSpec(
        num_scalar_prefetch=5,
        grid=(n_rows // tm,),
        in_specs=[pl.BlockSpec((tm, dp), lambda i, *_: (i, 0)),
                  pl.BlockSpec(memory_space=pl.ANY),
                  pl.BlockSpec(memory_space=pl.ANY),
                  pl.BlockSpec(memory_space=pl.ANY)],
        out_specs=pl.BlockSpec((tm, dp), lambda i, *_: (i, 0)),
        scratch_shapes=[pltpu.VMEM((2, d, D_EXPERT), jnp.float32),
                        pltpu.VMEM((2, d, D_EXPERT), jnp.float32),
                        pltpu.VMEM((2, D_EXPERT, d), jnp.float32),
                        pltpu.VMEM((d, 2 * D_EXPERT), jnp.bfloat16),
                        pltpu.VMEM((D_EXPERT, d), jnp.bfloat16),
                        pltpu.SemaphoreType.DMA((2,))],
    )
    return pl.pallas_call(
        _expert_kernel,
        grid_spec=grid_spec,
        out_shape=jax.ShapeDtypeStruct((n_rows, dp), jnp.uint32),
        compiler_params=pltpu.CompilerParams(dimension_semantics=("arbitrary",),
                                             vmem_limit_bytes=VMEM_LIMIT_BYTES),
        cost_estimate=pl.CostEstimate(flops=2 * n_rows * 3 * d * D_EXPERT, transcendentals=n_rows * D_EXPERT,
                                      bytes_accessed=2 * n_rows * dp * 4 + N_EXPERTS * 3 * d * D_EXPERT * 4),
        name="experts",
    )(tile_expert, tile_first, tile_nsub, tile_slot, tile_next, xs, w_gate, w_up, w_down)


def _combine_kernel(x1_ref, slab_ref, y1_ref, y2_ref, g_ref, b_ref, out_ref, *, alpha):
    slab = slab_ref[...]
    w1 = slab[:, COL_W1:COL_W1 + 1]
    w2 = slab[:, COL_W2:COL_W2 + 1]
    lo1, hi1 = _unpack_halves(y1_ref[...])
    lo2, hi2 = _unpack_halves(y2_ref[...])
    moe = jnp.concatenate([w1 * lo1 + w2 * lo2, w1 * hi1 + w2 * hi2], axis=1)
    out_ref[...] = _layer_norm(alpha * x1_ref[...] + moe, g_ref[...], b_ref[...])


def _combine(x1, slab, yg, g, b, alpha):
    n, d = x1.shape
    tt = TOKEN_TILE
    return pl.pallas_call(
        functools.partial(_combine_kernel, alpha=alpha),
        grid=(n // tt,),
        in_specs=[pl.BlockSpec((tt, d), lambda i: (i, 0)),
                  pl.BlockSpec((tt, LANES), lambda i: (i, 0)),
                  pl.BlockSpec((None, tt, D_PACK), lambda i: (0, i, 0)),
                  pl.BlockSpec((None, tt, D_PACK), lambda i: (1, i, 0)),
                  pl.BlockSpec((1, d), lambda i: (0, 0)),
                  pl.BlockSpec((1, d), lambda i: (0, 0))],
        out_specs=pl.BlockSpec((tt, d), lambda i: (i, 0)),
        out_shape=jax.ShapeDtypeStruct((n, d), jnp.float32),
        compiler_params=pltpu.CompilerParams(dimension_semantics=("arbitrary",),
                                             vmem_limit_bytes=VMEM_LIMIT_BYTES),
        cost_estimate=pl.CostEstimate(flops=12 * n * d, transcendentals=n,
                                      bytes_accessed=n * (2 * d * 4 + 2 * D_PACK * 4 + LANES * 4)),
        name="combine",
    )(x1, slab, yg, yg, g, b)


def _row(a):
    return a.reshape(1, -1)


def _mixer_params(w_in, b_in, conv_w, conv_b, sg_ln_g, sg_ln_b, w_s, b_s, w_o, b_o, ln1_g, ln1_b,
                  w_rc, b_rc, w_rf, b_rf):
    d = D_MODEL
    bf16 = jnp.bfloat16
    row = _row
    ws_pairs = w_s.reshape(N_SG_HEADS // 2, 2 * CHUNK, CHUNK).astype(bf16)
    bs_full = jnp.repeat(b_s.T, SG_HEAD_DIM, axis=1)
    w_r = jnp.zeros((d, LANES), jnp.float32).at[:, 0:N_EXPERTS].set(w_rf)
    w_r = w_r.at[:, COARSE_OFF:COARSE_OFF + N_EXPERT_GROUPS].set(w_rc).astype(bf16)
    b_r = jnp.zeros((1, LANES), jnp.float32).at[0, 0:N_EXPERTS].set(b_rf)
    b_r = b_r.at[0, COARSE_OFF:COARSE_OFF + N_EXPERT_GROUPS].set(b_rc)
    return (w_in.astype(bf16), row(b_in), conv_w, row(conv_b), row(sg_ln_g), row(sg_ln_b),
            ws_pairs, bs_full, w_o.astype(bf16), row(b_o), row(ln1_g), row(ln1_b), w_r, b_r)


def _encoder_layer(x, mixer_params, w_gate, w_up, w_down, ln2_g, ln2_b, alpha):
    n = x.shape[0] * x.shape[1]
    x1, xp, slab, route, carry = _mixer(x, mixer_params, alpha)

    tm = EXPERT_TILE
    n_tiles = 2 * n // tm + N_EXPERTS
    cnt = carry[0, 0:N_EXPERTS].astype(jnp.int32)
    tiles_e = (cnt + tm - 1) // tm
    tile_end = jnp.cumsum(tiles_e)
    tile_start = tile_end - tiles_e
    row_off = tile_start * tm
    experts = jnp.arange(N_EXPERTS, dtype=jnp.int32)[:, None]

    def sorted_pos(e_row, r_row):
        e = route[e_row].astype(jnp.int32)
        return jnp.sum(jnp.where(e[None, :] == experts, row_off[:, None], 0), axis=0) + route[r_row].astype(jnp.int32)

    pos1 = sorted_pos(COL_E1, COL_R1)
    pos2 = sorted_pos(COL_E2, COL_R2)
    tile_ids = jnp.arange(n_tiles, dtype=jnp.int32)
    n_used = tile_end[N_EXPERTS - 1:N_EXPERTS]
    tile_expert = jnp.sum((tile_end[None, :] <= tile_ids[:, None]).astype(jnp.int32), axis=1)
    last_used = jnp.sum((tile_end <= n_used[0] - 1).astype(jnp.int32))
    tile_expert = jnp.minimum(tile_expert, last_used)
    tile_first = jnp.any(tile_ids[:, None] == tile_start[None, :], axis=1) & (tile_ids < n_used[0])
    tile_first = tile_first.astype(jnp.int32).at[0].set(1)
    rows_left = jnp.sum(jnp.where(tile_expert[:, None] == experts.T, (row_off + cnt)[None, :], 0), axis=1) - tile_ids * tm
    tile_nsub = jnp.where(tile_ids < n_used[0], (jnp.clip(rows_left, 0, tm) + EXPERT_SUBTILE - 1) // EXPERT_SUBTILE, 0)

    used = tiles_e > 0
    ordinal = jnp.cumsum(used.astype(jnp.int32)) - 1
    e_ids = experts[:, 0]
    later_used = used[None, :] & (e_ids[None, :] > e_ids[:, None])
    next_used = jnp.min(jnp.where(later_used, e_ids[None, :], N_EXPERTS), axis=1)
    next_used = jnp.where(next_used < N_EXPERTS, next_used, -1)
    tile_slot = ordinal[tile_expert] % 2
    tile_next = next_used[tile_expert]

    xs = _sc_dispatch(pos1, pos2, xp, n_tiles * tm)
    ys = _experts(tile_expert, tile_first, tile_nsub.astype(jnp.int32), tile_slot.astype(jnp.int32),
                  tile_next.astype(jnp.int32), xs, w_gate, w_up, w_down)
    yg = _sc_gather(pos1, pos2, ys)
    return _combine(x1, slab, yg, _row(ln2_g), _row(ln2_b), alpha).reshape(x.shape)


def kernel(x_prompt, x_sample, w_in, b_in, conv_w, conv_b, sg_ln_g, sg_ln_b, w_s, b_s, w_o, b_o, ln1_g, ln1_b, w_rc, b_rc, w_rf, b_rf, w_gate, w_up, w_down, ln2_g, ln2_b):
    depth = w_in.shape[0]
    alpha = (2.0 * depth) ** 0.25
    xs = (x_prompt, x_sample)
    for l in range(depth):
        mixer_params = _mixer_params(w_in[l], b_in[l], conv_w[l], conv_b[l], sg_ln_g[l], sg_ln_b[l], w_s[l], b_s[l],
                                     w_o[l], b_o[l], ln1_g[l], ln1_b[l], w_rc[l], b_rc[l], w_rf[l], b_rf[l])
        xs = tuple(_encoder_layer(x, mixer_params, w_gate[l], w_up[l], w_down[l], ln2_g[l], ln2_b[l], alpha)
                   for x in xs)
    return xs
```

```python
import functools

import jax
import jax.numpy as jnp
from jax import lax
from jax.experimental import pallas as pl
from jax.experimental.pallas import tpu as pltpu
from jax.experimental.pallas import tpu_sc as plsc

D_MODEL = 1024
D_CONV = 512
D_SG = 512
N_SG_HEADS = 8
SG_HEAD_DIM = D_SG // N_SG_HEADS
CHUNK = 128
N_EXPERT_GROUPS = 4
EXPERTS_PER_GROUP = 8
N_EXPERTS = N_EXPERT_GROUPS * EXPERTS_PER_GROUP
D_EXPERT = 512
LN_EPS = 1e-5
D_PACK = D_MODEL // 2

LANES = 128
SUBLANES = 8
HALO = 16
SEQ_TILE = 512
EXPERT_TILE = 1024
EXPERT_SUBTILE = 256
TOKEN_TILE = 1024
SC_CORES = 2
SC_SUBCORES = 16
SC_WORKERS = SC_CORES * SC_SUBCORES
SC_ROWS = 64
VMEM_LIMIT_BYTES = 56 * 1024 * 1024

COL_E1, COL_E2, COL_W1, COL_W2, COL_R1, COL_R2 = 0, 1, 2, 3, 4, 5
COARSE_OFF = N_EXPERTS


def _dot(a, b):
    return jnp.dot(a, b, preferred_element_type=jnp.float32)


def _gelu_tanh(x):
    return 0.5 * x * (1.0 + jnp.tanh(0.7978845608028654 * (x + 0.044715 * (x * x * x))))


def _layer_norm(x, g, b):
    mu = jnp.mean(x, axis=-1, keepdims=True)
    xc = x - mu
    var = jnp.mean(xc * xc, axis=-1, keepdims=True)
    return xc * lax.rsqrt(var + LN_EPS) * g + b


def _pack_halves(x):
    u32 = jnp.uint32
    lo = lax.bitcast_convert_type(x[:, 0:D_PACK].astype(jnp.bfloat16).astype(jnp.float32), u32)
    hi = lax.bitcast_convert_type(x[:, D_PACK:D_MODEL].astype(jnp.bfloat16).astype(jnp.float32), u32)
    return (hi & u32(0xFFFF0000)) | (lo >> u32(16))


def _unpack_halves(w):
    u32 = jnp.uint32
    lo = lax.bitcast_convert_type(w << u32(16), jnp.float32)
    hi = lax.bitcast_convert_type(w & u32(0xFFFF0000), jnp.float32)
    return lo, hi


def _mixer_kernel(x_ref, x_prev_ref, x_next_ref, w_in_ref, b_in_ref, conv_w_ref, conv_b_ref,
                  sg_g_ref, sg_b_ref, ws_ref, bs_ref, w_o_ref, b_o_ref, ln1_g_ref, ln1_b_ref,
                  w_r_ref, b_r_ref,
                  x1_ref, xp_ref, slab_ref, route_ref, counts_ref,
                  xb_ref, ymix_ref, tri_ref, resid_ref, carry_ref, *, ts, nt, n_tiles, alpha):
    t = pl.program_id(0)
    j = jnp.minimum(t, n_tiles - 1) % nt
    bf16 = jnp.bfloat16

    @pl.when(t == 0)
    def _():
        carry_ref[...] = jnp.zeros_like(carry_ref)
        resid_ref[...] = jnp.zeros_like(resid_ref)
        ri = lax.broadcasted_iota(jnp.int32, (ts, ts), 0)
        ci = lax.broadcasted_iota(jnp.int32, (ts, ts), 1)
        tri_ref[...] = jnp.where(ri > ci, 1.0, 0.0).astype(bf16)

    lane = lax.broadcasted_iota(jnp.int32, (ts, LANES), 1)
    lane_f = lane.astype(jnp.float32)
    neg = jnp.float32(-jnp.inf)
    big = jnp.float32(1e9)
    is_c = (lane >= COARSE_OFF) & (lane < COARSE_OFF + N_EXPERT_GROUPS)

    def proj(lhs, lo, hi):
        return _dot(lhs, w_in_ref[:, lo:hi]) + b_in_ref[:, lo:hi]

    def back_norm():
        x1 = _layer_norm(resid_ref[...], ln1_g_ref[...], ln1_b_ref[...])
        x1_ref[...] = x1
        xp_ref[...] = _pack_halves(x1)
        return x1.astype(bf16)

    def back_route(x1b):
        logits = _dot(x1b, w_r_ref[...]) + b_r_ref[...]
        lc = jnp.where(is_c, logits, neg)
        mx = jnp.max(lc, axis=1, keepdims=True)
        grp = jnp.min(jnp.where(lc == mx, lane_f - COARSE_OFF, big), axis=1, keepdims=True)
        p_grp = 1.0 / jnp.sum(jnp.where(is_c, jnp.exp(logits - mx), 0.0), axis=1, keepdims=True)
        grp_lo = grp * EXPERTS_PER_GROUP
        in_grp = (lane_f >= grp_lo) & (lane_f < grp_lo + EXPERTS_PER_GROUP)
        lf = jnp.where(in_grp, logits, neg)
        v1 = jnp.max(lf, axis=1, keepdims=True)
        e1 = jnp.min(jnp.where(lf == v1, lane_f, big), axis=1, keepdims=True)
        lf2 = jnp.where(lane_f == e1, neg, lf)
        v2 = jnp.max(lf2, axis=1, keepdims=True)
        e2 = jnp.min(jnp.where(lf2 == v2, lane_f, big), axis=1, keepdims=True)
        a = jnp.exp(v2 - v1)
        return e1, e2, p_grp / (1.0 + a), p_grp * a / (1.0 + a)

    def back_rank(e1, e2, w1, w2):
        hit1 = lane_f == e1
        hit2 = lane_f == e2
        onehot = jnp.where(hit1 | hit2, 1.0, 0.0)
        carry = carry_ref[...]
        before = _dot(tri_ref[...], onehot.astype(bf16)) + carry
        r1 = jnp.sum(jnp.where(hit1, before, 0.0), axis=1, keepdims=True)
        r2 = jnp.sum(jnp.where(hit2, before, 0.0), axis=1, keepdims=True)
        carry = jnp.where(t > 0, carry + jnp.sum(onehot, axis=0, keepdims=True), carry)
        carry_ref[...] = carry
        counts_ref[...] = jnp.broadcast_to(carry, counts_ref.shape)
        slab = jnp.where(lane == COL_E1, e1, 0.0)
        slab = jnp.where(lane == COL_E2, e2, slab)
        slab = jnp.where(lane == COL_W1, w1, slab)
        slab = jnp.where(lane == COL_W2, w2, slab)
        slab = jnp.where(lane == COL_R1, r1, slab)
        slab = jnp.where(lane == COL_R2, r2, slab)
        slab_ref[...] = slab
        route_ref[...] = slab.T[0:SUBLANES, :]

    def front_load():
        xb_ref[0:HALO, :] = x_prev_ref[...].astype(bf16)
        xb_ref[HALO:HALO + ts, :] = x_ref[...].astype(bf16)
        xb_ref[HALO + ts:HALO + ts + HALO, :] = x_next_ref[...].astype(bf16)

    def front_conv():
        xe = xb_ref[...]
        g_e = proj(xe, 0, D_CONV) * proj(xe, 2 * D_CONV, 3 * D_CONV)
        row_e = lax.broadcasted_iota(jnp.int32, (ts + 2 * HALO, 1), 0)
        has_prev = jnp.where(j > 0, 1.0, 0.0)
        has_next = jnp.where(j < nt - 1, 1.0, 0.0)
        g_e = g_e * jnp.where(row_e < HALO, has_prev, jnp.where(row_e >= HALO + ts, has_next, 1.0))
        conv = (g_e[HALO - 1:HALO - 1 + ts, :] * conv_w_ref[0:1, :] + g_e[HALO:HALO + ts, :] * conv_w_ref[1:2, :]
                + g_e[HALO + 1:HALO + 1 + ts, :] * conv_w_ref[2:3, :] + conv_b_ref[...])
        y_a = proj(xb_ref[HALO:HALO + ts, :], D_CONV, 2 * D_CONV) * conv
        ymix_ref[:, 0:D_CONV] = y_a.astype(bf16)

    def front_gate_proj():
        xm = xb_ref[HALO:HALO + ts, :]
        return proj(xm, 3 * D_CONV, 3 * D_CONV + D_SG), proj(xm, 3 * D_CONV + D_SG, 3 * D_CONV + 2 * D_SG)

    def front_gate(u_pre, v_pre):
        u = _gelu_tanh(u_pre)
        v = _gelu_tanh(v_pre)
        v_ln = _layer_norm(v, sg_g_ref[...], sg_b_ref[...]).astype(bf16)
        first_head = lax.broadcasted_iota(jnp.int32, (CHUNK, LANES), 1) < SG_HEAD_DIM
        for q0 in range(0, ts, 2 * CHUNK):
            q1 = q0 + CHUNK
            for hp in range(N_SG_HEADS // 2):
                c0 = hp * LANES
                rhs = jnp.concatenate([v_ln[q0:q0 + CHUNK, c0:c0 + LANES], v_ln[q1:q1 + CHUNK, c0:c0 + LANES]], axis=1)
                res = _dot(ws_ref[hp], rhs)
                bias = bs_ref[:, c0:c0 + LANES]
                m0 = jnp.where(first_head, res[0:CHUNK, 0:LANES], res[CHUNK:2 * CHUNK, 0:LANES]) + bias
                m1 = jnp.where(first_head, res[0:CHUNK, LANES:2 * LANES], res[CHUNK:2 * CHUNK, LANES:2 * LANES]) + bias
                ymix_ref[q0:q0 + CHUNK, D_CONV + c0:D_CONV + c0 + LANES] = (u[q0:q0 + CHUNK, c0:c0 + LANES] * m0).astype(bf16)
                ymix_ref[q1:q1 + CHUNK, D_CONV + c0:D_CONV + c0 + LANES] = (u[q1:q1 + CHUNK, c0:c0 + LANES] * m1).astype(bf16)

    def front_out_a():
        resid_ref[...] = alpha * x_ref[...] + b_o_ref[...] + _dot(ymix_ref[:, 0:D_CONV], w_o_ref[0:D_CONV, :])

    def front_out_b():
        resid_ref[...] += _dot(ymix_ref[:, D_CONV:D_MODEL], w_o_ref[D_CONV:D_MODEL, :])

    front_load()
    x1b = back_norm()
    routing = back_route(x1b)
    front_conv()
    back_rank(*routing)
    gate_pre = front_gate_proj()
    front_out_a()
    front_gate(*gate_pre)
    front_out_b()


def _mixer(x, p, alpha):
    nb, s, d = x.shape
    ts = SEQ_TILE
    nt = s // ts
    hb = ts // HALO
    n_tiles = nb * nt
    n_total = n_tiles * ts

    front = lambda t: jnp.minimum(t, n_tiles - 1)
    back = lambda t: jnp.maximum(t - 1, 0)
    cur = lambda t: (front(t) // nt, front(t) % nt, 0)
    prv = lambda t: (front(t) // nt, jnp.maximum((front(t) % nt) * hb - 1, 0), 0)
    nxt = lambda t: (front(t) // nt, jnp.minimum((front(t) % nt + 1) * hb, s // HALO - 1), 0)
    full = lambda a: pl.BlockSpec(a.shape, lambda t: (0,) * a.ndim)
    in_specs = [pl.BlockSpec((None, ts, d), cur), pl.BlockSpec((None, HALO, d), prv),
                pl.BlockSpec((None, HALO, d), nxt)] + [full(a) for a in p]
    out_shape = (jax.ShapeDtypeStruct((n_total, d), jnp.float32),
                 jax.ShapeDtypeStruct((n_total, D_PACK), jnp.uint32),
                 jax.ShapeDtypeStruct((n_total, LANES), jnp.float32),
                 jax.ShapeDtypeStruct((SUBLANES, n_total), jnp.float32),
                 jax.ShapeDtypeStruct((SUBLANES, LANES), jnp.float32))
    row_blk = lambda t: (back(t), 0)
    out_specs = (pl.BlockSpec((ts, d), row_blk),
                 pl.BlockSpec((ts, D_PACK), row_blk),
                 pl.BlockSpec((ts, LANES), row_blk),
                 pl.BlockSpec((SUBLANES, ts), lambda t: (0, back(t))),
                 pl.BlockSpec((SUBLANES, LANES), lambda t: (0, 0)))
    return pl.pallas_call(
        functools.partial(_mixer_kernel, ts=ts, nt=nt, n_tiles=n_tiles, alpha=alpha),
        grid=(n_tiles + 1,),
        in_specs=in_specs,
        out_specs=out_specs,
        out_shape=out_shape,
        scratch_shapes=[pltpu.VMEM((ts + 2 * HALO, d), jnp.bfloat16),
                        pltpu.VMEM((ts, d), jnp.bfloat16),
                        pltpu.VMEM((ts, ts), jnp.bfloat16),
                        pltpu.VMEM((ts, d), jnp.float32),
                        pltpu.VMEM((1, LANES), jnp.float32)],
        compiler_params=pltpu.CompilerParams(dimension_semantics=("arbitrary",),
                                             vmem_limit_bytes=VMEM_LIMIT_BYTES),
        cost_estimate=pl.CostEstimate(
            flops=2 * n_total * (d * (3 * D_CONV + 2 * D_SG) + CHUNK * D_SG + d * d + d * LANES + ts * LANES),
            transcendentals=n_total * (2 * D_SG + 2 * LANES),
            bytes_accessed=n_total * (2 * d * 4 + D_PACK * 4 + LANES * 4 + SUBLANES * 4)),
        name="mixer",
    )(x, x, x, *p)


def _sc_mesh():
    return plsc.VectorSubcoreMesh(core_axis_name="c", subcore_axis_name="s",
                                  num_cores=SC_CORES, num_subcores=SC_SUBCORES)


def _sc_worker_base(per_worker):
    return (lax.axis_index("s") * SC_CORES + lax.axis_index("c")) * per_worker


def _sc_dispatch(pos1, pos2, xp, n_rows):
    n, dp = xp.shape
    per_worker = n // SC_WORKERS
    steps = per_worker // SC_ROWS

    def body(pos1_hbm, pos2_hbm, xp_hbm, xs_hbm, idx1_v, idx2_v, rows_v, sem):
        base = _sc_worker_base(per_worker)

        @pl.loop(0, steps)
        def _(k):
            off = pl.multiple_of(base + k * SC_ROWS, SC_ROWS)
            pltpu.sync_copy(pos1_hbm.at[pl.ds(off, SC_ROWS)], idx1_v)
            pltpu.sync_copy(pos2_hbm.at[pl.ds(off, SC_ROWS)], idx2_v)
            pltpu.sync_copy(xp_hbm.at[pl.ds(off, SC_ROWS)], rows_v)
            c1 = pltpu.async_copy(rows_v, xs_hbm.at[idx1_v], sem)
            c2 = pltpu.async_copy(rows_v, xs_hbm.at[idx2_v], sem)
            c1.wait()
            c2.wait()

    return pl.kernel(
        body,
        out_type=jax.ShapeDtypeStruct((n_rows, dp), xp.dtype),
        mesh=_sc_mesh(),
        scratch_types=[pltpu.VMEM((SC_ROWS,), jnp.int32), pltpu.VMEM((SC_ROWS,), jnp.int32),
                       pltpu.VMEM((SC_ROWS, dp), xp.dtype), pltpu.SemaphoreType.DMA],
        compiler_params=pltpu.CompilerParams(use_tc_tiling_on_sc=True),
        cost_estimate=pl.CostEstimate(flops=0, transcendentals=0, bytes_accessed=3 * n * dp * 4 + 2 * n * 4),
        name="sc_dispatch",
    )(pos1, pos2, xp)


def _sc_gather(pos1, pos2, ys):
    n = pos1.shape[0]
    dp = ys.shape[1]
    per_worker = n // SC_WORKERS
    steps = per_worker // SC_ROWS

    def body(pos1_hbm, pos2_hbm, ys_hbm, out_hbm, idx_v, rows_v, sem):
        base = _sc_worker_base(per_worker)

        @pl.loop(0, steps)
        def _(k):
            off = pl.multiple_of(base + k * SC_ROWS, SC_ROWS)
            for slot, pos_hbm in enumerate((pos1_hbm, pos2_hbm)):
                pltpu.sync_copy(pos_hbm.at[pl.ds(off, SC_ROWS)], idx_v)
                pltpu.async_copy(ys_hbm.at[idx_v], rows_v, sem).wait()
                pltpu.sync_copy(rows_v, out_hbm.at[slot, pl.ds(off, SC_ROWS)])

    return pl.kernel(
        body,
        out_type=jax.ShapeDtypeStruct((2, n, dp), ys.dtype),
        mesh=_sc_mesh(),
        scratch_types=[pltpu.VMEM((SC_ROWS,), jnp.int32), pltpu.VMEM((SC_ROWS, dp), ys.dtype),
                       pltpu.SemaphoreType.DMA],
        compiler_params=pltpu.CompilerParams(use_tc_tiling_on_sc=True),
        cost_estimate=pl.CostEstimate(flops=0, transcendentals=0, bytes_accessed=4 * n * dp * 4 + 2 * n * 4),
        name="sc_gather",
    )(pos1, pos2, ys)


def _expert_kernel(te_ref, first_ref, nsub_ref, slot_ref, next_ref, nused_ref, xs_ref, wg_hbm, wu_hbm, wd_hbm,
                   ys_ref, wg_buf, wu_buf, wd_buf, wgu_bf_ref, wd_bf_ref, sem):
    i = pl.program_id(0)
    bf16 = jnp.bfloat16
    sub = EXPERT_SUBTILE

    def weight_copies(expert, slot):
        return (pltpu.make_async_copy(wg_hbm.at[expert], wg_buf.at[slot], sem.at[slot]),
                pltpu.make_async_copy(wu_hbm.at[expert], wu_buf.at[slot], sem.at[slot]),
                pltpu.make_async_copy(wd_hbm.at[expert], wd_buf.at[slot], sem.at[slot]))

    @pl.when(first_ref[i] == 1)
    def _():
        slot = slot_ref[i]

        @pl.when(i == 0)
        def _():
            for c in weight_copies(te_ref[i], slot):
                c.start()

        for c in weight_copies(te_ref[i], slot):
            c.wait()

        @pl.when(next_ref[i] >= 0)
        def _():
            for c in weight_copies(next_ref[i], 1 - slot):
                c.start()

        wgu_bf_ref[:, 0:D_EXPERT] = wg_buf[slot].astype(bf16)
        wgu_bf_ref[:, D_EXPERT:2 * D_EXPERT] = wu_buf[slot].astype(bf16)
        wd_bf_ref[...] = wd_buf[slot].astype(bf16)

    def swiglu(r0):
        rows = pl.ds(r0, sub)
        lo, hi = _unpack_halves(xs_ref[rows, :])
        gu = (_dot(lo.astype(bf16), wgu_bf_ref[0:D_PACK, :])
              + _dot(hi.astype(bf16), wgu_bf_ref[D_PACK:D_MODEL, :]))
        g = gu[:, 0:D_EXPERT]
        h = g * jax.nn.sigmoid(g) * gu[:, D_EXPERT:2 * D_EXPERT]
        ys_ref[rows, :] = _pack_halves(_dot(h.astype(bf16), wd_bf_ref[...]))

    n_sub = nsub_ref[i]

    def pair(k, c):
        r0 = pl.multiple_of(k * (2 * sub), 2 * sub)
        swiglu(r0)
        swiglu(r0 + sub)
        return c

    lax.fori_loop(0, n_sub // 2, pair, 0)

    @pl.when(n_sub % 2 == 1)
    def _():
        swiglu(pl.multiple_of((n_sub - 1) * sub, sub))

    def clear(k, c):
        ys_ref[pl.ds(pl.multiple_of(k * sub, sub), sub), :] = jnp.zeros((sub, D_PACK), ys_ref.dtype)
        return c

    lax.fori_loop(n_sub, jnp.where(i < nused_ref[0], ys_ref.shape[0] // sub, 0), clear, 0)


def _experts(tile_expert, tile_first, tile_nsub, tile_slot, tile_next, n_used, xs, w_gate, w_up, w_down):
    n_rows, dp = xs.shape
    d = D_MODEL
    tm = EXPERT_TILE
    row_blk = lambda i, te, fi, ns, sl, nx, nu: (jnp.minimum(i, nu[0] - 1), 0)
    grid_spec = pltpu.PrefetchScalarGridSpec(
        num_scalar_prefetch=6,
        grid=(n_rows // tm,),
        in_specs=[pl.BlockSpec((tm, dp), row_blk),
                  pl.BlockSpec(memory_space=pl.ANY),
                  pl.BlockSpec(memory_space=pl.ANY),
                  pl.BlockSpec(memory_space=pl.ANY)],
        out_specs=pl.BlockSpec((tm, dp), row_blk),
        scratch_shapes=[pltpu.VMEM((2, d, D_EXPERT), jnp.float32),
                        pltpu.VMEM((2, d, D_EXPERT), jnp.float32),
                        pltpu.VMEM((2, D_EXPERT, d), jnp.float32),
                        pltpu.VMEM((d, 2 * D_EXPERT), jnp.bfloat16),
                        pltpu.VMEM((D_EXPERT, d), jnp.bfloat16),
                        pltpu.SemaphoreType.DMA((2,))],
    )
    return pl.pallas_call(
        _expert_kernel,
        grid_spec=grid_spec,
        out_shape=jax.ShapeDtypeStruct((n_rows, dp), jnp.uint32),
        compiler_params=pltpu.CompilerParams(dimension_semantics=("arbitrary",),
                                             vmem_limit_bytes=VMEM_LIMIT_BYTES),
        cost_estimate=pl.CostEstimate(flops=2 * n_rows * 3 * d * D_EXPERT, transcendentals=n_rows * D_EXPERT,
                                      bytes_accessed=2 * n_rows * dp * 4 + N_EXPERTS * 3 * d * D_EXPERT * 4),
        name="experts",
    )(tile_expert, tile_first, tile_nsub, tile_slot, tile_next, n_used, xs, w_gate, w_up, w_down)


def _combine_kernel(x1_ref, slab_ref, y1_ref, y2_ref, g_ref, b_ref, out_ref, *, alpha):
    slab = slab_ref[...]
    w1 = slab[:, COL_W1:COL_W1 + 1]
    w2 = slab[:, COL_W2:COL_W2 + 1]
    lo1, hi1 = _unpack_halves(y1_ref[...])
    lo2, hi2 = _unpack_halves(y2_ref[...])
    moe = jnp.concatenate([w1 * lo1 + w2 * lo2, w1 * hi1 + w2 * hi2], axis=1)
    out_ref[...] = _layer_norm(alpha * x1_ref[...] + moe, g_ref[...], b_ref[...])


def _combine(x1, slab, yg, g, b, alpha):
    n, d = x1.shape
    tt = TOKEN_TILE
    return pl.pallas_call(
        functools.partial(_combine_kernel, alpha=alpha),
        grid=(n // tt,),
        in_specs=[pl.BlockSpec((tt, d), lambda i: (i, 0)),
                  pl.BlockSpec((tt, LANES), lambda i: (i, 0)),
                  pl.BlockSpec((None, tt, D_PACK), lambda i: (0, i, 0)),
                  pl.BlockSpec((None, tt, D_PACK), lambda i: (1, i, 0)),
                  pl.BlockSpec((1, d), lambda i: (0, 0)),
                  pl.BlockSpec((1, d), lambda i: (0, 0))],
        out_specs=pl.BlockSpec((tt, d), lambda i: (i, 0)),
        out_shape=jax.ShapeDtypeStruct((n, d), jnp.float32),
        compiler_params=pltpu.CompilerParams(dimension_semantics=("arbitrary",),
                                             vmem_limit_bytes=VMEM_LIMIT_BYTES),
        cost_estimate=pl.CostEstimate(flops=12 * n * d, transcendentals=n,
                                      bytes_accessed=n * (2 * d * 4 + 2 * D_PACK * 4 + LANES * 4)),
        name="combine",
    )(x1, slab, yg, yg, g, b)


def _row(a):
    return a.reshape(1, -1)


def _mixer_params(w_in, b_in, conv_w, conv_b, sg_ln_g, sg_ln_b, w_s, b_s, w_o, b_o, ln1_g, ln1_b,
                  w_rc, b_rc, w_rf, b_rf):
    d = D_MODEL
    bf16 = jnp.bfloat16
    row = _row
    ws_pairs = w_s.reshape(N_SG_HEADS // 2, 2 * CHUNK, CHUNK).astype(bf16)
    bs_full = jnp.repeat(b_s.T, SG_HEAD_DIM, axis=1)
    w_r = jnp.zeros((d, LANES), jnp.float32).at[:, 0:N_EXPERTS].set(w_rf)
    w_r = w_r.at[:, COARSE_OFF:COARSE_OFF + N_EXPERT_GROUPS].set(w_rc).astype(bf16)
    b_r = jnp.zeros((1, LANES), jnp.float32).at[0, 0:N_EXPERTS].set(b_rf)
    b_r = b_r.at[0, COARSE_OFF:COARSE_OFF + N_EXPERT_GROUPS].set(b_rc)
    return (w_in.astype(bf16), row(b_in), conv_w, row(conv_b), row(sg_ln_g), row(sg_ln_b),
            ws_pairs, bs_full, w_o.astype(bf16), row(b_o), row(ln1_g), row(ln1_b), w_r, b_r)


def _encoder_layer(x, mixer_params, w_gate, w_up, w_down, ln2_g, ln2_b, alpha):
    n = x.shape[0] * x.shape[1]
    x1, xp, slab, route, carry = _mixer(x, mixer_params, alpha)

    tm = EXPERT_TILE
    n_tiles = 2 * n // tm + N_EXPERTS
    cnt = carry[0, 0:N_EXPERTS].astype(jnp.int32)
    tiles_e = (cnt + tm - 1) // tm
    tile_end = jnp.cumsum(tiles_e)
    tile_start = tile_end - tiles_e
    row_off = tile_start * tm
    experts = jnp.arange(N_EXPERTS, dtype=jnp.int32)[:, None]

    def sorted_pos(e_row, r_row):
        e = route[e_row].astype(jnp.int32)
        return jnp.sum(jnp.where(e[None, :] == experts, row_off[:, None], 0), axis=0) + route[r_row].astype(jnp.int32)

    pos1 = sorted_pos(COL_E1, COL_R1)
    pos2 = sorted_pos(COL_E2, COL_R2)
    tile_ids = jnp.arange(n_tiles, dtype=jnp.int32)
    n_used = tile_end[N_EXPERTS - 1:N_EXPERTS]
    tile_expert = jnp.sum((tile_end[None, :] <= tile_ids[:, None]).astype(jnp.int32), axis=1)
    last_used = jnp.sum((tile_end <= n_used[0] - 1).astype(jnp.int32))
    tile_expert = jnp.minimum(tile_expert, last_used)
    tile_first = jnp.any(tile_ids[:, None] == tile_start[None, :], axis=1) & (tile_ids < n_used[0])
    tile_first = tile_first.astype(jnp.int32).at[0].set(1)
    rows_left = jnp.sum(jnp.where(tile_expert[:, None] == experts.T, (row_off + cnt)[None, :], 0), axis=1) - tile_ids * tm
    tile_nsub = jnp.where(tile_ids < n_used[0], (jnp.clip(rows_left, 0, tm) + EXPERT_SUBTILE - 1) // EXPERT_SUBTILE, 0)

    used = tiles_e > 0
    ordinal = jnp.cumsum(used.astype(jnp.int32)) - 1
    e_ids = experts[:, 0]
    later_used = used[None, :] & (e_ids[None, :] > e_ids[:, None])
    next_used = jnp.min(jnp.where(later_used, e_ids[None, :], N_EXPERTS), axis=1)
    next_used = jnp.where(next_used < N_EXPERTS, next_used, -1)
    tile_slot = ordinal[tile_expert] % 2
    tile_next = next_used[tile_expert]

    xs = _sc_dispatch(pos1, pos2, xp, n_tiles * tm)
    ys = _experts(tile_expert, tile_first, tile_nsub.astype(jnp.int32), tile_slot.astype(jnp.int32),
                  tile_next.astype(jnp.int32), n_used, xs, w_gate, w_up, w_down)
    yg = _sc_gather(pos1, pos2, ys)
    return _combine(x1, slab, yg, _row(ln2_g), _row(ln2_b), alpha).reshape(x.shape)


def kernel(x_prompt, x_sample, w_in, b_in, conv_w, conv_b, sg_ln_g, sg_ln_b, w_s, b_s, w_o, b_o, ln1_g, ln1_b, w_rc, b_rc, w_rf, b_rf, w_gate, w_up, w_down, ln2_g, ln2_b):
    depth = w_in.shape[0]
    alpha = (2.0 * depth) ** 0.25
    xs = (x_prompt, x_sample)
    for l in range(depth):
        mixer_params = _mixer_params(w_in[l], b_in[l], conv_w[l], conv_b[l], sg_ln_g[l], sg_ln_b[l], w_s[l], b_s[l],
                                     w_o[l], b_o[l], ln1_g[l], ln1_b[l], w_rc[l], b_rc[l], w_rf[l], b_rf[l])
        xs = tuple(_encoder_layer(x, mixer_params, w_gate[l], w_up[l], w_down[l], ln2_g[l], ln2_b[l], alpha)
                   for x in xs)
    return xs
```

```python
import functools

import jax
import jax.numpy as jnp
from jax import lax
from jax.experimental import pallas as pl
from jax.experimental.pallas import tpu as pltpu
from jax.experimental.pallas import tpu_sc as plsc

D_MODEL = 1024
D_CONV = 512
D_SG = 512
N_SG_HEADS = 8
SG_HEAD_DIM = D_SG // N_SG_HEADS
CHUNK = 128
N_EXPERT_GROUPS = 4
EXPERTS_PER_GROUP = 8
N_EXPERTS = N_EXPERT_GROUPS * EXPERTS_PER_GROUP
D_EXPERT = 512
LN_EPS = 1e-5
D_PACK = D_MODEL // 2

LANES = 128
SUBLANES = 8
HALO = 16
SEQ_TILE = 512
EXPERT_TILE = 1024
EXPERT_SUBTILE = 256
TOKEN_TILE = 1024
COMBINE_PART = 16384
SC_CORES = 2
SC_SUBCORES = 16
SC_WORKERS = SC_CORES * SC_SUBCORES
SC_ROWS = 64
VMEM_LIMIT_BYTES = 56 * 1024 * 1024

COL_E1, COL_E2, COL_W1, COL_W2, COL_R1, COL_R2 = 0, 1, 2, 3, 4, 5
COARSE_OFF = N_EXPERTS


def _dot(a, b):
    return jnp.dot(a, b, preferred_element_type=jnp.float32)


def _gelu_tanh(x):
    return 0.5 * x * (1.0 + jnp.tanh(0.7978845608028654 * (x + 0.044715 * (x * x * x))))


def _layer_norm(x, g, b):
    mu = jnp.mean(x, axis=-1, keepdims=True)
    xc = x - mu
    var = jnp.mean(xc * xc, axis=-1, keepdims=True)
    return xc * lax.rsqrt(var + LN_EPS) * g + b


def _pack_halves(x):
    u32 = jnp.uint32
    lo = lax.bitcast_convert_type(x[:, 0:D_PACK].astype(jnp.bfloat16).astype(jnp.float32), u32)
    hi = lax.bitcast_convert_type(x[:, D_PACK:D_MODEL].astype(jnp.bfloat16).astype(jnp.float32), u32)
    return (hi & u32(0xFFFF0000)) | (lo >> u32(16))


def _unpack_halves(w):
    u32 = jnp.uint32
    lo = lax.bitcast_convert_type(w << u32(16), jnp.float32)
    hi = lax.bitcast_convert_type(w & u32(0xFFFF0000), jnp.float32)
    return lo, hi


def _mixer_kernel(x_ref, x_prev_ref, x_next_ref, w_in_ref, b_in_ref, conv_w_ref, conv_b_ref,
                  sg_g_ref, sg_b_ref, ws_ref, bs_ref, w_o_ref, b_o_ref, ln1_g_ref, ln1_b_ref,
                  w_r_ref, b_r_ref,
                  x1_ref, xp_ref, slab_ref, route_ref, counts_ref,
                  xb_ref, ymix_ref, tri_ref, resid_ref, carry_ref, *, ts, nt, n_tiles, alpha):
    t = pl.program_id(0)
    j = jnp.minimum(t, n_tiles - 1) % nt
    bf16 = jnp.bfloat16

    @pl.when(t == 0)
    def _():
        carry_ref[...] = jnp.zeros_like(carry_ref)
        resid_ref[...] = jnp.zeros_like(resid_ref)
        ri = lax.broadcasted_iota(jnp.int32, (ts, ts), 0)
        ci = lax.broadcasted_iota(jnp.int32, (ts, ts), 1)
        tri_ref[...] = jnp.where(ri > ci, 1.0, 0.0).astype(bf16)

    lane = lax.broadcasted_iota(jnp.int32, (ts, LANES), 1)
    lane_f = lane.astype(jnp.float32)
    neg = jnp.float32(-jnp.inf)
    big = jnp.float32(1e9)
    is_c = (lane >= COARSE_OFF) & (lane < COARSE_OFF + N_EXPERT_GROUPS)

    def proj(lhs, lo, hi):
        return _dot(lhs, w_in_ref[:, lo:hi]) + b_in_ref[:, lo:hi]

    def back_norm():
        x1 = _layer_norm(resid_ref[...], ln1_g_ref[...], ln1_b_ref[...])
        x1_ref[...] = x1
        xp_ref[...] = _pack_halves(x1)
        return x1.astype(bf16)

    def back_route(x1b):
        logits = _dot(x1b, w_r_ref[...]) + b_r_ref[...]
        lc = jnp.where(is_c, logits, neg)
        mx = jnp.max(lc, axis=1, keepdims=True)
        grp = jnp.min(jnp.where(lc == mx, lane_f - COARSE_OFF, big), axis=1, keepdims=True)
        p_grp = 1.0 / jnp.sum(jnp.where(is_c, jnp.exp(logits - mx), 0.0), axis=1, keepdims=True)
        grp_lo = grp * EXPERTS_PER_GROUP
        in_grp = (lane_f >= grp_lo) & (lane_f < grp_lo + EXPERTS_PER_GROUP)
        lf = jnp.where(in_grp, logits, neg)
        v1 = jnp.max(lf, axis=1, keepdims=True)
        e1 = jnp.min(jnp.where(lf == v1, lane_f, big), axis=1, keepdims=True)
        lf2 = jnp.where(lane_f == e1, neg, lf)
        v2 = jnp.max(lf2, axis=1, keepdims=True)
        e2 = jnp.min(jnp.where(lf2 == v2, lane_f, big), axis=1, keepdims=True)
        a = jnp.exp(v2 - v1)
        return e1, e2, p_grp / (1.0 + a), p_grp * a / (1.0 + a)

    def back_rank(e1, e2, w1, w2):
        hit1 = lane_f == e1
        hit2 = lane_f == e2
        onehot = jnp.where(hit1 | hit2, 1.0, 0.0)
        carry = carry_ref[...]
        before = _dot(tri_ref[...], onehot.astype(bf16)) + carry
        r1 = jnp.sum(jnp.where(hit1, before, 0.0), axis=1, keepdims=True)
        r2 = jnp.sum(jnp.where(hit2, before, 0.0), axis=1, keepdims=True)
        carry = jnp.where(t > 0, carry + jnp.sum(onehot, axis=0, keepdims=True), carry)
        carry_ref[...] = carry
        counts_ref[...] = jnp.broadcast_to(carry, counts_ref.shape)
        slab = jnp.where(lane == COL_E1, e1, 0.0)
        slab = jnp.where(lane == COL_E2, e2, slab)
        slab = jnp.where(lane == COL_W1, w1, slab)
        slab = jnp.where(lane == COL_W2, w2, slab)
        slab = jnp.where(lane == COL_R1, r1, slab)
        slab = jnp.where(lane == COL_R2, r2, slab)
        slab_ref[...] = slab
        route_ref[...] = slab.T[0:SUBLANES, :]

    def front_load():
        xb_ref[0:HALO, :] = x_prev_ref[...].astype(bf16)
        xb_ref[HALO:HALO + ts, :] = x_ref[...].astype(bf16)
        xb_ref[HALO + ts:HALO + ts + HALO, :] = x_next_ref[...].astype(bf16)

    def front_conv():
        xe = xb_ref[...]
        g_e = proj(xe, 0, D_CONV) * proj(xe, 2 * D_CONV, 3 * D_CONV)
        row_e = lax.broadcasted_iota(jnp.int32, (ts + 2 * HALO, 1), 0)
        has_prev = jnp.where(j > 0, 1.0, 0.0)
        has_next = jnp.where(j < nt - 1, 1.0, 0.0)
        g_e = g_e * jnp.where(row_e < HALO, has_prev, jnp.where(row_e >= HALO + ts, has_next, 1.0))
        conv = (g_e[HALO - 1:HALO - 1 + ts, :] * conv_w_ref[0:1, :] + g_e[HALO:HALO + ts, :] * conv_w_ref[1:2, :]
                + g_e[HALO + 1:HALO + 1 + ts, :] * conv_w_ref[2:3, :] + conv_b_ref[...])
        y_a = proj(xb_ref[HALO:HALO + ts, :], D_CONV, 2 * D_CONV) * conv
        ymix_ref[:, 0:D_CONV] = y_a.astype(bf16)

    def front_gate_proj():
        xm = xb_ref[HALO:HALO + ts, :]
        return proj(xm, 3 * D_CONV, 3 * D_CONV + D_SG), proj(xm, 3 * D_CONV + D_SG, 3 * D_CONV + 2 * D_SG)

    def front_gate(u_pre, v_pre):
        u = _gelu_tanh(u_pre)
        v = _gelu_tanh(v_pre)
        v_ln = _layer_norm(v, sg_g_ref[...], sg_b_ref[...]).astype(bf16)
        first_head = lax.broadcasted_iota(jnp.int32, (CHUNK, LANES), 1) < SG_HEAD_DIM
        for q0 in range(0, ts, 2 * CHUNK):
            q1 = q0 + CHUNK
            for hp in range(N_SG_HEADS // 2):
                c0 = hp * LANES
                rhs = jnp.concatenate([v_ln[q0:q0 + CHUNK, c0:c0 + LANES], v_ln[q1:q1 + CHUNK, c0:c0 + LANES]], axis=1)
                res = _dot(ws_ref[hp], rhs)
                bias = bs_ref[:, c0:c0 + LANES]
                m0 = jnp.where(first_head, res[0:CHUNK, 0:LANES], res[CHUNK:2 * CHUNK, 0:LANES]) + bias
                m1 = jnp.where(first_head, res[0:CHUNK, LANES:2 * LANES], res[CHUNK:2 * CHUNK, LANES:2 * LANES]) + bias
                ymix_ref[q0:q0 + CHUNK, D_CONV + c0:D_CONV + c0 + LANES] = (u[q0:q0 + CHUNK, c0:c0 + LANES] * m0).astype(bf16)
                ymix_ref[q1:q1 + CHUNK, D_CONV + c0:D_CONV + c0 + LANES] = (u[q1:q1 + CHUNK, c0:c0 + LANES] * m1).astype(bf16)

    def front_out_a():
        resid_ref[...] = alpha * x_ref[...] + b_o_ref[...] + _dot(ymix_ref[:, 0:D_CONV], w_o_ref[0:D_CONV, :])

    def front_out_b():
        resid_ref[...] += _dot(ymix_ref[:, D_CONV:D_MODEL], w_o_ref[D_CONV:D_MODEL, :])

    front_load()
    x1b = back_norm()
    routing = back_route(x1b)
    front_conv()
    back_rank(*routing)
    gate_pre = front_gate_proj()
    front_out_a()
    front_gate(*gate_pre)
    front_out_b()


def _mixer(x, p, alpha):
    nb, s, d = x.shape
    ts = SEQ_TILE
    nt = s // ts
    hb = ts // HALO
    n_tiles = nb * nt
    n_total = n_tiles * ts

    front = lambda t: jnp.minimum(t, n_tiles - 1)
    back = lambda t: jnp.maximum(t - 1, 0)
    cur = lambda t: (front(t) // nt, front(t) % nt, 0)
    prv = lambda t: (front(t) // nt, jnp.maximum((front(t) % nt) * hb - 1, 0), 0)
    nxt = lambda t: (front(t) // nt, jnp.minimum((front(t) % nt + 1) * hb, s // HALO - 1), 0)
    full = lambda a: pl.BlockSpec(a.shape, lambda t: (0,) * a.ndim)
    in_specs = [pl.BlockSpec((None, ts, d), cur), pl.BlockSpec((None, HALO, d), prv),
                pl.BlockSpec((None, HALO, d), nxt)] + [full(a) for a in p]
    out_shape = (jax.ShapeDtypeStruct((n_total, d), jnp.float32),
                 jax.ShapeDtypeStruct((n_total, D_PACK), jnp.uint32),
                 jax.ShapeDtypeStruct((n_total, LANES), jnp.float32),
                 jax.ShapeDtypeStruct((SUBLANES, n_total), jnp.float32),
                 jax.ShapeDtypeStruct((SUBLANES, LANES), jnp.float32))
    row_blk = lambda t: (back(t), 0)
    out_specs = (pl.BlockSpec((ts, d), row_blk),
                 pl.BlockSpec((ts, D_PACK), row_blk),
                 pl.BlockSpec((ts, LANES), row_blk),
                 pl.BlockSpec((SUBLANES, ts), lambda t: (0, back(t))),
                 pl.BlockSpec((SUBLANES, LANES), lambda t: (0, 0)))
    return pl.pallas_call(
        functools.partial(_mixer_kernel, ts=ts, nt=nt, n_tiles=n_tiles, alpha=alpha),
        grid=(n_tiles + 1,),
        in_specs=in_specs,
        out_specs=out_specs,
        out_shape=out_shape,
        scratch_shapes=[pltpu.VMEM((ts + 2 * HALO, d), jnp.bfloat16),
                        pltpu.VMEM((ts, d), jnp.bfloat16),
                        pltpu.VMEM((ts, ts), jnp.bfloat16),
                        pltpu.VMEM((ts, d), jnp.float32),
                        pltpu.VMEM((1, LANES), jnp.float32)],
        compiler_params=pltpu.CompilerParams(dimension_semantics=("arbitrary",),
                                             vmem_limit_bytes=VMEM_LIMIT_BYTES),
        cost_estimate=pl.CostEstimate(
            flops=2 * n_total * (d * (3 * D_CONV + 2 * D_SG) + CHUNK * D_SG + d * d + d * LANES + ts * LANES),
            transcendentals=n_total * (2 * D_SG + 2 * LANES),
            bytes_accessed=n_total * (2 * d * 4 + D_PACK * 4 + LANES * 4 + SUBLANES * 4)),
        name="mixer",
    )(x, x, x, *p)


def _sc_mesh():
    return plsc.VectorSubcoreMesh(core_axis_name="c", subcore_axis_name="s",
                                  num_cores=SC_CORES, num_subcores=SC_SUBCORES)


def _sc_worker_base(per_worker):
    return (lax.axis_index("s") * SC_CORES + lax.axis_index("c")) * per_worker


def _sc_dispatch(pos1, pos2, xp, n_rows):
    n, dp = xp.shape
    per_worker = n // SC_WORKERS
    steps = per_worker // SC_ROWS

    def body(pos1_hbm, pos2_hbm, xp_hbm, xs_hbm, idx1_v, idx2_v, rows_v, sem):
        base = _sc_worker_base(per_worker)

        @pl.loop(0, steps)
        def _(k):
            off = pl.multiple_of(base + k * SC_ROWS, SC_ROWS)
            pltpu.sync_copy(pos1_hbm.at[pl.ds(off, SC_ROWS)], idx1_v)
            pltpu.sync_copy(pos2_hbm.at[pl.ds(off, SC_ROWS)], idx2_v)
            pltpu.sync_copy(xp_hbm.at[pl.ds(off, SC_ROWS)], rows_v)
            c1 = pltpu.async_copy(rows_v, xs_hbm.at[idx1_v], sem)
            c2 = pltpu.async_copy(rows_v, xs_hbm.at[idx2_v], sem)
            c1.wait()
            c2.wait()

    return pl.kernel(
        body,
        out_type=jax.ShapeDtypeStruct((n_rows, dp), xp.dtype),
        mesh=_sc_mesh(),
        scratch_types=[pltpu.VMEM((SC_ROWS,), jnp.int32), pltpu.VMEM((SC_ROWS,), jnp.int32),
                       pltpu.VMEM((SC_ROWS, dp), xp.dtype), pltpu.SemaphoreType.DMA],
        compiler_params=pltpu.CompilerParams(use_tc_tiling_on_sc=True),
        cost_estimate=pl.CostEstimate(flops=0, transcendentals=0, bytes_accessed=3 * n * dp * 4 + 2 * n * 4),
        name="sc_dispatch",
    )(pos1, pos2, xp)


def _sc_gather(pos1, pos2, ys):
    n = pos1.shape[0]
    dp = ys.shape[1]
    per_worker = n // SC_WORKERS
    steps = per_worker // SC_ROWS

    def body(pos1_hbm, pos2_hbm, ys_hbm, out_hbm, idx_v, rows_v, sem):
        base = _sc_worker_base(per_worker)

        @pl.loop(0, steps)
        def _(k):
            off = pl.multiple_of(base + k * SC_ROWS, SC_ROWS)
            for slot, pos_hbm in enumerate((pos1_hbm, pos2_hbm)):
                pltpu.sync_copy(pos_hbm.at[pl.ds(off, SC_ROWS)], idx_v)
                pltpu.async_copy(ys_hbm.at[idx_v], rows_v, sem).wait()
                pltpu.sync_copy(rows_v, out_hbm.at[slot, pl.ds(off, SC_ROWS)])

    return pl.kernel(
        body,
        out_type=jax.ShapeDtypeStruct((2, n, dp), ys.dtype),
        mesh=_sc_mesh(),
        scratch_types=[pltpu.VMEM((SC_ROWS,), jnp.int32), pltpu.VMEM((SC_ROWS, dp), ys.dtype),
                       pltpu.SemaphoreType.DMA],
        compiler_params=pltpu.CompilerParams(use_tc_tiling_on_sc=True),
        cost_estimate=pl.CostEstimate(flops=0, transcendentals=0, bytes_accessed=4 * n * dp * 4 + 2 * n * 4),
        name="sc_gather",
    )(pos1, pos2, ys)


def _expert_kernel(te_ref, first_ref, nsub_ref, slot_ref, next_ref, nused_ref, xs_ref, wg_hbm, wu_hbm, wd_hbm,
                   ys_ref, wg_buf, wu_buf, wd_buf, wgu_bf_ref, wd_bf_ref, sem):
    i = pl.program_id(0)
    bf16 = jnp.bfloat16
    sub = EXPERT_SUBTILE

    def weight_copies(expert, slot):
        return (pltpu.make_async_copy(wg_hbm.at[expert], wg_buf.at[slot], sem.at[slot]),
                pltpu.make_async_copy(wu_hbm.at[expert], wu_buf.at[slot], sem.at[slot]),
                pltpu.make_async_copy(wd_hbm.at[expert], wd_buf.at[slot], sem.at[slot]))

    @pl.when(first_ref[i] == 1)
    def _():
        slot = slot_ref[i]

        @pl.when(i == 0)
        def _():
            for c in weight_copies(te_ref[i], slot):
                c.start()

        for c in weight_copies(te_ref[i], slot):
            c.wait()

        @pl.when(next_ref[i] >= 0)
        def _():
            for c in weight_copies(next_ref[i], 1 - slot):
                c.start()

        wgu_bf_ref[:, 0:D_EXPERT] = wg_buf[slot].astype(bf16)
        wgu_bf_ref[:, D_EXPERT:2 * D_EXPERT] = wu_buf[slot].astype(bf16)
        wd_bf_ref[...] = wd_buf[slot].astype(bf16)

    def swiglu(r0):
        rows = pl.ds(r0, sub)
        lo, hi = _unpack_halves(xs_ref[rows, :])
        gu = (_dot(lo.astype(bf16), wgu_bf_ref[0:D_PACK, :])
              + _dot(hi.astype(bf16), wgu_bf_ref[D_PACK:D_MODEL, :]))
        g = gu[:, 0:D_EXPERT]
        h = g * jax.nn.sigmoid(g) * gu[:, D_EXPERT:2 * D_EXPERT]
        ys_ref[rows, :] = _pack_halves(_dot(h.astype(bf16), wd_bf_ref[...]))

    n_sub = nsub_ref[i]

    def pair(k, c):
        r0 = pl.multiple_of(k * (2 * sub), 2 * sub)
        swiglu(r0)
        swiglu(r0 + sub)
        return c

    lax.fori_loop(0, n_sub // 2, pair, 0)

    @pl.when(n_sub % 2 == 1)
    def _():
        swiglu(pl.multiple_of((n_sub - 1) * sub, sub))

    def clear(k, c):
        ys_ref[pl.ds(pl.multiple_of(k * sub, sub), sub), :] = jnp.zeros((sub, D_PACK), ys_ref.dtype)
        return c

    lax.fori_loop(n_sub, jnp.where(i < nused_ref[0], ys_ref.shape[0] // sub, 0), clear, 0)


def _experts(tile_expert, tile_first, tile_nsub, tile_slot, tile_next, n_used, xs, w_gate, w_up, w_down):
    n_rows, dp = xs.shape
    d = D_MODEL
    tm = EXPERT_TILE
    row_blk = lambda i, te, fi, ns, sl, nx, nu: (jnp.minimum(i, nu[0] - 1), 0)
    grid_spec = pltpu.PrefetchScalarGridSpec(
        num_scalar_prefetch=6,
        grid=(n_rows // tm,),
        in_specs=[pl.BlockSpec((tm, dp), row_blk),
                  pl.BlockSpec(memory_space=pl.ANY),
                  pl.BlockSpec(memory_space=pl.ANY),
                  pl.BlockSpec(memory_space=pl.ANY)],
        out_specs=pl.BlockSpec((tm, dp), row_blk),
        scratch_shapes=[pltpu.VMEM((2, d, D_EXPERT), jnp.float32),
                        pltpu.VMEM((2, d, D_EXPERT), jnp.float32),
                        pltpu.VMEM((2, D_EXPERT, d), jnp.float32),
                        pltpu.VMEM((d, 2 * D_EXPERT), jnp.bfloat16),
                        pltpu.VMEM((D_EXPERT, d), jnp.bfloat16),
                        pltpu.SemaphoreType.DMA((2,))],
    )
    return pl.pallas_call(
        _expert_kernel,
        grid_spec=grid_spec,
        out_shape=jax.ShapeDtypeStruct((n_rows, dp), jnp.uint32),
        compiler_params=pltpu.CompilerParams(dimension_semantics=("arbitrary",),
                                             vmem_limit_bytes=VMEM_LIMIT_BYTES),
        cost_estimate=pl.CostEstimate(flops=2 * n_rows * 3 * d * D_EXPERT, transcendentals=n_rows * D_EXPERT,
                                      bytes_accessed=2 * n_rows * dp * 4 + N_EXPERTS * 3 * d * D_EXPERT * 4),
        name="experts",
    )(tile_expert, tile_first, tile_nsub, tile_slot, tile_next, n_used, xs, w_gate, w_up, w_down)


def _combine_kernel(x1_ref, slab_ref, y1_ref, y2_ref, g_ref, b_ref, *rest, alpha):
    out_ref = rest[-1]
    slab = slab_ref[...]
    w1 = slab[:, COL_W1:COL_W1 + 1]
    w2 = slab[:, COL_W2:COL_W2 + 1]
    lo1, hi1 = _unpack_halves(y1_ref[...])
    lo2, hi2 = _unpack_halves(y2_ref[...])
    moe = jnp.concatenate([w1 * lo1 + w2 * lo2, w1 * hi1 + w2 * hi2], axis=1)
    out_ref[...] = _layer_norm(alpha * x1_ref[...] + moe, g_ref[...], b_ref[...])


def _combine(x1, slab, yg, g, b, alpha, tile0, out_prev):
    n, d = x1.shape
    n_part = yg.shape[1]
    tt = TOKEN_TILE
    prev = [] if out_prev is None else [out_prev]
    return pl.pallas_call(
        functools.partial(_combine_kernel, alpha=alpha),
        grid=(n_part // tt,),
        in_specs=[pl.BlockSpec((tt, d), lambda i: (tile0 + i, 0)),
                  pl.BlockSpec((tt, LANES), lambda i: (tile0 + i, 0)),
                  pl.BlockSpec((None, tt, D_PACK), lambda i: (0, i, 0)),
                  pl.BlockSpec((None, tt, D_PACK), lambda i: (1, i, 0)),
                  pl.BlockSpec((1, d), lambda i: (0, 0)),
                  pl.BlockSpec((1, d), lambda i: (0, 0))] + [pl.BlockSpec(memory_space=pl.ANY)] * len(prev),
        out_specs=pl.BlockSpec((tt, d), lambda i: (tile0 + i, 0)),
        out_shape=jax.ShapeDtypeStruct((n, d), jnp.float32),
        input_output_aliases={6: 0} if prev else {},
        compiler_params=pltpu.CompilerParams(dimension_semantics=("arbitrary",),
                                             vmem_limit_bytes=VMEM_LIMIT_BYTES),
        cost_estimate=pl.CostEstimate(flops=12 * n_part * d, transcendentals=n_part,
                                      bytes_accessed=n_part * (2 * d * 4 + 2 * D_PACK * 4 + LANES * 4)),
        name="combine",
    )(x1, slab, yg, yg, g, b, *prev)


def _row(a):
    return a.reshape(1, -1)


def _mixer_params(w_in, b_in, conv_w, conv_b, sg_ln_g, sg_ln_b, w_s, b_s, w_o, b_o, ln1_g, ln1_b,
                  w_rc, b_rc, w_rf, b_rf):
    d = D_MODEL
    bf16 = jnp.bfloat16
    row = _row
    ws_pairs = w_s.reshape(N_SG_HEADS // 2, 2 * CHUNK, CHUNK).astype(bf16)
    bs_full = jnp.repeat(b_s.T, SG_HEAD_DIM, axis=1)
    w_r = jnp.zeros((d, LANES), jnp.float32).at[:, 0:N_EXPERTS].set(w_rf)
    w_r = w_r.at[:, COARSE_OFF:COARSE_OFF + N_EXPERT_GROUPS].set(w_rc).astype(bf16)
    b_r = jnp.zeros((1, LANES), jnp.float32).at[0, 0:N_EXPERTS].set(b_rf)
    b_r = b_r.at[0, COARSE_OFF:COARSE_OFF + N_EXPERT_GROUPS].set(b_rc)
    return (w_in.astype(bf16), row(b_in), conv_w, row(conv_b), row(sg_ln_g), row(sg_ln_b),
            ws_pairs, bs_full, w_o.astype(bf16), row(b_o), row(ln1_g), row(ln1_b), w_r, b_r)


def _encoder_layer(x, mixer_params, w_gate, w_up, w_down, ln2_g, ln2_b, alpha):
    n = x.shape[0] * x.shape[1]
    x1, xp, slab, route, carry = _mixer(x, mixer_params, alpha)

    tm = EXPERT_TILE
    n_tiles = 2 * n // tm + N_EXPERTS
    cnt = carry[0, 0:N_EXPERTS].astype(jnp.int32)
    tiles_e = (cnt + tm - 1) // tm
    tile_end = jnp.cumsum(tiles_e)
    tile_start = tile_end - tiles_e
    row_off = tile_start * tm
    experts = jnp.arange(N_EXPERTS, dtype=jnp.int32)[:, None]

    def sorted_pos(e_row, r_row):
        e = route[e_row].astype(jnp.int32)
        return jnp.sum(jnp.where(e[None, :] == experts, row_off[:, None], 0), axis=0) + route[r_row].astype(jnp.int32)

    pos1 = sorted_pos(COL_E1, COL_R1)
    pos2 = sorted_pos(COL_E2, COL_R2)
    tile_ids = jnp.arange(n_tiles, dtype=jnp.int32)
    n_used = tile_end[N_EXPERTS - 1:N_EXPERTS]
    tile_expert = jnp.sum((tile_end[None, :] <= tile_ids[:, None]).astype(jnp.int32), axis=1)
    last_used = jnp.sum((tile_end <= n_used[0] - 1).astype(jnp.int32))
    tile_expert = jnp.minimum(tile_expert, last_used)
    tile_first = jnp.any(tile_ids[:, None] == tile_start[None, :], axis=1) & (tile_ids < n_used[0])
    tile_first = tile_first.astype(jnp.int32).at[0].set(1)
    rows_left = jnp.sum(jnp.where(tile_expert[:, None] == experts.T, (row_off + cnt)[None, :], 0), axis=1) - tile_ids * tm
    tile_nsub = jnp.where(tile_ids < n_used[0], (jnp.clip(rows_left, 0, tm) + EXPERT_SUBTILE - 1) // EXPERT_SUBTILE, 0)

    used = tiles_e > 0
    ordinal = jnp.cumsum(used.astype(jnp.int32)) - 1
    e_ids = experts[:, 0]
    later_used = used[None, :] & (e_ids[None, :] > e_ids[:, None])
    next_used = jnp.min(jnp.where(later_used, e_ids[None, :], N_EXPERTS), axis=1)
    next_used = jnp.where(next_used < N_EXPERTS, next_used, -1)
    tile_slot = ordinal[tile_expert] % 2
    tile_next = next_used[tile_expert]

    xs = _sc_dispatch(pos1, pos2, xp, n_tiles * tm)
    ys = _experts(tile_expert, tile_first, tile_nsub.astype(jnp.int32), tile_slot.astype(jnp.int32),
                  tile_next.astype(jnp.int32), n_used, xs, w_gate, w_up, w_down)
    n_part = min(n, COMBINE_PART)
    out = None
    for t0 in range(0, n, n_part):
        yg = _sc_gather(pos1[t0:t0 + n_part], pos2[t0:t0 + n_part], ys)
        out = _combine(x1, slab, yg, _row(ln2_g), _row(ln2_b), alpha, t0 // TOKEN_TILE, out)
    return out.reshape(x.shape)


def kernel(x_prompt, x_sample, w_in, b_in, conv_w, conv_b, sg_ln_g, sg_ln_b, w_s, b_s, w_o, b_o, ln1_g, ln1_b, w_rc, b_rc, w_rf, b_rf, w_gate, w_up, w_down, ln2_g, ln2_b):
    depth = w_in.shape[0]
    alpha = (2.0 * depth) ** 0.25
    xs = (x_prompt, x_sample)
    for l in range(depth):
        mixer_params = _mixer_params(w_in[l], b_in[l], conv_w[l], conv_b[l], sg_ln_g[l], sg_ln_b[l], w_s[l], b_s[l],
                                     w_o[l], b_o[l], ln1_g[l], ln1_b[l], w_rc[l], b_rc[l], w_rf[l], b_rf[l])
        xs = tuple(_encoder_layer(x, mixer_params, w_gate[l], w_up[l], w_down[l], ln2_g[l], ln2_b[l], alpha)
                   for x in xs)
    return xs
```

```python
import functools

import jax
import jax.numpy as jnp
from jax import lax
from jax.experimental import pallas as pl
from jax.experimental.pallas import tpu as pltpu
from jax.experimental.pallas import tpu_sc as plsc

D_MODEL = 1024
D_CONV = 512
D_SG = 512
N_SG_HEADS = 8
SG_HEAD_DIM = D_SG // N_SG_HEADS
CHUNK = 128
N_EXPERT_GROUPS = 4
EXPERTS_PER_GROUP = 8
N_EXPERTS = N_EXPERT_GROUPS * EXPERTS_PER_GROUP
D_EXPERT = 512
LN_EPS = 1e-5
D_PACK = D_MODEL // 2

LANES = 128
SUBLANES = 8
HALO = 16
SEQ_TILE = 512
EXPERT_TILE = 1024
EXPERT_SUBTILE = 256
TOKEN_TILE = 1024
COMBINE_PART = 16384
SC_CORES = 2
SC_SUBCORES = 16
SC_WORKERS = SC_CORES * SC_SUBCORES
SC_ROWS = 64
VMEM_LIMIT_BYTES = 56 * 1024 * 1024

COL_E1, COL_E2, COL_W1, COL_W2, COL_R1, COL_R2 = 0, 1, 2, 3, 4, 5
COARSE_OFF = N_EXPERTS


def _dot(a, b):
    return jnp.dot(a, b, preferred_element_type=jnp.float32)


def _gelu_tanh(x):
    return 0.5 * x * (1.0 + jnp.tanh(0.7978845608028654 * (x + 0.044715 * (x * x * x))))


def _layer_norm(x, g, b):
    mu = jnp.mean(x, axis=-1, keepdims=True)
    xc = x - mu
    var = jnp.mean(xc * xc, axis=-1, keepdims=True)
    return xc * lax.rsqrt(var + LN_EPS) * g + b


def _pack_halves(x):
    u32 = jnp.uint32
    lo = lax.bitcast_convert_type(x[:, 0:D_PACK].astype(jnp.bfloat16).astype(jnp.float32), u32)
    hi = lax.bitcast_convert_type(x[:, D_PACK:D_MODEL].astype(jnp.bfloat16).astype(jnp.float32), u32)
    return (hi & u32(0xFFFF0000)) | (lo >> u32(16))


def _unpack_halves(w):
    u32 = jnp.uint32
    lo = lax.bitcast_convert_type(w << u32(16), jnp.float32)
    hi = lax.bitcast_convert_type(w & u32(0xFFFF0000), jnp.float32)
    return lo, hi


def _mixer_kernel(order_ref, x_ref, x_prev_ref, x_next_ref, w_in_ref, b_in_ref, conv_w_ref, conv_b_ref,
                  sg_g_ref, sg_b_ref, ws_ref, bs_ref, w_o_ref, b_o_ref, ln1_g_ref, ln1_b_ref,
                  w_r_ref, b_r_ref,
                  x1_ref, xp_ref, slab_ref, route_ref, counts_ref,
                  xb_ref, ymix_ref, tri_ref, resid_ref, carry_ref, *, ts, nt, n_tiles, alpha):
    del order_ref
    t = pl.program_id(0)
    j = jnp.minimum(t, n_tiles - 1) % nt
    bf16 = jnp.bfloat16

    @pl.when(t == 0)
    def _():
        carry_ref[...] = jnp.zeros_like(carry_ref)
        resid_ref[...] = jnp.zeros_like(resid_ref)
        ri = lax.broadcasted_iota(jnp.int32, (ts, ts), 0)
        ci = lax.broadcasted_iota(jnp.int32, (ts, ts), 1)
        tri_ref[...] = jnp.where(ri > ci, 1.0, 0.0).astype(bf16)

    lane = lax.broadcasted_iota(jnp.int32, (ts, LANES), 1)
    lane_f = lane.astype(jnp.float32)
    neg = jnp.float32(-jnp.inf)
    big = jnp.float32(1e9)
    is_c = (lane >= COARSE_OFF) & (lane < COARSE_OFF + N_EXPERT_GROUPS)

    def proj(lhs, lo, hi):
        return _dot(lhs, w_in_ref[:, lo:hi]) + b_in_ref[:, lo:hi]

    def back_norm():
        x1 = _layer_norm(resid_ref[...], ln1_g_ref[...], ln1_b_ref[...])
        x1_ref[...] = x1
        xp_ref[...] = _pack_halves(x1)
        return x1.astype(bf16)

    def back_route(x1b):
        logits = _dot(x1b, w_r_ref[...]) + b_r_ref[...]
        lc = jnp.where(is_c, logits, neg)
        mx = jnp.max(lc, axis=1, keepdims=True)
        grp = jnp.min(jnp.where(lc == mx, lane_f - COARSE_OFF, big), axis=1, keepdims=True)
        p_grp = 1.0 / jnp.sum(jnp.where(is_c, jnp.exp(logits - mx), 0.0), axis=1, keepdims=True)
        grp_lo = grp * EXPERTS_PER_GROUP
        in_grp = (lane_f >= grp_lo) & (lane_f < grp_lo + EXPERTS_PER_GROUP)
        lf = jnp.where(in_grp, logits, neg)
        v1 = jnp.max(lf, axis=1, keepdims=True)
        e1 = jnp.min(jnp.where(lf == v1, lane_f, big), axis=1, keepdims=True)
        lf2 = jnp.where(lane_f == e1, neg, lf)
        v2 = jnp.max(lf2, axis=1, keepdims=True)
        e2 = jnp.min(jnp.where(lf2 == v2, lane_f, big), axis=1, keepdims=True)
        a = jnp.exp(v2 - v1)
        return e1, e2, p_grp / (1.0 + a), p_grp * a / (1.0 + a)

    def back_rank(e1, e2, w1, w2):
        hit1 = lane_f == e1
        hit2 = lane_f == e2
        onehot = jnp.where(hit1 | hit2, 1.0, 0.0)
        carry = carry_ref[...]
        before = _dot(tri_ref[...], onehot.astype(bf16)) + carry
        r1 = jnp.sum(jnp.where(hit1, before, 0.0), axis=1, keepdims=True)
        r2 = jnp.sum(jnp.where(hit2, before, 0.0), axis=1, keepdims=True)
        carry = jnp.where(t > 0, carry + jnp.sum(onehot, axis=0, keepdims=True), carry)
        carry_ref[...] = carry
        counts_ref[...] = jnp.broadcast_to(carry, counts_ref.shape)
        slab = jnp.where(lane == COL_E1, e1, 0.0)
        slab = jnp.where(lane == COL_E2, e2, slab)
        slab = jnp.where(lane == COL_W1, w1, slab)
        slab = jnp.where(lane == COL_W2, w2, slab)
        slab = jnp.where(lane == COL_R1, r1, slab)
        slab = jnp.where(lane == COL_R2, r2, slab)
        slab_ref[...] = slab
        route_ref[...] = slab.T[0:SUBLANES, :]

    def front_load():
        xb_ref[0:HALO, :] = x_prev_ref[...].astype(bf16)
        xb_ref[HALO:HALO + ts, :] = x_ref[...].astype(bf16)
        xb_ref[HALO + ts:HALO + ts + HALO, :] = x_next_ref[...].astype(bf16)

    def front_conv():
        xe = xb_ref[...]
        g_e = proj(xe, 0, D_CONV) * proj(xe, 2 * D_CONV, 3 * D_CONV)
        row_e = lax.broadcasted_iota(jnp.int32, (ts + 2 * HALO, 1), 0)
        has_prev = jnp.where(j > 0, 1.0, 0.0)
        has_next = jnp.where(j < nt - 1, 1.0, 0.0)
        g_e = g_e * jnp.where(row_e < HALO, has_prev, jnp.where(row_e >= HALO + ts, has_next, 1.0))
        conv = (g_e[HALO - 1:HALO - 1 + ts, :] * conv_w_ref[0:1, :] + g_e[HALO:HALO + ts, :] * conv_w_ref[1:2, :]
                + g_e[HALO + 1:HALO + 1 + ts, :] * conv_w_ref[2:3, :] + conv_b_ref[...])
        y_a = proj(xb_ref[HALO:HALO + ts, :], D_CONV, 2 * D_CONV) * conv
        ymix_ref[:, 0:D_CONV] = y_a.astype(bf16)

    def front_gate_proj():
        xm = xb_ref[HALO:HALO + ts, :]
        return proj(xm, 3 * D_CONV, 3 * D_CONV + D_SG), proj(xm, 3 * D_CONV + D_SG, 3 * D_CONV + 2 * D_SG)

    def front_gate(u_pre, v_pre):
        u = _gelu_tanh(u_pre)
        v = _gelu_tanh(v_pre)
        v_ln = _layer_norm(v, sg_g_ref[...], sg_b_ref[...]).astype(bf16)
        first_head = lax.broadcasted_iota(jnp.int32, (CHUNK, LANES), 1) < SG_HEAD_DIM
        for q0 in range(0, ts, 2 * CHUNK):
            q1 = q0 + CHUNK
            for hp in range(N_SG_HEADS // 2):
                c0 = hp * LANES
                rhs = jnp.concatenate([v_ln[q0:q0 + CHUNK, c0:c0 + LANES], v_ln[q1:q1 + CHUNK, c0:c0 + LANES]], axis=1)
                res = _dot(ws_ref[hp], rhs)
                bias = bs_ref[:, c0:c0 + LANES]
                m0 = jnp.where(first_head, res[0:CHUNK, 0:LANES], res[CHUNK:2 * CHUNK, 0:LANES]) + bias
                m1 = jnp.where(first_head, res[0:CHUNK, LANES:2 * LANES], res[CHUNK:2 * CHUNK, LANES:2 * LANES]) + bias
                ymix_ref[q0:q0 + CHUNK, D_CONV + c0:D_CONV + c0 + LANES] = (u[q0:q0 + CHUNK, c0:c0 + LANES] * m0).astype(bf16)
                ymix_ref[q1:q1 + CHUNK, D_CONV + c0:D_CONV + c0 + LANES] = (u[q1:q1 + CHUNK, c0:c0 + LANES] * m1).astype(bf16)

    def front_out_a():
        resid_ref[...] = alpha * x_ref[...] + b_o_ref[...] + _dot(ymix_ref[:, 0:D_CONV], w_o_ref[0:D_CONV, :])

    def front_out_b():
        resid_ref[...] += _dot(ymix_ref[:, D_CONV:D_MODEL], w_o_ref[D_CONV:D_MODEL, :])

    front_load()
    x1b = back_norm()
    routing = back_route(x1b)
    front_conv()
    back_rank(*routing)
    gate_pre = front_gate_proj()
    front_out_a()
    front_gate(*gate_pre)
    front_out_b()


def _mixer(x, p, alpha, order):
    nb, s, d = x.shape
    ts = SEQ_TILE
    nt = s // ts
    hb = ts // HALO
    n_tiles = nb * nt
    n_total = n_tiles * ts

    front = lambda t: jnp.minimum(t, n_tiles - 1)
    back = lambda t: jnp.maximum(t - 1, 0)
    cur = lambda t: (front(t) // nt, front(t) % nt, 0)
    prv = lambda t: (front(t) // nt, jnp.maximum((front(t) % nt) * hb - 1, 0), 0)
    nxt = lambda t: (front(t) // nt, jnp.minimum((front(t) % nt + 1) * hb, s // HALO - 1), 0)
    full = lambda a: pl.BlockSpec(a.shape, lambda t: (0,) * a.ndim)
    in_specs = [full(order), pl.BlockSpec((None, ts, d), cur), pl.BlockSpec((None, HALO, d), prv),
                pl.BlockSpec((None, HALO, d), nxt)] + [full(a) for a in p]
    out_shape = (jax.ShapeDtypeStruct((n_total, d), jnp.float32),
                 jax.ShapeDtypeStruct((n_total, D_PACK), jnp.uint32),
                 jax.ShapeDtypeStruct((n_total, LANES), jnp.float32),
                 jax.ShapeDtypeStruct((SUBLANES, n_total), jnp.float32),
                 jax.ShapeDtypeStruct((SUBLANES, LANES), jnp.float32))
    row_blk = lambda t: (back(t), 0)
    out_specs = (pl.BlockSpec((ts, d), row_blk),
                 pl.BlockSpec((ts, D_PACK), row_blk),
                 pl.BlockSpec((ts, LANES), row_blk),
                 pl.BlockSpec((SUBLANES, ts), lambda t: (0, back(t))),
                 pl.BlockSpec((SUBLANES, LANES), lambda t: (0, 0)))
    return pl.pallas_call(
        functools.partial(_mixer_kernel, ts=ts, nt=nt, n_tiles=n_tiles, alpha=alpha),
        grid=(n_tiles + 1,),
        in_specs=in_specs,
        out_specs=out_specs,
        out_shape=out_shape,
        scratch_shapes=[pltpu.VMEM((ts + 2 * HALO, d), jnp.bfloat16),
                        pltpu.VMEM((ts, d), jnp.bfloat16),
                        pltpu.VMEM((ts, ts), jnp.bfloat16),
                        pltpu.VMEM((ts, d), jnp.float32),
                        pltpu.VMEM((1, LANES), jnp.float32)],
        compiler_params=pltpu.CompilerParams(dimension_semantics=("arbitrary",),
                                             vmem_limit_bytes=VMEM_LIMIT_BYTES),
        cost_estimate=pl.CostEstimate(
            flops=2 * n_total * (d * (3 * D_CONV + 2 * D_SG) + CHUNK * D_SG + d * d + d * LANES + ts * LANES),
            transcendentals=n_total * (2 * D_SG + 2 * LANES),
            bytes_accessed=n_total * (2 * d * 4 + D_PACK * 4 + LANES * 4 + SUBLANES * 4)),
        name="mixer",
    )(order, x, x, x, *p)


def _sc_mesh():
    return plsc.VectorSubcoreMesh(core_axis_name="c", subcore_axis_name="s",
                                  num_cores=SC_CORES, num_subcores=SC_SUBCORES)


def _sc_worker_base(per_worker):
    return (lax.axis_index("s") * SC_CORES + lax.axis_index("c")) * per_worker


def _sc_dispatch(pos1, pos2, xp, n_rows):
    n, dp = xp.shape
    per_worker = n // SC_WORKERS
    steps = per_worker // SC_ROWS

    def body(pos1_hbm, pos2_hbm, xp_hbm, xs_hbm, idx1_v, idx2_v, rows_v, sem):
        base = _sc_worker_base(per_worker)

        @pl.loop(0, steps)
        def _(k):
            off = pl.multiple_of(base + k * SC_ROWS, SC_ROWS)
            pltpu.sync_copy(pos1_hbm.at[pl.ds(off, SC_ROWS)], idx1_v)
            pltpu.sync_copy(pos2_hbm.at[pl.ds(off, SC_ROWS)], idx2_v)
            pltpu.sync_copy(xp_hbm.at[pl.ds(off, SC_ROWS)], rows_v)
            c1 = pltpu.async_copy(rows_v, xs_hbm.at[idx1_v], sem)
            c2 = pltpu.async_copy(rows_v, xs_hbm.at[idx2_v], sem)
            c1.wait()
            c2.wait()

    return pl.kernel(
        body,
        out_type=jax.ShapeDtypeStruct((n_rows, dp), xp.dtype),
        mesh=_sc_mesh(),
        scratch_types=[pltpu.VMEM((SC_ROWS,), jnp.int32), pltpu.VMEM((SC_ROWS,), jnp.int32),
                       pltpu.VMEM((SC_ROWS, dp), xp.dtype), pltpu.SemaphoreType.DMA],
        compiler_params=pltpu.CompilerParams(use_tc_tiling_on_sc=True),
        cost_estimate=pl.CostEstimate(flops=0, transcendentals=0, bytes_accessed=3 * n * dp * 4 + 2 * n * 4),
        name="sc_dispatch",
    )(pos1, pos2, xp)


def _sc_gather(pos1, pos2, ys):
    n = pos1.shape[0]
    dp = ys.shape[1]
    per_worker = n // SC_WORKERS
    steps = per_worker // SC_ROWS

    def body(pos1_hbm, pos2_hbm, ys_hbm, out_hbm, idx_v, rows_v, sem):
        base = _sc_worker_base(per_worker)

        @pl.loop(0, steps)
        def _(k):
            off = pl.multiple_of(base + k * SC_ROWS, SC_ROWS)
            for slot, pos_hbm in enumerate((pos1_hbm, pos2_hbm)):
                pltpu.sync_copy(pos_hbm.at[pl.ds(off, SC_ROWS)], idx_v)
                pltpu.async_copy(ys_hbm.at[idx_v], rows_v, sem).wait()
                pltpu.sync_copy(rows_v, out_hbm.at[slot, pl.ds(off, SC_ROWS)])

    return pl.kernel(
        body,
        out_type=jax.ShapeDtypeStruct((2, n, dp), ys.dtype),
        mesh=_sc_mesh(),
        scratch_types=[pltpu.VMEM((SC_ROWS,), jnp.int32), pltpu.VMEM((SC_ROWS, dp), ys.dtype),
                       pltpu.SemaphoreType.DMA],
        compiler_params=pltpu.CompilerParams(use_tc_tiling_on_sc=True),
        cost_estimate=pl.CostEstimate(flops=0, transcendentals=0, bytes_accessed=4 * n * dp * 4 + 2 * n * 4),
        name="sc_gather",
    )(pos1, pos2, ys)


def _expert_kernel(te_ref, first_ref, nsub_ref, slot_ref, next_ref, nused_ref, xs_ref, wg_hbm, wu_hbm, wd_hbm,
                   ys_ref, wg_buf, wu_buf, wd_buf, wgu_bf_ref, wd_bf_ref, sem):
    i = pl.program_id(0)
    bf16 = jnp.bfloat16
    sub = EXPERT_SUBTILE

    def weight_copies(expert, slot):
        return (pltpu.make_async_copy(wg_hbm.at[expert], wg_buf.at[slot], sem.at[slot]),
                pltpu.make_async_copy(wu_hbm.at[expert], wu_buf.at[slot], sem.at[slot]),
                pltpu.make_async_copy(wd_hbm.at[expert], wd_buf.at[slot], sem.at[slot]))

    @pl.when(first_ref[i] == 1)
    def _():
        slot = slot_ref[i]

        @pl.when(i == 0)
        def _():
            for c in weight_copies(te_ref[i], slot):
                c.start()

        for c in weight_copies(te_ref[i], slot):
            c.wait()

        @pl.when(next_ref[i] >= 0)
        def _():
            for c in weight_copies(next_ref[i], 1 - slot):
                c.start()

        wgu_bf_ref[:, 0:D_EXPERT] = wg_buf[slot].astype(bf16)
        wgu_bf_ref[:, D_EXPERT:2 * D_EXPERT] = wu_buf[slot].astype(bf16)
        wd_bf_ref[...] = wd_buf[slot].astype(bf16)

    def swiglu(r0):
        rows = pl.ds(r0, sub)
        lo, hi = _unpack_halves(xs_ref[rows, :])
        gu = (_dot(lo.astype(bf16), wgu_bf_ref[0:D_PACK, :])
              + _dot(hi.astype(bf16), wgu_bf_ref[D_PACK:D_MODEL, :]))
        g = gu[:, 0:D_EXPERT]
        h = g * jax.nn.sigmoid(g) * gu[:, D_EXPERT:2 * D_EXPERT]
        ys_ref[rows, :] = _pack_halves(_dot(h.astype(bf16), wd_bf_ref[...]))

    n_sub = nsub_ref[i]

    def pair(k, c):
        r0 = pl.multiple_of(k * (2 * sub), 2 * sub)
        swiglu(r0)
        swiglu(r0 + sub)
        return c

    lax.fori_loop(0, n_sub // 2, pair, 0)

    @pl.when(n_sub % 2 == 1)
    def _():
        swiglu(pl.multiple_of((n_sub - 1) * sub, sub))

    def clear(k, c):
        ys_ref[pl.ds(pl.multiple_of(k * sub, sub), sub), :] = jnp.zeros((sub, D_PACK), ys_ref.dtype)
        return c

    lax.fori_loop(n_sub, jnp.where(i < nused_ref[0], ys_ref.shape[0] // sub, 0), clear, 0)


def _experts(tile_expert, tile_first, tile_nsub, tile_slot, tile_next, n_used, xs, w_gate, w_up, w_down):
    n_rows, dp = xs.shape
    d = D_MODEL
    tm = EXPERT_TILE
    row_blk = lambda i, te, fi, ns, sl, nx, nu: (jnp.minimum(i, nu[0] - 1), 0)
    grid_spec = pltpu.PrefetchScalarGridSpec(
        num_scalar_prefetch=6,
        grid=(n_rows // tm,),
        in_specs=[pl.BlockSpec((tm, dp), row_blk),
                  pl.BlockSpec(memory_space=pl.ANY),
                  pl.BlockSpec(memory_space=pl.ANY),
                  pl.BlockSpec(memory_space=pl.ANY)],
        out_specs=pl.BlockSpec((tm, dp), row_blk),
        scratch_shapes=[pltpu.VMEM((2, d, D_EXPERT), jnp.float32),
                        pltpu.VMEM((2, d, D_EXPERT), jnp.float32),
                        pltpu.VMEM((2, D_EXPERT, d), jnp.float32),
                        pltpu.VMEM((d, 2 * D_EXPERT), jnp.bfloat16),
                        pltpu.VMEM((D_EXPERT, d), jnp.bfloat16),
                        pltpu.SemaphoreType.DMA((2,))],
    )
    return pl.pallas_call(
        _expert_kernel,
        grid_spec=grid_spec,
        out_shape=jax.ShapeDtypeStruct((n_rows, dp), jnp.uint32),
        compiler_params=pltpu.CompilerParams(dimension_semantics=("arbitrary",),
                                             vmem_limit_bytes=VMEM_LIMIT_BYTES),
        cost_estimate=pl.CostEstimate(flops=2 * n_rows * 3 * d * D_EXPERT, transcendentals=n_rows * D_EXPERT,
                                      bytes_accessed=2 * n_rows * dp * 4 + N_EXPERTS * 3 * d * D_EXPERT * 4),
        name="experts",
    )(tile_expert, tile_first, tile_nsub, tile_slot, tile_next, n_used, xs, w_gate, w_up, w_down)


def _combine_kernel(x1_ref, slab_ref, y1_ref, y2_ref, g_ref, b_ref, *rest, alpha):
    out_ref = rest[-1]
    slab = slab_ref[...]
    w1 = slab[:, COL_W1:COL_W1 + 1]
    w2 = slab[:, COL_W2:COL_W2 + 1]
    lo1, hi1 = _unpack_halves(y1_ref[...])
    lo2, hi2 = _unpack_halves(y2_ref[...])
    moe = jnp.concatenate([w1 * lo1 + w2 * lo2, w1 * hi1 + w2 * hi2], axis=1)
    out_ref[...] = _layer_norm(alpha * x1_ref[...] + moe, g_ref[...], b_ref[...])


def _combine(x1, slab, yg, g, b, alpha, tile0, out_prev):
    n, d = x1.shape
    n_part = yg.shape[1]
    tt = TOKEN_TILE
    prev = [] if out_prev is None else [out_prev]
    return pl.pallas_call(
        functools.partial(_combine_kernel, alpha=alpha),
        grid=(n_part // tt,),
        in_specs=[pl.BlockSpec((tt, d), lambda i: (tile0 + i, 0)),
                  pl.BlockSpec((tt, LANES), lambda i: (tile0 + i, 0)),
                  pl.BlockSpec((None, tt, D_PACK), lambda i: (0, i, 0)),
                  pl.BlockSpec((None, tt, D_PACK), lambda i: (1, i, 0)),
                  pl.BlockSpec((1, d), lambda i: (0, 0)),
                  pl.BlockSpec((1, d), lambda i: (0, 0))] + [pl.BlockSpec(memory_space=pl.ANY)] * len(prev),
        out_specs=pl.BlockSpec((tt, d), lambda i: (tile0 + i, 0)),
        out_shape=jax.ShapeDtypeStruct((n, d), jnp.float32),
        input_output_aliases={6: 0} if prev else {},
        compiler_params=pltpu.CompilerParams(dimension_semantics=("arbitrary",),
                                             vmem_limit_bytes=VMEM_LIMIT_BYTES),
        cost_estimate=pl.CostEstimate(flops=12 * n_part * d, transcendentals=n_part,
                                      bytes_accessed=n_part * (2 * d * 4 + 2 * D_PACK * 4 + LANES * 4)),
        name="combine",
    )(x1, slab, yg, yg, g, b, *prev)


def _row(a):
    return a.reshape(1, -1)


def _mixer_params(w_in, b_in, conv_w, conv_b, sg_ln_g, sg_ln_b, w_s, b_s, w_o, b_o, ln1_g, ln1_b,
                  w_rc, b_rc, w_rf, b_rf):
    d = D_MODEL
    bf16 = jnp.bfloat16
    row = _row
    ws_pairs = w_s.reshape(N_SG_HEADS // 2, 2 * CHUNK, CHUNK).astype(bf16)
    bs_full = jnp.repeat(b_s.T, SG_HEAD_DIM, axis=1)
    w_r = jnp.zeros((d, LANES), jnp.float32).at[:, 0:N_EXPERTS].set(w_rf)
    w_r = w_r.at[:, COARSE_OFF:COARSE_OFF + N_EXPERT_GROUPS].set(w_rc).astype(bf16)
    b_r = jnp.zeros((1, LANES), jnp.float32).at[0, 0:N_EXPERTS].set(b_rf)
    b_r = b_r.at[0, COARSE_OFF:COARSE_OFF + N_EXPERT_GROUPS].set(b_rc)
    return (w_in.astype(bf16), row(b_in), conv_w, row(conv_b), row(sg_ln_g), row(sg_ln_b),
            ws_pairs, bs_full, w_o.astype(bf16), row(b_o), row(ln1_g), row(ln1_b), w_r, b_r)


def _mix_and_plan(x, mixer_params, alpha, order):
    n = x.shape[0] * x.shape[1]
    x1, xp, slab, route, carry = _mixer(x, mixer_params, alpha, order)

    tm = EXPERT_TILE
    n_tiles = 2 * n // tm + N_EXPERTS
    cnt = carry[0, 0:N_EXPERTS].astype(jnp.int32)
    tiles_e = (cnt + tm - 1) // tm
    tile_end = jnp.cumsum(tiles_e)
    tile_start = tile_end - tiles_e
    row_off = tile_start * tm
    experts = jnp.arange(N_EXPERTS, dtype=jnp.int32)[:, None]

    def sorted_pos(e_row, r_row):
        e = route[e_row].astype(jnp.int32)
        return jnp.sum(jnp.where(e[None, :] == experts, row_off[:, None], 0), axis=0) + route[r_row].astype(jnp.int32)

    pos1 = sorted_pos(COL_E1, COL_R1)
    pos2 = sorted_pos(COL_E2, COL_R2)
    tile_ids = jnp.arange(n_tiles, dtype=jnp.int32)
    n_used = tile_end[N_EXPERTS - 1:N_EXPERTS]
    tile_expert = jnp.sum((tile_end[None, :] <= tile_ids[:, None]).astype(jnp.int32), axis=1)
    last_used = jnp.sum((tile_end <= n_used[0] - 1).astype(jnp.int32))
    tile_expert = jnp.minimum(tile_expert, last_used)
    tile_first = jnp.any(tile_ids[:, None] == tile_start[None, :], axis=1) & (tile_ids < n_used[0])
    tile_first = tile_first.astype(jnp.int32).at[0].set(1)
    rows_left = jnp.sum(jnp.where(tile_expert[:, None] == experts.T, (row_off + cnt)[None, :], 0), axis=1) - tile_ids * tm
    tile_nsub = jnp.where(tile_ids < n_used[0], (jnp.clip(rows_left, 0, tm) + EXPERT_SUBTILE - 1) // EXPERT_SUBTILE, 0)

    used = tiles_e > 0
    ordinal = jnp.cumsum(used.astype(jnp.int32)) - 1
    e_ids = experts[:, 0]
    later_used = used[None, :] & (e_ids[None, :] > e_ids[:, None])
    next_used = jnp.min(jnp.where(later_used, e_ids[None, :], N_EXPERTS), axis=1)
    next_used = jnp.where(next_used < N_EXPERTS, next_used, -1)
    tile_slot = ordinal[tile_expert] % 2
    tile_next = next_used[tile_expert]
    tiles = (tile_expert, tile_first, tile_nsub.astype(jnp.int32), tile_slot.astype(jnp.int32),
             tile_next.astype(jnp.int32), n_used)
    return x1, xp, slab, pos1, pos2, tiles, n_tiles * tm


def _route_and_combine(x, plan, w_gate, w_up, w_down, ln2_g, ln2_b, alpha):
    n = x.shape[0] * x.shape[1]
    x1, xp, slab, pos1, pos2, tiles, n_rows = plan
    xs = _sc_dispatch(pos1, pos2, xp, n_rows)
    ys = _experts(*tiles, xs, w_gate, w_up, w_down)
    n_part = min(n, COMBINE_PART)
    out = None
    for t0 in range(0, n, n_part):
        yg = _sc_gather(pos1[t0:t0 + n_part], pos2[t0:t0 + n_part], ys)
        out = _combine(x1, slab, yg, _row(ln2_g), _row(ln2_b), alpha, t0 // TOKEN_TILE, out)
    return out.reshape(x.shape)


def kernel(x_prompt, x_sample, w_in, b_in, conv_w, conv_b, sg_ln_g, sg_ln_b, w_s, b_s, w_o, b_o, ln1_g, ln1_b, w_rc, b_rc, w_rf, b_rf, w_gate, w_up, w_down, ln2_g, ln2_b):
    depth = w_in.shape[0]
    alpha = (2.0 * depth) ** 0.25
    xs = (x_prompt, x_sample)
    for l in range(depth):
        mixer_params = _mixer_params(w_in[l], b_in[l], conv_w[l], conv_b[l], sg_ln_g[l], sg_ln_b[l], w_s[l], b_s[l],
                                     w_o[l], b_o[l], ln1_g[l], ln1_b[l], w_rc[l], b_rc[l], w_rf[l], b_rf[l])
        plans = []
        order = jnp.zeros((SUBLANES, LANES), jnp.int32)
        for x in xs:
            plans.append(_mix_and_plan(x, mixer_params, alpha, order))
            order = plans[-1][3][0:SUBLANES * LANES].reshape(SUBLANES, LANES)
        xs = tuple(_route_and_combine(x, plan, w_gate[l], w_up[l], w_down[l], ln2_g[l], ln2_b[l], alpha)
                   for x, plan in zip(xs, plans))
    return xs
```

```python
import functools

import jax
import jax.numpy as jnp
from jax import lax
from jax.experimental import pallas as pl
from jax.experimental.pallas import tpu as pltpu
from jax.experimental.pallas import tpu_sc as plsc

D_MODEL = 1024
D_CONV = 512
D_SG = 512
N_SG_HEADS = 8
SG_HEAD_DIM = D_SG // N_SG_HEADS
CHUNK = 128
N_EXPERT_GROUPS = 4
EXPERTS_PER_GROUP = 8
N_EXPERTS = N_EXPERT_GROUPS * EXPERTS_PER_GROUP
D_EXPERT = 512
LN_EPS = 1e-5
D_PACK = D_MODEL // 2

LANES = 128
SUBLANES = 8
HALO = 16
SEQ_TILE = 512
EXPERT_TILE = 1024
EXPERT_SUBTILE = 256
TOKEN_TILE = 1024
COMBINE_PART = 16384
SC_CORES = 2
SC_SUBCORES = 16
SC_WORKERS = SC_CORES * SC_SUBCORES
SC_ROWS = 64
VMEM_LIMIT_BYTES = 56 * 1024 * 1024

COL_E1, COL_E2, COL_W1, COL_W2, COL_R1, COL_R2 = 0, 1, 2, 3, 4, 5
COARSE_OFF = N_EXPERTS


def _dot(a, b):
    return jnp.dot(a, b, preferred_element_type=jnp.float32)


def _gelu_tanh(x):
    return 0.5 * x * (1.0 + jnp.tanh(0.7978845608028654 * (x + 0.044715 * (x * x * x))))


def _layer_norm(x, g, b):
    mu = jnp.mean(x, axis=-1, keepdims=True)
    xc = x - mu
    var = jnp.mean(xc * xc, axis=-1, keepdims=True)
    return xc * lax.rsqrt(var + LN_EPS) * g + b


def _pack_halves(x):
    u32 = jnp.uint32
    lo = lax.bitcast_convert_type(x[:, 0:D_PACK].astype(jnp.bfloat16).astype(jnp.float32), u32)
    hi = lax.bitcast_convert_type(x[:, D_PACK:D_MODEL].astype(jnp.bfloat16).astype(jnp.float32), u32)
    return (hi & u32(0xFFFF0000)) | (lo >> u32(16))


def _unpack_halves(w):
    u32 = jnp.uint32
    lo = lax.bitcast_convert_type(w << u32(16), jnp.float32)
    hi = lax.bitcast_convert_type(w & u32(0xFFFF0000), jnp.float32)
    return lo, hi


def _mixer_kernel(x_ref, x_prev_ref, x_next_ref, w_in_ref, b_in_ref, conv_w_ref, conv_b_ref,
                  sg_g_ref, sg_b_ref, ws_ref, bs_ref, w_o_ref, b_o_ref, ln1_g_ref, ln1_b_ref,
                  w_r_ref, b_r_ref,
                  x1_ref, xp_ref, slab_ref, route_ref, counts_ref,
                  xb_ref, ymix_ref, tri_ref, resid_ref, carry_ref, *, ts, nt, n_tiles, alpha):
    t = pl.program_id(0)
    j = jnp.minimum(t, n_tiles - 1) % nt
    bf16 = jnp.bfloat16

    @pl.when(t == 0)
    def _():
        carry_ref[...] = jnp.zeros_like(carry_ref)
        resid_ref[...] = jnp.zeros_like(resid_ref)
        ri = lax.broadcasted_iota(jnp.int32, (ts, ts), 0)
        ci = lax.broadcasted_iota(jnp.int32, (ts, ts), 1)
        tri_ref[...] = jnp.where(ri > ci, 1.0, 0.0).astype(bf16)

    lane = lax.broadcasted_iota(jnp.int32, (ts, LANES), 1)
    lane_f = lane.astype(jnp.float32)
    neg = jnp.float32(-jnp.inf)
    big = jnp.float32(1e9)
    is_c = (lane >= COARSE_OFF) & (lane < COARSE_OFF + N_EXPERT_GROUPS)

    def proj(lhs, lo, hi):
        return _dot(lhs, w_in_ref[:, lo:hi]) + b_in_ref[:, lo:hi]

    def back_norm():
        x1 = _layer_norm(resid_ref[...], ln1_g_ref[...], ln1_b_ref[...])
        x1_ref[...] = x1
        xp_ref[...] = _pack_halves(x1)
        return x1.astype(bf16)

    def back_route(x1b):
        logits = _dot(x1b, w_r_ref[...]) + b_r_ref[...]
        lc = jnp.where(is_c, logits, neg)
        mx = jnp.max(lc, axis=1, keepdims=True)
        grp = jnp.min(jnp.where(lc == mx, lane_f - COARSE_OFF, big), axis=1, keepdims=True)
        p_grp = 1.0 / jnp.sum(jnp.where(is_c, jnp.exp(logits - mx), 0.0), axis=1, keepdims=True)
        grp_lo = grp * EXPERTS_PER_GROUP
        in_grp = (lane_f >= grp_lo) & (lane_f < grp_lo + EXPERTS_PER_GROUP)
        lf = jnp.where(in_grp, logits, neg)
        v1 = jnp.max(lf, axis=1, keepdims=True)
        e1 = jnp.min(jnp.where(lf == v1, lane_f, big), axis=1, keepdims=True)
        lf2 = jnp.where(lane_f == e1, neg, lf)
        v2 = jnp.max(lf2, axis=1, keepdims=True)
        e2 = jnp.min(jnp.where(lf2 == v2, lane_f, big), axis=1, keepdims=True)
        a = jnp.exp(v2 - v1)
        return e1, e2, p_grp / (1.0 + a), p_grp * a / (1.0 + a)

    def back_rank(e1, e2, w1, w2):
        hit1 = lane_f == e1
        hit2 = lane_f == e2
        onehot = jnp.where(hit1 | hit2, 1.0, 0.0)
        carry = carry_ref[...]
        before = _dot(tri_ref[...], onehot.astype(bf16)) + carry
        r1 = jnp.sum(jnp.where(hit1, before, 0.0), axis=1, keepdims=True)
        r2 = jnp.sum(jnp.where(hit2, before, 0.0), axis=1, keepdims=True)
        carry = jnp.where(t > 0, carry + jnp.sum(onehot, axis=0, keepdims=True), carry)
        carry_ref[...] = carry
        counts_ref[...] = jnp.broadcast_to(carry, counts_ref.shape)
        slab = jnp.where(lane == COL_E1, e1, 0.0)
        slab = jnp.where(lane == COL_E2, e2, slab)
        slab = jnp.where(lane == COL_W1, w1, slab)
        slab = jnp.where(lane == COL_W2, w2, slab)
        slab = jnp.where(lane == COL_R1, r1, slab)
        slab = jnp.where(lane == COL_R2, r2, slab)
        slab_ref[...] = slab
        route_ref[...] = slab.T[0:SUBLANES, :]

    def front_load():
        xb_ref[0:HALO, :] = x_prev_ref[...].astype(bf16)
        xb_ref[HALO:HALO + ts, :] = x_ref[...].astype(bf16)
        xb_ref[HALO + ts:HALO + ts + HALO, :] = x_next_ref[...].astype(bf16)

    def front_conv():
        xe = xb_ref[...]
        g_e = proj(xe, 0, D_CONV) * proj(xe, 2 * D_CONV, 3 * D_CONV)
        row_e = lax.broadcasted_iota(jnp.int32, (ts + 2 * HALO, 1), 0)
        has_prev = jnp.where(j > 0, 1.0, 0.0)
        has_next = jnp.where(j < nt - 1, 1.0, 0.0)
        g_e = g_e * jnp.where(row_e < HALO, has_prev, jnp.where(row_e >= HALO + ts, has_next, 1.0))
        conv = (g_e[HALO - 1:HALO - 1 + ts, :] * conv_w_ref[0:1, :] + g_e[HALO:HALO + ts, :] * conv_w_ref[1:2, :]
                + g_e[HALO + 1:HALO + 1 + ts, :] * conv_w_ref[2:3, :] + conv_b_ref[...])
        y_a = proj(xb_ref[HALO:HALO + ts, :], D_CONV, 2 * D_CONV) * conv
        ymix_ref[:, 0:D_CONV] = y_a.astype(bf16)

    def front_gate_proj():
        xm = xb_ref[HALO:HALO + ts, :]
        return proj(xm, 3 * D_CONV, 3 * D_CONV + D_SG), proj(xm, 3 * D_CONV + D_SG, 3 * D_CONV + 2 * D_SG)

    def front_gate(u_pre, v_pre):
        u = _gelu_tanh(u_pre)
        v = _gelu_tanh(v_pre)
        v_ln = _layer_norm(v, sg_g_ref[...], sg_b_ref[...]).astype(bf16)
        first_head = lax.broadcasted_iota(jnp.int32, (CHUNK, LANES), 1) < SG_HEAD_DIM
        for q0 in range(0, ts, 2 * CHUNK):
            q1 = q0 + CHUNK
            for hp in range(N_SG_HEADS // 2):
                c0 = hp * LANES
                rhs = jnp.concatenate([v_ln[q0:q0 + CHUNK, c0:c0 + LANES], v_ln[q1:q1 + CHUNK, c0:c0 + LANES]], axis=1)
                res = _dot(ws_ref[hp], rhs)
                bias = bs_ref[:, c0:c0 + LANES]
                m0 = jnp.where(first_head, res[0:CHUNK, 0:LANES], res[CHUNK:2 * CHUNK, 0:LANES]) + bias
                m1 = jnp.where(first_head, res[0:CHUNK, LANES:2 * LANES], res[CHUNK:2 * CHUNK, LANES:2 * LANES]) + bias
                ymix_ref[q0:q0 + CHUNK, D_CONV + c0:D_CONV + c0 + LANES] = (u[q0:q0 + CHUNK, c0:c0 + LANES] * m0).astype(bf16)
                ymix_ref[q1:q1 + CHUNK, D_CONV + c0:D_CONV + c0 + LANES] = (u[q1:q1 + CHUNK, c0:c0 + LANES] * m1).astype(bf16)

    def front_out_a():
        resid_ref[...] = alpha * x_ref[...] + b_o_ref[...] + _dot(ymix_ref[:, 0:D_CONV], w_o_ref[0:D_CONV, :])

    def front_out_b():
        resid_ref[...] += _dot(ymix_ref[:, D_CONV:D_MODEL], w_o_ref[D_CONV:D_MODEL, :])

    front_load()
    x1b = back_norm()
    routing = back_route(x1b)
    front_conv()
    back_rank(*routing)
    gate_pre = front_gate_proj()
    front_out_a()
    front_gate(*gate_pre)
    front_out_b()


def _mixer(x, p, alpha):
    nb, s, d = x.shape
    ts = SEQ_TILE
    nt = s // ts
    hb = ts // HALO
    n_tiles = nb * nt
    n_total = n_tiles * ts

    front = lambda t: jnp.minimum(t, n_tiles - 1)
    back = lambda t: jnp.maximum(t - 1, 0)
    cur = lambda t: (front(t) // nt, front(t) % nt, 0)
    prv = lambda t: (front(t) // nt, jnp.maximum((front(t) % nt) * hb - 1, 0), 0)
    nxt = lambda t: (front(t) // nt, jnp.minimum((front(t) % nt + 1) * hb, s // HALO - 1), 0)
    full = lambda a: pl.BlockSpec(a.shape, lambda t: (0,) * a.ndim)
    in_specs = [pl.BlockSpec((None, ts, d), cur), pl.BlockSpec((None, HALO, d), prv),
                pl.BlockSpec((None, HALO, d), nxt)] + [full(a) for a in p]
    out_shape = (jax.ShapeDtypeStruct((n_total, d), jnp.float32),
                 jax.ShapeDtypeStruct((n_total, D_PACK), jnp.uint32),
                 jax.ShapeDtypeStruct((n_total, LANES), jnp.float32),
                 jax.ShapeDtypeStruct((SUBLANES, n_total), jnp.float32),
                 jax.ShapeDtypeStruct((SUBLANES, LANES), jnp.float32))
    row_blk = lambda t: (back(t), 0)
    out_specs = (pl.BlockSpec((ts, d), row_blk),
                 pl.BlockSpec((ts, D_PACK), row_blk),
                 pl.BlockSpec((ts, LANES), row_blk),
                 pl.BlockSpec((SUBLANES, ts), lambda t: (0, back(t))),
                 pl.BlockSpec((SUBLANES, LANES), lambda t: (0, 0)))
    return pl.pallas_call(
        functools.partial(_mixer_kernel, ts=ts, nt=nt, n_tiles=n_tiles, alpha=alpha),
        grid=(n_tiles + 1,),
        in_specs=in_specs,
        out_specs=out_specs,
        out_shape=out_shape,
        scratch_shapes=[pltpu.VMEM((ts + 2 * HALO, d), jnp.bfloat16),
                        pltpu.VMEM((ts, d), jnp.bfloat16),
                        pltpu.VMEM((ts, ts), jnp.bfloat16),
                        pltpu.VMEM((ts, d), jnp.float32),
                        pltpu.VMEM((1, LANES), jnp.float32)],
        compiler_params=pltpu.CompilerParams(dimension_semantics=("arbitrary",),
                                             vmem_limit_bytes=VMEM_LIMIT_BYTES),
        cost_estimate=pl.CostEstimate(
            flops=2 * n_total * (d * (3 * D_CONV + 2 * D_SG) + CHUNK * D_SG + d * d + d * LANES + ts * LANES),
            transcendentals=n_total * (2 * D_SG + 2 * LANES),
            bytes_accessed=n_total * (2 * d * 4 + D_PACK * 4 + LANES * 4 + SUBLANES * 4)),
        name="mixer",
    )(x, x, x, *p)


def _sc_mesh():
    return plsc.VectorSubcoreMesh(core_axis_name="c", subcore_axis_name="s",
                                  num_cores=SC_CORES, num_subcores=SC_SUBCORES)


def _sc_worker_base(per_worker):
    return (lax.axis_index("s") * SC_CORES + lax.axis_index("c")) * per_worker


def _sc_dispatch(pos1, pos2, xp, n_rows, order):
    n, dp = xp.shape
    per_worker = n // SC_WORKERS
    steps = per_worker // SC_ROWS

    def body(pos1_hbm, pos2_hbm, xp_hbm, order_hbm, xs_hbm, idx1_v, idx2_v, rows_v, sem):
        del order_hbm
        base = _sc_worker_base(per_worker)

        @pl.loop(0, steps)
        def _(k):
            off = pl.multiple_of(base + k * SC_ROWS, SC_ROWS)
            pltpu.sync_copy(pos1_hbm.at[pl.ds(off, SC_ROWS)], idx1_v)
            pltpu.sync_copy(pos2_hbm.at[pl.ds(off, SC_ROWS)], idx2_v)
            pltpu.sync_copy(xp_hbm.at[pl.ds(off, SC_ROWS)], rows_v)
            c1 = pltpu.async_copy(rows_v, xs_hbm.at[idx1_v], sem)
            c2 = pltpu.async_copy(rows_v, xs_hbm.at[idx2_v], sem)
            c1.wait()
            c2.wait()

    return pl.kernel(
        body,
        out_type=jax.ShapeDtypeStruct((n_rows, dp), xp.dtype),
        mesh=_sc_mesh(),
        scratch_types=[pltpu.VMEM((SC_ROWS,), jnp.int32), pltpu.VMEM((SC_ROWS,), jnp.int32),
                       pltpu.VMEM((SC_ROWS, dp), xp.dtype), pltpu.SemaphoreType.DMA],
        compiler_params=pltpu.CompilerParams(use_tc_tiling_on_sc=True),
        cost_estimate=pl.CostEstimate(flops=0, transcendentals=0, bytes_accessed=3 * n * dp * 4 + 2 * n * 4),
        name="sc_dispatch",
    )(pos1, pos2, xp, order)


def _sc_gather(pos1, pos2, ys):
    n = pos1.shape[0]
    dp = ys.shape[1]
    per_worker = n // SC_WORKERS
    steps = per_worker // SC_ROWS

    def body(pos1_hbm, pos2_hbm, ys_hbm, out_hbm, idx_v, rows_v, sem):
        base = _sc_worker_base(per_worker)

        @pl.loop(0, steps)
        def _(k):
            off = pl.multiple_of(base + k * SC_ROWS, SC_ROWS)
            for slot, pos_hbm in enumerate((pos1_hbm, pos2_hbm)):
                pltpu.sync_copy(pos_hbm.at[pl.ds(off, SC_ROWS)], idx_v)
                pltpu.async_copy(ys_hbm.at[idx_v], rows_v, sem).wait()
                pltpu.sync_copy(rows_v, out_hbm.at[slot, pl.ds(off, SC_ROWS)])

    return pl.kernel(
        body,
        out_type=jax.ShapeDtypeStruct((2, n, dp), ys.dtype),
        mesh=_sc_mesh(),
        scratch_types=[pltpu.VMEM((SC_ROWS,), jnp.int32), pltpu.VMEM((SC_ROWS, dp), ys.dtype),
                       pltpu.SemaphoreType.DMA],
        compiler_params=pltpu.CompilerParams(use_tc_tiling_on_sc=True),
        cost_estimate=pl.CostEstimate(flops=0, transcendentals=0, bytes_accessed=4 * n * dp * 4 + 2 * n * 4),
        name="sc_gather",
    )(pos1, pos2, ys)


def _expert_kernel(te_ref, first_ref, nsub_ref, slot_ref, next_ref, nused_ref, xs_ref, wg_hbm, wu_hbm, wd_hbm,
                   ys_ref, wg_buf, wu_buf, wd_buf, wgu_bf_ref, wd_bf_ref, sem):
    i = pl.program_id(0)
    bf16 = jnp.bfloat16
    sub = EXPERT_SUBTILE

    def weight_copies(expert, slot):
        return (pltpu.make_async_copy(wg_hbm.at[expert], wg_buf.at[slot], sem.at[slot]),
                pltpu.make_async_copy(wu_hbm.at[expert], wu_buf.at[slot], sem.at[slot]),
                pltpu.make_async_copy(wd_hbm.at[expert], wd_buf.at[slot], sem.at[slot]))

    @pl.when(first_ref[i] == 1)
    def _():
        slot = slot_ref[i]

        @pl.when(i == 0)
        def _():
            for c in weight_copies(te_ref[i], slot):
                c.start()

        for c in weight_copies(te_ref[i], slot):
            c.wait()

        @pl.when(next_ref[i] >= 0)
        def _():
            for c in weight_copies(next_ref[i], 1 - slot):
                c.start()

        wgu_bf_ref[:, 0:D_EXPERT] = wg_buf[slot].astype(bf16)
        wgu_bf_ref[:, D_EXPERT:2 * D_EXPERT] = wu_buf[slot].astype(bf16)
        wd_bf_ref[...] = wd_buf[slot].astype(bf16)

    def swiglu(r0):
        rows = pl.ds(r0, sub)
        lo, hi = _unpack_halves(xs_ref[rows, :])
        gu = (_dot(lo.astype(bf16), wgu_bf_ref[0:D_PACK, :])
              + _dot(hi.astype(bf16), wgu_bf_ref[D_PACK:D_MODEL, :]))
        g = gu[:, 0:D_EXPERT]
        h = g * jax.nn.sigmoid(g) * gu[:, D_EXPERT:2 * D_EXPERT]
        ys_ref[rows, :] = _pack_halves(_dot(h.astype(bf16), wd_bf_ref[...]))

    n_sub = nsub_ref[i]

    def pair(k, c):
        r0 = pl.multiple_of(k * (2 * sub), 2 * sub)
        swiglu(r0)
        swiglu(r0 + sub)
        return c

    lax.fori_loop(0, n_sub // 2, pair, 0)

    @pl.when(n_sub % 2 == 1)
    def _():
        swiglu(pl.multiple_of((n_sub - 1) * sub, sub))

    def clear(k, c):
        ys_ref[pl.ds(pl.multiple_of(k * sub, sub), sub), :] = jnp.zeros((sub, D_PACK), ys_ref.dtype)
        return c

    lax.fori_loop(n_sub, jnp.where(i < nused_ref[0], ys_ref.shape[0] // sub, 0), clear, 0)


def _experts(tile_expert, tile_first, tile_nsub, tile_slot, tile_next, n_used, xs, w_gate, w_up, w_down):
    n_rows, dp = xs.shape
    d = D_MODEL
    tm = EXPERT_TILE
    row_blk = lambda i, te, fi, ns, sl, nx, nu: (jnp.minimum(i, nu[0] - 1), 0)
    grid_spec = pltpu.PrefetchScalarGridSpec(
        num_scalar_prefetch=6,
        grid=(n_rows // tm,),
        in_specs=[pl.BlockSpec((tm, dp), row_blk),
                  pl.BlockSpec(memory_space=pl.ANY),
                  pl.BlockSpec(memory_space=pl.ANY),
                  pl.BlockSpec(memory_space=pl.ANY)],
        out_specs=pl.BlockSpec((tm, dp), row_blk),
        scratch_shapes=[pltpu.VMEM((2, d, D_EXPERT), jnp.float32),
                        pltpu.VMEM((2, d, D_EXPERT), jnp.float32),
                        pltpu.VMEM((2, D_EXPERT, d), jnp.float32),
                        pltpu.VMEM((d, 2 * D_EXPERT), jnp.bfloat16),
                        pltpu.VMEM((D_EXPERT, d), jnp.bfloat16),
                        pltpu.SemaphoreType.DMA((2,))],
    )
    return pl.pallas_call(
        _expert_kernel,
        grid_spec=grid_spec,
        out_shape=jax.ShapeDtypeStruct((n_rows, dp), jnp.uint32),
        compiler_params=pltpu.CompilerParams(dimension_semantics=("arbitrary",),
                                             vmem_limit_bytes=VMEM_LIMIT_BYTES),
        cost_estimate=pl.CostEstimate(flops=2 * n_rows * 3 * d * D_EXPERT, transcendentals=n_rows * D_EXPERT,
                                      bytes_accessed=2 * n_rows * dp * 4 + N_EXPERTS * 3 * d * D_EXPERT * 4),
        name="experts",
    )(tile_expert, tile_first, tile_nsub, tile_slot, tile_next, n_used, xs, w_gate, w_up, w_down)


def _combine_kernel(x1_ref, slab_ref, y1_ref, y2_ref, g_ref, b_ref, *rest, alpha):
    out_ref = rest[-1]
    slab = slab_ref[...]
    w1 = slab[:, COL_W1:COL_W1 + 1]
    w2 = slab[:, COL_W2:COL_W2 + 1]
    lo1, hi1 = _unpack_halves(y1_ref[...])
    lo2, hi2 = _unpack_halves(y2_ref[...])
    moe = jnp.concatenate([w1 * lo1 + w2 * lo2, w1 * hi1 + w2 * hi2], axis=1)
    out_ref[...] = _layer_norm(alpha * x1_ref[...] + moe, g_ref[...], b_ref[...])


def _combine(x1, slab, yg, g, b, alpha, tile0, out_prev):
    n, d = x1.shape
    n_part = yg.shape[1]
    tt = TOKEN_TILE
    prev = [] if out_prev is None else [out_prev]
    return pl.pallas_call(
        functools.partial(_combine_kernel, alpha=alpha),
        grid=(n_part // tt,),
        in_specs=[pl.BlockSpec((tt, d), lambda i: (tile0 + i, 0)),
                  pl.BlockSpec((tt, LANES), lambda i: (tile0 + i, 0)),
                  pl.BlockSpec((None, tt, D_PACK), lambda i: (0, i, 0)),
                  pl.BlockSpec((None, tt, D_PACK), lambda i: (1, i, 0)),
                  pl.BlockSpec((1, d), lambda i: (0, 0)),
                  pl.BlockSpec((1, d), lambda i: (0, 0))] + [pl.BlockSpec(memory_space=pl.ANY)] * len(prev),
        out_specs=pl.BlockSpec((tt, d), lambda i: (tile0 + i, 0)),
        out_shape=jax.ShapeDtypeStruct((n, d), jnp.float32),
        input_output_aliases={6: 0} if prev else {},
        compiler_params=pltpu.CompilerParams(dimension_semantics=("arbitrary",),
                                             vmem_limit_bytes=VMEM_LIMIT_BYTES),
        cost_estimate=pl.CostEstimate(flops=12 * n_part * d, transcendentals=n_part,
                                      bytes_accessed=n_part * (2 * d * 4 + 2 * D_PACK * 4 + LANES * 4)),
        name="combine",
    )(x1, slab, yg, yg, g, b, *prev)


def _row(a):
    return a.reshape(1, -1)


def _mixer_params(w_in, b_in, conv_w, conv_b, sg_ln_g, sg_ln_b, w_s, b_s, w_o, b_o, ln1_g, ln1_b,
                  w_rc, b_rc, w_rf, b_rf):
    d = D_MODEL
    bf16 = jnp.bfloat16
    row = _row
    ws_pairs = w_s.reshape(N_SG_HEADS // 2, 2 * CHUNK, CHUNK).astype(bf16)
    bs_full = jnp.repeat(b_s.T, SG_HEAD_DIM, axis=1)
    w_r = jnp.zeros((d, LANES), jnp.float32).at[:, 0:N_EXPERTS].set(w_rf)
    w_r = w_r.at[:, COARSE_OFF:COARSE_OFF + N_EXPERT_GROUPS].set(w_rc).astype(bf16)
    b_r = jnp.zeros((1, LANES), jnp.float32).at[0, 0:N_EXPERTS].set(b_rf)
    b_r = b_r.at[0, COARSE_OFF:COARSE_OFF + N_EXPERT_GROUPS].set(b_rc)
    return (w_in.astype(bf16), row(b_in), conv_w, row(conv_b), row(sg_ln_g), row(sg_ln_b),
            ws_pairs, bs_full, w_o.astype(bf16), row(b_o), row(ln1_g), row(ln1_b), w_r, b_r)


def _mix_and_plan(x, mixer_params, alpha):
    n = x.shape[0] * x.shape[1]
    x1, xp, slab, route, carry = _mixer(x, mixer_params, alpha)

    tm = EXPERT_TILE
    n_tiles = 2 * n // tm + N_EXPERTS
    cnt = carry[0, 0:N_EXPERTS].astype(jnp.int32)
    tiles_e = (cnt + tm - 1) // tm
    tile_end = jnp.cumsum(tiles_e)
    tile_start = tile_end - tiles_e
    row_off = tile_start * tm
    experts = jnp.arange(N_EXPERTS, dtype=jnp.int32)[:, None]

    def sorted_pos(e_row, r_row):
        e = route[e_row].astype(jnp.int32)
        return jnp.sum(jnp.where(e[None, :] == experts, row_off[:, None], 0), axis=0) + route[r_row].astype(jnp.int32)

    pos1 = sorted_pos(COL_E1, COL_R1)
    pos2 = sorted_pos(COL_E2, COL_R2)
    tile_ids = jnp.arange(n_tiles, dtype=jnp.int32)
    n_used = tile_end[N_EXPERTS - 1:N_EXPERTS]
    tile_expert = jnp.sum((tile_end[None, :] <= tile_ids[:, None]).astype(jnp.int32), axis=1)
    last_used = jnp.sum((tile_end <= n_used[0] - 1).astype(jnp.int32))
    tile_expert = jnp.minimum(tile_expert, last_used)
    tile_first = jnp.any(tile_ids[:, None] == tile_start[None, :], axis=1) & (tile_ids < n_used[0])
    tile_first = tile_first.astype(jnp.int32).at[0].set(1)
    rows_left = jnp.sum(jnp.where(tile_expert[:, None] == experts.T, (row_off + cnt)[None, :], 0), axis=1) - tile_ids * tm
    tile_nsub = jnp.where(tile_ids < n_used[0], (jnp.clip(rows_left, 0, tm) + EXPERT_SUBTILE - 1) // EXPERT_SUBTILE, 0)

    used = tiles_e > 0
    ordinal = jnp.cumsum(used.astype(jnp.int32)) - 1
    e_ids = experts[:, 0]
    later_used = used[None, :] & (e_ids[None, :] > e_ids[:, None])
    next_used = jnp.min(jnp.where(later_used, e_ids[None, :], N_EXPERTS), axis=1)
    next_used = jnp.where(next_used < N_EXPERTS, next_used, -1)
    tile_slot = ordinal[tile_expert] % 2
    tile_next = next_used[tile_expert]
    tiles = (tile_expert, tile_first, tile_nsub.astype(jnp.int32), tile_slot.astype(jnp.int32),
             tile_next.astype(jnp.int32), n_used)
    return x1, xp, slab, pos1, pos2, tiles, n_tiles * tm


def _experts_and_combine(x, plan, xs, w_gate, w_up, w_down, ln2_g, ln2_b, alpha):
    n = x.shape[0] * x.shape[1]
    x1, _, slab, pos1, pos2, tiles, _ = plan
    ys = _experts(*tiles, xs, w_gate, w_up, w_down)
    n_part = min(n, COMBINE_PART)
    out = None
    for t0 in range(0, n, n_part):
        yg = _sc_gather(pos1[t0:t0 + n_part], pos2[t0:t0 + n_part], ys)
        out = _combine(x1, slab, yg, _row(ln2_g), _row(ln2_b), alpha, t0 // TOKEN_TILE, out)
    return out.reshape(x.shape)


def kernel(x_prompt, x_sample, w_in, b_in, conv_w, conv_b, sg_ln_g, sg_ln_b, w_s, b_s, w_o, b_o, ln1_g, ln1_b, w_rc, b_rc, w_rf, b_rf, w_gate, w_up, w_down, ln2_g, ln2_b):
    depth = w_in.shape[0]
    alpha = (2.0 * depth) ** 0.25
    xs = (x_prompt, x_sample)
    for l in range(depth):
        mixer_params = _mixer_params(w_in[l], b_in[l], conv_w[l], conv_b[l], sg_ln_g[l], sg_ln_b[l], w_s[l], b_s[l],
                                     w_o[l], b_o[l], ln1_g[l], ln1_b[l], w_rc[l], b_rc[l], w_rf[l], b_rf[l])
        plans = [_mix_and_plan(x, mixer_params, alpha) for x in xs]
        sorted_rows = []
        order = jnp.zeros((SUBLANES, D_PACK), jnp.uint32)
        for _, xp, _, pos1, pos2, _, n_rows in plans:
            sorted_rows.append(_sc_dispatch(pos1, pos2, xp, n_rows, order))
            order = sorted_rows[-1][0:SUBLANES]
        xs = tuple(_experts_and_combine(x, plan, rows, w_gate[l], w_up[l], w_down[l], ln2_g[l], ln2_b[l], alpha)
                   for x, plan, rows in zip(xs, plans, sorted_rows))
    return xs
```

```python
import functools

import jax
import jax.numpy as jnp
from jax import lax
from jax.experimental import pallas as pl
from jax.experimental.pallas import tpu as pltpu
from jax.experimental.pallas import tpu_sc as plsc

D_MODEL = 1024
D_CONV = 512
D_SG = 512
N_SG_HEADS = 8
SG_HEAD_DIM = D_SG // N_SG_HEADS
CHUNK = 128
N_EXPERT_GROUPS = 4
EXPERTS_PER_GROUP = 8
N_EXPERTS = N_EXPERT_GROUPS * EXPERTS_PER_GROUP
D_EXPERT = 512
LN_EPS = 1e-5
D_PACK = D_MODEL // 2

LANES = 128
SUBLANES = 8
HALO = 16
SEQ_TILE = 512
EXPERT_TILE = 1024
EXPERT_SUBTILE = 256
TOKEN_TILE = 1024
SC_CORES = 2
SC_SUBCORES = 16
SC_WORKERS = SC_CORES * SC_SUBCORES
SC_ROWS = 64
VMEM_LIMIT_BYTES = 56 * 1024 * 1024

COL_E1, COL_E2, COL_W1, COL_W2, COL_R1, COL_R2 = 0, 1, 2, 3, 4, 5
COARSE_OFF = N_EXPERTS


def _dot(a, b):
    return jnp.dot(a, b, preferred_element_type=jnp.float32)


def _gelu_tanh(x):
    return 0.5 * x * (1.0 + jnp.tanh(0.7978845608028654 * (x + 0.044715 * (x * x * x))))


def _layer_norm(x, g, b):
    mu = jnp.mean(x, axis=-1, keepdims=True)
    xc = x - mu
    var = jnp.mean(xc * xc, axis=-1, keepdims=True)
    return xc * lax.rsqrt(var + LN_EPS) * g + b


def _pack_halves(x):
    u32 = jnp.uint32
    lo = lax.bitcast_convert_type(x[:, 0:D_PACK].astype(jnp.bfloat16).astype(jnp.float32), u32)
    hi = lax.bitcast_convert_type(x[:, D_PACK:D_MODEL].astype(jnp.bfloat16).astype(jnp.float32), u32)
    return (hi & u32(0xFFFF0000)) | (lo >> u32(16))


def _unpack_halves(w):
    u32 = jnp.uint32
    lo = lax.bitcast_convert_type(w << u32(16), jnp.float32)
    hi = lax.bitcast_convert_type(w & u32(0xFFFF0000), jnp.float32)
    return lo, hi


def _mixer_kernel(x_ref, x_prev_ref, x_next_ref, w_in_ref, b_in_ref, conv_w_ref, conv_b_ref,
                  sg_g_ref, sg_b_ref, ws_ref, bs_ref, w_o_ref, b_o_ref, ln1_g_ref, ln1_b_ref,
                  w_r_ref, b_r_ref,
                  x1_ref, xp_ref, slab_ref, route_ref, counts_ref,
                  xb_ref, ymix_ref, tri_ref, resid_ref, carry_ref, *, ts, nt, n_tiles, alpha):
    t = pl.program_id(0)
    j = jnp.minimum(t, n_tiles - 1) % nt
    bf16 = jnp.bfloat16

    @pl.when(t == 0)
    def _():
        carry_ref[...] = jnp.zeros_like(carry_ref)
        resid_ref[...] = jnp.zeros_like(resid_ref)
        ri = lax.broadcasted_iota(jnp.int32, (ts, ts), 0)
        ci = lax.broadcasted_iota(jnp.int32, (ts, ts), 1)
        tri_ref[...] = jnp.where(ri > ci, 1.0, 0.0).astype(bf16)

    lane = lax.broadcasted_iota(jnp.int32, (ts, LANES), 1)
    lane_f = lane.astype(jnp.float32)
    neg = jnp.float32(-jnp.inf)
    big = jnp.float32(1e9)
    is_c = (lane >= COARSE_OFF) & (lane < COARSE_OFF + N_EXPERT_GROUPS)

    def proj(lhs, lo, hi):
        return _dot(lhs, w_in_ref[:, lo:hi]) + b_in_ref[:, lo:hi]

    def back_norm():
        x1 = _layer_norm(resid_ref[...], ln1_g_ref[...], ln1_b_ref[...])
        x1_ref[...] = x1
        xp_ref[...] = _pack_halves(x1)
        return x1.astype(bf16)

    def back_route(x1b):
        logits = _dot(x1b, w_r_ref[...]) + b_r_ref[...]
        lc = jnp.where(is_c, logits, neg)
        mx = jnp.max(lc, axis=1, keepdims=True)
        grp = jnp.min(jnp.where(lc == mx, lane_f - COARSE_OFF, big), axis=1, keepdims=True)
        p_grp = 1.0 / jnp.sum(jnp.where(is_c, jnp.exp(logits - mx), 0.0), axis=1, keepdims=True)
        grp_lo = grp * EXPERTS_PER_GROUP
        in_grp = (lane_f >= grp_lo) & (lane_f < grp_lo + EXPERTS_PER_GROUP)
        lf = jnp.where(in_grp, logits, neg)
        v1 = jnp.max(lf, axis=1, keepdims=True)
        e1 = jnp.min(jnp.where(lf == v1, lane_f, big), axis=1, keepdims=True)
        lf2 = jnp.where(lane_f == e1, neg, lf)
        v2 = jnp.max(lf2, axis=1, keepdims=True)
        e2 = jnp.min(jnp.where(lf2 == v2, lane_f, big), axis=1, keepdims=True)
        a = jnp.exp(v2 - v1)
        return e1, e2, p_grp / (1.0 + a), p_grp * a / (1.0 + a)

    def back_rank(e1, e2, w1, w2):
        hit1 = lane_f == e1
        hit2 = lane_f == e2
        onehot = jnp.where(hit1 | hit2, 1.0, 0.0)
        carry = carry_ref[...]
        before = _dot(tri_ref[...], onehot.astype(bf16)) + carry
        r1 = jnp.sum(jnp.where(hit1, before, 0.0), axis=1, keepdims=True)
        r2 = jnp.sum(jnp.where(hit2, before, 0.0), axis=1, keepdims=True)
        carry = jnp.where(t > 0, carry + jnp.sum(onehot, axis=0, keepdims=True), carry)
        carry_ref[...] = carry
        counts_ref[...] = jnp.broadcast_to(carry, counts_ref.shape)
        slab = jnp.where(lane == COL_E1, e1, 0.0)
        slab = jnp.where(lane == COL_E2, e2, slab)
        slab = jnp.where(lane == COL_W1, w1, slab)
        slab = jnp.where(lane == COL_W2, w2, slab)
        slab = jnp.where(lane == COL_R1, r1, slab)
        slab = jnp.where(lane == COL_R2, r2, slab)
        slab_ref[...] = slab
        route_ref[...] = slab.T[0:SUBLANES, :]

    def front_load():
        xb_ref[0:HALO, :] = x_prev_ref[...].astype(bf16)
        xb_ref[HALO:HALO + ts, :] = x_ref[...].astype(bf16)
        xb_ref[HALO + ts:HALO + ts + HALO, :] = x_next_ref[...].astype(bf16)

    def front_conv():
        xe = xb_ref[...]
        g_e = proj(xe, 0, D_CONV) * proj(xe, 2 * D_CONV, 3 * D_CONV)
        row_e = lax.broadcasted_iota(jnp.int32, (ts + 2 * HALO, 1), 0)
        has_prev = jnp.where(j > 0, 1.0, 0.0)
        has_next = jnp.where(j < nt - 1, 1.0, 0.0)
        g_e = g_e * jnp.where(row_e < HALO, has_prev, jnp.where(row_e >= HALO + ts, has_next, 1.0))
        conv = (g_e[HALO - 1:HALO - 1 + ts, :] * conv_w_ref[0:1, :] + g_e[HALO:HALO + ts, :] * conv_w_ref[1:2, :]
                + g_e[HALO + 1:HALO + 1 + ts, :] * conv_w_ref[2:3, :] + conv_b_ref[...])
        y_a = proj(xb_ref[HALO:HALO + ts, :], D_CONV, 2 * D_CONV) * conv
        ymix_ref[:, 0:D_CONV] = y_a.astype(bf16)

    def front_gate_proj():
        xm = xb_ref[HALO:HALO + ts, :]
        return proj(xm, 3 * D_CONV, 3 * D_CONV + D_SG), proj(xm, 3 * D_CONV + D_SG, 3 * D_CONV + 2 * D_SG)

    def front_gate(u_pre, v_pre):
        u = _gelu_tanh(u_pre)
        v = _gelu_tanh(v_pre)
        v_ln = _layer_norm(v, sg_g_ref[...], sg_b_ref[...]).astype(bf16)
        first_head = lax.broadcasted_iota(jnp.int32, (CHUNK, LANES), 1) < SG_HEAD_DIM
        for q0 in range(0, ts, 2 * CHUNK):
            q1 = q0 + CHUNK
            for hp in range(N_SG_HEADS // 2):
                c0 = hp * LANES
                rhs = jnp.concatenate([v_ln[q0:q0 + CHUNK, c0:c0 + LANES], v_ln[q1:q1 + CHUNK, c0:c0 + LANES]], axis=1)
                res = _dot(ws_ref[hp], rhs)
                bias = bs_ref[:, c0:c0 + LANES]
                m0 = jnp.where(first_head, res[0:CHUNK, 0:LANES], res[CHUNK:2 * CHUNK, 0:LANES]) + bias
                m1 = jnp.where(first_head, res[0:CHUNK, LANES:2 * LANES], res[CHUNK:2 * CHUNK, LANES:2 * LANES]) + bias
                ymix_ref[q0:q0 + CHUNK, D_CONV + c0:D_CONV + c0 + LANES] = (u[q0:q0 + CHUNK, c0:c0 + LANES] * m0).astype(bf16)
                ymix_ref[q1:q1 + CHUNK, D_CONV + c0:D_CONV + c0 + LANES] = (u[q1:q1 + CHUNK, c0:c0 + LANES] * m1).astype(bf16)

    def front_out_a():
        resid_ref[...] = alpha * x_ref[...] + b_o_ref[...] + _dot(ymix_ref[:, 0:D_CONV], w_o_ref[0:D_CONV, :])

    def front_out_b():
        resid_ref[...] += _dot(ymix_ref[:, D_CONV:D_MODEL], w_o_ref[D_CONV:D_MODEL, :])

    front_load()
    x1b = back_norm()
    routing = back_route(x1b)
    front_conv()
    back_rank(*routing)
    gate_pre = front_gate_proj()
    front_out_a()
    front_gate(*gate_pre)
    front_out_b()


def _mixer(x, p, alpha):
    nb, s, d = x.shape
    ts = SEQ_TILE
    nt = s // ts
    hb = ts // HALO
    n_tiles = nb * nt
    n_total = n_tiles * ts

    front = lambda t: jnp.minimum(t, n_tiles - 1)
    back = lambda t: jnp.maximum(t - 1, 0)
    cur = lambda t: (front(t) // nt, front(t) % nt, 0)
    prv = lambda t: (front(t) // nt, jnp.maximum((front(t) % nt) * hb - 1, 0), 0)
    nxt = lambda t: (front(t) // nt, jnp.minimum((front(t) % nt + 1) * hb, s // HALO - 1), 0)
    full = lambda a: pl.BlockSpec(a.shape, lambda t: (0,) * a.ndim)
    in_specs = [pl.BlockSpec((None, ts, d), cur), pl.BlockSpec((None, HALO, d), prv),
                pl.BlockSpec((None, HALO, d), nxt)] + [full(a) for a in p]
    out_shape = (jax.ShapeDtypeStruct((n_total, d), jnp.float32),
                 jax.ShapeDtypeStruct((n_total, D_PACK), jnp.uint32),
                 jax.ShapeDtypeStruct((n_total, LANES), jnp.float32),
                 jax.ShapeDtypeStruct((SUBLANES, n_total), jnp.float32),
                 jax.ShapeDtypeStruct((SUBLANES, LANES), jnp.float32))
    row_blk = lambda t: (back(t), 0)
    out_specs = (pl.BlockSpec((ts, d), row_blk),
                 pl.BlockSpec((ts, D_PACK), row_blk),
                 pl.BlockSpec((ts, LANES), row_blk),
                 pl.BlockSpec((SUBLANES, ts), lambda t: (0, back(t))),
                 pl.BlockSpec((SUBLANES, LANES), lambda t: (0, 0)))
    return pl.pallas_call(
        functools.partial(_mixer_kernel, ts=ts, nt=nt, n_tiles=n_tiles, alpha=alpha),
        grid=(n_tiles + 1,),
        in_specs=in_specs,
        out_specs=out_specs,
        out_shape=out_shape,
        scratch_shapes=[pltpu.VMEM((ts + 2 * HALO, d), jnp.bfloat16),
                        pltpu.VMEM((ts, d), jnp.bfloat16),
                        pltpu.VMEM((ts, ts), jnp.bfloat16),
                        pltpu.VMEM((ts, d), jnp.float32),
                        pltpu.VMEM((1, LANES), jnp.float32)],
        compiler_params=pltpu.CompilerParams(dimension_semantics=("arbitrary",),
                                             vmem_limit_bytes=VMEM_LIMIT_BYTES),
        cost_estimate=pl.CostEstimate(
            flops=2 * n_total * (d * (3 * D_CONV + 2 * D_SG) + CHUNK * D_SG + d * d + d * LANES + ts * LANES),
            transcendentals=n_total * (2 * D_SG + 2 * LANES),
            bytes_accessed=n_total * (2 * d * 4 + D_PACK * 4 + LANES * 4 + SUBLANES * 4)),
        name="mixer",
    )(x, x, x, *p)


def _sc_mesh():
    return plsc.VectorSubcoreMesh(core_axis_name="c", subcore_axis_name="s",
                                  num_cores=SC_CORES, num_subcores=SC_SUBCORES)


def _sc_worker_base(per_worker):
    return (lax.axis_index("s") * SC_CORES + lax.axis_index("c")) * per_worker


def _sc_dispatch(pos1, pos2, xp, n_rows):
    n, dp = xp.shape
    per_worker = n // SC_WORKERS
    steps = per_worker // SC_ROWS

    def body(pos1_hbm, pos2_hbm, xp_hbm, xs_hbm, idx1_v, idx2_v, rows_v, sem):
        base = _sc_worker_base(per_worker)

        @pl.loop(0, steps)
        def _(k):
            off = pl.multiple_of(base + k * SC_ROWS, SC_ROWS)
            pltpu.sync_copy(pos1_hbm.at[pl.ds(off, SC_ROWS)], idx1_v)
            pltpu.sync_copy(pos2_hbm.at[pl.ds(off, SC_ROWS)], idx2_v)
            pltpu.sync_copy(xp_hbm.at[pl.ds(off, SC_ROWS)], rows_v)
            c1 = pltpu.async_copy(rows_v, xs_hbm.at[idx1_v], sem)
            c2 = pltpu.async_copy(rows_v, xs_hbm.at[idx2_v], sem)
            c1.wait()
            c2.wait()

    return pl.kernel(
        body,
        out_type=jax.ShapeDtypeStruct((n_rows, dp), xp.dtype),
        mesh=_sc_mesh(),
        scratch_types=[pltpu.VMEM((SC_ROWS,), jnp.int32), pltpu.VMEM((SC_ROWS,), jnp.int32),
                       pltpu.VMEM((SC_ROWS, dp), xp.dtype), pltpu.SemaphoreType.DMA],
        compiler_params=pltpu.CompilerParams(use_tc_tiling_on_sc=True),
        cost_estimate=pl.CostEstimate(flops=0, transcendentals=0, bytes_accessed=3 * n * dp * 4 + 2 * n * 4),
        name="sc_dispatch",
    )(pos1, pos2, xp)


def _sc_gather(pos1, pos2, ys):
    n = pos1.shape[0]
    dp = ys.shape[1]
    per_worker = n // SC_WORKERS
    steps = per_worker // SC_ROWS

    def body(pos1_hbm, pos2_hbm, ys_hbm, out_hbm, idx_v, rows_v, sem):
        base = _sc_worker_base(per_worker)

        @pl.loop(0, steps)
        def _(k):
            off = pl.multiple_of(base + k * SC_ROWS, SC_ROWS)
            for slot, pos_hbm in enumerate((pos1_hbm, pos2_hbm)):
                pltpu.sync_copy(pos_hbm.at[pl.ds(off, SC_ROWS)], idx_v)
                pltpu.async_copy(ys_hbm.at[idx_v], rows_v, sem).wait()
                pltpu.sync_copy(rows_v, out_hbm.at[slot, pl.ds(off, SC_ROWS)])

    return pl.kernel(
        body,
        out_type=jax.ShapeDtypeStruct((2, n, dp), ys.dtype),
        mesh=_sc_mesh(),
        scratch_types=[pltpu.VMEM((SC_ROWS,), jnp.int32), pltpu.VMEM((SC_ROWS, dp), ys.dtype),
                       pltpu.SemaphoreType.DMA],
        compiler_params=pltpu.CompilerParams(use_tc_tiling_on_sc=True),
        cost_estimate=pl.CostEstimate(flops=0, transcendentals=0, bytes_accessed=4 * n * dp * 4 + 2 * n * 4),
        name="sc_gather",
    )(pos1, pos2, ys)


def _expert_kernel(te_ref, first_ref, nsub_ref, slot_ref, next_ref, nused_ref, xs_ref, wg_hbm, wu_hbm, wd_hbm,
                   ys_ref, wg_buf, wu_buf, wd_buf, wgu_bf_ref, wd_bf_ref, sem):
    i = pl.program_id(0)
    bf16 = jnp.bfloat16
    sub = EXPERT_SUBTILE

    def weight_copies(expert, slot):
        return (pltpu.make_async_copy(wg_hbm.at[expert], wg_buf.at[slot], sem.at[slot]),
                pltpu.make_async_copy(wu_hbm.at[expert], wu_buf.at[slot], sem.at[slot]),
                pltpu.make_async_copy(wd_hbm.at[expert], wd_buf.at[slot], sem.at[slot]))

    @pl.when(first_ref[i] == 1)
    def _():
        slot = slot_ref[i]

        @pl.when(i == 0)
        def _():
            for c in weight_copies(te_ref[i], slot):
                c.start()

        for c in weight_copies(te_ref[i], slot):
            c.wait()

        @pl.when(next_ref[i] >= 0)
        def _():
            for c in weight_copies(next_ref[i], 1 - slot):
                c.start()

        wgu_bf_ref[:, 0:D_EXPERT] = wg_buf[slot].astype(bf16)
        wgu_bf_ref[:, D_EXPERT:2 * D_EXPERT] = wu_buf[slot].astype(bf16)
        wd_bf_ref[...] = wd_buf[slot].astype(bf16)

    def swiglu(r0):
        rows = pl.ds(r0, sub)
        lo, hi = _unpack_halves(xs_ref[rows, :])
        gu = (_dot(lo.astype(bf16), wgu_bf_ref[0:D_PACK, :])
              + _dot(hi.astype(bf16), wgu_bf_ref[D_PACK:D_MODEL, :]))
        g = gu[:, 0:D_EXPERT]
        h = g * jax.nn.sigmoid(g) * gu[:, D_EXPERT:2 * D_EXPERT]
        ys_ref[rows, :] = _pack_halves(_dot(h.astype(bf16), wd_bf_ref[...]))

    n_sub = nsub_ref[i]

    def pair(k, c):
        r0 = pl.multiple_of(k * (2 * sub), 2 * sub)
        swiglu(r0)
        swiglu(r0 + sub)
        return c

    lax.fori_loop(0, n_sub // 2, pair, 0)

    @pl.when(n_sub % 2 == 1)
    def _():
        swiglu(pl.multiple_of((n_sub - 1) * sub, sub))

    def clear(k, c):
        ys_ref[pl.ds(pl.multiple_of(k * sub, sub), sub), :] = jnp.zeros((sub, D_PACK), ys_ref.dtype)
        return c

    lax.fori_loop(n_sub, jnp.where(i < nused_ref[0], ys_ref.shape[0] // sub, 0), clear, 0)


def _experts(tile_expert, tile_first, tile_nsub, tile_slot, tile_next, n_used, xs, w_gate, w_up, w_down):
    n_rows, dp = xs.shape
    d = D_MODEL
    tm = EXPERT_TILE
    row_blk = lambda i, te, fi, ns, sl, nx, nu: (jnp.minimum(i, nu[0] - 1), 0)
    grid_spec = pltpu.PrefetchScalarGridSpec(
        num_scalar_prefetch=6,
        grid=(n_rows // tm,),
        in_specs=[pl.BlockSpec((tm, dp), row_blk),
                  pl.BlockSpec(memory_space=pl.ANY),
                  pl.BlockSpec(memory_space=pl.ANY),
                  pl.BlockSpec(memory_space=pl.ANY)],
        out_specs=pl.BlockSpec((tm, dp), row_blk),
        scratch_shapes=[pltpu.VMEM((2, d, D_EXPERT), jnp.float32),
                        pltpu.VMEM((2, d, D_EXPERT), jnp.float32),
                        pltpu.VMEM((2, D_EXPERT, d), jnp.float32),
                        pltpu.VMEM((d, 2 * D_EXPERT), jnp.bfloat16),
                        pltpu.VMEM((D_EXPERT, d), jnp.bfloat16),
                        pltpu.SemaphoreType.DMA((2,))],
    )
    return pl.pallas_call(
        _expert_kernel,
        grid_spec=grid_spec,
        out_shape=jax.ShapeDtypeStruct((n_rows, dp), jnp.uint32),
        compiler_params=pltpu.CompilerParams(dimension_semantics=("arbitrary",),
                                             vmem_limit_bytes=VMEM_LIMIT_BYTES),
        cost_estimate=pl.CostEstimate(flops=2 * n_rows * 3 * d * D_EXPERT, transcendentals=n_rows * D_EXPERT,
                                      bytes_accessed=2 * n_rows * dp * 4 + N_EXPERTS * 3 * d * D_EXPERT * 4),
        name="experts",
    )(tile_expert, tile_first, tile_nsub, tile_slot, tile_next, n_used, xs, w_gate, w_up, w_down)


def _combine_kernel(x1_ref, slab_ref, y1_ref, y2_ref, g_ref, b_ref, out_ref, *, alpha):
    slab = slab_ref[...]
    w1 = slab[:, COL_W1:COL_W1 + 1]
    w2 = slab[:, COL_W2:COL_W2 + 1]
    lo1, hi1 = _unpack_halves(y1_ref[...])
    lo2, hi2 = _unpack_halves(y2_ref[...])
    moe = jnp.concatenate([w1 * lo1 + w2 * lo2, w1 * hi1 + w2 * hi2], axis=1)
    out_ref[...] = _layer_norm(alpha * x1_ref[...] + moe, g_ref[...], b_ref[...])


def _combine(x1, slab, yg, g, b, alpha):
    n, d = x1.shape
    tt = TOKEN_TILE
    return pl.pallas_call(
        functools.partial(_combine_kernel, alpha=alpha),
        grid=(n // tt,),
        in_specs=[pl.BlockSpec((tt, d), lambda i: (i, 0)),
                  pl.BlockSpec((tt, LANES), lambda i: (i, 0)),
                  pl.BlockSpec((None, tt, D_PACK), lambda i: (0, i, 0)),
                  pl.BlockSpec((None, tt, D_PACK), lambda i: (1, i, 0)),
                  pl.BlockSpec((1, d), lambda i: (0, 0)),
                  pl.BlockSpec((1, d), lambda i: (0, 0))],
        out_specs=pl.BlockSpec((tt, d), lambda i: (i, 0)),
        out_shape=jax.ShapeDtypeStruct((n, d), jnp.float32),
        compiler_params=pltpu.CompilerParams(dimension_semantics=("arbitrary",),
                                             vmem_limit_bytes=VMEM_LIMIT_BYTES),
        cost_estimate=pl.CostEstimate(flops=12 * n * d, transcendentals=n,
                                      bytes_accessed=n * (2 * d * 4 + 2 * D_PACK * 4 + LANES * 4)),
        name="combine",
    )(x1, slab, yg, yg, g, b)


def _row(a):
    return a.reshape(1, -1)


def _mixer_params(w_in, b_in, conv_w, conv_b, sg_ln_g, sg_ln_b, w_s, b_s, w_o, b_o, ln1_g, ln1_b,
                  w_rc, b_rc, w_rf, b_rf):
    bf16 = jnp.bfloat16
    row = _row
    ws_pairs = w_s.reshape(N_SG_HEADS // 2, 2 * CHUNK, CHUNK).astype(bf16)
    bs_full = jnp.repeat(b_s.T, SG_HEAD_DIM, axis=1)
    pad = LANES - N_EXPERTS - N_EXPERT_GROUPS
    w_r = jnp.pad(jnp.concatenate([w_rf, w_rc], axis=1), ((0, 0), (0, pad))).astype(bf16)
    b_r = jnp.pad(jnp.concatenate([b_rf, b_rc]), (0, pad)).reshape(1, LANES)
    return (w_in.astype(bf16), row(b_in), conv_w, row(conv_b), row(sg_ln_g), row(sg_ln_b),
            ws_pairs, bs_full, w_o.astype(bf16), row(b_o), row(ln1_g), row(ln1_b), w_r, b_r)


def _encoder_layer(x, mixer_params, w_gate, w_up, w_down, ln2_g, ln2_b, alpha):
    n = x.shape[0] * x.shape[1]
    x1, xp, slab, route, carry = _mixer(x, mixer_params, alpha)

    tm = EXPERT_TILE
    n_tiles = 2 * n // tm + N_EXPERTS
    cnt = carry[0, 0:N_EXPERTS].astype(jnp.int32)
    tiles_e = (cnt + tm - 1) // tm
    experts = jnp.arange(N_EXPERTS, dtype=jnp.int32)[:, None]
    up_to = experts.T <= experts
    tile_end = jnp.sum(jnp.where(up_to, tiles_e[None, :], 0), axis=1)
    tile_start = tile_end - tiles_e
    row_off = tile_start * tm

    def sorted_pos(e_row, r_row):
        e = route[e_row].astype(jnp.int32)
        return jnp.sum(jnp.where(e[None, :] == experts, row_off[:, None], 0), axis=0) + route[r_row].astype(jnp.int32)

    pos1 = sorted_pos(COL_E1, COL_R1)
    pos2 = sorted_pos(COL_E2, COL_R2)
    tile_ids = jnp.arange(n_tiles, dtype=jnp.int32)
    n_used = tile_end[N_EXPERTS - 1:N_EXPERTS]
    tile_expert = jnp.sum((tile_end[None, :] <= tile_ids[:, None]).astype(jnp.int32), axis=1)
    last_used = jnp.sum((tile_end <= n_used[0] - 1).astype(jnp.int32))
    tile_expert = jnp.minimum(tile_expert, last_used)
    tile_first = (jnp.any(tile_ids[:, None] == tile_start[None, :], axis=1) & (tile_ids < n_used[0])) | (tile_ids == 0)
    tile_first = tile_first.astype(jnp.int32)
    rows_left = jnp.sum(jnp.where(tile_expert[:, None] == experts.T, (row_off + cnt)[None, :], 0), axis=1) - tile_ids * tm
    tile_nsub = jnp.where(tile_ids < n_used[0], (jnp.clip(rows_left, 0, tm) + EXPERT_SUBTILE - 1) // EXPERT_SUBTILE, 0)

    used = tiles_e > 0
    ordinal = jnp.sum((up_to & used[None, :]).astype(jnp.int32), axis=1) - 1
    e_ids = experts[:, 0]
    later_used = used[None, :] & (e_ids[None, :] > e_ids[:, None])
    next_used = jnp.min(jnp.where(later_used, e_ids[None, :], N_EXPERTS), axis=1)
    next_used = jnp.where(next_used < N_EXPERTS, next_used, -1)
    of_tile = tile_expert[:, None] == experts.T
    tile_slot = jnp.sum(jnp.where(of_tile, ordinal[None, :], 0), axis=1) % 2
    tile_next = jnp.sum(jnp.where(of_tile, next_used[None, :], 0), axis=1)

    xs = _sc_dispatch(pos1, pos2, xp, n_tiles * tm)
    ys = _experts(tile_expert, tile_first, tile_nsub.astype(jnp.int32), tile_slot.astype(jnp.int32),
                  tile_next.astype(jnp.int32), n_used, xs, w_gate, w_up, w_down)
    yg = _sc_gather(pos1, pos2, ys)
    return _combine(x1, slab, yg, _row(ln2_g), _row(ln2_b), alpha).reshape(x.shape)


def kernel(x_prompt, x_sample, w_in, b_in, conv_w, conv_b, sg_ln_g, sg_ln_b, w_s, b_s, w_o, b_o, ln1_g, ln1_b, w_rc, b_rc, w_rf, b_rf, w_gate, w_up, w_down, ln2_g, ln2_b):
    depth = w_in.shape[0]
    alpha = (2.0 * depth) ** 0.25
    xs = (x_prompt, x_sample)
    for l in range(depth):
        mixer_params = _mixer_params(w_in[l], b_in[l], conv_w[l], conv_b[l], sg_ln_g[l], sg_ln_b[l], w_s[l], b_s[l],
                                     w_o[l], b_o[l], ln1_g[l], ln1_b[l], w_rc[l], b_rc[l], w_rf[l], b_rf[l])
        xs = tuple(_encoder_layer(x, mixer_params, w_gate[l], w_up[l], w_down[l], ln2_g[l], ln2_b[l], alpha)
                   for x in xs)
    return xs
```

```python
import functools

import jax
import jax.numpy as jnp
from jax import lax
from jax.experimental import pallas as pl
from jax.experimental.pallas import tpu as pltpu
from jax.experimental.pallas import tpu_sc as plsc

D_MODEL = 1024
D_CONV = 512
D_SG = 512
N_SG_HEADS = 8
SG_HEAD_DIM = D_SG // N_SG_HEADS
CHUNK = 128
N_EXPERT_GROUPS = 4
EXPERTS_PER_GROUP = 8
N_EXPERTS = N_EXPERT_GROUPS * EXPERTS_PER_GROUP
D_EXPERT = 512
LN_EPS = 1e-5
D_PACK = D_MODEL // 2

LANES = 128
SUBLANES = 8
HALO = 16
SEQ_TILE = 512
EXPERT_TILE = 512
EXPERT_SUBTILE = 256
TOKEN_TILE = 1024
SC_CORES = 2
SC_SUBCORES = 16
SC_WORKERS = SC_CORES * SC_SUBCORES
SC_ROWS = 64
VMEM_LIMIT_BYTES = 56 * 1024 * 1024

COL_E1, COL_E2, COL_W1, COL_W2, COL_R1, COL_R2 = 0, 1, 2, 3, 4, 5
COARSE_OFF = N_EXPERTS


def _dot(a, b):
    return jnp.dot(a, b, preferred_element_type=jnp.float32)


def _gelu_tanh(x):
    return 0.5 * x * (1.0 + jnp.tanh(0.7978845608028654 * (x + 0.044715 * (x * x * x))))


def _layer_norm(x, g, b):
    mu = jnp.mean(x, axis=-1, keepdims=True)
    xc = x - mu
    var = jnp.mean(xc * xc, axis=-1, keepdims=True)
    return xc * lax.rsqrt(var + LN_EPS) * g + b


def _pack_halves(x):
    u32 = jnp.uint32
    lo = lax.bitcast_convert_type(x[:, 0:D_PACK].astype(jnp.bfloat16).astype(jnp.float32), u32)
    hi = lax.bitcast_convert_type(x[:, D_PACK:D_MODEL].astype(jnp.bfloat16).astype(jnp.float32), u32)
    return (hi & u32(0xFFFF0000)) | (lo >> u32(16))


def _unpack_halves(w):
    u32 = jnp.uint32
    lo = lax.bitcast_convert_type(w << u32(16), jnp.float32)
    hi = lax.bitcast_convert_type(w & u32(0xFFFF0000), jnp.float32)
    return lo, hi


def _mixer_kernel(x_ref, x_prev_ref, x_next_ref, w_in_ref, b_in_ref, conv_w_ref, conv_b_ref,
                  sg_g_ref, sg_b_ref, ws_ref, bs_ref, w_o_ref, b_o_ref, ln1_g_ref, ln1_b_ref,
                  w_r_ref, b_r_ref,
                  x1_ref, xp_ref, slab_ref, route_ref, counts_ref,
                  xb_ref, ymix_ref, tri_ref, resid_ref, carry_ref, *, ts, nt, n_tiles, alpha):
    t = pl.program_id(0)
    j = jnp.minimum(t, n_tiles - 1) % nt
    bf16 = jnp.bfloat16

    @pl.when(t == 0)
    def _():
        carry_ref[...] = jnp.zeros_like(carry_ref)
        resid_ref[...] = jnp.zeros_like(resid_ref)
        ri = lax.broadcasted_iota(jnp.int32, (ts, ts), 0)
        ci = lax.broadcasted_iota(jnp.int32, (ts, ts), 1)
        tri_ref[...] = jnp.where(ri > ci, 1.0, 0.0).astype(bf16)

    lane = lax.broadcasted_iota(jnp.int32, (ts, LANES), 1)
    lane_f = lane.astype(jnp.float32)
    neg = jnp.float32(-jnp.inf)
    big = jnp.float32(1e9)
    is_c = (lane >= COARSE_OFF) & (lane < COARSE_OFF + N_EXPERT_GROUPS)

    def proj(lhs, lo, hi):
        return _dot(lhs, w_in_ref[:, lo:hi]) + b_in_ref[:, lo:hi]

    def back_norm():
        x1 = _layer_norm(resid_ref[...], ln1_g_ref[...], ln1_b_ref[...])
        x1_ref[...] = x1
        xp_ref[...] = _pack_halves(x1)
        return x1.astype(bf16)

    def back_route(x1b):
        logits = _dot(x1b, w_r_ref[...]) + b_r_ref[...]
        lc = jnp.where(is_c, logits, neg)
        mx = jnp.max(lc, axis=1, keepdims=True)
        grp = jnp.min(jnp.where(lc == mx, lane_f - COARSE_OFF, big), axis=1, keepdims=True)
        p_grp = 1.0 / jnp.sum(jnp.where(is_c, jnp.exp(logits - mx), 0.0), axis=1, keepdims=True)
        grp_lo = grp * EXPERTS_PER_GROUP
        in_grp = (lane_f >= grp_lo) & (lane_f < grp_lo + EXPERTS_PER_GROUP)
        lf = jnp.where(in_grp, logits, neg)
        v1 = jnp.max(lf, axis=1, keepdims=True)
        e1 = jnp.min(jnp.where(lf == v1, lane_f, big), axis=1, keepdims=True)
        lf2 = jnp.where(lane_f == e1, neg, lf)
        v2 = jnp.max(lf2, axis=1, keepdims=True)
        e2 = jnp.min(jnp.where(lf2 == v2, lane_f, big), axis=1, keepdims=True)
        a = jnp.exp(v2 - v1)
        return e1, e2, p_grp / (1.0 + a), p_grp * a / (1.0 + a)

    def back_rank(e1, e2, w1, w2):
        hit1 = lane_f == e1
        hit2 = lane_f == e2
        onehot = jnp.where(hit1 | hit2, 1.0, 0.0)
        carry = carry_ref[...]
        before = _dot(tri_ref[...], onehot.astype(bf16)) + carry
        r1 = jnp.sum(jnp.where(hit1, before, 0.0), axis=1, keepdims=True)
        r2 = jnp.sum(jnp.where(hit2, before, 0.0), axis=1, keepdims=True)
        carry = jnp.where(t > 0, carry + jnp.sum(onehot, axis=0, keepdims=True), carry)
        carry_ref[...] = carry
        counts_ref[...] = jnp.broadcast_to(carry, counts_ref.shape)
        slab = jnp.where(lane == COL_E1, e1, 0.0)
        slab = jnp.where(lane == COL_E2, e2, slab)
        slab = jnp.where(lane == COL_W1, w1, slab)
        slab = jnp.where(lane == COL_W2, w2, slab)
        slab = jnp.where(lane == COL_R1, r1, slab)
        slab = jnp.where(lane == COL_R2, r2, slab)
        slab_ref[...] = slab
        route_ref[...] = slab.T[0:SUBLANES, :]

    def front_load():
        xb_ref[0:HALO, :] = x_prev_ref[...].astype(bf16)
        xb_ref[HALO:HALO + ts, :] = x_ref[...].astype(bf16)
        xb_ref[HALO + ts:HALO + ts + HALO, :] = x_next_ref[...].astype(bf16)

    def front_conv():
        xe = xb_ref[...]
        g_e = proj(xe, 0, D_CONV) * proj(xe, 2 * D_CONV, 3 * D_CONV)
        row_e = lax.broadcasted_iota(jnp.int32, (ts + 2 * HALO, 1), 0)
        has_prev = jnp.where(j > 0, 1.0, 0.0)
        has_next = jnp.where(j < nt - 1, 1.0, 0.0)
        g_e = g_e * jnp.where(row_e < HALO, has_prev, jnp.where(row_e >= HALO + ts, has_next, 1.0))
        conv = (g_e[HALO - 1:HALO - 1 + ts, :] * conv_w_ref[0:1, :] + g_e[HALO:HALO + ts, :] * conv_w_ref[1:2, :]
                + g_e[HALO + 1:HALO + 1 + ts, :] * conv_w_ref[2:3, :] + conv_b_ref[...])
        y_a = proj(xb_ref[HALO:HALO + ts, :], D_CONV, 2 * D_CONV) * conv
        ymix_ref[:, 0:D_CONV] = y_a.astype(bf16)

    def front_gate_proj():
        xm = xb_ref[HALO:HALO + ts, :]
        return proj(xm, 3 * D_CONV, 3 * D_CONV + D_SG), proj(xm, 3 * D_CONV + D_SG, 3 * D_CONV + 2 * D_SG)

    def front_gate(u_pre, v_pre):
        u = _gelu_tanh(u_pre)
        v = _gelu_tanh(v_pre)
        v_ln = _layer_norm(v, sg_g_ref[...], sg_b_ref[...]).astype(bf16)
        first_head = lax.broadcasted_iota(jnp.int32, (CHUNK, LANES), 1) < SG_HEAD_DIM
        for q0 in range(0, ts, 2 * CHUNK):
            q1 = q0 + CHUNK
            for hp in range(N_SG_HEADS // 2):
                c0 = hp * LANES
                rhs = jnp.concatenate([v_ln[q0:q0 + CHUNK, c0:c0 + LANES], v_ln[q1:q1 + CHUNK, c0:c0 + LANES]], axis=1)
                res = _dot(ws_ref[hp], rhs)
                bias = bs_ref[:, c0:c0 + LANES]
                m0 = jnp.where(first_head, res[0:CHUNK, 0:LANES], res[CHUNK:2 * CHUNK, 0:LANES]) + bias
                m1 = jnp.where(first_head, res[0:CHUNK, LANES:2 * LANES], res[CHUNK:2 * CHUNK, LANES:2 * LANES]) + bias
                ymix_ref[q0:q0 + CHUNK, D_CONV + c0:D_CONV + c0 + LANES] = (u[q0:q0 + CHUNK, c0:c0 + LANES] * m0).astype(bf16)
                ymix_ref[q1:q1 + CHUNK, D_CONV + c0:D_CONV + c0 + LANES] = (u[q1:q1 + CHUNK, c0:c0 + LANES] * m1).astype(bf16)

    def front_out_a():
        resid_ref[...] = alpha * x_ref[...] + b_o_ref[...] + _dot(ymix_ref[:, 0:D_CONV], w_o_ref[0:D_CONV, :])

    def front_out_b():
        resid_ref[...] += _dot(ymix_ref[:, D_CONV:D_MODEL], w_o_ref[D_CONV:D_MODEL, :])

    front_load()
    x1b = back_norm()
    routing = back_route(x1b)
    front_conv()
    back_rank(*routing)
    gate_pre = front_gate_proj()
    front_out_a()
    front_gate(*gate_pre)
    front_out_b()


def _mixer(x, p, alpha):
    nb, s, d = x.shape
    ts = SEQ_TILE
    nt = s // ts
    hb = ts // HALO
    n_tiles = nb * nt
    n_total = n_tiles * ts

    front = lambda t: jnp.minimum(t, n_tiles - 1)
    back = lambda t: jnp.maximum(t - 1, 0)
    cur = lambda t: (front(t) // nt, front(t) % nt, 0)
    prv = lambda t: (front(t) // nt, jnp.maximum((front(t) % nt) * hb - 1, 0), 0)
    nxt = lambda t: (front(t) // nt, jnp.minimum((front(t) % nt + 1) * hb, s // HALO - 1), 0)
    full = lambda a: pl.BlockSpec(a.shape, lambda t: (0,) * a.ndim)
    in_specs = [pl.BlockSpec((None, ts, d), cur), pl.BlockSpec((None, HALO, d), prv),
                pl.BlockSpec((None, HALO, d), nxt)] + [full(a) for a in p]
    out_shape = (jax.ShapeDtypeStruct((n_total, d), jnp.float32),
                 jax.ShapeDtypeStruct((n_total, D_PACK), jnp.uint32),
                 jax.ShapeDtypeStruct((n_total, LANES), jnp.float32),
                 jax.ShapeDtypeStruct((SUBLANES, n_total), jnp.float32),
                 jax.ShapeDtypeStruct((SUBLANES, LANES), jnp.float32))
    row_blk = lambda t: (back(t), 0)
    out_specs = (pl.BlockSpec((ts, d), row_blk),
                 pl.BlockSpec((ts, D_PACK), row_blk),
                 pl.BlockSpec((ts, LANES), row_blk),
                 pl.BlockSpec((SUBLANES, ts), lambda t: (0, back(t))),
                 pl.BlockSpec((SUBLANES, LANES), lambda t: (0, 0)))
    return pl.pallas_call(
        functools.partial(_mixer_kernel, ts=ts, nt=nt, n_tiles=n_tiles, alpha=alpha),
        grid=(n_tiles + 1,),
        in_specs=in_specs,
        out_specs=out_specs,
        out_shape=out_shape,
        scratch_shapes=[pltpu.VMEM((ts + 2 * HALO, d), jnp.bfloat16),
                        pltpu.VMEM((ts, d), jnp.bfloat16),
                        pltpu.VMEM((ts, ts), jnp.bfloat16),
                        pltpu.VMEM((ts, d), jnp.float32),
                        pltpu.VMEM((1, LANES), jnp.float32)],
        compiler_params=pltpu.CompilerParams(dimension_semantics=("arbitrary",),
                                             vmem_limit_bytes=VMEM_LIMIT_BYTES),
        cost_estimate=pl.CostEstimate(
            flops=2 * n_total * (d * (3 * D_CONV + 2 * D_SG) + CHUNK * D_SG + d * d + d * LANES + ts * LANES),
            transcendentals=n_total * (2 * D_SG + 2 * LANES),
            bytes_accessed=n_total * (2 * d * 4 + D_PACK * 4 + LANES * 4 + SUBLANES * 4)),
        name="mixer",
    )(x, x, x, *p)


def _sc_mesh():
    return plsc.VectorSubcoreMesh(core_axis_name="c", subcore_axis_name="s",
                                  num_cores=SC_CORES, num_subcores=SC_SUBCORES)


def _sc_worker_base(per_worker):
    return (lax.axis_index("s") * SC_CORES + lax.axis_index("c")) * per_worker


def _sc_dispatch(pos1, pos2, xp, n_rows):
    n, dp = xp.shape
    per_worker = n // SC_WORKERS
    steps = per_worker // SC_ROWS

    def body(pos1_hbm, pos2_hbm, xp_hbm, xs_hbm, idx1_v, idx2_v, rows_v, sem):
        base = _sc_worker_base(per_worker)

        @pl.loop(0, steps)
        def _(k):
            off = pl.multiple_of(base + k * SC_ROWS, SC_ROWS)
            pltpu.sync_copy(pos1_hbm.at[pl.ds(off, SC_ROWS)], idx1_v)
            pltpu.sync_copy(pos2_hbm.at[pl.ds(off, SC_ROWS)], idx2_v)
            pltpu.sync_copy(xp_hbm.at[pl.ds(off, SC_ROWS)], rows_v)
            c1 = pltpu.async_copy(rows_v, xs_hbm.at[idx1_v], sem)
            c2 = pltpu.async_copy(rows_v, xs_hbm.at[idx2_v], sem)
            c1.wait()
            c2.wait()

    return pl.kernel(
        body,
        out_type=jax.ShapeDtypeStruct((n_rows, dp), xp.dtype),
        mesh=_sc_mesh(),
        scratch_types=[pltpu.VMEM((SC_ROWS,), jnp.int32), pltpu.VMEM((SC_ROWS,), jnp.int32),
                       pltpu.VMEM((SC_ROWS, dp), xp.dtype), pltpu.SemaphoreType.DMA],
        compiler_params=pltpu.CompilerParams(use_tc_tiling_on_sc=True),
        cost_estimate=pl.CostEstimate(flops=0, transcendentals=0, bytes_accessed=3 * n * dp * 4 + 2 * n * 4),
        name="sc_dispatch",
    )(pos1, pos2, xp)


def _sc_gather(pos1, pos2, ys):
    n = pos1.shape[0]
    dp = ys.shape[1]
    per_worker = n // SC_WORKERS
    steps = per_worker // SC_ROWS

    def body(pos1_hbm, pos2_hbm, ys_hbm, out_hbm, idx_v, rows_v, sem):
        base = _sc_worker_base(per_worker)

        @pl.loop(0, steps)
        def _(k):
            off = pl.multiple_of(base + k * SC_ROWS, SC_ROWS)
            for slot, pos_hbm in enumerate((pos1_hbm, pos2_hbm)):
                pltpu.sync_copy(pos_hbm.at[pl.ds(off, SC_ROWS)], idx_v)
                pltpu.async_copy(ys_hbm.at[idx_v], rows_v, sem).wait()
                pltpu.sync_copy(rows_v, out_hbm.at[slot, pl.ds(off, SC_ROWS)])

    return pl.kernel(
        body,
        out_type=jax.ShapeDtypeStruct((2, n, dp), ys.dtype),
        mesh=_sc_mesh(),
        scratch_types=[pltpu.VMEM((SC_ROWS,), jnp.int32), pltpu.VMEM((SC_ROWS, dp), ys.dtype),
                       pltpu.SemaphoreType.DMA],
        compiler_params=pltpu.CompilerParams(use_tc_tiling_on_sc=True),
        cost_estimate=pl.CostEstimate(flops=0, transcendentals=0, bytes_accessed=4 * n * dp * 4 + 2 * n * 4),
        name="sc_gather",
    )(pos1, pos2, ys)


def _expert_kernel(te_ref, first_ref, nsub_ref, slot_ref, next_ref, nused_ref, xs_ref, wg_hbm, wu_hbm, wd_hbm,
                   ys_ref, wg_buf, wu_buf, wd_buf, wgu_bf_ref, wd_bf_ref, sem):
    i = pl.program_id(0)
    bf16 = jnp.bfloat16
    sub = EXPERT_SUBTILE

    def weight_copies(expert, slot):
        return (pltpu.make_async_copy(wg_hbm.at[expert], wg_buf.at[slot], sem.at[slot]),
                pltpu.make_async_copy(wu_hbm.at[expert], wu_buf.at[slot], sem.at[slot]),
                pltpu.make_async_copy(wd_hbm.at[expert], wd_buf.at[slot], sem.at[slot]))

    @pl.when(first_ref[i] == 1)
    def _():
        slot = slot_ref[i]

        @pl.when(i == 0)
        def _():
            for c in weight_copies(te_ref[i], slot):
                c.start()

        for c in weight_copies(te_ref[i], slot):
            c.wait()

        @pl.when(next_ref[i] >= 0)
        def _():
            for c in weight_copies(next_ref[i], 1 - slot):
                c.start()

        wgu_bf_ref[:, 0:D_EXPERT] = wg_buf[slot].astype(bf16)
        wgu_bf_ref[:, D_EXPERT:2 * D_EXPERT] = wu_buf[slot].astype(bf16)
        wd_bf_ref[...] = wd_buf[slot].astype(bf16)

    def swiglu(r0):
        rows = pl.ds(r0, sub)
        lo, hi = _unpack_halves(xs_ref[rows, :])
        gu = (_dot(lo.astype(bf16), wgu_bf_ref[0:D_PACK, :])
              + _dot(hi.astype(bf16), wgu_bf_ref[D_PACK:D_MODEL, :]))
        g = gu[:, 0:D_EXPERT]
        h = g * jax.nn.sigmoid(g) * gu[:, D_EXPERT:2 * D_EXPERT]
        ys_ref[rows, :] = _pack_halves(_dot(h.astype(bf16), wd_bf_ref[...]))

    n_sub = nsub_ref[i]

    def pair(k, c):
        r0 = pl.multiple_of(k * (2 * sub), 2 * sub)
        swiglu(r0)
        swiglu(r0 + sub)
        return c

    lax.fori_loop(0, n_sub // 2, pair, 0)

    @pl.when(n_sub % 2 == 1)
    def _():
        swiglu(pl.multiple_of((n_sub - 1) * sub, sub))

    def clear(k, c):
        ys_ref[pl.ds(pl.multiple_of(k * sub, sub), sub), :] = jnp.zeros((sub, D_PACK), ys_ref.dtype)
        return c

    lax.fori_loop(n_sub, jnp.where(i < nused_ref[0], ys_ref.shape[0] // sub, 0), clear, 0)


def _experts(tile_expert, tile_first, tile_nsub, tile_slot, tile_next, n_used, xs, w_gate, w_up, w_down):
    n_rows, dp = xs.shape
    d = D_MODEL
    tm = EXPERT_TILE
    row_blk = lambda i, te, fi, ns, sl, nx, nu: (jnp.minimum(i, nu[0] - 1), 0)
    grid_spec = pltpu.PrefetchScalarGridSpec(
        num_scalar_prefetch=6,
        grid=(n_rows // tm,),
        in_specs=[pl.BlockSpec((tm, dp), row_blk),
                  pl.BlockSpec(memory_space=pl.ANY),
                  pl.BlockSpec(memory_space=pl.ANY),
                  pl.BlockSpec(memory_space=pl.ANY)],
        out_specs=pl.BlockSpec((tm, dp), row_blk),
        scratch_shapes=[pltpu.VMEM((2, d, D_EXPERT), jnp.float32),
                        pltpu.VMEM((2, d, D_EXPERT), jnp.float32),
                        pltpu.VMEM((2, D_EXPERT, d), jnp.float32),
                        pltpu.VMEM((d, 2 * D_EXPERT), jnp.bfloat16),
                        pltpu.VMEM((D_EXPERT, d), jnp.bfloat16),
                        pltpu.SemaphoreType.DMA((2,))],
    )
    return pl.pallas_call(
        _expert_kernel,
        grid_spec=grid_spec,
        out_shape=jax.ShapeDtypeStruct((n_rows, dp), jnp.uint32),
        compiler_params=pltpu.CompilerParams(dimension_semantics=("arbitrary",),
                                             vmem_limit_bytes=VMEM_LIMIT_BYTES),
        cost_estimate=pl.CostEstimate(flops=2 * n_rows * 3 * d * D_EXPERT, transcendentals=n_rows * D_EXPERT,
                                      bytes_accessed=2 * n_rows * dp * 4 + N_EXPERTS * 3 * d * D_EXPERT * 4),
        name="experts",
    )(tile_expert, tile_first, tile_nsub, tile_slot, tile_next, n_used, xs, w_gate, w_up, w_down)


def _combine_kernel(x1_ref, slab_ref, y1_ref, y2_ref, g_ref, b_ref, out_ref, *, alpha):
    slab = slab_ref[...]
    w1 = slab[:, COL_W1:COL_W1 + 1]
    w2 = slab[:, COL_W2:COL_W2 + 1]
    lo1, hi1 = _unpack_halves(y1_ref[...])
    lo2, hi2 = _unpack_halves(y2_ref[...])
    moe = jnp.concatenate([w1 * lo1 + w2 * lo2, w1 * hi1 + w2 * hi2], axis=1)
    out_ref[...] = _layer_norm(alpha * x1_ref[...] + moe, g_ref[...], b_ref[...])


def _combine(x1, slab, yg, g, b, alpha):
    n, d = x1.shape
    tt = TOKEN_TILE
    return pl.pallas_call(
        functools.partial(_combine_kernel, alpha=alpha),
        grid=(n // tt,),
        in_specs=[pl.BlockSpec((tt, d), lambda i: (i, 0)),
                  pl.BlockSpec((tt, LANES), lambda i: (i, 0)),
                  pl.BlockSpec((None, tt, D_PACK), lambda i: (0, i, 0)),
                  pl.BlockSpec((None, tt, D_PACK), lambda i: (1, i, 0)),
                  pl.BlockSpec((1, d), lambda i: (0, 0)),
                  pl.BlockSpec((1, d), lambda i: (0, 0))],
        out_specs=pl.BlockSpec((tt, d), lambda i: (i, 0)),
        out_shape=jax.ShapeDtypeStruct((n, d), jnp.float32),
        compiler_params=pltpu.CompilerParams(dimension_semantics=("arbitrary",),
                                             vmem_limit_bytes=VMEM_LIMIT_BYTES),
        cost_estimate=pl.CostEstimate(flops=12 * n * d, transcendentals=n,
                                      bytes_accessed=n * (2 * d * 4 + 2 * D_PACK * 4 + LANES * 4)),
        name="combine",
    )(x1, slab, yg, yg, g, b)


def _row(a):
    return a.reshape(1, -1)


def _mixer_params(w_in, b_in, conv_w, conv_b, sg_ln_g, sg_ln_b, w_s, b_s, w_o, b_o, ln1_g, ln1_b,
                  w_rc, b_rc, w_rf, b_rf):
    bf16 = jnp.bfloat16
    row = _row
    ws_pairs = w_s.reshape(N_SG_HEADS // 2, 2 * CHUNK, CHUNK).astype(bf16)
    bs_full = jnp.repeat(b_s.T, SG_HEAD_DIM, axis=1)
    pad = LANES - N_EXPERTS - N_EXPERT_GROUPS
    w_r = jnp.pad(jnp.concatenate([w_rf, w_rc], axis=1), ((0, 0), (0, pad))).astype(bf16)
    b_r = jnp.pad(jnp.concatenate([b_rf, b_rc]), (0, pad)).reshape(1, LANES)
    return (w_in.astype(bf16), row(b_in), conv_w, row(conv_b), row(sg_ln_g), row(sg_ln_b),
            ws_pairs, bs_full, w_o.astype(bf16), row(b_o), row(ln1_g), row(ln1_b), w_r, b_r)


def _encoder_layer(x, mixer_params, w_gate, w_up, w_down, ln2_g, ln2_b, alpha):
    n = x.shape[0] * x.shape[1]
    x1, xp, slab, route, carry = _mixer(x, mixer_params, alpha)

    tm = EXPERT_TILE
    n_tiles = 2 * n // tm + N_EXPERTS
    cnt = carry[0, 0:N_EXPERTS].astype(jnp.int32)
    tiles_e = (cnt + tm - 1) // tm
    experts = jnp.arange(N_EXPERTS, dtype=jnp.int32)[:, None]
    up_to = experts.T <= experts
    tile_end = jnp.sum(jnp.where(up_to, tiles_e[None, :], 0), axis=1)
    tile_start = tile_end - tiles_e
    row_off = tile_start * tm

    def sorted_pos(e_row, r_row):
        e = route[e_row].astype(jnp.int32)
        return jnp.sum(jnp.where(e[None, :] == experts, row_off[:, None], 0), axis=0) + route[r_row].astype(jnp.int32)

    pos1 = sorted_pos(COL_E1, COL_R1)
    pos2 = sorted_pos(COL_E2, COL_R2)
    tile_ids = jnp.arange(n_tiles, dtype=jnp.int32)
    n_used = tile_end[N_EXPERTS - 1:N_EXPERTS]
    tile_expert = jnp.sum((tile_end[None, :] <= tile_ids[:, None]).astype(jnp.int32), axis=1)
    last_used = jnp.sum((tile_end <= n_used[0] - 1).astype(jnp.int32))
    tile_expert = jnp.minimum(tile_expert, last_used)
    tile_first = (jnp.any(tile_ids[:, None] == tile_start[None, :], axis=1) & (tile_ids < n_used[0])) | (tile_ids == 0)
    tile_first = tile_first.astype(jnp.int32)
    rows_left = jnp.sum(jnp.where(tile_expert[:, None] == experts.T, (row_off + cnt)[None, :], 0), axis=1) - tile_ids * tm
    tile_nsub = jnp.where(tile_ids < n_used[0], (jnp.clip(rows_left, 0, tm) + EXPERT_SUBTILE - 1) // EXPERT_SUBTILE, 0)

    used = tiles_e > 0
    ordinal = jnp.sum((up_to & used[None, :]).astype(jnp.int32), axis=1) - 1
    e_ids = experts[:, 0]
    later_used = used[None, :] & (e_ids[None, :] > e_ids[:, None])
    next_used = jnp.min(jnp.where(later_used, e_ids[None, :], N_EXPERTS), axis=1)
    next_used = jnp.where(next_used < N_EXPERTS, next_used, -1)
    of_tile = tile_expert[:, None] == experts.T
    tile_slot = jnp.sum(jnp.where(of_tile, ordinal[None, :], 0), axis=1) % 2
    tile_next = jnp.sum(jnp.where(of_tile, next_used[None, :], 0), axis=1)

    xs = _sc_dispatch(pos1, pos2, xp, n_tiles * tm)
    ys = _experts(tile_expert, tile_first, tile_nsub.astype(jnp.int32), tile_slot.astype(jnp.int32),
                  tile_next.astype(jnp.int32), n_used, xs, w_gate, w_up, w_down)
    yg = _sc_gather(pos1, pos2, ys)
    return _combine(x1, slab, yg, _row(ln2_g), _row(ln2_b), alpha).reshape(x.shape)


def kernel(x_prompt, x_sample, w_in, b_in, conv_w, conv_b, sg_ln_g, sg_ln_b, w_s, b_s, w_o, b_o, ln1_g, ln1_b, w_rc, b_rc, w_rf, b_rf, w_gate, w_up, w_down, ln2_g, ln2_b):
    depth = w_in.shape[0]
    alpha = (2.0 * depth) ** 0.25
    xs = (x_prompt, x_sample)
    for l in range(depth):
        mixer_params = _mixer_params(w_in[l], b_in[l], conv_w[l], conv_b[l], sg_ln_g[l], sg_ln_b[l], w_s[l], b_s[l],
                                     w_o[l], b_o[l], ln1_g[l], ln1_b[l], w_rc[l], b_rc[l], w_rf[l], b_rf[l])
        xs = tuple(_encoder_layer(x, mixer_params, w_gate[l], w_up[l], w_down[l], ln2_g[l], ln2_b[l], alpha)
                   for x in xs)
    return xs
```

```python
import functools

import jax
import jax.numpy as jnp
from jax import lax
from jax.experimental import pallas as pl
from jax.experimental.pallas import tpu as pltpu
from jax.experimental.pallas import tpu_sc as plsc

D_MODEL = 1024
D_CONV = 512
D_SG = 512
N_SG_HEADS = 8
SG_HEAD_DIM = D_SG // N_SG_HEADS
CHUNK = 128
N_EXPERT_GROUPS = 4
EXPERTS_PER_GROUP = 8
N_EXPERTS = N_EXPERT_GROUPS * EXPERTS_PER_GROUP
D_EXPERT = 512
LN_EPS = 1e-5
D_PACK = D_MODEL // 2

LANES = 128
SUBLANES = 8
HALO = 16
SEQ_TILE = 512
EXPERT_TILE = 1024
EXPERT_SUBTILE = 256
TOKEN_TILE = 1024
SC_CORES = 2
SC_SUBCORES = 16
SC_WORKERS = SC_CORES * SC_SUBCORES
SC_ROWS = 64
VMEM_LIMIT_BYTES = 56 * 1024 * 1024

COL_E1, COL_E2, COL_W1, COL_W2, COL_R1, COL_R2 = 0, 1, 2, 3, 4, 5
COARSE_OFF = N_EXPERTS


def _dot(a, b):
    return jnp.dot(a, b, preferred_element_type=jnp.float32)


def _gelu_tanh(x):
    return 0.5 * x * (1.0 + jnp.tanh(0.7978845608028654 * (x + 0.044715 * (x * x * x))))


def _layer_norm(x, g, b):
    mu = jnp.mean(x, axis=-1, keepdims=True)
    xc = x - mu
    var = jnp.mean(xc * xc, axis=-1, keepdims=True)
    return xc * lax.rsqrt(var + LN_EPS) * g + b


def _pack_halves(x):
    u32 = jnp.uint32
    lo = lax.bitcast_convert_type(x[:, 0:D_PACK].astype(jnp.bfloat16).astype(jnp.float32), u32)
    hi = lax.bitcast_convert_type(x[:, D_PACK:D_MODEL].astype(jnp.bfloat16).astype(jnp.float32), u32)
    return (hi & u32(0xFFFF0000)) | (lo >> u32(16))


def _unpack_halves(w):
    u32 = jnp.uint32
    lo = lax.bitcast_convert_type(w << u32(16), jnp.float32)
    hi = lax.bitcast_convert_type(w & u32(0xFFFF0000), jnp.float32)
    return lo, hi


def _mixer_kernel(x_ref, x_prev_ref, x_next_ref, w_in_ref, b_in_ref, conv_w_ref, conv_b_ref,
                  sg_g_ref, sg_b_ref, ws_ref, bs_ref, w_o_ref, b_o_ref, ln1_g_ref, ln1_b_ref,
                  w_r_ref, b_r_ref,
                  x1_ref, xp_ref, slab_ref, route_ref, counts_ref,
                  xb_ref, ymix_ref, tri_ref, resid_ref, carry_ref, *, ts, nt, n_tiles, alpha):
    t = pl.program_id(0)
    j = jnp.minimum(t, n_tiles - 1) % nt
    bf16 = jnp.bfloat16

    @pl.when(t == 0)
    def _():
        carry_ref[...] = jnp.zeros_like(carry_ref)
        resid_ref[...] = jnp.zeros_like(resid_ref)
        ri = lax.broadcasted_iota(jnp.int32, (ts, ts), 0)
        ci = lax.broadcasted_iota(jnp.int32, (ts, ts), 1)
        tri_ref[...] = jnp.where(ri > ci, 1.0, 0.0).astype(bf16)

    lane = lax.broadcasted_iota(jnp.int32, (ts, LANES), 1)
    lane_f = lane.astype(jnp.float32)
    neg = jnp.float32(-jnp.inf)
    big = jnp.float32(1e9)
    is_c = (lane >= COARSE_OFF) & (lane < COARSE_OFF + N_EXPERT_GROUPS)

    def proj(lhs, lo, hi):
        return _dot(lhs, w_in_ref[:, lo:hi]) + b_in_ref[:, lo:hi]

    def back_norm():
        x1 = _layer_norm(resid_ref[...], ln1_g_ref[...], ln1_b_ref[...])
        x1_ref[...] = x1
        xp_ref[...] = _pack_halves(x1)
        return x1.astype(bf16)

    def back_route(x1b):
        logits = _dot(x1b, w_r_ref[...]) + b_r_ref[...]
        lc = jnp.where(is_c, logits, neg)
        mx = jnp.max(lc, axis=1, keepdims=True)
        grp = jnp.min(jnp.where(lc == mx, lane_f - COARSE_OFF, big), axis=1, keepdims=True)
        p_grp = 1.0 / jnp.sum(jnp.where(is_c, jnp.exp(logits - mx), 0.0), axis=1, keepdims=True)
        grp_lo = grp * EXPERTS_PER_GROUP
        in_grp = (lane_f >= grp_lo) & (lane_f < grp_lo + EXPERTS_PER_GROUP)
        lf = jnp.where(in_grp, logits, neg)
        v1 = jnp.max(lf, axis=1, keepdims=True)
        e1 = jnp.min(jnp.where(lf == v1, lane_f, big), axis=1, keepdims=True)
        lf2 = jnp.where(lane_f == e1, neg, lf)
        v2 = jnp.max(lf2, axis=1, keepdims=True)
        e2 = jnp.min(jnp.where(lf2 == v2, lane_f, big), axis=1, keepdims=True)
        a = jnp.exp(v2 - v1)
        return e1, e2, p_grp / (1.0 + a), p_grp * a / (1.0 + a)

    def back_rank(e1, e2, w1, w2):
        hit1 = lane_f == e1
        hit2 = lane_f == e2
        onehot = jnp.where(hit1 | hit2, 1.0, 0.0)
        carry = carry_ref[...]
        before = _dot(tri_ref[...], onehot.astype(bf16)) + carry
        r1 = jnp.sum(jnp.where(hit1, before, 0.0), axis=1, keepdims=True)
        r2 = jnp.sum(jnp.where(hit2, before, 0.0), axis=1, keepdims=True)
        carry = jnp.where(t > 0, carry + jnp.sum(onehot, axis=0, keepdims=True), carry)
        carry_ref[...] = carry
        counts_ref[...] = jnp.broadcast_to(carry, counts_ref.shape)
        slab = jnp.where(lane == COL_E1, e1, 0.0)
        slab = jnp.where(lane == COL_E2, e2, slab)
        slab = jnp.where(lane == COL_W1, w1, slab)
        slab = jnp.where(lane == COL_W2, w2, slab)
        slab = jnp.where(lane == COL_R1, r1, slab)
        slab = jnp.where(lane == COL_R2, r2, slab)
        slab_ref[...] = slab
        route_ref[...] = slab.T[0:SUBLANES, :]

    def front_load():
        xb_ref[0:HALO, :] = x_prev_ref[...].astype(bf16)
        xb_ref[HALO:HALO + ts, :] = x_ref[...].astype(bf16)
        xb_ref[HALO + ts:HALO + ts + HALO, :] = x_next_ref[...].astype(bf16)

    def front_conv():
        xe = xb_ref[...]
        g_e = proj(xe, 0, D_CONV) * proj(xe, 2 * D_CONV, 3 * D_CONV)
        row_e = lax.broadcasted_iota(jnp.int32, (ts + 2 * HALO, 1), 0)
        has_prev = jnp.where(j > 0, 1.0, 0.0)
        has_next = jnp.where(j < nt - 1, 1.0, 0.0)
        g_e = g_e * jnp.where(row_e < HALO, has_prev, jnp.where(row_e >= HALO + ts, has_next, 1.0))
        conv = (g_e[HALO - 1:HALO - 1 + ts, :] * conv_w_ref[0:1, :] + g_e[HALO:HALO + ts, :] * conv_w_ref[1:2, :]
                + g_e[HALO + 1:HALO + 1 + ts, :] * conv_w_ref[2:3, :] + conv_b_ref[...])
        y_a = proj(xb_ref[HALO:HALO + ts, :], D_CONV, 2 * D_CONV) * conv
        ymix_ref[:, 0:D_CONV] = y_a.astype(bf16)

    def front_gate_proj():
        xm = xb_ref[HALO:HALO + ts, :]
        return proj(xm, 3 * D_CONV, 3 * D_CONV + D_SG), proj(xm, 3 * D_CONV + D_SG, 3 * D_CONV + 2 * D_SG)

    def front_gate(u_pre, v_pre):
        u = _gelu_tanh(u_pre)
        v = _gelu_tanh(v_pre)
        v_ln = _layer_norm(v, sg_g_ref[...], sg_b_ref[...]).astype(bf16)
        first_head = lax.broadcasted_iota(jnp.int32, (CHUNK, LANES), 1) < SG_HEAD_DIM
        for q0 in range(0, ts, 2 * CHUNK):
            q1 = q0 + CHUNK
            for hp in range(N_SG_HEADS // 2):
                c0 = hp * LANES
                rhs = jnp.concatenate([v_ln[q0:q0 + CHUNK, c0:c0 + LANES], v_ln[q1:q1 + CHUNK, c0:c0 + LANES]], axis=1)
                res = _dot(ws_ref[hp], rhs)
                bias = bs_ref[:, c0:c0 + LANES]
                m0 = jnp.where(first_head, res[0:CHUNK, 0:LANES], res[CHUNK:2 * CHUNK, 0:LANES]) + bias
                m1 = jnp.where(first_head, res[0:CHUNK, LANES:2 * LANES], res[CHUNK:2 * CHUNK, LANES:2 * LANES]) + bias
                ymix_ref[q0:q0 + CHUNK, D_CONV + c0:D_CONV + c0 + LANES] = (u[q0:q0 + CHUNK, c0:c0 + LANES] * m0).astype(bf16)
                ymix_ref[q1:q1 + CHUNK, D_CONV + c0:D_CONV + c0 + LANES] = (u[q1:q1 + CHUNK, c0:c0 + LANES] * m1).astype(bf16)

    def front_out_a():
        resid_ref[...] = alpha * x_ref[...] + b_o_ref[...] + _dot(ymix_ref[:, 0:D_CONV], w_o_ref[0:D_CONV, :])

    def front_out_b():
        resid_ref[...] += _dot(ymix_ref[:, D_CONV:D_MODEL], w_o_ref[D_CONV:D_MODEL, :])

    front_load()
    x1b = back_norm()
    routing = back_route(x1b)
    front_conv()
    back_rank(*routing)
    gate_pre = front_gate_proj()
    front_out_a()
    front_gate(*gate_pre)
    front_out_b()


def _mixer(x, p, alpha):
    nb, s, d = x.shape
    ts = SEQ_TILE
    nt = s // ts
    hb = ts // HALO
    n_tiles = nb * nt
    n_total = n_tiles * ts

    front = lambda t: jnp.minimum(t, n_tiles - 1)
    back = lambda t: jnp.maximum(t - 1, 0)
    cur = lambda t: (front(t) // nt, front(t) % nt, 0)
    prv = lambda t: (front(t) // nt, jnp.maximum((front(t) % nt) * hb - 1, 0), 0)
    nxt = lambda t: (front(t) // nt, jnp.minimum((front(t) % nt + 1) * hb, s // HALO - 1), 0)
    full = lambda a: pl.BlockSpec(a.shape, lambda t: (0,) * a.ndim)
    in_specs = [pl.BlockSpec((None, ts, d), cur), pl.BlockSpec((None, HALO, d), prv),
                pl.BlockSpec((None, HALO, d), nxt)] + [full(a) for a in p]
    out_shape = (jax.ShapeDtypeStruct((n_total, d), jnp.float32),
                 jax.ShapeDtypeStruct((n_total, D_PACK), jnp.uint32),
                 jax.ShapeDtypeStruct((n_total, LANES), jnp.float32),
                 jax.ShapeDtypeStruct((SUBLANES, n_total), jnp.float32),
                 jax.ShapeDtypeStruct((SUBLANES, LANES), jnp.float32))
    row_blk = lambda t: (back(t), 0)
    out_specs = (pl.BlockSpec((ts, d), row_blk),
                 pl.BlockSpec((ts, D_PACK), row_blk),
                 pl.BlockSpec((ts, LANES), row_blk),
                 pl.BlockSpec((SUBLANES, ts), lambda t: (0, back(t))),
                 pl.BlockSpec((SUBLANES, LANES), lambda t: (0, 0)))
    return pl.pallas_call(
        functools.partial(_mixer_kernel, ts=ts, nt=nt, n_tiles=n_tiles, alpha=alpha),
        grid=(n_tiles + 1,),
        in_specs=in_specs,
        out_specs=out_specs,
        out_shape=out_shape,
        scratch_shapes=[pltpu.VMEM((ts + 2 * HALO, d), jnp.bfloat16),
                        pltpu.VMEM((ts, d), jnp.bfloat16),
                        pltpu.VMEM((ts, ts), jnp.bfloat16),
                        pltpu.VMEM((ts, d), jnp.float32),
                        pltpu.VMEM((1, LANES), jnp.float32)],
        compiler_params=pltpu.CompilerParams(dimension_semantics=("arbitrary",),
                                             vmem_limit_bytes=VMEM_LIMIT_BYTES),
        cost_estimate=pl.CostEstimate(
            flops=2 * n_total * (d * (3 * D_CONV + 2 * D_SG) + CHUNK * D_SG + d * d + d * LANES + ts * LANES),
            transcendentals=n_total * (2 * D_SG + 2 * LANES),
            bytes_accessed=n_total * (2 * d * 4 + D_PACK * 4 + LANES * 4 + SUBLANES * 4)),
        name="mixer",
    )(x, x, x, *p)


def _sc_mesh():
    return plsc.VectorSubcoreMesh(core_axis_name="c", subcore_axis_name="s",
                                  num_cores=SC_CORES, num_subcores=SC_SUBCORES)


def _sc_worker_base(per_worker):
    return (lax.axis_index("s") * SC_CORES + lax.axis_index("c")) * per_worker


def _sc_dispatch(pos1, pos2, xp, n_rows):
    n, dp = xp.shape
    per_worker = n // SC_WORKERS
    steps = per_worker // SC_ROWS

    def body(pos1_hbm, pos2_hbm, xp_hbm, xs_hbm, idx1_v, idx2_v, rows_v, sem):
        base = _sc_worker_base(per_worker)

        @pl.loop(0, steps)
        def _(k):
            off = pl.multiple_of(base + k * SC_ROWS, SC_ROWS)
            pltpu.sync_copy(pos1_hbm.at[pl.ds(off, SC_ROWS)], idx1_v)
            pltpu.sync_copy(pos2_hbm.at[pl.ds(off, SC_ROWS)], idx2_v)
            pltpu.sync_copy(xp_hbm.at[pl.ds(off, SC_ROWS)], rows_v)
            c1 = pltpu.async_copy(rows_v, xs_hbm.at[idx1_v], sem)
            c2 = pltpu.async_copy(rows_v, xs_hbm.at[idx2_v], sem)
            c1.wait()
            c2.wait()

    return pl.kernel(
        body,
        out_type=jax.ShapeDtypeStruct((n_rows, dp), xp.dtype),
        mesh=_sc_mesh(),
        scratch_types=[pltpu.VMEM((SC_ROWS,), jnp.int32), pltpu.VMEM((SC_ROWS,), jnp.int32),
                       pltpu.VMEM((SC_ROWS, dp), xp.dtype), pltpu.SemaphoreType.DMA],
        compiler_params=pltpu.CompilerParams(use_tc_tiling_on_sc=True),
        cost_estimate=pl.CostEstimate(flops=0, transcendentals=0, bytes_accessed=3 * n * dp * 4 + 2 * n * 4),
        name="sc_dispatch",
    )(pos1, pos2, xp)


def _sc_gather(pos1, pos2, ys):
    n = pos1.shape[0]
    dp = ys.shape[1]
    per_worker = n // SC_WORKERS
    steps = per_worker // SC_ROWS

    def body(pos1_hbm, pos2_hbm, ys_hbm, out_hbm, idx_v, rows_v, sem):
        base = _sc_worker_base(per_worker)

        @pl.loop(0, steps)
        def _(k):
            off = pl.multiple_of(base + k * SC_ROWS, SC_ROWS)
            for slot, pos_hbm in enumerate((pos1_hbm, pos2_hbm)):
                pltpu.sync_copy(pos_hbm.at[pl.ds(off, SC_ROWS)], idx_v)
                pltpu.async_copy(ys_hbm.at[idx_v], rows_v, sem).wait()
                pltpu.sync_copy(rows_v, out_hbm.at[slot, pl.ds(off, SC_ROWS)])

    return pl.kernel(
        body,
        out_type=jax.ShapeDtypeStruct((2, n, dp), ys.dtype),
        mesh=_sc_mesh(),
        scratch_types=[pltpu.VMEM((SC_ROWS,), jnp.int32), pltpu.VMEM((SC_ROWS, dp), ys.dtype),
                       pltpu.SemaphoreType.DMA],
        compiler_params=pltpu.CompilerParams(use_tc_tiling_on_sc=True),
        cost_estimate=pl.CostEstimate(flops=0, transcendentals=0, bytes_accessed=4 * n * dp * 4 + 2 * n * 4),
        name="sc_gather",
    )(pos1, pos2, ys)


def _expert_kernel(te_ref, first_ref, nsub_ref, slot_ref, next_ref, nused_ref, xs_ref, *refs, cast_weights):
    i = pl.program_id(0)
    bf16 = jnp.bfloat16
    sub = EXPERT_SUBTILE
    slot = slot_ref[i]

    if cast_weights:
        (wg_hbm, wu_hbm, wd_hbm, ys_ref, wgu_out, wd_out,
         wg_buf, wu_buf, wd_buf, wgu_bf_ref, wd_bf_ref, sem, out_sem) = refs
        wgu_ref, wd_ref = wgu_bf_ref, wd_bf_ref

        def weight_copies(expert, s):
            return (pltpu.make_async_copy(wg_hbm.at[expert], wg_buf.at[s], sem.at[s]),
                    pltpu.make_async_copy(wu_hbm.at[expert], wu_buf.at[s], sem.at[s]),
                    pltpu.make_async_copy(wd_hbm.at[expert], wd_buf.at[s], sem.at[s]))

        def emit_copies(expert):
            return (pltpu.make_async_copy(wgu_bf_ref, wgu_out.at[expert], out_sem),
                    pltpu.make_async_copy(wd_bf_ref, wd_out.at[expert], out_sem))
    else:
        wgu_hbm, wd_hbm, ys_ref, wgu_buf, wd_buf, sem = refs
        wgu_ref, wd_ref = wgu_buf.at[slot], wd_buf.at[slot]

        def weight_copies(expert, s):
            return (pltpu.make_async_copy(wgu_hbm.at[expert], wgu_buf.at[s], sem.at[s]),
                    pltpu.make_async_copy(wd_hbm.at[expert], wd_buf.at[s], sem.at[s]))

    @pl.when(first_ref[i] == 1)
    def _():
        @pl.when(i == 0)
        def _():
            for c in weight_copies(te_ref[i], slot):
                c.start()

        for c in weight_copies(te_ref[i], slot):
            c.wait()

        @pl.when(next_ref[i] >= 0)
        def _():
            for c in weight_copies(next_ref[i], 1 - slot):
                c.start()

        if cast_weights:
            @pl.when(i > 0)
            def _():
                for c in emit_copies(te_ref[i]):
                    c.wait()

            wgu_bf_ref[:, 0:D_EXPERT] = wg_buf[slot].astype(bf16)
            wgu_bf_ref[:, D_EXPERT:2 * D_EXPERT] = wu_buf[slot].astype(bf16)
            wd_bf_ref[...] = wd_buf[slot].astype(bf16)
            for c in emit_copies(te_ref[i]):
                c.start()

    def swiglu(r0):
        rows = pl.ds(r0, sub)
        lo, hi = _unpack_halves(xs_ref[rows, :])
        gu = (_dot(lo.astype(bf16), wgu_ref[0:D_PACK, :])
              + _dot(hi.astype(bf16), wgu_ref[D_PACK:D_MODEL, :]))
        g = gu[:, 0:D_EXPERT]
        h = g * jax.nn.sigmoid(g) * gu[:, D_EXPERT:2 * D_EXPERT]
        ys_ref[rows, :] = _pack_halves(_dot(h.astype(bf16), wd_ref[...]))

    n_sub = nsub_ref[i]

    def pair(k, c):
        r0 = pl.multiple_of(k * (2 * sub), 2 * sub)
        swiglu(r0)
        swiglu(r0 + sub)
        return c

    lax.fori_loop(0, n_sub // 2, pair, 0)

    @pl.when(n_sub % 2 == 1)
    def _():
        swiglu(pl.multiple_of((n_sub - 1) * sub, sub))

    def clear(k, c):
        ys_ref[pl.ds(pl.multiple_of(k * sub, sub), sub), :] = jnp.zeros((sub, D_PACK), ys_ref.dtype)
        return c

    lax.fori_loop(n_sub, jnp.where(i < nused_ref[0], ys_ref.shape[0] // sub, 0), clear, 0)

    if cast_weights:
        @pl.when(i == pl.num_programs(0) - 1)
        def _():
            for c in emit_copies(te_ref[i]):
                c.wait()


def _experts(tiles, xs, weights):
    n_rows, dp = xs.shape
    d = D_MODEL
    tm = EXPERT_TILE
    cast_weights = len(weights) == 3
    row_blk = lambda i, te, fi, ns, sl, nx, nu: (jnp.minimum(i, nu[0] - 1), 0)
    any_spec = pl.BlockSpec(memory_space=pl.ANY)
    ys_shape = jax.ShapeDtypeStruct((n_rows, dp), jnp.uint32)
    ys_spec = pl.BlockSpec((tm, dp), row_blk)
    wgu_bf = ((d, 2 * D_EXPERT), jnp.bfloat16)
    wd_bf = ((D_EXPERT, d), jnp.bfloat16)
    if cast_weights:
        out_shape = (ys_shape, jax.ShapeDtypeStruct((N_EXPERTS,) + wgu_bf[0], wgu_bf[1]),
                     jax.ShapeDtypeStruct((N_EXPERTS,) + wd_bf[0], wd_bf[1]))
        out_specs = (ys_spec, any_spec, any_spec)
        scratch = [pltpu.VMEM((2, d, D_EXPERT), jnp.float32), pltpu.VMEM((2, d, D_EXPERT), jnp.float32),
                   pltpu.VMEM((2, D_EXPERT, d), jnp.float32), pltpu.VMEM(*wgu_bf), pltpu.VMEM(*wd_bf),
                   pltpu.SemaphoreType.DMA((2,)), pltpu.SemaphoreType.DMA]
        weight_bytes = N_EXPERTS * 3 * d * D_EXPERT * (4 + 2)
    else:
        out_shape, out_specs = ys_shape, ys_spec
        scratch = [pltpu.VMEM((2,) + wgu_bf[0], wgu_bf[1]), pltpu.VMEM((2,) + wd_bf[0], wd_bf[1]),
                   pltpu.SemaphoreType.DMA((2,))]
        weight_bytes = N_EXPERTS * 3 * d * D_EXPERT * 2
    grid_spec = pltpu.PrefetchScalarGridSpec(
        num_scalar_prefetch=6,
        grid=(n_rows // tm,),
        in_specs=[pl.BlockSpec((tm, dp), row_blk)] + [any_spec] * len(weights),
        out_specs=out_specs,
        scratch_shapes=scratch,
    )
    return pl.pallas_call(
        functools.partial(_expert_kernel, cast_weights=cast_weights),
        grid_spec=grid_spec,
        out_shape=out_shape,
        compiler_params=pltpu.CompilerParams(dimension_semantics=("arbitrary",),
                                             vmem_limit_bytes=VMEM_LIMIT_BYTES),
        cost_estimate=pl.CostEstimate(flops=2 * n_rows * 3 * d * D_EXPERT, transcendentals=n_rows * D_EXPERT,
                                      bytes_accessed=2 * n_rows * dp * 4 + weight_bytes),
        name="experts",
    )(*tiles, xs, *weights)


def _combine_kernel(x1_ref, slab_ref, y1_ref, y2_ref, g_ref, b_ref, out_ref, *, alpha):
    slab = slab_ref[...]
    w1 = slab[:, COL_W1:COL_W1 + 1]
    w2 = slab[:, COL_W2:COL_W2 + 1]
    lo1, hi1 = _unpack_halves(y1_ref[...])
    lo2, hi2 = _unpack_halves(y2_ref[...])
    moe = jnp.concatenate([w1 * lo1 + w2 * lo2, w1 * hi1 + w2 * hi2], axis=1)
    out_ref[...] = _layer_norm(alpha * x1_ref[...] + moe, g_ref[...], b_ref[...])


def _combine(x1, slab, yg, g, b, alpha):
    n, d = x1.shape
    tt = TOKEN_TILE
    return pl.pallas_call(
        functools.partial(_combine_kernel, alpha=alpha),
        grid=(n // tt,),
        in_specs=[pl.BlockSpec((tt, d), lambda i: (i, 0)),
                  pl.BlockSpec((tt, LANES), lambda i: (i, 0)),
                  pl.BlockSpec((None, tt, D_PACK), lambda i: (0, i, 0)),
                  pl.BlockSpec((None, tt, D_PACK), lambda i: (1, i, 0)),
                  pl.BlockSpec((1, d), lambda i: (0, 0)),
                  pl.BlockSpec((1, d), lambda i: (0, 0))],
        out_specs=pl.BlockSpec((tt, d), lambda i: (i, 0)),
        out_shape=jax.ShapeDtypeStruct((n, d), jnp.float32),
        compiler_params=pltpu.CompilerParams(dimension_semantics=("arbitrary",),
                                             vmem_limit_bytes=VMEM_LIMIT_BYTES),
        cost_estimate=pl.CostEstimate(flops=12 * n * d, transcendentals=n,
                                      bytes_accessed=n * (2 * d * 4 + 2 * D_PACK * 4 + LANES * 4)),
        name="combine",
    )(x1, slab, yg, yg, g, b)


def _row(a):
    return a.reshape(1, -1)


def _mixer_params(w_in, b_in, conv_w, conv_b, sg_ln_g, sg_ln_b, w_s, b_s, w_o, b_o, ln1_g, ln1_b,
                  w_rc, b_rc, w_rf, b_rf):
    bf16 = jnp.bfloat16
    row = _row
    ws_pairs = w_s.reshape(N_SG_HEADS // 2, 2 * CHUNK, CHUNK).astype(bf16)
    bs_full = jnp.repeat(b_s.T, SG_HEAD_DIM, axis=1)
    pad = LANES - N_EXPERTS - N_EXPERT_GROUPS
    w_r = jnp.pad(jnp.concatenate([w_rf, w_rc], axis=1), ((0, 0), (0, pad))).astype(bf16)
    b_r = jnp.pad(jnp.concatenate([b_rf, b_rc]), (0, pad)).reshape(1, LANES)
    return (w_in.astype(bf16), row(b_in), conv_w, row(conv_b), row(sg_ln_g), row(sg_ln_b),
            ws_pairs, bs_full, w_o.astype(bf16), row(b_o), row(ln1_g), row(ln1_b), w_r, b_r)


def _encoder_layer(x, mixer_params, expert_weights, ln2_g, ln2_b, alpha):
    n = x.shape[0] * x.shape[1]
    x1, xp, slab, route, carry = _mixer(x, mixer_params, alpha)

    tm = EXPERT_TILE
    n_tiles = 2 * n // tm + N_EXPERTS
    cnt = carry[0, 0:N_EXPERTS].astype(jnp.int32)
    tiles_e = jnp.maximum((cnt + tm - 1) // tm, 1 if len(expert_weights) == 3 else 0)
    experts = jnp.arange(N_EXPERTS, dtype=jnp.int32)[:, None]
    up_to = experts.T <= experts
    tile_end = jnp.sum(jnp.where(up_to, tiles_e[None, :], 0), axis=1)
    tile_start = tile_end - tiles_e
    row_off = tile_start * tm

    def sorted_pos(e_row, r_row):
        e = route[e_row].astype(jnp.int32)
        return jnp.sum(jnp.where(e[None, :] == experts, row_off[:, None], 0), axis=0) + route[r_row].astype(jnp.int32)

    pos1 = sorted_pos(COL_E1, COL_R1)
    pos2 = sorted_pos(COL_E2, COL_R2)
    tile_ids = jnp.arange(n_tiles, dtype=jnp.int32)
    n_used = tile_end[N_EXPERTS - 1:N_EXPERTS]
    tile_expert = jnp.sum((tile_end[None, :] <= tile_ids[:, None]).astype(jnp.int32), axis=1)
    last_used = jnp.sum((tile_end <= n_used[0] - 1).astype(jnp.int32))
    tile_expert = jnp.minimum(tile_expert, last_used)
    tile_first = (jnp.any(tile_ids[:, None] == tile_start[None, :], axis=1) & (tile_ids < n_used[0])) | (tile_ids == 0)
    tile_first = tile_first.astype(jnp.int32)
    rows_left = jnp.sum(jnp.where(tile_expert[:, None] == experts.T, (row_off + cnt)[None, :], 0), axis=1) - tile_ids * tm
    tile_nsub = jnp.where(tile_ids < n_used[0], (jnp.clip(rows_left, 0, tm) + EXPERT_SUBTILE - 1) // EXPERT_SUBTILE, 0)

    used = tiles_e > 0
    ordinal = jnp.sum((up_to & used[None, :]).astype(jnp.int32), axis=1) - 1
    e_ids = experts[:, 0]
    later_used = used[None, :] & (e_ids[None, :] > e_ids[:, None])
    next_used = jnp.min(jnp.where(later_used, e_ids[None, :], N_EXPERTS), axis=1)
    next_used = jnp.where(next_used < N_EXPERTS, next_used, -1)
    of_tile = tile_expert[:, None] == experts.T
    tile_slot = jnp.sum(jnp.where(of_tile, ordinal[None, :], 0), axis=1) % 2
    tile_next = jnp.sum(jnp.where(of_tile, next_used[None, :], 0), axis=1)

    tiles = (tile_expert, tile_first, tile_nsub.astype(jnp.int32), tile_slot.astype(jnp.int32),
             tile_next.astype(jnp.int32), n_used)
    xs = _sc_dispatch(pos1, pos2, xp, n_tiles * tm)
    if len(expert_weights) == 3:
        ys, *expert_weights = _experts(tiles, xs, expert_weights)
    else:
        ys = _experts(tiles, xs, expert_weights)
    yg = _sc_gather(pos1, pos2, ys)
    return _combine(x1, slab, yg, _row(ln2_g), _row(ln2_b), alpha).reshape(x.shape), tuple(expert_weights)


def kernel(x_prompt, x_sample, w_in, b_in, conv_w, conv_b, sg_ln_g, sg_ln_b, w_s, b_s, w_o, b_o, ln1_g, ln1_b, w_rc, b_rc, w_rf, b_rf, w_gate, w_up, w_down, ln2_g, ln2_b):
    depth = w_in.shape[0]
    alpha = (2.0 * depth) ** 0.25
    xs = (x_prompt, x_sample)
    for l in range(depth):
        mixer_params = _mixer_params(w_in[l], b_in[l], conv_w[l], conv_b[l], sg_ln_g[l], sg_ln_b[l], w_s[l], b_s[l],
                                     w_o[l], b_o[l], ln1_g[l], ln1_b[l], w_rc[l], b_rc[l], w_rf[l], b_rf[l])
        expert_weights = (w_gate[l], w_up[l], w_down[l])
        outs = [None] * len(xs)
        for k in sorted(range(len(xs)), key=lambda k: -xs[k].shape[0] * xs[k].shape[1]):
            outs[k], expert_weights = _encoder_layer(xs[k], mixer_params, expert_weights, ln2_g[l], ln2_b[l], alpha)
        xs = tuple(outs)
    return xs
```

```python
import functools

import jax
import jax.numpy as jnp
from jax import lax
from jax.experimental import pallas as pl
from jax.experimental.pallas import tpu as pltpu
from jax.experimental.pallas import tpu_sc as plsc

D_MODEL = 1024
D_CONV = 512
D_SG = 512
N_SG_HEADS = 8
SG_HEAD_DIM = D_SG // N_SG_HEADS
CHUNK = 128
N_EXPERT_GROUPS = 4
EXPERTS_PER_GROUP = 8
N_EXPERTS = N_EXPERT_GROUPS * EXPERTS_PER_GROUP
D_EXPERT = 512
LN_EPS = 1e-5
D_PACK = D_MODEL // 2

LANES = 128
SUBLANES = 8
HALO = 16
SEQ_TILE = 512
EXPERT_TILE = 1024
EXPERT_SUBTILE = 256
TOKEN_TILE = 1024
SC_CORES = 2
SC_SUBCORES = 16
SC_WORKERS = SC_CORES * SC_SUBCORES
SC_ROWS = 128
VMEM_LIMIT_BYTES = 56 * 1024 * 1024

COL_E1, COL_E2, COL_W1, COL_W2, COL_R1, COL_R2 = 0, 1, 2, 3, 4, 5
COARSE_OFF = N_EXPERTS


def _dot(a, b):
    return jnp.dot(a, b, preferred_element_type=jnp.float32)


def _gelu_tanh(x):
    return 0.5 * x * (1.0 + jnp.tanh(0.7978845608028654 * (x + 0.044715 * (x * x * x))))


def _layer_norm(x, g, b):
    mu = jnp.mean(x, axis=-1, keepdims=True)
    xc = x - mu
    var = jnp.mean(xc * xc, axis=-1, keepdims=True)
    return xc * lax.rsqrt(var + LN_EPS) * g + b


def _pack_halves(x):
    u32 = jnp.uint32
    lo = lax.bitcast_convert_type(x[:, 0:D_PACK].astype(jnp.bfloat16).astype(jnp.float32), u32)
    hi = lax.bitcast_convert_type(x[:, D_PACK:D_MODEL].astype(jnp.bfloat16).astype(jnp.float32), u32)
    return (hi & u32(0xFFFF0000)) | (lo >> u32(16))


def _unpack_halves(w):
    u32 = jnp.uint32
    lo = lax.bitcast_convert_type(w << u32(16), jnp.float32)
    hi = lax.bitcast_convert_type(w & u32(0xFFFF0000), jnp.float32)
    return lo, hi


def _mixer_kernel(x_ref, x_prev_ref, x_next_ref, w_in_ref, b_in_ref, conv_w_ref, conv_b_ref,
                  sg_g_ref, sg_b_ref, ws_ref, bs_ref, w_o_ref, b_o_ref, ln1_g_ref, ln1_b_ref,
                  w_r_ref, b_r_ref,
                  x1_ref, xp_ref, slab_ref, route_ref, counts_ref,
                  xb_ref, ymix_ref, tri_ref, resid_ref, carry_ref, *, ts, nt, n_tiles, alpha):
    t = pl.program_id(0)
    j = jnp.minimum(t, n_tiles - 1) % nt
    bf16 = jnp.bfloat16

    @pl.when(t == 0)
    def _():
        carry_ref[...] = jnp.zeros_like(carry_ref)
        resid_ref[...] = jnp.zeros_like(resid_ref)
        ri = lax.broadcasted_iota(jnp.int32, (ts, ts), 0)
        ci = lax.broadcasted_iota(jnp.int32, (ts, ts), 1)
        tri_ref[...] = jnp.where(ri > ci, 1.0, 0.0).astype(bf16)

    lane = lax.broadcasted_iota(jnp.int32, (ts, LANES), 1)
    lane_f = lane.astype(jnp.float32)
    neg = jnp.float32(-jnp.inf)
    big = jnp.float32(1e9)
    is_c = (lane >= COARSE_OFF) & (lane < COARSE_OFF + N_EXPERT_GROUPS)

    def proj(lhs, lo, hi):
        return _dot(lhs, w_in_ref[:, lo:hi]) + b_in_ref[:, lo:hi]

    def back_norm():
        x1 = _layer_norm(resid_ref[...], ln1_g_ref[...], ln1_b_ref[...])
        x1_ref[...] = x1
        xp_ref[...] = _pack_halves(x1)
        return x1.astype(bf16)

    def back_route(x1b):
        logits = _dot(x1b, w_r_ref[...]) + b_r_ref[...]
        lc = jnp.where(is_c, logits, neg)
        mx = jnp.max(lc, axis=1, keepdims=True)
        grp = jnp.min(jnp.where(lc == mx, lane_f - COARSE_OFF, big), axis=1, keepdims=True)
        p_grp = 1.0 / jnp.sum(jnp.where(is_c, jnp.exp(logits - mx), 0.0), axis=1, keepdims=True)
        grp_lo = grp * EXPERTS_PER_GROUP
        in_grp = (lane_f >= grp_lo) & (lane_f < grp_lo + EXPERTS_PER_GROUP)
        lf = jnp.where(in_grp, logits, neg)
        v1 = jnp.max(lf, axis=1, keepdims=True)
        e1 = jnp.min(jnp.where(lf == v1, lane_f, big), axis=1, keepdims=True)
        lf2 = jnp.where(lane_f == e1, neg, lf)
        v2 = jnp.max(lf2, axis=1, keepdims=True)
        e2 = jnp.min(jnp.where(lf2 == v2, lane_f, big), axis=1, keepdims=True)
        a = jnp.exp(v2 - v1)
        return e1, e2, p_grp / (1.0 + a), p_grp * a / (1.0 + a)

    def back_rank(e1, e2, w1, w2):
        hit1 = lane_f == e1
        hit2 = lane_f == e2
        onehot = jnp.where(hit1 | hit2, 1.0, 0.0)
        carry = carry_ref[...]
        before = _dot(tri_ref[...], onehot.astype(bf16)) + carry
        r1 = jnp.sum(jnp.where(hit1, before, 0.0), axis=1, keepdims=True)
        r2 = jnp.sum(jnp.where(hit2, before, 0.0), axis=1, keepdims=True)
        carry = jnp.where(t > 0, carry + jnp.sum(onehot, axis=0, keepdims=True), carry)
        carry_ref[...] = carry
        counts_ref[...] = jnp.broadcast_to(carry, counts_ref.shape)
        slab = jnp.where(lane == COL_E1, e1, 0.0)
        slab = jnp.where(lane == COL_E2, e2, slab)
        slab = jnp.where(lane == COL_W1, w1, slab)
        slab = jnp.where(lane == COL_W2, w2, slab)
        slab = jnp.where(lane == COL_R1, r1, slab)
        slab = jnp.where(lane == COL_R2, r2, slab)
        slab_ref[...] = slab
        route_ref[...] = slab.T[0:SUBLANES, :]

    def front_load():
        xb_ref[0:HALO, :] = x_prev_ref[...].astype(bf16)
        xb_ref[HALO:HALO + ts, :] = x_ref[...].astype(bf16)
        xb_ref[HALO + ts:HALO + ts + HALO, :] = x_next_ref[...].astype(bf16)

    def front_conv():
        xe = xb_ref[...]
        g_e = proj(xe, 0, D_CONV) * proj(xe, 2 * D_CONV, 3 * D_CONV)
        row_e = lax.broadcasted_iota(jnp.int32, (ts + 2 * HALO, 1), 0)
        has_prev = jnp.where(j > 0, 1.0, 0.0)
        has_next = jnp.where(j < nt - 1, 1.0, 0.0)
        g_e = g_e * jnp.where(row_e < HALO, has_prev, jnp.where(row_e >= HALO + ts, has_next, 1.0))
        conv = (g_e[HALO - 1:HALO - 1 + ts, :] * conv_w_ref[0:1, :] + g_e[HALO:HALO + ts, :] * conv_w_ref[1:2, :]
                + g_e[HALO + 1:HALO + 1 + ts, :] * conv_w_ref[2:3, :] + conv_b_ref[...])
        y_a = proj(xb_ref[HALO:HALO + ts, :], D_CONV, 2 * D_CONV) * conv
        ymix_ref[:, 0:D_CONV] = y_a.astype(bf16)

    def front_gate_proj():
        xm = xb_ref[HALO:HALO + ts, :]
        return proj(xm, 3 * D_CONV, 3 * D_CONV + D_SG), proj(xm, 3 * D_CONV + D_SG, 3 * D_CONV + 2 * D_SG)

    def front_gate(u_pre, v_pre):
        u = _gelu_tanh(u_pre)
        v = _gelu_tanh(v_pre)
        v_ln = _layer_norm(v, sg_g_ref[...], sg_b_ref[...]).astype(bf16)
        first_head = lax.broadcasted_iota(jnp.int32, (CHUNK, LANES), 1) < SG_HEAD_DIM
        for q0 in range(0, ts, 2 * CHUNK):
            q1 = q0 + CHUNK
            for hp in range(N_SG_HEADS // 2):
                c0 = hp * LANES
                rhs = jnp.concatenate([v_ln[q0:q0 + CHUNK, c0:c0 + LANES], v_ln[q1:q1 + CHUNK, c0:c0 + LANES]], axis=1)
                res = _dot(ws_ref[hp], rhs)
                bias = bs_ref[:, c0:c0 + LANES]
                m0 = jnp.where(first_head, res[0:CHUNK, 0:LANES], res[CHUNK:2 * CHUNK, 0:LANES]) + bias
                m1 = jnp.where(first_head, res[0:CHUNK, LANES:2 * LANES], res[CHUNK:2 * CHUNK, LANES:2 * LANES]) + bias
                ymix_ref[q0:q0 + CHUNK, D_CONV + c0:D_CONV + c0 + LANES] = (u[q0:q0 + CHUNK, c0:c0 + LANES] * m0).astype(bf16)
                ymix_ref[q1:q1 + CHUNK, D_CONV + c0:D_CONV + c0 + LANES] = (u[q1:q1 + CHUNK, c0:c0 + LANES] * m1).astype(bf16)

    def front_out_a():
        resid_ref[...] = alpha * x_ref[...] + b_o_ref[...] + _dot(ymix_ref[:, 0:D_CONV], w_o_ref[0:D_CONV, :])

    def front_out_b():
        resid_ref[...] += _dot(ymix_ref[:, D_CONV:D_MODEL], w_o_ref[D_CONV:D_MODEL, :])

    front_load()
    x1b = back_norm()
    routing = back_route(x1b)
    front_conv()
    back_rank(*routing)
    gate_pre = front_gate_proj()
    front_out_a()
    front_gate(*gate_pre)
    front_out_b()


def _mixer(x, p, alpha):
    nb, s, d = x.shape
    ts = SEQ_TILE
    nt = s // ts
    hb = ts // HALO
    n_tiles = nb * nt
    n_total = n_tiles * ts

    front = lambda t: jnp.minimum(t, n_tiles - 1)
    back = lambda t: jnp.maximum(t - 1, 0)
    cur = lambda t: (front(t) // nt, front(t) % nt, 0)
    prv = lambda t: (front(t) // nt, jnp.maximum((front(t) % nt) * hb - 1, 0), 0)
    nxt = lambda t: (front(t) // nt, jnp.minimum((front(t) % nt + 1) * hb, s // HALO - 1), 0)
    full = lambda a: pl.BlockSpec(a.shape, lambda t: (0,) * a.ndim)
    in_specs = [pl.BlockSpec((None, ts, d), cur), pl.BlockSpec((None, HALO, d), prv),
                pl.BlockSpec((None, HALO, d), nxt)] + [full(a) for a in p]
    out_shape = (jax.ShapeDtypeStruct((n_total, d), jnp.float32),
                 jax.ShapeDtypeStruct((n_total, D_PACK), jnp.uint32),
                 jax.ShapeDtypeStruct((n_total, LANES), jnp.float32),
                 jax.ShapeDtypeStruct((SUBLANES, n_total), jnp.float32),
                 jax.ShapeDtypeStruct((SUBLANES, LANES), jnp.float32))
    row_blk = lambda t: (back(t), 0)
    out_specs = (pl.BlockSpec((ts, d), row_blk),
                 pl.BlockSpec((ts, D_PACK), row_blk),
                 pl.BlockSpec((ts, LANES), row_blk),
                 pl.BlockSpec((SUBLANES, ts), lambda t: (0, back(t))),
                 pl.BlockSpec((SUBLANES, LANES), lambda t: (0, 0)))
    return pl.pallas_call(
        functools.partial(_mixer_kernel, ts=ts, nt=nt, n_tiles=n_tiles, alpha=alpha),
        grid=(n_tiles + 1,),
        in_specs=in_specs,
        out_specs=out_specs,
        out_shape=out_shape,
        scratch_shapes=[pltpu.VMEM((ts + 2 * HALO, d), jnp.bfloat16),
                        pltpu.VMEM((ts, d), jnp.bfloat16),
                        pltpu.VMEM((ts, ts), jnp.bfloat16),
                        pltpu.VMEM((ts, d), jnp.float32),
                        pltpu.VMEM((1, LANES), jnp.float32)],
        compiler_params=pltpu.CompilerParams(dimension_semantics=("arbitrary",),
                                             vmem_limit_bytes=VMEM_LIMIT_BYTES),
        cost_estimate=pl.CostEstimate(
            flops=2 * n_total * (d * (3 * D_CONV + 2 * D_SG) + CHUNK * D_SG + d * d + d * LANES + ts * LANES),
            transcendentals=n_total * (2 * D_SG + 2 * LANES),
            bytes_accessed=n_total * (2 * d * 4 + D_PACK * 4 + LANES * 4 + SUBLANES * 4)),
        name="mixer",
    )(x, x, x, *p)


def _sc_mesh():
    return plsc.VectorSubcoreMesh(core_axis_name="c", subcore_axis_name="s",
                                  num_cores=SC_CORES, num_subcores=SC_SUBCORES)


def _sc_worker_base(per_worker):
    return (lax.axis_index("s") * SC_CORES + lax.axis_index("c")) * per_worker


def _sc_dispatch(pos1, pos2, xp, n_rows):
    n, dp = xp.shape
    per_worker = n // SC_WORKERS
    steps = per_worker // SC_ROWS

    def body(pos1_hbm, pos2_hbm, xp_hbm, xs_hbm, idx1_v, idx2_v, rows_v, sem):
        base = _sc_worker_base(per_worker)

        @pl.loop(0, steps)
        def _(k):
            off = pl.multiple_of(base + k * SC_ROWS, SC_ROWS)
            pltpu.sync_copy(pos1_hbm.at[pl.ds(off, SC_ROWS)], idx1_v)
            pltpu.sync_copy(pos2_hbm.at[pl.ds(off, SC_ROWS)], idx2_v)
            pltpu.sync_copy(xp_hbm.at[pl.ds(off, SC_ROWS)], rows_v)
            c1 = pltpu.async_copy(rows_v, xs_hbm.at[idx1_v], sem)
            c2 = pltpu.async_copy(rows_v, xs_hbm.at[idx2_v], sem)
            c1.wait()
            c2.wait()

    return pl.kernel(
        body,
        out_type=jax.ShapeDtypeStruct((n_rows, dp), xp.dtype),
        mesh=_sc_mesh(),
        scratch_types=[pltpu.VMEM((SC_ROWS,), jnp.int32), pltpu.VMEM((SC_ROWS,), jnp.int32),
                       pltpu.VMEM((SC_ROWS, dp), xp.dtype), pltpu.SemaphoreType.DMA],
        compiler_params=pltpu.CompilerParams(use_tc_tiling_on_sc=True),
        cost_estimate=pl.CostEstimate(flops=0, transcendentals=0, bytes_accessed=3 * n * dp * 4 + 2 * n * 4),
        name="sc_dispatch",
    )(pos1, pos2, xp)


def _sc_gather(pos1, pos2, ys):
    n = pos1.shape[0]
    dp = ys.shape[1]
    per_worker = n // SC_WORKERS
    steps = per_worker // SC_ROWS

    def body(pos1_hbm, pos2_hbm, ys_hbm, out_hbm, idx_v, rows_v, sem):
        base = _sc_worker_base(per_worker)

        @pl.loop(0, steps)
        def _(k):
            off = pl.multiple_of(base + k * SC_ROWS, SC_ROWS)
            for slot, pos_hbm in enumerate((pos1_hbm, pos2_hbm)):
                pltpu.sync_copy(pos_hbm.at[pl.ds(off, SC_ROWS)], idx_v)
                pltpu.async_copy(ys_hbm.at[idx_v], rows_v, sem).wait()
                pltpu.sync_copy(rows_v, out_hbm.at[slot, pl.ds(off, SC_ROWS)])

    return pl.kernel(
        body,
        out_type=jax.ShapeDtypeStruct((2, n, dp), ys.dtype),
        mesh=_sc_mesh(),
        scratch_types=[pltpu.VMEM((SC_ROWS,), jnp.int32), pltpu.VMEM((SC_ROWS, dp), ys.dtype),
                       pltpu.SemaphoreType.DMA],
        compiler_params=pltpu.CompilerParams(use_tc_tiling_on_sc=True),
        cost_estimate=pl.CostEstimate(flops=0, transcendentals=0, bytes_accessed=4 * n * dp * 4 + 2 * n * 4),
        name="sc_gather",
    )(pos1, pos2, ys)


def _expert_kernel(te_ref, first_ref, nsub_ref, slot_ref, next_ref, nused_ref, xs_ref, *refs, cast_weights):
    i = pl.program_id(0)
    bf16 = jnp.bfloat16
    sub = EXPERT_SUBTILE
    slot = slot_ref[i]

    if cast_weights:
        (wg_hbm, wu_hbm, wd_hbm, ys_ref, wgu_out, wd_out,
         wg_buf, wu_buf, wd_buf, wgu_bf_ref, wd_bf_ref, sem, out_sem) = refs
        wgu_ref, wd_ref = wgu_bf_ref, wd_bf_ref

        def weight_copies(expert, s):
            return (pltpu.make_async_copy(wg_hbm.at[expert], wg_buf.at[s], sem.at[s]),
                    pltpu.make_async_copy(wu_hbm.at[expert], wu_buf.at[s], sem.at[s]),
                    pltpu.make_async_copy(wd_hbm.at[expert], wd_buf.at[s], sem.at[s]))

        def emit_copies(expert):
            return (pltpu.make_async_copy(wgu_bf_ref, wgu_out.at[expert], out_sem),
                    pltpu.make_async_copy(wd_bf_ref, wd_out.at[expert], out_sem))
    else:
        wgu_hbm, wd_hbm, ys_ref, wgu_buf, wd_buf, sem = refs
        wgu_ref, wd_ref = wgu_buf.at[slot], wd_buf.at[slot]

        def weight_copies(expert, s):
            return (pltpu.make_async_copy(wgu_hbm.at[expert], wgu_buf.at[s], sem.at[s]),
                    pltpu.make_async_copy(wd_hbm.at[expert], wd_buf.at[s], sem.at[s]))

    @pl.when(first_ref[i] == 1)
    def _():
        @pl.when(i == 0)
        def _():
            for c in weight_copies(te_ref[i], slot):
                c.start()

        for c in weight_copies(te_ref[i], slot):
            c.wait()

        @pl.when(next_ref[i] >= 0)
        def _():
            for c in weight_copies(next_ref[i], 1 - slot):
                c.start()

        if cast_weights:
            @pl.when(i > 0)
            def _():
                for c in emit_copies(te_ref[i]):
                    c.wait()

            wgu_bf_ref[:, 0:D_EXPERT] = wg_buf[slot].astype(bf16)
            wgu_bf_ref[:, D_EXPERT:2 * D_EXPERT] = wu_buf[slot].astype(bf16)
            wd_bf_ref[...] = wd_buf[slot].astype(bf16)
            for c in emit_copies(te_ref[i]):
                c.start()

    def swiglu(r0):
        rows = pl.ds(r0, sub)
        lo, hi = _unpack_halves(xs_ref[rows, :])
        gu = (_dot(lo.astype(bf16), wgu_ref[0:D_PACK, :])
              + _dot(hi.astype(bf16), wgu_ref[D_PACK:D_MODEL, :]))
        g = gu[:, 0:D_EXPERT]
        h = g * jax.nn.sigmoid(g) * gu[:, D_EXPERT:2 * D_EXPERT]
        ys_ref[rows, :] = _pack_halves(_dot(h.astype(bf16), wd_ref[...]))

    n_sub = nsub_ref[i]

    def pair(k, c):
        r0 = pl.multiple_of(k * (2 * sub), 2 * sub)
        swiglu(r0)
        swiglu(r0 + sub)
        return c

    lax.fori_loop(0, n_sub // 2, pair, 0)

    @pl.when(n_sub % 2 == 1)
    def _():
        swiglu(pl.multiple_of((n_sub - 1) * sub, sub))

    def clear(k, c):
        ys_ref[pl.ds(pl.multiple_of(k * sub, sub), sub), :] = jnp.zeros((sub, D_PACK), ys_ref.dtype)
        return c

    lax.fori_loop(n_sub, jnp.where(i < nused_ref[0], ys_ref.shape[0] // sub, 0), clear, 0)

    if cast_weights:
        @pl.when(i == pl.num_programs(0) - 1)
        def _():
            for c in emit_copies(te_ref[i]):
                c.wait()


def _experts(tiles, xs, weights):
    n_rows, dp = xs.shape
    d = D_MODEL
    tm = EXPERT_TILE
    cast_weights = len(weights) == 3
    row_blk = lambda i, te, fi, ns, sl, nx, nu: (jnp.minimum(i, nu[0] - 1), 0)
    any_spec = pl.BlockSpec(memory_space=pl.ANY)
    ys_shape = jax.ShapeDtypeStruct((n_rows, dp), jnp.uint32)
    ys_spec = pl.BlockSpec((tm, dp), row_blk)
    wgu_bf = ((d, 2 * D_EXPERT), jnp.bfloat16)
    wd_bf = ((D_EXPERT, d), jnp.bfloat16)
    if cast_weights:
        out_shape = (ys_shape, jax.ShapeDtypeStruct((N_EXPERTS,) + wgu_bf[0], wgu_bf[1]),
                     jax.ShapeDtypeStruct((N_EXPERTS,) + wd_bf[0], wd_bf[1]))
        out_specs = (ys_spec, any_spec, any_spec)
        scratch = [pltpu.VMEM((2, d, D_EXPERT), jnp.float32), pltpu.VMEM((2, d, D_EXPERT), jnp.float32),
                   pltpu.VMEM((2, D_EXPERT, d), jnp.float32), pltpu.VMEM(*wgu_bf), pltpu.VMEM(*wd_bf),
                   pltpu.SemaphoreType.DMA((2,)), pltpu.SemaphoreType.DMA]
        weight_bytes = N_EXPERTS * 3 * d * D_EXPERT * (4 + 2)
    else:
        out_shape, out_specs = ys_shape, ys_spec
        scratch = [pltpu.VMEM((2,) + wgu_bf[0], wgu_bf[1]), pltpu.VMEM((2,) + wd_bf[0], wd_bf[1]),
                   pltpu.SemaphoreType.DMA((2,))]
        weight_bytes = N_EXPERTS * 3 * d * D_EXPERT * 2
    grid_spec = pltpu.PrefetchScalarGridSpec(
        num_scalar_prefetch=6,
        grid=(n_rows // tm,),
        in_specs=[pl.BlockSpec((tm, dp), row_blk)] + [any_spec] * len(weights),
        out_specs=out_specs,
        scratch_shapes=scratch,
    )
    return pl.pallas_call(
        functools.partial(_expert_kernel, cast_weights=cast_weights),
        grid_spec=grid_spec,
        out_shape=out_shape,
        compiler_params=pltpu.CompilerParams(dimension_semantics=("arbitrary",),
                                             vmem_limit_bytes=VMEM_LIMIT_BYTES),
        cost_estimate=pl.CostEstimate(flops=2 * n_rows * 3 * d * D_EXPERT, transcendentals=n_rows * D_EXPERT,
                                      bytes_accessed=2 * n_rows * dp * 4 + weight_bytes),
        name="experts",
    )(*tiles, xs, *weights)


def _combine_kernel(x1_ref, slab_ref, y1_ref, y2_ref, g_ref, b_ref, out_ref, *, alpha):
    slab = slab_ref[...]
    w1 = slab[:, COL_W1:COL_W1 + 1]
    w2 = slab[:, COL_W2:COL_W2 + 1]
    lo1, hi1 = _unpack_halves(y1_ref[...])
    lo2, hi2 = _unpack_halves(y2_ref[...])
    moe = jnp.concatenate([w1 * lo1 + w2 * lo2, w1 * hi1 + w2 * hi2], axis=1)
    out_ref[...] = _layer_norm(alpha * x1_ref[...] + moe, g_ref[...], b_ref[...])


def _combine(x1, slab, yg, g, b, alpha):
    n, d = x1.shape
    tt = TOKEN_TILE
    return pl.pallas_call(
        functools.partial(_combine_kernel, alpha=alpha),
        grid=(n // tt,),
        in_specs=[pl.BlockSpec((tt, d), lambda i: (i, 0)),
                  pl.BlockSpec((tt, LANES), lambda i: (i, 0)),
                  pl.BlockSpec((None, tt, D_PACK), lambda i: (0, i, 0)),
                  pl.BlockSpec((None, tt, D_PACK), lambda i: (1, i, 0)),
                  pl.BlockSpec((1, d), lambda i: (0, 0)),
                  pl.BlockSpec((1, d), lambda i: (0, 0))],
        out_specs=pl.BlockSpec((tt, d), lambda i: (i, 0)),
        out_shape=jax.ShapeDtypeStruct((n, d), jnp.float32),
        compiler_params=pltpu.CompilerParams(dimension_semantics=("arbitrary",),
                                             vmem_limit_bytes=VMEM_LIMIT_BYTES),
        cost_estimate=pl.CostEstimate(flops=12 * n * d, transcendentals=n,
                                      bytes_accessed=n * (2 * d * 4 + 2 * D_PACK * 4 + LANES * 4)),
        name="combine",
    )(x1, slab, yg, yg, g, b)


def _row(a):
    return a.reshape(1, -1)


def _mixer_params(w_in, b_in, conv_w, conv_b, sg_ln_g, sg_ln_b, w_s, b_s, w_o, b_o, ln1_g, ln1_b,
                  w_rc, b_rc, w_rf, b_rf):
    bf16 = jnp.bfloat16
    row = _row
    ws_pairs = w_s.reshape(N_SG_HEADS // 2, 2 * CHUNK, CHUNK).astype(bf16)
    bs_full = jnp.repeat(b_s.T, SG_HEAD_DIM, axis=1)
    pad = LANES - N_EXPERTS - N_EXPERT_GROUPS
    w_r = jnp.pad(jnp.concatenate([w_rf, w_rc], axis=1), ((0, 0), (0, pad))).astype(bf16)
    b_r = jnp.pad(jnp.concatenate([b_rf, b_rc]), (0, pad)).reshape(1, LANES)
    return (w_in.astype(bf16), row(b_in), conv_w, row(conv_b), row(sg_ln_g), row(sg_ln_b),
            ws_pairs, bs_full, w_o.astype(bf16), row(b_o), row(ln1_g), row(ln1_b), w_r, b_r)


def _encoder_layer(x, mixer_params, expert_weights, ln2_g, ln2_b, alpha):
    n = x.shape[0] * x.shape[1]
    x1, xp, slab, route, carry = _mixer(x, mixer_params, alpha)

    tm = EXPERT_TILE
    n_tiles = 2 * n // tm + N_EXPERTS
    cnt = carry[0, 0:N_EXPERTS].astype(jnp.int32)
    tiles_e = jnp.maximum((cnt + tm - 1) // tm, 1 if len(expert_weights) == 3 else 0)
    experts = jnp.arange(N_EXPERTS, dtype=jnp.int32)[:, None]
    up_to = experts.T <= experts
    tile_end = jnp.sum(jnp.where(up_to, tiles_e[None, :], 0), axis=1)
    tile_start = tile_end - tiles_e
    row_off = tile_start * tm

    def sorted_pos(e_row, r_row):
        e = route[e_row].astype(jnp.int32)
        return jnp.sum(jnp.where(e[None, :] == experts, row_off[:, None], 0), axis=0) + route[r_row].astype(jnp.int32)

    pos1 = sorted_pos(COL_E1, COL_R1)
    pos2 = sorted_pos(COL_E2, COL_R2)
    tile_ids = jnp.arange(n_tiles, dtype=jnp.int32)
    n_used = tile_end[N_EXPERTS - 1:N_EXPERTS]
    tile_expert = jnp.sum((tile_end[None, :] <= tile_ids[:, None]).astype(jnp.int32), axis=1)
    last_used = jnp.sum((tile_end <= n_used[0] - 1).astype(jnp.int32))
    tile_expert = jnp.minimum(tile_expert, last_used)
    tile_first = (jnp.any(tile_ids[:, None] == tile_start[None, :], axis=1) & (tile_ids < n_used[0])) | (tile_ids == 0)
    tile_first = tile_first.astype(jnp.int32)
    rows_left = jnp.sum(jnp.where(tile_expert[:, None] == experts.T, (row_off + cnt)[None, :], 0), axis=1) - tile_ids * tm
    tile_nsub = jnp.where(tile_ids < n_used[0], (jnp.clip(rows_left, 0, tm) + EXPERT_SUBTILE - 1) // EXPERT_SUBTILE, 0)

    used = tiles_e > 0
    ordinal = jnp.sum((up_to & used[None, :]).astype(jnp.int32), axis=1) - 1
    e_ids = experts[:, 0]
    later_used = used[None, :] & (e_ids[None, :] > e_ids[:, None])
    next_used = jnp.min(jnp.where(later_used, e_ids[None, :], N_EXPERTS), axis=1)
    next_used = jnp.where(next_used < N_EXPERTS, next_used, -1)
    of_tile = tile_expert[:, None] == experts.T
    tile_slot = jnp.sum(jnp.where(of_tile, ordinal[None, :], 0), axis=1) % 2
    tile_next = jnp.sum(jnp.where(of_tile, next_used[None, :], 0), axis=1)

    tiles = (tile_expert, tile_first, tile_nsub.astype(jnp.int32), tile_slot.astype(jnp.int32),
             tile_next.astype(jnp.int32), n_used)
    xs = _sc_dispatch(pos1, pos2, xp, n_tiles * tm)
    if len(expert_weights) == 3:
        ys, *expert_weights = _experts(tiles, xs, expert_weights)
    else:
        ys = _experts(tiles, xs, expert_weights)
    yg = _sc_gather(pos1, pos2, ys)
    return _combine(x1, slab, yg, _row(ln2_g), _row(ln2_b), alpha).reshape(x.shape), tuple(expert_weights)


def kernel(x_prompt, x_sample, w_in, b_in, conv_w, conv_b, sg_ln_g, sg_ln_b, w_s, b_s, w_o, b_o, ln1_g, ln1_b, w_rc, b_rc, w_rf, b_rf, w_gate, w_up, w_down, ln2_g, ln2_b):
    depth = w_in.shape[0]
    alpha = (2.0 * depth) ** 0.25
    xs = (x_prompt, x_sample)
    for l in range(depth):
        mixer_params = _mixer_params(w_in[l], b_in[l], conv_w[l], conv_b[l], sg_ln_g[l], sg_ln_b[l], w_s[l], b_s[l],
                                     w_o[l], b_o[l], ln1_g[l], ln1_b[l], w_rc[l], b_rc[l], w_rf[l], b_rf[l])
        expert_weights = (w_gate[l], w_up[l], w_down[l])
        outs = [None] * len(xs)
        for k in sorted(range(len(xs)), key=lambda k: -xs[k].shape[0] * xs[k].shape[1]):
            outs[k], expert_weights = _encoder_layer(xs[k], mixer_params, expert_weights, ln2_g[l], ln2_b[l], alpha)
        xs = tuple(outs)
    return xs
```

```python
import functools

import jax
import jax.numpy as jnp
from jax import lax
from jax.experimental import pallas as pl
from jax.experimental.pallas import tpu as pltpu
from jax.experimental.pallas import tpu_sc as plsc

D_MODEL = 1024
D_CONV = 512
D_SG = 512
N_SG_HEADS = 8
SG_HEAD_DIM = D_SG // N_SG_HEADS
CHUNK = 128
N_EXPERT_GROUPS = 4
EXPERTS_PER_GROUP = 8
N_EXPERTS = N_EXPERT_GROUPS * EXPERTS_PER_GROUP
D_EXPERT = 512
LN_EPS = 1e-5
D_PACK = D_MODEL // 2

LANES = 128
SUBLANES = 8
HALO = 16
SEQ_TILE = 512
EXPERT_TILE = 1024
EXPERT_SUBTILE = 256
TOKEN_TILE = 1024
SC_CORES = 2
SC_SUBCORES = 16
SC_WORKERS = SC_CORES * SC_SUBCORES
SC_ROWS = 64
VMEM_LIMIT_BYTES = 56 * 1024 * 1024

COL_E1, COL_E2, COL_W1, COL_W2, COL_R1, COL_R2 = 0, 1, 2, 3, 4, 5
COARSE_OFF = N_EXPERTS


def _dot(a, b):
    return jnp.dot(a, b, preferred_element_type=jnp.float32)


def _gelu_tanh(x):
    return 0.5 * x * (1.0 + jnp.tanh(0.7978845608028654 * (x + 0.044715 * (x * x * x))))


def _layer_norm(x, g, b):
    mu = jnp.mean(x, axis=-1, keepdims=True)
    xc = x - mu
    var = jnp.mean(xc * xc, axis=-1, keepdims=True)
    return xc * lax.rsqrt(var + LN_EPS) * g + b


def _pack_halves(x):
    u32 = jnp.uint32
    lo = lax.bitcast_convert_type(x[:, 0:D_PACK].astype(jnp.bfloat16).astype(jnp.float32), u32)
    hi = lax.bitcast_convert_type(x[:, D_PACK:D_MODEL].astype(jnp.bfloat16).astype(jnp.float32), u32)
    return (hi & u32(0xFFFF0000)) | (lo >> u32(16))


def _unpack_halves(w):
    u32 = jnp.uint32
    lo = lax.bitcast_convert_type(w << u32(16), jnp.float32)
    hi = lax.bitcast_convert_type(w & u32(0xFFFF0000), jnp.float32)
    return lo, hi


def _mixer_kernel(x_ref, x_prev_ref, x_next_ref, w_in_ref, b_in_ref, conv_w_ref, conv_b_ref,
                  sg_g_ref, sg_b_ref, ws_ref, bs_ref, w_o_ref, b_o_ref, ln1_g_ref, ln1_b_ref,
                  w_r_ref, b_r_ref,
                  x1_ref, xp_ref, slab_ref, route_ref, counts_ref,
                  xb_ref, ymix_ref, tri_ref, resid_ref, carry_ref, *, ts, nt, n_tiles, alpha):
    t = pl.program_id(0)
    j = jnp.minimum(t, n_tiles - 1) % nt
    bf16 = jnp.bfloat16

    @pl.when(t == 0)
    def _():
        carry_ref[...] = jnp.zeros_like(carry_ref)
        resid_ref[...] = jnp.zeros_like(resid_ref)
        ri = lax.broadcasted_iota(jnp.int32, (ts, ts), 0)
        ci = lax.broadcasted_iota(jnp.int32, (ts, ts), 1)
        tri_ref[...] = jnp.where(ri > ci, 1.0, 0.0).astype(bf16)

    lane = lax.broadcasted_iota(jnp.int32, (ts, LANES), 1)
    lane_f = lane.astype(jnp.float32)
    neg = jnp.float32(-jnp.inf)
    big = jnp.float32(1e9)
    is_c = (lane >= COARSE_OFF) & (lane < COARSE_OFF + N_EXPERT_GROUPS)

    def proj(lhs, lo, hi):
        return _dot(lhs, w_in_ref[:, lo:hi]) + b_in_ref[:, lo:hi]

    def back_norm():
        x1 = _layer_norm(resid_ref[...], ln1_g_ref[...], ln1_b_ref[...])
        x1_ref[...] = x1
        xp_ref[...] = _pack_halves(x1)
        return x1.astype(bf16)

    def back_route(x1b):
        logits = _dot(x1b, w_r_ref[...]) + b_r_ref[...]
        lc = jnp.where(is_c, logits, neg)
        mx = jnp.max(lc, axis=1, keepdims=True)
        grp = jnp.min(jnp.where(lc == mx, lane_f - COARSE_OFF, big), axis=1, keepdims=True)
        p_grp = 1.0 / jnp.sum(jnp.where(is_c, jnp.exp(logits - mx), 0.0), axis=1, keepdims=True)
        grp_lo = grp * EXPERTS_PER_GROUP
        in_grp = (lane_f >= grp_lo) & (lane_f < grp_lo + EXPERTS_PER_GROUP)
        lf = jnp.where(in_grp, logits, neg)
        v1 = jnp.max(lf, axis=1, keepdims=True)
        e1 = jnp.min(jnp.where(lf == v1, lane_f, big), axis=1, keepdims=True)
        lf2 = jnp.where(lane_f == e1, neg, lf)
        v2 = jnp.max(lf2, axis=1, keepdims=True)
        e2 = jnp.min(jnp.where(lf2 == v2, lane_f, big), axis=1, keepdims=True)
        a = jnp.exp(v2 - v1)
        return e1, e2, p_grp / (1.0 + a), p_grp * a / (1.0 + a)

    def back_rank(e1, e2, w1, w2):
        hit1 = lane_f == e1
        hit2 = lane_f == e2
        onehot = jnp.where(hit1 | hit2, 1.0, 0.0)
        carry = carry_ref[...]
        before = _dot(tri_ref[...], onehot.astype(bf16)) + carry
        r1 = jnp.sum(jnp.where(hit1, before, 0.0), axis=1, keepdims=True)
        r2 = jnp.sum(jnp.where(hit2, before, 0.0), axis=1, keepdims=True)
        carry = jnp.where(t > 0, carry + jnp.sum(onehot, axis=0, keepdims=True), carry)
        carry_ref[...] = carry
        counts_ref[...] = jnp.broadcast_to(carry, counts_ref.shape)
        slab = jnp.where(lane == COL_E1, e1, 0.0)
        slab = jnp.where(lane == COL_E2, e2, slab)
        slab = jnp.where(lane == COL_W1, w1, slab)
        slab = jnp.where(lane == COL_W2, w2, slab)
        slab = jnp.where(lane == COL_R1, r1, slab)
        slab = jnp.where(lane == COL_R2, r2, slab)
        slab_ref[...] = slab
        route_ref[...] = slab.T[0:SUBLANES, :]

    def front_load():
        xb_ref[0:HALO, :] = x_prev_ref[...].astype(bf16)
        xb_ref[HALO:HALO + ts, :] = x_ref[...].astype(bf16)
        xb_ref[HALO + ts:HALO + ts + HALO, :] = x_next_ref[...].astype(bf16)

    def front_conv():
        xe = xb_ref[...]
        g_e = proj(xe, 0, D_CONV) * proj(xe, 2 * D_CONV, 3 * D_CONV)
        row_e = lax.broadcasted_iota(jnp.int32, (ts + 2 * HALO, 1), 0)
        has_prev = jnp.where(j > 0, 1.0, 0.0)
        has_next = jnp.where(j < nt - 1, 1.0, 0.0)
        g_e = g_e * jnp.where(row_e < HALO, has_prev, jnp.where(row_e >= HALO + ts, has_next, 1.0))
        conv = (g_e[HALO - 1:HALO - 1 + ts, :] * conv_w_ref[0:1, :] + g_e[HALO:HALO + ts, :] * conv_w_ref[1:2, :]
                + g_e[HALO + 1:HALO + 1 + ts, :] * conv_w_ref[2:3, :] + conv_b_ref[...])
        y_a = proj(xb_ref[HALO:HALO + ts, :], D_CONV, 2 * D_CONV) * conv
        ymix_ref[:, 0:D_CONV] = y_a.astype(bf16)

    def front_gate_proj():
        xm = xb_ref[HALO:HALO + ts, :]
        return proj(xm, 3 * D_CONV, 3 * D_CONV + D_SG), proj(xm, 3 * D_CONV + D_SG, 3 * D_CONV + 2 * D_SG)

    def front_gate(u_pre, v_pre):
        u = _gelu_tanh(u_pre)
        v = _gelu_tanh(v_pre)
        v_ln = _layer_norm(v, sg_g_ref[...], sg_b_ref[...]).astype(bf16)
        first_head = lax.broadcasted_iota(jnp.int32, (CHUNK, LANES), 1) < SG_HEAD_DIM
        for q0 in range(0, ts, 2 * CHUNK):
            q1 = q0 + CHUNK
            for hp in range(N_SG_HEADS // 2):
                c0 = hp * LANES
                rhs = jnp.concatenate([v_ln[q0:q0 + CHUNK, c0:c0 + LANES], v_ln[q1:q1 + CHUNK, c0:c0 + LANES]], axis=1)
                res = _dot(ws_ref[hp], rhs)
                bias = bs_ref[:, c0:c0 + LANES]
                m0 = jnp.where(first_head, res[0:CHUNK, 0:LANES], res[CHUNK:2 * CHUNK, 0:LANES]) + bias
                m1 = jnp.where(first_head, res[0:CHUNK, LANES:2 * LANES], res[CHUNK:2 * CHUNK, LANES:2 * LANES]) + bias
                ymix_ref[q0:q0 + CHUNK, D_CONV + c0:D_CONV + c0 + LANES] = (u[q0:q0 + CHUNK, c0:c0 + LANES] * m0).astype(bf16)
                ymix_ref[q1:q1 + CHUNK, D_CONV + c0:D_CONV + c0 + LANES] = (u[q1:q1 + CHUNK, c0:c0 + LANES] * m1).astype(bf16)

    def front_out_a():
        resid_ref[...] = alpha * x_ref[...] + b_o_ref[...] + _dot(ymix_ref[:, 0:D_CONV], w_o_ref[0:D_CONV, :])

    def front_out_b():
        resid_ref[...] += _dot(ymix_ref[:, D_CONV:D_MODEL], w_o_ref[D_CONV:D_MODEL, :])

    front_load()
    x1b = back_norm()
    routing = back_route(x1b)
    front_conv()
    back_rank(*routing)
    gate_pre = front_gate_proj()
    front_out_a()
    front_gate(*gate_pre)
    front_out_b()


def _mixer(x, p, alpha):
    nb, s, d = x.shape
    ts = SEQ_TILE
    nt = s // ts
    hb = ts // HALO
    n_tiles = nb * nt
    n_total = n_tiles * ts

    front = lambda t: jnp.minimum(t, n_tiles - 1)
    back = lambda t: jnp.maximum(t - 1, 0)
    cur = lambda t: (front(t) // nt, front(t) % nt, 0)
    prv = lambda t: (front(t) // nt, jnp.maximum((front(t) % nt) * hb - 1, 0), 0)
    nxt = lambda t: (front(t) // nt, jnp.minimum((front(t) % nt + 1) * hb, s // HALO - 1), 0)
    full = lambda a: pl.BlockSpec(a.shape, lambda t: (0,) * a.ndim)
    in_specs = [pl.BlockSpec((None, ts, d), cur), pl.BlockSpec((None, HALO, d), prv),
                pl.BlockSpec((None, HALO, d), nxt)] + [full(a) for a in p]
    out_shape = (jax.ShapeDtypeStruct((n_total, d), jnp.float32),
                 jax.ShapeDtypeStruct((n_total, D_PACK), jnp.uint32),
                 jax.ShapeDtypeStruct((n_total, LANES), jnp.float32),
                 jax.ShapeDtypeStruct((SUBLANES, n_total), jnp.float32),
                 jax.ShapeDtypeStruct((SUBLANES, LANES), jnp.float32))
    row_blk = lambda t: (back(t), 0)
    out_specs = (pl.BlockSpec((ts, d), row_blk),
                 pl.BlockSpec((ts, D_PACK), row_blk),
                 pl.BlockSpec((ts, LANES), row_blk),
                 pl.BlockSpec((SUBLANES, ts), lambda t: (0, back(t))),
                 pl.BlockSpec((SUBLANES, LANES), lambda t: (0, 0)))
    return pl.pallas_call(
        functools.partial(_mixer_kernel, ts=ts, nt=nt, n_tiles=n_tiles, alpha=alpha),
        grid=(n_tiles + 1,),
        in_specs=in_specs,
        out_specs=out_specs,
        out_shape=out_shape,
        scratch_shapes=[pltpu.VMEM((ts + 2 * HALO, d), jnp.bfloat16),
                        pltpu.VMEM((ts, d), jnp.bfloat16),
                        pltpu.VMEM((ts, ts), jnp.bfloat16),
                        pltpu.VMEM((ts, d), jnp.float32),
                        pltpu.VMEM((1, LANES), jnp.float32)],
        compiler_params=pltpu.CompilerParams(dimension_semantics=("arbitrary",),
                                             vmem_limit_bytes=VMEM_LIMIT_BYTES),
        cost_estimate=pl.CostEstimate(
            flops=2 * n_total * (d * (3 * D_CONV + 2 * D_SG) + CHUNK * D_SG + d * d + d * LANES + ts * LANES),
            transcendentals=n_total * (2 * D_SG + 2 * LANES),
            bytes_accessed=n_total * (2 * d * 4 + D_PACK * 4 + LANES * 4 + SUBLANES * 4)),
        name="mixer",
    )(x, x, x, *p)


def _sc_mesh():
    return plsc.VectorSubcoreMesh(core_axis_name="c", subcore_axis_name="s",
                                  num_cores=SC_CORES, num_subcores=SC_SUBCORES)


def _sc_worker_base(per_worker):
    return (lax.axis_index("s") * SC_CORES + lax.axis_index("c")) * per_worker


def _sc_dispatch(pos1, pos2, xp, n_rows):
    n, dp = xp.shape
    per_worker = n // SC_WORKERS
    steps = per_worker // SC_ROWS

    def body(pos1_hbm, pos2_hbm, xp_hbm, xs_hbm, idx1_v, idx2_v, rows_v, sem):
        base = _sc_worker_base(per_worker)

        @pl.loop(0, steps)
        def _(k):
            off = pl.multiple_of(base + k * SC_ROWS, SC_ROWS)
            pltpu.sync_copy(pos1_hbm.at[pl.ds(off, SC_ROWS)], idx1_v)
            pltpu.sync_copy(pos2_hbm.at[pl.ds(off, SC_ROWS)], idx2_v)
            pltpu.sync_copy(xp_hbm.at[pl.ds(off, SC_ROWS)], rows_v)
            c1 = pltpu.async_copy(rows_v, xs_hbm.at[idx1_v], sem)
            c2 = pltpu.async_copy(rows_v, xs_hbm.at[idx2_v], sem)
            c1.wait()
            c2.wait()

    return pl.kernel(
        body,
        out_type=jax.ShapeDtypeStruct((n_rows, dp), xp.dtype),
        mesh=_sc_mesh(),
        scratch_types=[pltpu.VMEM((SC_ROWS,), jnp.int32), pltpu.VMEM((SC_ROWS,), jnp.int32),
                       pltpu.VMEM((SC_ROWS, dp), xp.dtype), pltpu.SemaphoreType.DMA],
        compiler_params=pltpu.CompilerParams(use_tc_tiling_on_sc=True),
        cost_estimate=pl.CostEstimate(flops=0, transcendentals=0, bytes_accessed=3 * n * dp * 4 + 2 * n * 4),
        name="sc_dispatch",
    )(pos1, pos2, xp)


def _sc_gather(pos1, pos2, ys):
    n = pos1.shape[0]
    dp = ys.shape[1]
    per_worker = n // SC_WORKERS
    steps = per_worker // SC_ROWS

    def body(pos1_hbm, pos2_hbm, ys_hbm, out_hbm, idx_v, rows_v, sem):
        base = _sc_worker_base(per_worker)

        @pl.loop(0, steps)
        def _(k):
            off = pl.multiple_of(base + k * SC_ROWS, SC_ROWS)
            for slot, pos_hbm in enumerate((pos1_hbm, pos2_hbm)):
                pltpu.sync_copy(pos_hbm.at[pl.ds(off, SC_ROWS)], idx_v)
                pltpu.async_copy(ys_hbm.at[idx_v], rows_v, sem).wait()
                pltpu.sync_copy(rows_v, out_hbm.at[slot, pl.ds(off, SC_ROWS)])

    return pl.kernel(
        body,
        out_type=jax.ShapeDtypeStruct((2, n, dp), ys.dtype),
        mesh=_sc_mesh(),
        scratch_types=[pltpu.VMEM((SC_ROWS,), jnp.int32), pltpu.VMEM((SC_ROWS, dp), ys.dtype),
                       pltpu.SemaphoreType.DMA],
        compiler_params=pltpu.CompilerParams(use_tc_tiling_on_sc=True),
        cost_estimate=pl.CostEstimate(flops=0, transcendentals=0, bytes_accessed=4 * n * dp * 4 + 2 * n * 4),
        name="sc_gather",
    )(pos1, pos2, ys)


def _expert_kernel(te_ref, first_ref, nsub_ref, slot_ref, next_ref, nused_ref, xs_ref, *refs, cast_weights):
    i = pl.program_id(0)
    bf16 = jnp.bfloat16
    sub = EXPERT_SUBTILE
    slot = slot_ref[i]

    if cast_weights:
        (wg_hbm, wu_hbm, wd_hbm, ys_ref, wgu_out, wd_out,
         wg_buf, wu_buf, wd_buf, wgu_bf_ref, wd_bf_ref, sem, out_sem) = refs
        wgu_ref, wd_ref = wgu_bf_ref, wd_bf_ref

        def weight_copies(expert, s):
            return (pltpu.make_async_copy(wg_hbm.at[expert], wg_buf.at[s], sem.at[s]),
                    pltpu.make_async_copy(wu_hbm.at[expert], wu_buf.at[s], sem.at[s]),
                    pltpu.make_async_copy(wd_hbm.at[expert], wd_buf.at[s], sem.at[s]))

        def emit_copies(expert):
            return (pltpu.make_async_copy(wgu_bf_ref, wgu_out.at[expert], out_sem),
                    pltpu.make_async_copy(wd_bf_ref, wd_out.at[expert], out_sem))
    else:
        wgu_hbm, wd_hbm, ys_ref, wgu_buf, wd_buf, sem = refs
        wgu_ref, wd_ref = wgu_buf.at[slot], wd_buf.at[slot]

        def weight_copies(expert, s):
            return (pltpu.make_async_copy(wgu_hbm.at[expert], wgu_buf.at[s], sem.at[s]),
                    pltpu.make_async_copy(wd_hbm.at[expert], wd_buf.at[s], sem.at[s]))

    @pl.when(first_ref[i] == 1)
    def _():
        @pl.when(i == 0)
        def _():
            for c in weight_copies(te_ref[i], slot):
                c.start()

        for c in weight_copies(te_ref[i], slot):
            c.wait()

        @pl.when(next_ref[i] >= 0)
        def _():
            for c in weight_copies(next_ref[i], 1 - slot):
                c.start()

        if cast_weights:
            @pl.when(i > 0)
            def _():
                for c in emit_copies(te_ref[i]):
                    c.wait()

            wgu_bf_ref[:, 0:D_EXPERT] = wg_buf[slot].astype(bf16)
            wgu_bf_ref[:, D_EXPERT:2 * D_EXPERT] = wu_buf[slot].astype(bf16)
            wd_bf_ref[...] = wd_buf[slot].astype(bf16)
            for c in emit_copies(te_ref[i]):
                c.start()

    def swiglu(r0):
        rows = pl.ds(r0, sub)
        lo, hi = _unpack_halves(xs_ref[rows, :])
        gu = (_dot(lo.astype(bf16), wgu_ref[0:D_PACK, :])
              + _dot(hi.astype(bf16), wgu_ref[D_PACK:D_MODEL, :]))
        g = gu[:, 0:D_EXPERT]
        h = g * jax.nn.sigmoid(g) * gu[:, D_EXPERT:2 * D_EXPERT]
        ys_ref[rows, :] = _pack_halves(_dot(h.astype(bf16), wd_ref[...]))

    n_sub = nsub_ref[i]

    def pair(k, c):
        r0 = pl.multiple_of(k * (2 * sub), 2 * sub)
        swiglu(r0)
        swiglu(r0 + sub)
        return c

    lax.fori_loop(0, n_sub // 2, pair, 0)

    @pl.when(n_sub % 2 == 1)
    def _():
        swiglu(pl.multiple_of((n_sub - 1) * sub, sub))

    def clear(k, c):
        ys_ref[pl.ds(pl.multiple_of(k * sub, sub), sub), :] = jnp.zeros((sub, D_PACK), ys_ref.dtype)
        return c

    lax.fori_loop(n_sub, jnp.where(i < nused_ref[0], ys_ref.shape[0] // sub, 0), clear, 0)

    if cast_weights:
        @pl.when(i == pl.num_programs(0) - 1)
        def _():
            for c in emit_copies(te_ref[i]):
                c.wait()


def _experts(tiles, xs, weights):
    n_rows, dp = xs.shape
    d = D_MODEL
    tm = EXPERT_TILE
    cast_weights = len(weights) == 3
    row_blk = lambda i, te, fi, ns, sl, nx, nu: (jnp.minimum(i, nu[0] - 1), 0)
    any_spec = pl.BlockSpec(memory_space=pl.ANY)
    ys_shape = jax.ShapeDtypeStruct((n_rows, dp), jnp.uint32)
    ys_spec = pl.BlockSpec((tm, dp), row_blk)
    wgu_bf = ((d, 2 * D_EXPERT), jnp.bfloat16)
    wd_bf = ((D_EXPERT, d), jnp.bfloat16)
    if cast_weights:
        out_shape = (ys_shape, jax.ShapeDtypeStruct((N_EXPERTS,) + wgu_bf[0], wgu_bf[1]),
                     jax.ShapeDtypeStruct((N_EXPERTS,) + wd_bf[0], wd_bf[1]))
        out_specs = (ys_spec, any_spec, any_spec)
        scratch = [pltpu.VMEM((2, d, D_EXPERT), jnp.float32), pltpu.VMEM((2, d, D_EXPERT), jnp.float32),
                   pltpu.VMEM((2, D_EXPERT, d), jnp.float32), pltpu.VMEM(*wgu_bf), pltpu.VMEM(*wd_bf),
                   pltpu.SemaphoreType.DMA((2,)), pltpu.SemaphoreType.DMA]
        weight_bytes = N_EXPERTS * 3 * d * D_EXPERT * (4 + 2)
    else:
        out_shape, out_specs = ys_shape, ys_spec
        scratch = [pltpu.VMEM((2,) + wgu_bf[0], wgu_bf[1]), pltpu.VMEM((2,) + wd_bf[0], wd_bf[1]),
                   pltpu.SemaphoreType.DMA((2,))]
        weight_bytes = N_EXPERTS * 3 * d * D_EXPERT * 2
    grid_spec = pltpu.PrefetchScalarGridSpec(
        num_scalar_prefetch=6,
        grid=(n_rows // tm,),
        in_specs=[pl.BlockSpec((tm, dp), row_blk)] + [any_spec] * len(weights),
        out_specs=out_specs,
        scratch_shapes=scratch,
    )
    return pl.pallas_call(
        functools.partial(_expert_kernel, cast_weights=cast_weights),
        grid_spec=grid_spec,
        out_shape=out_shape,
        compiler_params=pltpu.CompilerParams(dimension_semantics=("arbitrary",),
                                             vmem_limit_bytes=VMEM_LIMIT_BYTES),
        cost_estimate=pl.CostEstimate(flops=2 * n_rows * 3 * d * D_EXPERT, transcendentals=n_rows * D_EXPERT,
                                      bytes_accessed=2 * n_rows * dp * 4 + weight_bytes),
        name="experts",
    )(*tiles, xs, *weights)


def _combine_kernel(x1_ref, slab_ref, y1_ref, y2_ref, g_ref, b_ref, out_ref, *, alpha):
    slab = slab_ref[...]
    w1 = slab[:, COL_W1:COL_W1 + 1]
    w2 = slab[:, COL_W2:COL_W2 + 1]
    lo1, hi1 = _unpack_halves(y1_ref[...])
    lo2, hi2 = _unpack_halves(y2_ref[...])
    moe = jnp.concatenate([w1 * lo1 + w2 * lo2, w1 * hi1 + w2 * hi2], axis=1)
    out_ref[...] = _layer_norm(alpha * x1_ref[...] + moe, g_ref[...], b_ref[...])


def _combine(x1, slab, yg, g, b, alpha):
    n, d = x1.shape
    tt = TOKEN_TILE
    return pl.pallas_call(
        functools.partial(_combine_kernel, alpha=alpha),
        grid=(n // tt,),
        in_specs=[pl.BlockSpec((tt, d), lambda i: (i, 0)),
                  pl.BlockSpec((tt, LANES), lambda i: (i, 0)),
                  pl.BlockSpec((None, tt, D_PACK), lambda i: (0, i, 0)),
                  pl.BlockSpec((None, tt, D_PACK), lambda i: (1, i, 0)),
                  pl.BlockSpec((1, d), lambda i: (0, 0)),
                  pl.BlockSpec((1, d), lambda i: (0, 0))],
        out_specs=pl.BlockSpec((tt, d), lambda i: (i, 0)),
        out_shape=jax.ShapeDtypeStruct((n, d), jnp.float32),
        compiler_params=pltpu.CompilerParams(dimension_semantics=("arbitrary",),
                                             vmem_limit_bytes=VMEM_LIMIT_BYTES),
        cost_estimate=pl.CostEstimate(flops=12 * n * d, transcendentals=n,
                                      bytes_accessed=n * (2 * d * 4 + 2 * D_PACK * 4 + LANES * 4)),
        name="combine",
    )(x1, slab, yg, yg, g, b)


def _row(a):
    return a.reshape(1, -1)


def _mixer_params(w_in, b_in, conv_w, conv_b, sg_ln_g, sg_ln_b, w_s, b_s, w_o, b_o, ln1_g, ln1_b,
                  w_rc, b_rc, w_rf, b_rf):
    bf16 = jnp.bfloat16
    ws_pairs = w_s.reshape(N_SG_HEADS // 2, 2 * CHUNK, CHUNK).astype(bf16)
    bs_full = jnp.repeat(b_s.T, SG_HEAD_DIM, axis=1)
    pad = LANES - N_EXPERTS - N_EXPERT_GROUPS
    w_r = jnp.pad(jnp.concatenate([w_rf, w_rc], axis=1), ((0, 0), (0, pad))).astype(bf16)
    b_r = jnp.pad(jnp.concatenate([b_rf, b_rc]), (0, pad)).reshape(1, LANES)
    return (w_in.astype(bf16), _row(b_in), conv_w, _row(conv_b), _row(sg_ln_g), _row(sg_ln_b),
            ws_pairs, bs_full, w_o.astype(bf16), _row(b_o), _row(ln1_g), _row(ln1_b), w_r, b_r)


def _encoder_layer(x, mixer_params, expert_weights, ln2_g, ln2_b, alpha):
    n = x.shape[0] * x.shape[1]
    x1, xp, slab, route, carry = _mixer(x, mixer_params, alpha)

    tm = EXPERT_TILE
    n_tiles = 2 * n // tm + N_EXPERTS
    cnt = carry[0, 0:N_EXPERTS].astype(jnp.int32)
    tiles_e = jnp.maximum((cnt + tm - 1) // tm, 1 if len(expert_weights) == 3 else 0)
    experts = jnp.arange(N_EXPERTS, dtype=jnp.int32)[:, None]
    up_to = experts.T <= experts
    tile_end = jnp.sum(jnp.where(up_to, tiles_e[None, :], 0), axis=1)
    tile_start = tile_end - tiles_e
    row_off = tile_start * tm

    def sorted_pos(e_row, r_row):
        e = route[e_row].astype(jnp.int32)
        return jnp.sum(jnp.where(e[None, :] == experts, row_off[:, None], 0), axis=0) + route[r_row].astype(jnp.int32)

    pos1 = sorted_pos(COL_E1, COL_R1)
    pos2 = sorted_pos(COL_E2, COL_R2)
    tile_ids = jnp.arange(n_tiles, dtype=jnp.int32)
    n_used = tile_end[N_EXPERTS - 1:N_EXPERTS]
    tile_expert = jnp.sum((tile_end[None, :] <= tile_ids[:, None]).astype(jnp.int32), axis=1)
    last_used = jnp.sum((tile_end <= n_used[0] - 1).astype(jnp.int32))
    tile_expert = jnp.minimum(tile_expert, last_used)
    tile_first = (jnp.any(tile_ids[:, None] == tile_start[None, :], axis=1) & (tile_ids < n_used[0])) | (tile_ids == 0)
    tile_first = tile_first.astype(jnp.int32)
    rows_left = jnp.sum(jnp.where(tile_expert[:, None] == experts.T, (row_off + cnt)[None, :], 0), axis=1) - tile_ids * tm
    tile_nsub = jnp.where(tile_ids < n_used[0], (jnp.clip(rows_left, 0, tm) + EXPERT_SUBTILE - 1) // EXPERT_SUBTILE, 0)

    used = tiles_e > 0
    ordinal = jnp.sum((up_to & used[None, :]).astype(jnp.int32), axis=1) - 1
    e_ids = experts[:, 0]
    later_used = used[None, :] & (e_ids[None, :] > e_ids[:, None])
    next_used = jnp.min(jnp.where(later_used, e_ids[None, :], N_EXPERTS), axis=1)
    next_used = jnp.where(next_used < N_EXPERTS, next_used, -1)
    of_tile = tile_expert[:, None] == experts.T
    tile_slot = jnp.sum(jnp.where(of_tile, ordinal[None, :], 0), axis=1) % 2
    tile_next = jnp.sum(jnp.where(of_tile, next_used[None, :], 0), axis=1)

    tiles = (tile_expert, tile_first, tile_nsub.astype(jnp.int32), tile_slot.astype(jnp.int32),
             tile_next.astype(jnp.int32), n_used)
    xs = _sc_dispatch(pos1, pos2, xp, n_tiles * tm)
    if len(expert_weights) == 3:
        ys, *expert_weights = _experts(tiles, xs, expert_weights)
    else:
        ys = _experts(tiles, xs, expert_weights)
    yg = _sc_gather(pos1, pos2, ys)
    return _combine(x1, slab, yg, _row(ln2_g), _row(ln2_b), alpha).reshape(x.shape), tuple(expert_weights)


def kernel(x_prompt, x_sample, w_in, b_in, conv_w, conv_b, sg_ln_g, sg_ln_b, w_s, b_s, w_o, b_o, ln1_g, ln1_b, w_rc, b_rc, w_rf, b_rf, w_gate, w_up, w_down, ln2_g, ln2_b):
    depth = w_in.shape[0]
    alpha = (2.0 * depth) ** 0.25
    xs = (x_prompt, x_sample)
    for l in range(depth):
        mixer_params = _mixer_params(w_in[l], b_in[l], conv_w[l], conv_b[l], sg_ln_g[l], sg_ln_b[l], w_s[l], b_s[l],
                                     w_o[l], b_o[l], ln1_g[l], ln1_b[l], w_rc[l], b_rc[l], w_rf[l], b_rf[l])
        expert_weights = (w_gate[l], w_up[l], w_down[l])
        outs = [None] * len(xs)
        for k in sorted(range(len(xs)), key=lambda k: -xs[k].shape[0] * xs[k].shape[1]):
            outs[k], expert_weights = _encoder_layer(xs[k], mixer_params, expert_weights, ln2_g[l], ln2_b[l], alpha)
        xs = tuple(outs)
    return xs
```

```python
import functools

import jax
import jax.numpy as jnp
from jax import lax
from jax.experimental import pallas as pl
from jax.experimental.pallas import tpu as pltpu
from jax.experimental.pallas import tpu_sc as plsc

D_MODEL = 1024
D_CONV = 512
D_SG = 512
N_SG_HEADS = 8
SG_HEAD_DIM = D_SG // N_SG_HEADS
CHUNK = 128
N_EXPERT_GROUPS = 4
EXPERTS_PER_GROUP = 8
N_EXPERTS = N_EXPERT_GROUPS * EXPERTS_PER_GROUP
D_EXPERT = 512
LN_EPS = 1e-5
D_PACK = D_MODEL // 2

LANES = 128
SUBLANES = 8
HALO = 16
SEQ_TILE = 512
EXPERT_TILE = 1024
EXPERT_SUBTILE = 256
TOKEN_TILE = 1024
SC_CORES = 2
SC_SUBCORES = 16
SC_WORKERS = SC_CORES * SC_SUBCORES
SC_ROWS = 64
VMEM_LIMIT_BYTES = 56 * 1024 * 1024

COL_E1, COL_E2, COL_W1, COL_W2, COL_R1, COL_R2 = 0, 1, 2, 3, 4, 5
COARSE_OFF = N_EXPERTS


def _dot(a, b):
    return jnp.dot(a, b, preferred_element_type=jnp.float32)


def _gelu_tanh(x):
    return 0.5 * x * (1.0 + jnp.tanh(0.7978845608028654 * (x + 0.044715 * (x * x * x))))


def _layer_norm(x, g, b):
    mu = jnp.mean(x, axis=-1, keepdims=True)
    xc = x - mu
    var = jnp.mean(xc * xc, axis=-1, keepdims=True)
    return xc * lax.rsqrt(var + LN_EPS) * g + b


def _pack_halves(x):
    u32 = jnp.uint32
    lo = lax.bitcast_convert_type(x[:, 0:D_PACK].astype(jnp.bfloat16).astype(jnp.float32), u32)
    hi = lax.bitcast_convert_type(x[:, D_PACK:D_MODEL].astype(jnp.bfloat16).astype(jnp.float32), u32)
    return (hi & u32(0xFFFF0000)) | (lo >> u32(16))


def _unpack_halves(w):
    u32 = jnp.uint32
    lo = lax.bitcast_convert_type(w << u32(16), jnp.float32)
    hi = lax.bitcast_convert_type(w & u32(0xFFFF0000), jnp.float32)
    return lo, hi


def _mixer_kernel(x_ref, x_prev_ref, x_next_ref, w_in_ref, b_in_ref, conv_w_ref, conv_b_ref,
                  sg_g_ref, sg_b_ref, ws_ref, bs_ref, w_o_ref, b_o_ref, ln1_g_ref, ln1_b_ref,
                  w_r_ref, b_r_ref,
                  x1_ref, xp_ref, slab_ref, route_ref, counts_ref,
                  xb_ref, ymix_ref, tri_ref, resid_ref, carry_ref, *, ts, nt, n_tiles, alpha):
    t = pl.program_id(0)
    j = jnp.minimum(t, n_tiles - 1) % nt
    bf16 = jnp.bfloat16

    @pl.when(t == 0)
    def _():
        carry_ref[...] = jnp.zeros_like(carry_ref)
        resid_ref[...] = jnp.zeros_like(resid_ref)
        ri = lax.broadcasted_iota(jnp.int32, (ts, ts), 0)
        ci = lax.broadcasted_iota(jnp.int32, (ts, ts), 1)
        tri_ref[...] = jnp.where(ri > ci, 1.0, 0.0).astype(bf16)

    lane = lax.broadcasted_iota(jnp.int32, (ts, LANES), 1)
    lane_f = lane.astype(jnp.float32)
    neg = jnp.float32(-jnp.inf)
    big = jnp.float32(1e9)
    is_c = (lane >= COARSE_OFF) & (lane < COARSE_OFF + N_EXPERT_GROUPS)

    def proj(lhs, lo, hi):
        return _dot(lhs, w_in_ref[:, lo:hi]) + b_in_ref[:, lo:hi]

    def back_norm():
        x1 = _layer_norm(resid_ref[...], ln1_g_ref[...], ln1_b_ref[...])
        x1_ref[...] = x1
        xp_ref[...] = _pack_halves(x1)
        return x1.astype(bf16)

    def back_route(x1b):
        logits = _dot(x1b, w_r_ref[...]) + b_r_ref[...]
        lc = jnp.where(is_c, logits, neg)
        mx = jnp.max(lc, axis=1, keepdims=True)
        grp = jnp.min(jnp.where(lc == mx, lane_f - COARSE_OFF, big), axis=1, keepdims=True)
        p_grp = 1.0 / jnp.sum(jnp.where(is_c, jnp.exp(logits - mx), 0.0), axis=1, keepdims=True)
        grp_lo = grp * EXPERTS_PER_GROUP
        in_grp = (lane_f >= grp_lo) & (lane_f < grp_lo + EXPERTS_PER_GROUP)
        lf = jnp.where(in_grp, logits, neg)
        v1 = jnp.max(lf, axis=1, keepdims=True)
        e1 = jnp.min(jnp.where(lf == v1, lane_f, big), axis=1, keepdims=True)
        lf2 = jnp.where(lane_f == e1, neg, lf)
        v2 = jnp.max(lf2, axis=1, keepdims=True)
        e2 = jnp.min(jnp.where(lf2 == v2, lane_f, big), axis=1, keepdims=True)
        a = jnp.exp(v2 - v1)
        return e1, e2, p_grp / (1.0 + a), p_grp * a / (1.0 + a)

    def back_rank(e1, e2, w1, w2):
        hit1 = lane_f == e1
        hit2 = lane_f == e2
        onehot = jnp.where(hit1 | hit2, 1.0, 0.0)
        carry = carry_ref[...]
        before = _dot(tri_ref[...], onehot.astype(bf16)) + carry
        r1 = jnp.sum(jnp.where(hit1, before, 0.0), axis=1, keepdims=True)
        r2 = jnp.sum(jnp.where(hit2, before, 0.0), axis=1, keepdims=True)
        carry = jnp.where(t > 0, carry + jnp.sum(onehot, axis=0, keepdims=True), carry)
        carry_ref[...] = carry
        counts_ref[...] = jnp.broadcast_to(carry, counts_ref.shape)
        slab = jnp.where(lane == COL_E1, e1, 0.0)
        slab = jnp.where(lane == COL_E2, e2, slab)
        slab = jnp.where(lane == COL_W1, w1, slab)
        slab = jnp.where(lane == COL_W2, w2, slab)
        slab = jnp.where(lane == COL_R1, r1, slab)
        slab = jnp.where(lane == COL_R2, r2, slab)
        slab_ref[...] = slab
        route_ref[...] = slab.T[0:SUBLANES, :]

    def front_load():
        xb_ref[0:HALO, :] = x_prev_ref[...].astype(bf16)
        xb_ref[HALO:HALO + ts, :] = x_ref[...].astype(bf16)
        xb_ref[HALO + ts:HALO + ts + HALO, :] = x_next_ref[...].astype(bf16)

    def front_conv():
        xe = xb_ref[...]
        g_e = proj(xe, 0, D_CONV) * proj(xe, 2 * D_CONV, 3 * D_CONV)
        row_e = lax.broadcasted_iota(jnp.int32, (ts + 2 * HALO, 1), 0)
        has_prev = jnp.where(j > 0, 1.0, 0.0)
        has_next = jnp.where(j < nt - 1, 1.0, 0.0)
        g_e = g_e * jnp.where(row_e < HALO, has_prev, jnp.where(row_e >= HALO + ts, has_next, 1.0))
        conv = (g_e[HALO - 1:HALO - 1 + ts, :] * conv_w_ref[0:1, :] + g_e[HALO:HALO + ts, :] * conv_w_ref[1:2, :]
                + g_e[HALO + 1:HALO + 1 + ts, :] * conv_w_ref[2:3, :] + conv_b_ref[...])
        y_a = proj(xb_ref[HALO:HALO + ts, :], D_CONV, 2 * D_CONV) * conv
        ymix_ref[:, 0:D_CONV] = y_a.astype(bf16)

    def front_gate_proj():
        xm = xb_ref[HALO:HALO + ts, :]
        return proj(xm, 3 * D_CONV, 3 * D_CONV + D_SG), proj(xm, 3 * D_CONV + D_SG, 3 * D_CONV + 2 * D_SG)

    def front_gate(u_pre, v_pre):
        u = _gelu_tanh(u_pre)
        v = _gelu_tanh(v_pre)
        v_ln = _layer_norm(v, sg_g_ref[...], sg_b_ref[...]).astype(bf16)
        first_head = lax.broadcasted_iota(jnp.int32, (CHUNK, LANES), 1) < SG_HEAD_DIM
        for q0 in range(0, ts, 2 * CHUNK):
            q1 = q0 + CHUNK
            for hp in range(N_SG_HEADS // 2):
                c0 = hp * LANES
                rhs = jnp.concatenate([v_ln[q0:q0 + CHUNK, c0:c0 + LANES], v_ln[q1:q1 + CHUNK, c0:c0 + LANES]], axis=1)
                res = _dot(ws_ref[hp], rhs)
                bias = bs_ref[:, c0:c0 + LANES]
                m0 = jnp.where(first_head, res[0:CHUNK, 0:LANES], res[CHUNK:2 * CHUNK, 0:LANES]) + bias
                m1 = jnp.where(first_head, res[0:CHUNK, LANES:2 * LANES], res[CHUNK:2 * CHUNK, LANES:2 * LANES]) + bias
                ymix_ref[q0:q0 + CHUNK, D_CONV + c0:D_CONV + c0 + LANES] = (u[q0:q0 + CHUNK, c0:c0 + LANES] * m0).astype(bf16)
                ymix_ref[q1:q1 + CHUNK, D_CONV + c0:D_CONV + c0 + LANES] = (u[q1:q1 + CHUNK, c0:c0 + LANES] * m1).astype(bf16)

    def front_out_a():
        resid_ref[...] = alpha * x_ref[...] + b_o_ref[...] + _dot(ymix_ref[:, 0:D_CONV], w_o_ref[0:D_CONV, :])

    def front_out_b():
        resid_ref[...] += _dot(ymix_ref[:, D_CONV:D_MODEL], w_o_ref[D_CONV:D_MODEL, :])

    front_load()
    gate_pre = front_gate_proj()
    x1b = back_norm()
    routing = back_route(x1b)
    front_conv()
    back_rank(*routing)
    front_out_a()
    front_gate(*gate_pre)
    front_out_b()


def _mixer(x, p, alpha):
    nb, s, d = x.shape
    ts = SEQ_TILE
    nt = s // ts
    hb = ts // HALO
    n_tiles = nb * nt
    n_total = n_tiles * ts

    front = lambda t: jnp.minimum(t, n_tiles - 1)
    back = lambda t: jnp.maximum(t - 1, 0)
    cur = lambda t: (front(t) // nt, front(t) % nt, 0)
    prv = lambda t: (front(t) // nt, jnp.maximum((front(t) % nt) * hb - 1, 0), 0)
    nxt = lambda t: (front(t) // nt, jnp.minimum((front(t) % nt + 1) * hb, s // HALO - 1), 0)
    full = lambda a: pl.BlockSpec(a.shape, lambda t: (0,) * a.ndim)
    in_specs = [pl.BlockSpec((None, ts, d), cur), pl.BlockSpec((None, HALO, d), prv),
                pl.BlockSpec((None, HALO, d), nxt)] + [full(a) for a in p]
    out_shape = (jax.ShapeDtypeStruct((n_total, d), jnp.float32),
                 jax.ShapeDtypeStruct((n_total, D_PACK), jnp.uint32),
                 jax.ShapeDtypeStruct((n_total, LANES), jnp.float32),
                 jax.ShapeDtypeStruct((SUBLANES, n_total), jnp.float32),
                 jax.ShapeDtypeStruct((SUBLANES, LANES), jnp.float32))
    row_blk = lambda t: (back(t), 0)
    out_specs = (pl.BlockSpec((ts, d), row_blk),
                 pl.BlockSpec((ts, D_PACK), row_blk),
                 pl.BlockSpec((ts, LANES), row_blk),
                 pl.BlockSpec((SUBLANES, ts), lambda t: (0, back(t))),
                 pl.BlockSpec((SUBLANES, LANES), lambda t: (0, 0)))
    return pl.pallas_call(
        functools.partial(_mixer_kernel, ts=ts, nt=nt, n_tiles=n_tiles, alpha=alpha),
        grid=(n_tiles + 1,),
        in_specs=in_specs,
        out_specs=out_specs,
        out_shape=out_shape,
        scratch_shapes=[pltpu.VMEM((ts + 2 * HALO, d), jnp.bfloat16),
                        pltpu.VMEM((ts, d), jnp.bfloat16),
                        pltpu.VMEM((ts, ts), jnp.bfloat16),
                        pltpu.VMEM((ts, d), jnp.float32),
                        pltpu.VMEM((1, LANES), jnp.float32)],
        compiler_params=pltpu.CompilerParams(dimension_semantics=("arbitrary",),
                                             vmem_limit_bytes=VMEM_LIMIT_BYTES),
        cost_estimate=pl.CostEstimate(
            flops=2 * n_total * (d * (3 * D_CONV + 2 * D_SG) + CHUNK * D_SG + d * d + d * LANES + ts * LANES),
            transcendentals=n_total * (2 * D_SG + 2 * LANES),
            bytes_accessed=n_total * (2 * d * 4 + D_PACK * 4 + LANES * 4 + SUBLANES * 4)),
        name="mixer",
    )(x, x, x, *p)


def _sc_mesh():
    return plsc.VectorSubcoreMesh(core_axis_name="c", subcore_axis_name="s",
                                  num_cores=SC_CORES, num_subcores=SC_SUBCORES)


def _sc_worker_base(per_worker):
    return (lax.axis_index("s") * SC_CORES + lax.axis_index("c")) * per_worker


def _sc_dispatch(pos1, pos2, xp, n_rows):
    n, dp = xp.shape
    per_worker = n // SC_WORKERS
    steps = per_worker // SC_ROWS

    def body(pos1_hbm, pos2_hbm, xp_hbm, xs_hbm, idx1_v, idx2_v, rows_v, sem):
        base = _sc_worker_base(per_worker)

        @pl.loop(0, steps)
        def _(k):
            off = pl.multiple_of(base + k * SC_ROWS, SC_ROWS)
            pltpu.sync_copy(pos1_hbm.at[pl.ds(off, SC_ROWS)], idx1_v)
            pltpu.sync_copy(pos2_hbm.at[pl.ds(off, SC_ROWS)], idx2_v)
            pltpu.sync_copy(xp_hbm.at[pl.ds(off, SC_ROWS)], rows_v)
            c1 = pltpu.async_copy(rows_v, xs_hbm.at[idx1_v], sem)
            c2 = pltpu.async_copy(rows_v, xs_hbm.at[idx2_v], sem)
            c1.wait()
            c2.wait()

    return pl.kernel(
        body,
        out_type=jax.ShapeDtypeStruct((n_rows, dp), xp.dtype),
        mesh=_sc_mesh(),
        scratch_types=[pltpu.VMEM((SC_ROWS,), jnp.int32), pltpu.VMEM((SC_ROWS,), jnp.int32),
                       pltpu.VMEM((SC_ROWS, dp), xp.dtype), pltpu.SemaphoreType.DMA],
        compiler_params=pltpu.CompilerParams(use_tc_tiling_on_sc=True),
        cost_estimate=pl.CostEstimate(flops=0, transcendentals=0, bytes_accessed=3 * n * dp * 4 + 2 * n * 4),
        name="sc_dispatch",
    )(pos1, pos2, xp)


def _sc_gather(pos1, pos2, ys):
    n = pos1.shape[0]
    dp = ys.shape[1]
    per_worker = n // SC_WORKERS
    steps = per_worker // SC_ROWS

    def body(pos1_hbm, pos2_hbm, ys_hbm, out_hbm, idx_v, rows_v, sem):
        base = _sc_worker_base(per_worker)

        @pl.loop(0, steps)
        def _(k):
            off = pl.multiple_of(base + k * SC_ROWS, SC_ROWS)
            for slot, pos_hbm in enumerate((pos1_hbm, pos2_hbm)):
                pltpu.sync_copy(pos_hbm.at[pl.ds(off, SC_ROWS)], idx_v)
                pltpu.async_copy(ys_hbm.at[idx_v], rows_v, sem).wait()
                pltpu.sync_copy(rows_v, out_hbm.at[slot, pl.ds(off, SC_ROWS)])

    return pl.kernel(
        body,
        out_type=jax.ShapeDtypeStruct((2, n, dp), ys.dtype),
        mesh=_sc_mesh(),
        scratch_types=[pltpu.VMEM((SC_ROWS,), jnp.int32), pltpu.VMEM((SC_ROWS, dp), ys.dtype),
                       pltpu.SemaphoreType.DMA],
        compiler_params=pltpu.CompilerParams(use_tc_tiling_on_sc=True),
        cost_estimate=pl.CostEstimate(flops=0, transcendentals=0, bytes_accessed=4 * n * dp * 4 + 2 * n * 4),
        name="sc_gather",
    )(pos1, pos2, ys)


def _expert_kernel(te_ref, first_ref, nsub_ref, slot_ref, next_ref, nused_ref, xs_ref, *refs, cast_weights):
    i = pl.program_id(0)
    bf16 = jnp.bfloat16
    sub = EXPERT_SUBTILE
    slot = slot_ref[i]

    if cast_weights:
        (wg_hbm, wu_hbm, wd_hbm, ys_ref, wgu_out, wd_out,
         wg_buf, wu_buf, wd_buf, wgu_bf_ref, wd_bf_ref, sem, out_sem) = refs
        wgu_ref, wd_ref = wgu_bf_ref, wd_bf_ref

        def weight_copies(expert, s):
            return (pltpu.make_async_copy(wg_hbm.at[expert], wg_buf.at[s], sem.at[s]),
                    pltpu.make_async_copy(wu_hbm.at[expert], wu_buf.at[s], sem.at[s]),
                    pltpu.make_async_copy(wd_hbm.at[expert], wd_buf.at[s], sem.at[s]))

        def emit_copies(expert):
            return (pltpu.make_async_copy(wgu_bf_ref, wgu_out.at[expert], out_sem),
                    pltpu.make_async_copy(wd_bf_ref, wd_out.at[expert], out_sem))
    else:
        wgu_hbm, wd_hbm, ys_ref, wgu_buf, wd_buf, sem = refs
        wgu_ref, wd_ref = wgu_buf.at[slot], wd_buf.at[slot]

        def weight_copies(expert, s):
            return (pltpu.make_async_copy(wgu_hbm.at[expert], wgu_buf.at[s], sem.at[s]),
                    pltpu.make_async_copy(wd_hbm.at[expert], wd_buf.at[s], sem.at[s]))

    @pl.when(first_ref[i] == 1)
    def _():
        @pl.when(i == 0)
        def _():
            for c in weight_copies(te_ref[i], slot):
                c.start()

        for c in weight_copies(te_ref[i], slot):
            c.wait()

        @pl.when(next_ref[i] >= 0)
        def _():
            for c in weight_copies(next_ref[i], 1 - slot):
                c.start()

        if cast_weights:
            @pl.when(i > 0)
            def _():
                for c in emit_copies(te_ref[i]):
                    c.wait()

            wgu_bf_ref[:, 0:D_EXPERT] = wg_buf[slot].astype(bf16)
            wgu_bf_ref[:, D_EXPERT:2 * D_EXPERT] = wu_buf[slot].astype(bf16)
            wd_bf_ref[...] = wd_buf[slot].astype(bf16)
            for c in emit_copies(te_ref[i]):
                c.start()

    def swiglu(r0):
        rows = pl.ds(r0, sub)
        lo, hi = _unpack_halves(xs_ref[rows, :])
        gu = (_dot(lo.astype(bf16), wgu_ref[0:D_PACK, :])
              + _dot(hi.astype(bf16), wgu_ref[D_PACK:D_MODEL, :]))
        g = gu[:, 0:D_EXPERT]
        h = g * jax.nn.sigmoid(g) * gu[:, D_EXPERT:2 * D_EXPERT]
        ys_ref[rows, :] = _pack_halves(_dot(h.astype(bf16), wd_ref[...]))

    n_sub = nsub_ref[i]

    def pair(k, c):
        r0 = pl.multiple_of(k * (2 * sub), 2 * sub)
        swiglu(r0)
        swiglu(r0 + sub)
        return c

    lax.fori_loop(0, n_sub // 2, pair, 0)

    @pl.when(n_sub % 2 == 1)
    def _():
        swiglu(pl.multiple_of((n_sub - 1) * sub, sub))

    def clear(k, c):
        ys_ref[pl.ds(pl.multiple_of(k * sub, sub), sub), :] = jnp.zeros((sub, D_PACK), ys_ref.dtype)
        return c

    lax.fori_loop(n_sub, jnp.where(i < nused_ref[0], ys_ref.shape[0] // sub, 0), clear, 0)

    if cast_weights:
        @pl.when(i == pl.num_programs(0) - 1)
        def _():
            for c in emit_copies(te_ref[i]):
                c.wait()


def _experts(tiles, xs, weights):
    n_rows, dp = xs.shape
    d = D_MODEL
    tm = EXPERT_TILE
    cast_weights = len(weights) == 3
    row_blk = lambda i, te, fi, ns, sl, nx, nu: (jnp.minimum(i, nu[0] - 1), 0)
    any_spec = pl.BlockSpec(memory_space=pl.ANY)
    ys_shape = jax.ShapeDtypeStruct((n_rows, dp), jnp.uint32)
    ys_spec = pl.BlockSpec((tm, dp), row_blk)
    wgu_bf = ((d, 2 * D_EXPERT), jnp.bfloat16)
    wd_bf = ((D_EXPERT, d), jnp.bfloat16)
    if cast_weights:
        out_shape = (ys_shape, jax.ShapeDtypeStruct((N_EXPERTS,) + wgu_bf[0], wgu_bf[1]),
                     jax.ShapeDtypeStruct((N_EXPERTS,) + wd_bf[0], wd_bf[1]))
        out_specs = (ys_spec, any_spec, any_spec)
        scratch = [pltpu.VMEM((2, d, D_EXPERT), jnp.float32), pltpu.VMEM((2, d, D_EXPERT), jnp.float32),
                   pltpu.VMEM((2, D_EXPERT, d), jnp.float32), pltpu.VMEM(*wgu_bf), pltpu.VMEM(*wd_bf),
                   pltpu.SemaphoreType.DMA((2,)), pltpu.SemaphoreType.DMA]
        weight_bytes = N_EXPERTS * 3 * d * D_EXPERT * (4 + 2)
    else:
        out_shape, out_specs = ys_shape, ys_spec
        scratch = [pltpu.VMEM((2,) + wgu_bf[0], wgu_bf[1]), pltpu.VMEM((2,) + wd_bf[0], wd_bf[1]),
                   pltpu.SemaphoreType.DMA((2,))]
        weight_bytes = N_EXPERTS * 3 * d * D_EXPERT * 2
    grid_spec = pltpu.PrefetchScalarGridSpec(
        num_scalar_prefetch=6,
        grid=(n_rows // tm,),
        in_specs=[pl.BlockSpec((tm, dp), row_blk)] + [any_spec] * len(weights),
        out_specs=out_specs,
        scratch_shapes=scratch,
    )
    return pl.pallas_call(
        functools.partial(_expert_kernel, cast_weights=cast_weights),
        grid_spec=grid_spec,
        out_shape=out_shape,
        compiler_params=pltpu.CompilerParams(dimension_semantics=("arbitrary",),
                                             vmem_limit_bytes=VMEM_LIMIT_BYTES),
        cost_estimate=pl.CostEstimate(flops=2 * n_rows * 3 * d * D_EXPERT, transcendentals=n_rows * D_EXPERT,
                                      bytes_accessed=2 * n_rows * dp * 4 + weight_bytes),
        name="experts",
    )(*tiles, xs, *weights)


def _combine_kernel(x1_ref, slab_ref, y1_ref, y2_ref, g_ref, b_ref, out_ref, *, alpha):
    slab = slab_ref[...]
    w1 = slab[:, COL_W1:COL_W1 + 1]
    w2 = slab[:, COL_W2:COL_W2 + 1]
    lo1, hi1 = _unpack_halves(y1_ref[...])
    lo2, hi2 = _unpack_halves(y2_ref[...])
    moe = jnp.concatenate([w1 * lo1 + w2 * lo2, w1 * hi1 + w2 * hi2], axis=1)
    out_ref[...] = _layer_norm(alpha * x1_ref[...] + moe, g_ref[...], b_ref[...])


def _combine(x1, slab, yg, g, b, alpha):
    n, d = x1.shape
    tt = TOKEN_TILE
    return pl.pallas_call(
        functools.partial(_combine_kernel, alpha=alpha),
        grid=(n // tt,),
        in_specs=[pl.BlockSpec((tt, d), lambda i: (i, 0)),
                  pl.BlockSpec((tt, LANES), lambda i: (i, 0)),
                  pl.BlockSpec((None, tt, D_PACK), lambda i: (0, i, 0)),
                  pl.BlockSpec((None, tt, D_PACK), lambda i: (1, i, 0)),
                  pl.BlockSpec((1, d), lambda i: (0, 0)),
                  pl.BlockSpec((1, d), lambda i: (0, 0))],
        out_specs=pl.BlockSpec((tt, d), lambda i: (i, 0)),
        out_shape=jax.ShapeDtypeStruct((n, d), jnp.float32),
        compiler_params=pltpu.CompilerParams(dimension_semantics=("arbitrary",),
                                             vmem_limit_bytes=VMEM_LIMIT_BYTES),
        cost_estimate=pl.CostEstimate(flops=12 * n * d, transcendentals=n,
                                      bytes_accessed=n * (2 * d * 4 + 2 * D_PACK * 4 + LANES * 4)),
        name="combine",
    )(x1, slab, yg, yg, g, b)


def _row(a):
    return a.reshape(1, -1)


def _mixer_params(w_in, b_in, conv_w, conv_b, sg_ln_g, sg_ln_b, w_s, b_s, w_o, b_o, ln1_g, ln1_b,
                  w_rc, b_rc, w_rf, b_rf):
    bf16 = jnp.bfloat16
    ws_pairs = w_s.reshape(N_SG_HEADS // 2, 2 * CHUNK, CHUNK).astype(bf16)
    bs_full = jnp.repeat(b_s.T, SG_HEAD_DIM, axis=1)
    pad = LANES - N_EXPERTS - N_EXPERT_GROUPS
    w_r = jnp.pad(jnp.concatenate([w_rf, w_rc], axis=1), ((0, 0), (0, pad))).astype(bf16)
    b_r = jnp.pad(jnp.concatenate([b_rf, b_rc]), (0, pad)).reshape(1, LANES)
    return (w_in.astype(bf16), _row(b_in), conv_w, _row(conv_b), _row(sg_ln_g), _row(sg_ln_b),
            ws_pairs, bs_full, w_o.astype(bf16), _row(b_o), _row(ln1_g), _row(ln1_b), w_r, b_r)


def _encoder_layer(x, mixer_params, expert_weights, ln2_g, ln2_b, alpha):
    n = x.shape[0] * x.shape[1]
    x1, xp, slab, route, carry = _mixer(x, mixer_params, alpha)

    tm = EXPERT_TILE
    n_tiles = 2 * n // tm + N_EXPERTS
    cnt = carry[0, 0:N_EXPERTS].astype(jnp.int32)
    tiles_e = jnp.maximum((cnt + tm - 1) // tm, 1 if len(expert_weights) == 3 else 0)
    experts = jnp.arange(N_EXPERTS, dtype=jnp.int32)[:, None]
    up_to = experts.T <= experts
    tile_end = jnp.sum(jnp.where(up_to, tiles_e[None, :], 0), axis=1)
    tile_start = tile_end - tiles_e
    row_off = tile_start * tm

    def sorted_pos(e_row, r_row):
        e = route[e_row].astype(jnp.int32)
        return jnp.sum(jnp.where(e[None, :] == experts, row_off[:, None], 0), axis=0) + route[r_row].astype(jnp.int32)

    pos1 = sorted_pos(COL_E1, COL_R1)
    pos2 = sorted_pos(COL_E2, COL_R2)
    tile_ids = jnp.arange(n_tiles, dtype=jnp.int32)
    n_used = tile_end[N_EXPERTS - 1:N_EXPERTS]
    tile_expert = jnp.sum((tile_end[None, :] <= tile_ids[:, None]).astype(jnp.int32), axis=1)
    last_used = jnp.sum((tile_end <= n_used[0] - 1).astype(jnp.int32))
    tile_expert = jnp.minimum(tile_expert, last_used)
    tile_first = (jnp.any(tile_ids[:, None] == tile_start[None, :], axis=1) & (tile_ids < n_used[0])) | (tile_ids == 0)
    tile_first = tile_first.astype(jnp.int32)
    rows_left = jnp.sum(jnp.where(tile_expert[:, None] == experts.T, (row_off + cnt)[None, :], 0), axis=1) - tile_ids * tm
    tile_nsub = jnp.where(tile_ids < n_used[0], (jnp.clip(rows_left, 0, tm) + EXPERT_SUBTILE - 1) // EXPERT_SUBTILE, 0)

    used = tiles_e > 0
    ordinal = jnp.sum((up_to & used[None, :]).astype(jnp.int32), axis=1) - 1
    e_ids = experts[:, 0]
    later_used = used[None, :] & (e_ids[None, :] > e_ids[:, None])
    next_used = jnp.min(jnp.where(later_used, e_ids[None, :], N_EXPERTS), axis=1)
    next_used = jnp.where(next_used < N_EXPERTS, next_used, -1)
    of_tile = tile_expert[:, None] == experts.T
    tile_slot = jnp.sum(jnp.where(of_tile, ordinal[None, :], 0), axis=1) % 2
    tile_next = jnp.sum(jnp.where(of_tile, next_used[None, :], 0), axis=1)

    tiles = (tile_expert, tile_first, tile_nsub.astype(jnp.int32), tile_slot.astype(jnp.int32),
             tile_next.astype(jnp.int32), n_used)
    xs = _sc_dispatch(pos1, pos2, xp, n_tiles * tm)
    if len(expert_weights) == 3:
        ys, *expert_weights = _experts(tiles, xs, expert_weights)
    else:
        ys = _experts(tiles, xs, expert_weights)
    yg = _sc_gather(pos1, pos2, ys)
    return _combine(x1, slab, yg, _row(ln2_g), _row(ln2_b), alpha).reshape(x.shape), tuple(expert_weights)


def kernel(x_prompt, x_sample, w_in, b_in, conv_w, conv_b, sg_ln_g, sg_ln_b, w_s, b_s, w_o, b_o, ln1_g, ln1_b, w_rc, b_rc, w_rf, b_rf, w_gate, w_up, w_down, ln2_g, ln2_b):
    depth = w_in.shape[0]
    alpha = (2.0 * depth) ** 0.25
    xs = (x_prompt, x_sample)
    for l in range(depth):
        mixer_params = _mixer_params(w_in[l], b_in[l], conv_w[l], conv_b[l], sg_ln_g[l], sg_ln_b[l], w_s[l], b_s[l],
                                     w_o[l], b_o[l], ln1_g[l], ln1_b[l], w_rc[l], b_rc[l], w_rf[l], b_rf[l])
        expert_weights = (w_gate[l], w_up[l], w_down[l])
        outs = [None] * len(xs)
        for k in sorted(range(len(xs)), key=lambda k: -xs[k].shape[0] * xs[k].shape[1]):
            outs[k], expert_weights = _encoder_layer(xs[k], mixer_params, expert_weights, ln2_g[l], ln2_b[l], alpha)
        xs = tuple(outs)
    return xs
```

```python
import functools

import jax
import jax.numpy as jnp
from jax import lax
from jax.experimental import pallas as pl
from jax.experimental.pallas import tpu as pltpu
from jax.experimental.pallas import tpu_sc as plsc

D_MODEL = 1024
D_CONV = 512
D_SG = 512
N_SG_HEADS = 8
SG_HEAD_DIM = D_SG // N_SG_HEADS
CHUNK = 128
N_EXPERT_GROUPS = 4
EXPERTS_PER_GROUP = 8
N_EXPERTS = N_EXPERT_GROUPS * EXPERTS_PER_GROUP
D_EXPERT = 512
LN_EPS = 1e-5
D_PACK = D_MODEL // 2

LANES = 128
SUBLANES = 8
HALO = 16
SEQ_TILE = 512
EXPERT_TILE = 1024
EXPERT_SUBTILE = 256
TOKEN_TILE = 1024
SC_CORES = 2
SC_SUBCORES = 16
SC_WORKERS = SC_CORES * SC_SUBCORES
SC_ROWS = 64
VMEM_LIMIT_BYTES = 56 * 1024 * 1024

COL_E1, COL_E2, COL_W1, COL_W2, COL_R1, COL_R2 = 0, 1, 2, 3, 4, 5
COARSE_OFF = N_EXPERTS


def _dot(a, b):
    return jnp.dot(a, b, preferred_element_type=jnp.float32)


def _gelu_tanh(x):
    return 0.5 * x * (1.0 + jnp.tanh(0.7978845608028654 * (x + 0.044715 * (x * x * x))))


def _layer_norm(x, g, b):
    mu = jnp.mean(x, axis=-1, keepdims=True)
    xc = x - mu
    var = jnp.mean(xc * xc, axis=-1, keepdims=True)
    return xc * lax.rsqrt(var + LN_EPS) * g + b


def _pack_halves(x):
    u32 = jnp.uint32
    lo = lax.bitcast_convert_type(x[:, 0:D_PACK].astype(jnp.bfloat16).astype(jnp.float32), u32)
    hi = lax.bitcast_convert_type(x[:, D_PACK:D_MODEL].astype(jnp.bfloat16).astype(jnp.float32), u32)
    return (hi & u32(0xFFFF0000)) | (lo >> u32(16))


def _unpack_halves(w):
    u32 = jnp.uint32
    lo = lax.bitcast_convert_type(w << u32(16), jnp.float32)
    hi = lax.bitcast_convert_type(w & u32(0xFFFF0000), jnp.float32)
    return lo, hi


def _mixer_kernel(x_ref, x_prev_ref, x_next_ref, w_in_ref, b_in_ref, conv_w_ref, conv_b_ref,
                  sg_g_ref, sg_b_ref, ws_ref, bs_ref, w_o_ref, b_o_ref, ln1_g_ref, ln1_b_ref,
                  w_r_ref, b_r_ref,
                  x1_ref, xp_ref, slab_ref, route_ref, counts_ref,
                  xb_ref, ymix_ref, tri_ref, resid_ref, carry_ref, *, ts, nt, n_tiles, alpha):
    t = pl.program_id(0)
    j = jnp.minimum(t, n_tiles - 1) % nt
    bf16 = jnp.bfloat16

    @pl.when(t == 0)
    def _():
        carry_ref[...] = jnp.zeros_like(carry_ref)
        resid_ref[...] = jnp.zeros_like(resid_ref)
        ri = lax.broadcasted_iota(jnp.int32, (ts, ts), 0)
        ci = lax.broadcasted_iota(jnp.int32, (ts, ts), 1)
        tri_ref[...] = jnp.where(ri > ci, 1.0, 0.0).astype(bf16)

    lane = lax.broadcasted_iota(jnp.int32, (ts, LANES), 1)
    lane_f = lane.astype(jnp.float32)
    neg = jnp.float32(-jnp.inf)
    big = jnp.float32(1e9)
    is_c = (lane >= COARSE_OFF) & (lane < COARSE_OFF + N_EXPERT_GROUPS)

    def proj(lhs, lo, hi):
        return _dot(lhs, w_in_ref[:, lo:hi]) + b_in_ref[:, lo:hi]

    def back_norm():
        x1 = _layer_norm(resid_ref[...], ln1_g_ref[...], ln1_b_ref[...])
        x1_ref[...] = x1
        xp_ref[...] = _pack_halves(x1)
        return x1.astype(bf16)

    def back_route(x1b):
        logits = _dot(x1b, w_r_ref[...]) + b_r_ref[...]
        lc = jnp.where(is_c, logits, neg)
        mx = jnp.max(lc, axis=1, keepdims=True)
        grp = jnp.min(jnp.where(lc == mx, lane_f - COARSE_OFF, big), axis=1, keepdims=True)
        p_grp = 1.0 / jnp.sum(jnp.where(is_c, jnp.exp(logits - mx), 0.0), axis=1, keepdims=True)
        grp_lo = grp * EXPERTS_PER_GROUP
        in_grp = (lane_f >= grp_lo) & (lane_f < grp_lo + EXPERTS_PER_GROUP)
        lf = jnp.where(in_grp, logits, neg)
        v1 = jnp.max(lf, axis=1, keepdims=True)
        e1 = jnp.min(jnp.where(lf == v1, lane_f, big), axis=1, keepdims=True)
        lf2 = jnp.where(lane_f == e1, neg, lf)
        v2 = jnp.max(lf2, axis=1, keepdims=True)
        e2 = jnp.min(jnp.where(lf2 == v2, lane_f, big), axis=1, keepdims=True)
        a = jnp.exp(v2 - v1)
        return e1, e2, p_grp / (1.0 + a), p_grp * a / (1.0 + a)

    def back_rank(e1, e2, w1, w2):
        hit1 = lane_f == e1
        hit2 = lane_f == e2
        onehot = jnp.where(hit1 | hit2, 1.0, 0.0)
        carry = carry_ref[...]
        before = _dot(tri_ref[...], onehot.astype(bf16)) + carry
        r1 = jnp.sum(jnp.where(hit1, before, 0.0), axis=1, keepdims=True)
        r2 = jnp.sum(jnp.where(hit2, before, 0.0), axis=1, keepdims=True)
        carry = jnp.where(t > 0, carry + jnp.sum(onehot, axis=0, keepdims=True), carry)
        carry_ref[...] = carry
        counts_ref[...] = jnp.broadcast_to(carry, counts_ref.shape)
        slab = jnp.where(lane == COL_E1, e1, 0.0)
        slab = jnp.where(lane == COL_E2, e2, slab)
        slab = jnp.where(lane == COL_W1, w1, slab)
        slab = jnp.where(lane == COL_W2, w2, slab)
        slab = jnp.where(lane == COL_R1, r1, slab)
        slab = jnp.where(lane == COL_R2, r2, slab)
        slab_ref[...] = slab
        route_ref[...] = slab.T[0:SUBLANES, :]

    def front_load():
        xb_ref[0:HALO, :] = x_prev_ref[...].astype(bf16)
        xb_ref[HALO:HALO + ts, :] = x_ref[...].astype(bf16)
        xb_ref[HALO + ts:HALO + ts + HALO, :] = x_next_ref[...].astype(bf16)

    def front_conv():
        xe = xb_ref[...]
        g_e = proj(xe, 0, D_CONV) * proj(xe, 2 * D_CONV, 3 * D_CONV)
        row_e = lax.broadcasted_iota(jnp.int32, (ts + 2 * HALO, 1), 0)
        has_prev = jnp.where(j > 0, 1.0, 0.0)
        has_next = jnp.where(j < nt - 1, 1.0, 0.0)
        g_e = g_e * jnp.where(row_e < HALO, has_prev, jnp.where(row_e >= HALO + ts, has_next, 1.0))
        conv = (g_e[HALO - 1:HALO - 1 + ts, :] * conv_w_ref[0:1, :] + g_e[HALO:HALO + ts, :] * conv_w_ref[1:2, :]
                + g_e[HALO + 1:HALO + 1 + ts, :] * conv_w_ref[2:3, :] + conv_b_ref[...])
        y_a = proj(xb_ref[HALO:HALO + ts, :], D_CONV, 2 * D_CONV) * conv
        ymix_ref[:, 0:D_CONV] = y_a.astype(bf16)

    def front_gate_proj():
        xm = xb_ref[HALO:HALO + ts, :]
        return proj(xm, 3 * D_CONV, 3 * D_CONV + D_SG), proj(xm, 3 * D_CONV + D_SG, 3 * D_CONV + 2 * D_SG)

    def front_gate(u_pre, v_pre):
        u = _gelu_tanh(u_pre)
        v = _gelu_tanh(v_pre)
        v_ln = _layer_norm(v, sg_g_ref[...], sg_b_ref[...]).astype(bf16)
        first_head = lax.broadcasted_iota(jnp.int32, (CHUNK, LANES), 1) < SG_HEAD_DIM
        for q0 in range(0, ts, 2 * CHUNK):
            q1 = q0 + CHUNK
            for hp in range(N_SG_HEADS // 2):
                c0 = hp * LANES
                rhs = jnp.concatenate([v_ln[q0:q0 + CHUNK, c0:c0 + LANES], v_ln[q1:q1 + CHUNK, c0:c0 + LANES]], axis=1)
                res = _dot(ws_ref[hp], rhs)
                bias = bs_ref[:, c0:c0 + LANES]
                m0 = jnp.where(first_head, res[0:CHUNK, 0:LANES], res[CHUNK:2 * CHUNK, 0:LANES]) + bias
                m1 = jnp.where(first_head, res[0:CHUNK, LANES:2 * LANES], res[CHUNK:2 * CHUNK, LANES:2 * LANES]) + bias
                ymix_ref[q0:q0 + CHUNK, D_CONV + c0:D_CONV + c0 + LANES] = (u[q0:q0 + CHUNK, c0:c0 + LANES] * m0).astype(bf16)
                ymix_ref[q1:q1 + CHUNK, D_CONV + c0:D_CONV + c0 + LANES] = (u[q1:q1 + CHUNK, c0:c0 + LANES] * m1).astype(bf16)

    def front_out_a():
        resid_ref[...] = alpha * x_ref[...] + b_o_ref[...] + _dot(ymix_ref[:, 0:D_CONV], w_o_ref[0:D_CONV, :])

    def front_out_b():
        resid_ref[...] += _dot(ymix_ref[:, D_CONV:D_MODEL], w_o_ref[D_CONV:D_MODEL, :])

    front_load()
    gate_pre = front_gate_proj()
    x1b = back_norm()
    routing = back_route(x1b)
    front_conv()
    back_rank(*routing)
    front_out_a()
    front_gate(*gate_pre)
    front_out_b()


def _mixer(x, p, alpha):
    nb, s, d = x.shape
    ts = SEQ_TILE
    nt = s // ts
    hb = ts // HALO
    n_tiles = nb * nt
    n_total = n_tiles * ts

    front = lambda t: jnp.minimum(t, n_tiles - 1)
    back = lambda t: jnp.maximum(t - 1, 0)
    cur = lambda t: (front(t) // nt, front(t) % nt, 0)
    prv = lambda t: (front(t) // nt, jnp.maximum((front(t) % nt) * hb - 1, 0), 0)
    nxt = lambda t: (front(t) // nt, jnp.minimum((front(t) % nt + 1) * hb, s // HALO - 1), 0)
    full = lambda a: pl.BlockSpec(a.shape, lambda t: (0,) * a.ndim)
    in_specs = [pl.BlockSpec((None, ts, d), cur), pl.BlockSpec((None, HALO, d), prv),
                pl.BlockSpec((None, HALO, d), nxt)] + [full(a) for a in p]
    out_shape = (jax.ShapeDtypeStruct((n_total, d), jnp.float32),
                 jax.ShapeDtypeStruct((n_total, D_PACK), jnp.uint32),
                 jax.ShapeDtypeStruct((n_total, LANES), jnp.float32),
                 jax.ShapeDtypeStruct((SUBLANES, n_total), jnp.float32),
                 jax.ShapeDtypeStruct((SUBLANES, LANES), jnp.float32))
    row_blk = lambda t: (back(t), 0)
    out_specs = (pl.BlockSpec((ts, d), row_blk),
                 pl.BlockSpec((ts, D_PACK), row_blk),
                 pl.BlockSpec((ts, LANES), row_blk),
                 pl.BlockSpec((SUBLANES, ts), lambda t: (0, back(t))),
                 pl.BlockSpec((SUBLANES, LANES), lambda t: (0, 0)))
    return pl.pallas_call(
        functools.partial(_mixer_kernel, ts=ts, nt=nt, n_tiles=n_tiles, alpha=alpha),
        grid=(n_tiles + 1,),
        in_specs=in_specs,
        out_specs=out_specs,
        out_shape=out_shape,
        scratch_shapes=[pltpu.VMEM((ts + 2 * HALO, d), jnp.bfloat16),
                        pltpu.VMEM((ts, d), jnp.bfloat16),
                        pltpu.VMEM((ts, ts), jnp.bfloat16),
                        pltpu.VMEM((ts, d), jnp.float32),
                        pltpu.VMEM((1, LANES), jnp.float32)],
        compiler_params=pltpu.CompilerParams(dimension_semantics=("arbitrary",),
                                             vmem_limit_bytes=VMEM_LIMIT_BYTES),
        cost_estimate=pl.CostEstimate(
            flops=2 * n_total * (d * (3 * D_CONV + 2 * D_SG) + CHUNK * D_SG + d * d + d * LANES + ts * LANES),
            transcendentals=n_total * (2 * D_SG + 2 * LANES),
            bytes_accessed=n_total * (2 * d * 4 + D_PACK * 4 + LANES * 4 + SUBLANES * 4)),
        name="mixer",
    )(x, x, x, *p)


def _sc_mesh():
    return plsc.VectorSubcoreMesh(core_axis_name="c", subcore_axis_name="s",
                                  num_cores=SC_CORES, num_subcores=SC_SUBCORES)


def _sc_worker_base(per_worker):
    return (lax.axis_index("s") * SC_CORES + lax.axis_index("c")) * per_worker


def _sc_dispatch(pos1, pos2, xp, n_rows):
    n, dp = xp.shape
    per_worker = n // SC_WORKERS
    steps = per_worker // SC_ROWS

    def body(pos1_hbm, pos2_hbm, xp_hbm, xs_hbm, idx1_v, idx2_v, rows_v, sem):
        base = _sc_worker_base(per_worker)

        @pl.loop(0, steps)
        def _(k):
            off = pl.multiple_of(base + k * SC_ROWS, SC_ROWS)
            pltpu.sync_copy(pos1_hbm.at[pl.ds(off, SC_ROWS)], idx1_v)
            pltpu.sync_copy(pos2_hbm.at[pl.ds(off, SC_ROWS)], idx2_v)
            pltpu.sync_copy(xp_hbm.at[pl.ds(off, SC_ROWS)], rows_v)
            c1 = pltpu.async_copy(rows_v, xs_hbm.at[idx1_v], sem)
            c2 = pltpu.async_copy(rows_v, xs_hbm.at[idx2_v], sem)
            c1.wait()
            c2.wait()

    return pl.kernel(
        body,
        out_type=jax.ShapeDtypeStruct((n_rows, dp), xp.dtype),
        mesh=_sc_mesh(),
        scratch_types=[pltpu.VMEM((SC_ROWS,), jnp.int32), pltpu.VMEM((SC_ROWS,), jnp.int32),
                       pltpu.VMEM((SC_ROWS, dp), xp.dtype), pltpu.SemaphoreType.DMA],
        compiler_params=pltpu.CompilerParams(use_tc_tiling_on_sc=True),
        cost_estimate=pl.CostEstimate(flops=0, transcendentals=0, bytes_accessed=3 * n * dp * 4 + 2 * n * 4),
        name="sc_dispatch",
    )(pos1, pos2, xp)


def _sc_gather(pos1, pos2, ys):
    n = pos1.shape[0]
    dp = ys.shape[1]
    per_worker = n // SC_WORKERS
    steps = per_worker // SC_ROWS

    def body(pos1_hbm, pos2_hbm, ys_hbm, out_hbm, idx_v, rows_v, sem):
        base = _sc_worker_base(per_worker)

        @pl.loop(0, steps)
        def _(k):
            off = pl.multiple_of(base + k * SC_ROWS, SC_ROWS)
            for slot, pos_hbm in enumerate((pos1_hbm, pos2_hbm)):
                pltpu.sync_copy(pos_hbm.at[pl.ds(off, SC_ROWS)], idx_v)
                pltpu.async_copy(ys_hbm.at[idx_v], rows_v, sem).wait()
                pltpu.sync_copy(rows_v, out_hbm.at[slot, pl.ds(off, SC_ROWS)])

    return pl.kernel(
        body,
        out_type=jax.ShapeDtypeStruct((2, n, dp), ys.dtype),
        mesh=_sc_mesh(),
        scratch_types=[pltpu.VMEM((SC_ROWS,), jnp.int32), pltpu.VMEM((SC_ROWS, dp), ys.dtype),
                       pltpu.SemaphoreType.DMA],
        compiler_params=pltpu.CompilerParams(use_tc_tiling_on_sc=True),
        cost_estimate=pl.CostEstimate(flops=0, transcendentals=0, bytes_accessed=4 * n * dp * 4 + 2 * n * 4),
        name="sc_gather",
    )(pos1, pos2, ys)


def _expert_kernel(te_ref, first_ref, nsub_ref, slot_ref, next_ref, nused_ref, xs_ref, *refs, cast_weights):
    i = pl.program_id(0)
    bf16 = jnp.bfloat16
    sub = EXPERT_SUBTILE
    slot = slot_ref[i]

    if cast_weights:
        (wg_hbm, wu_hbm, wd_hbm, ys_ref, wgu_out, wd_out,
         wg_buf, wu_buf, wd_buf, wgu_bf_ref, wd_bf_ref, sem, out_sem) = refs
        wgu_ref, wd_ref = wgu_bf_ref, wd_bf_ref

        def weight_copies(expert, s):
            return (pltpu.make_async_copy(wg_hbm.at[expert], wg_buf.at[s], sem.at[s]),
                    pltpu.make_async_copy(wu_hbm.at[expert], wu_buf.at[s], sem.at[s]),
                    pltpu.make_async_copy(wd_hbm.at[expert], wd_buf.at[s], sem.at[s]))

        def emit_copies(expert):
            return (pltpu.make_async_copy(wgu_bf_ref, wgu_out.at[expert], out_sem),
                    pltpu.make_async_copy(wd_bf_ref, wd_out.at[expert], out_sem))
    else:
        wgu_hbm, wd_hbm, ys_ref, wgu_buf, wd_buf, sem = refs
        wgu_ref, wd_ref = wgu_buf.at[slot], wd_buf.at[slot]

        def weight_copies(expert, s):
            return (pltpu.make_async_copy(wgu_hbm.at[expert], wgu_buf.at[s], sem.at[s]),
                    pltpu.make_async_copy(wd_hbm.at[expert], wd_buf.at[s], sem.at[s]))

    @pl.when(first_ref[i] == 1)
    def _():
        @pl.when(i == 0)
        def _():
            for c in weight_copies(te_ref[i], slot):
                c.start()

        for c in weight_copies(te_ref[i], slot):
            c.wait()

        @pl.when(next_ref[i] >= 0)
        def _():
            for c in weight_copies(next_ref[i], 1 - slot):
                c.start()

        if cast_weights:
            @pl.when(i > 0)
            def _():
                for c in emit_copies(te_ref[i]):
                    c.wait()

            wgu_bf_ref[:, 0:D_EXPERT] = wg_buf[slot].astype(bf16)
            wgu_bf_ref[:, D_EXPERT:2 * D_EXPERT] = wu_buf[slot].astype(bf16)
            wd_bf_ref[...] = wd_buf[slot].astype(bf16)
            for c in emit_copies(te_ref[i]):
                c.start()

    def swiglu(starts):
        rows = [pl.ds(r0, sub) for r0 in starts]
        gate_up = []
        for rw in rows:
            lo, hi = _unpack_halves(xs_ref[rw, :])
            gate_up.append(_dot(lo.astype(bf16), wgu_ref[0:D_PACK, :])
                           + _dot(hi.astype(bf16), wgu_ref[D_PACK:D_MODEL, :]))
        for rw, gu in zip(rows, gate_up):
            g = gu[:, 0:D_EXPERT]
            h = g * jax.nn.sigmoid(g) * gu[:, D_EXPERT:2 * D_EXPERT]
            ys_ref[rw, :] = _pack_halves(_dot(h.astype(bf16), wd_ref[...]))

    n_sub = nsub_ref[i]

    def pair(k, c):
        r0 = pl.multiple_of(k * (2 * sub), 2 * sub)
        swiglu([r0, r0 + sub])
        return c

    lax.fori_loop(0, n_sub // 2, pair, 0)

    @pl.when(n_sub % 2 == 1)
    def _():
        swiglu([pl.multiple_of((n_sub - 1) * sub, sub)])

    def clear(k, c):
        ys_ref[pl.ds(pl.multiple_of(k * sub, sub), sub), :] = jnp.zeros((sub, D_PACK), ys_ref.dtype)
        return c

    lax.fori_loop(n_sub, jnp.where(i < nused_ref[0], ys_ref.shape[0] // sub, 0), clear, 0)

    if cast_weights:
        @pl.when(i == pl.num_programs(0) - 1)
        def _():
            for c in emit_copies(te_ref[i]):
                c.wait()


def _experts(tiles, xs, weights):
    n_rows, dp = xs.shape
    d = D_MODEL
    tm = EXPERT_TILE
    cast_weights = len(weights) == 3
    row_blk = lambda i, te, fi, ns, sl, nx, nu: (jnp.minimum(i, nu[0] - 1), 0)
    any_spec = pl.BlockSpec(memory_space=pl.ANY)
    ys_shape = jax.ShapeDtypeStruct((n_rows, dp), jnp.uint32)
    ys_spec = pl.BlockSpec((tm, dp), row_blk)
    wgu_bf = ((d, 2 * D_EXPERT), jnp.bfloat16)
    wd_bf = ((D_EXPERT, d), jnp.bfloat16)
    if cast_weights:
        out_shape = (ys_shape, jax.ShapeDtypeStruct((N_EXPERTS,) + wgu_bf[0], wgu_bf[1]),
                     jax.ShapeDtypeStruct((N_EXPERTS,) + wd_bf[0], wd_bf[1]))
        out_specs = (ys_spec, any_spec, any_spec)
        scratch = [pltpu.VMEM((2, d, D_EXPERT), jnp.float32), pltpu.VMEM((2, d, D_EXPERT), jnp.float32),
                   pltpu.VMEM((2, D_EXPERT, d), jnp.float32), pltpu.VMEM(*wgu_bf), pltpu.VMEM(*wd_bf),
                   pltpu.SemaphoreType.DMA((2,)), pltpu.SemaphoreType.DMA]
        weight_bytes = N_EXPERTS * 3 * d * D_EXPERT * (4 + 2)
    else:
        out_shape, out_specs = ys_shape, ys_spec
        scratch = [pltpu.VMEM((2,) + wgu_bf[0], wgu_bf[1]), pltpu.VMEM((2,) + wd_bf[0], wd_bf[1]),
                   pltpu.SemaphoreType.DMA((2,))]
        weight_bytes = N_EXPERTS * 3 * d * D_EXPERT * 2
    grid_spec = pltpu.PrefetchScalarGridSpec(
        num_scalar_prefetch=6,
        grid=(n_rows // tm,),
        in_specs=[pl.BlockSpec((tm, dp), row_blk)] + [any_spec] * len(weights),
        out_specs=out_specs,
        scratch_shapes=scratch,
    )
    return pl.pallas_call(
        functools.partial(_expert_kernel, cast_weights=cast_weights),
        grid_spec=grid_spec,
        out_shape=out_shape,
        compiler_params=pltpu.CompilerParams(dimension_semantics=("arbitrary",),
                                             vmem_limit_bytes=VMEM_LIMIT_BYTES),
        cost_estimate=pl.CostEstimate(flops=2 * n_rows * 3 * d * D_EXPERT, transcendentals=n_rows * D_EXPERT,
                                      bytes_accessed=2 * n_rows * dp * 4 + weight_bytes),
        name="experts",
    )(*tiles, xs, *weights)


def _combine_kernel(x1_ref, slab_ref, y1_ref, y2_ref, g_ref, b_ref, out_ref, *, alpha):
    slab = slab_ref[...]
    w1 = slab[:, COL_W1:COL_W1 + 1]
    w2 = slab[:, COL_W2:COL_W2 + 1]
    lo1, hi1 = _unpack_halves(y1_ref[...])
    lo2, hi2 = _unpack_halves(y2_ref[...])
    moe = jnp.concatenate([w1 * lo1 + w2 * lo2, w1 * hi1 + w2 * hi2], axis=1)
    out_ref[...] = _layer_norm(alpha * x1_ref[...] + moe, g_ref[...], b_ref[...])


def _combine(x1, slab, yg, g, b, alpha):
    n, d = x1.shape
    tt = TOKEN_TILE
    return pl.pallas_call(
        functools.partial(_combine_kernel, alpha=alpha),
        grid=(n // tt,),
        in_specs=[pl.BlockSpec((tt, d), lambda i: (i, 0)),
                  pl.BlockSpec((tt, LANES), lambda i: (i, 0)),
                  pl.BlockSpec((None, tt, D_PACK), lambda i: (0, i, 0)),
                  pl.BlockSpec((None, tt, D_PACK), lambda i: (1, i, 0)),
                  pl.BlockSpec((1, d), lambda i: (0, 0)),
                  pl.BlockSpec((1, d), lambda i: (0, 0))],
        out_specs=pl.BlockSpec((tt, d), lambda i: (i, 0)),
        out_shape=jax.ShapeDtypeStruct((n, d), jnp.float32),
        compiler_params=pltpu.CompilerParams(dimension_semantics=("arbitrary",),
                                             vmem_limit_bytes=VMEM_LIMIT_BYTES),
        cost_estimate=pl.CostEstimate(flops=12 * n * d, transcendentals=n,
                                      bytes_accessed=n * (2 * d * 4 + 2 * D_PACK * 4 + LANES * 4)),
        name="combine",
    )(x1, slab, yg, yg, g, b)


def _row(a):
    return a.reshape(1, -1)


def _mixer_params(w_in, b_in, conv_w, conv_b, sg_ln_g, sg_ln_b, w_s, b_s, w_o, b_o, ln1_g, ln1_b,
                  w_rc, b_rc, w_rf, b_rf):
    bf16 = jnp.bfloat16
    ws_pairs = w_s.reshape(N_SG_HEADS // 2, 2 * CHUNK, CHUNK).astype(bf16)
    bs_full = jnp.repeat(b_s.T, SG_HEAD_DIM, axis=1)
    pad = LANES - N_EXPERTS - N_EXPERT_GROUPS
    w_r = jnp.pad(jnp.concatenate([w_rf, w_rc], axis=1), ((0, 0), (0, pad))).astype(bf16)
    b_r = jnp.pad(jnp.concatenate([b_rf, b_rc]), (0, pad)).reshape(1, LANES)
    return (w_in.astype(bf16), _row(b_in), conv_w, _row(conv_b), _row(sg_ln_g), _row(sg_ln_b),
            ws_pairs, bs_full, w_o.astype(bf16), _row(b_o), _row(ln1_g), _row(ln1_b), w_r, b_r)


def _encoder_layer(x, mixer_params, expert_weights, ln2_g, ln2_b, alpha):
    n = x.shape[0] * x.shape[1]
    x1, xp, slab, route, carry = _mixer(x, mixer_params, alpha)

    tm = EXPERT_TILE
    n_tiles = 2 * n // tm + N_EXPERTS
    cnt = carry[0, 0:N_EXPERTS].astype(jnp.int32)
    tiles_e = jnp.maximum((cnt + tm - 1) // tm, 1 if len(expert_weights) == 3 else 0)
    experts = jnp.arange(N_EXPERTS, dtype=jnp.int32)[:, None]
    up_to = experts.T <= experts
    tile_end = jnp.sum(jnp.where(up_to, tiles_e[None, :], 0), axis=1)
    tile_start = tile_end - tiles_e
    row_off = tile_start * tm

    def sorted_pos(e_row, r_row):
        e = route[e_row].astype(jnp.int32)
        return jnp.sum(jnp.where(e[None, :] == experts, row_off[:, None], 0), axis=0) + route[r_row].astype(jnp.int32)

    pos1 = sorted_pos(COL_E1, COL_R1)
    pos2 = sorted_pos(COL_E2, COL_R2)
    tile_ids = jnp.arange(n_tiles, dtype=jnp.int32)
    n_used = tile_end[N_EXPERTS - 1:N_EXPERTS]
    tile_expert = jnp.sum((tile_end[None, :] <= tile_ids[:, None]).astype(jnp.int32), axis=1)
    last_used = jnp.sum((tile_end <= n_used[0] - 1).astype(jnp.int32))
    tile_expert = jnp.minimum(tile_expert, last_used)
    tile_first = (jnp.any(tile_ids[:, None] == tile_start[None, :], axis=1) & (tile_ids < n_used[0])) | (tile_ids == 0)
    tile_first = tile_first.astype(jnp.int32)
    rows_left = jnp.sum(jnp.where(tile_expert[:, None] == experts.T, (row_off + cnt)[None, :], 0), axis=1) - tile_ids * tm
    tile_nsub = jnp.where(tile_ids < n_used[0], (jnp.clip(rows_left, 0, tm) + EXPERT_SUBTILE - 1) // EXPERT_SUBTILE, 0)

    used = tiles_e > 0
    ordinal = jnp.sum((up_to & used[None, :]).astype(jnp.int32), axis=1) - 1
    e_ids = experts[:, 0]
    later_used = used[None, :] & (e_ids[None, :] > e_ids[:, None])
    next_used = jnp.min(jnp.where(later_used, e_ids[None, :], N_EXPERTS), axis=1)
    next_used = jnp.where(next_used < N_EXPERTS, next_used, -1)
    of_tile = tile_expert[:, None] == experts.T
    tile_slot = jnp.sum(jnp.where(of_tile, ordinal[None, :], 0), axis=1) % 2
    tile_next = jnp.sum(jnp.where(of_tile, next_used[None, :], 0), axis=1)

    tiles = (tile_expert, tile_first, tile_nsub.astype(jnp.int32), tile_slot.astype(jnp.int32),
             tile_next.astype(jnp.int32), n_used)
    xs = _sc_dispatch(pos1, pos2, xp, n_tiles * tm)
    if len(expert_weights) == 3:
        ys, *expert_weights = _experts(tiles, xs, expert_weights)
    else:
        ys = _experts(tiles, xs, expert_weights)
    yg = _sc_gather(pos1, pos2, ys)
    return _combine(x1, slab, yg, _row(ln2_g), _row(ln2_b), alpha).reshape(x.shape), tuple(expert_weights)


def kernel(x_prompt, x_sample, w_in, b_in, conv_w, conv_b, sg_ln_g, sg_ln_b, w_s, b_s, w_o, b_o, ln1_g, ln1_b, w_rc, b_rc, w_rf, b_rf, w_gate, w_up, w_down, ln2_g, ln2_b):
    depth = w_in.shape[0]
    alpha = (2.0 * depth) ** 0.25
    xs = (x_prompt, x_sample)
    for l in range(depth):
        mixer_params = _mixer_params(w_in[l], b_in[l], conv_w[l], conv_b[l], sg_ln_g[l], sg_ln_b[l], w_s[l], b_s[l],
                                     w_o[l], b_o[l], ln1_g[l], ln1_b[l], w_rc[l], b_rc[l], w_rf[l], b_rf[l])
        expert_weights = (w_gate[l], w_up[l], w_down[l])
        outs = [None] * len(xs)
        for k in sorted(range(len(xs)), key=lambda k: -xs[k].shape[0] * xs[k].shape[1]):
            outs[k], expert_weights = _encoder_layer(xs[k], mixer_params, expert_weights, ln2_g[l], ln2_b[l], alpha)
        xs = tuple(outs)
    return xs
```

```python
import functools

import jax
import jax.numpy as jnp
from jax import lax
from jax.experimental import pallas as pl
from jax.experimental.pallas import tpu as pltpu
from jax.experimental.pallas import tpu_sc as plsc

D_MODEL = 1024
D_CONV = 512
D_SG = 512
N_SG_HEADS = 8
SG_HEAD_DIM = D_SG // N_SG_HEADS
CHUNK = 128
N_EXPERT_GROUPS = 4
EXPERTS_PER_GROUP = 8
N_EXPERTS = N_EXPERT_GROUPS * EXPERTS_PER_GROUP
D_EXPERT = 512
LN_EPS = 1e-5
D_PACK = D_MODEL // 2

LANES = 128
SUBLANES = 8
HALO = 16
SEQ_TILE = 512
EXPERT_TILE = 1024
EXPERT_TILE_CAST = 2048
EXPERT_SUBTILE = 256
TOKEN_TILE = 1024
SC_CORES = 2
SC_SUBCORES = 16
SC_WORKERS = SC_CORES * SC_SUBCORES
SC_ROWS = 64
VMEM_LIMIT_BYTES = 56 * 1024 * 1024

COL_E1, COL_E2, COL_W1, COL_W2, COL_R1, COL_R2 = 0, 1, 2, 3, 4, 5
COARSE_OFF = N_EXPERTS


def _dot(a, b):
    return jnp.dot(a, b, preferred_element_type=jnp.float32)


def _gelu_tanh(x):
    return 0.5 * x * (1.0 + jnp.tanh(0.7978845608028654 * (x + 0.044715 * (x * x * x))))


def _layer_norm(x, g, b):
    mu = jnp.mean(x, axis=-1, keepdims=True)
    xc = x - mu
    var = jnp.mean(xc * xc, axis=-1, keepdims=True)
    return xc * lax.rsqrt(var + LN_EPS) * g + b


def _pack_halves(x):
    u32 = jnp.uint32
    lo = lax.bitcast_convert_type(x[:, 0:D_PACK].astype(jnp.bfloat16).astype(jnp.float32), u32)
    hi = lax.bitcast_convert_type(x[:, D_PACK:D_MODEL].astype(jnp.bfloat16).astype(jnp.float32), u32)
    return (hi & u32(0xFFFF0000)) | (lo >> u32(16))


def _unpack_halves(w):
    u32 = jnp.uint32
    lo = lax.bitcast_convert_type(w << u32(16), jnp.float32)
    hi = lax.bitcast_convert_type(w & u32(0xFFFF0000), jnp.float32)
    return lo, hi


def _mixer_kernel(x_ref, x_prev_ref, x_next_ref, w_in_ref, b_in_ref, conv_w_ref, conv_b_ref,
                  sg_g_ref, sg_b_ref, ws_ref, bs_ref, w_o_ref, b_o_ref, ln1_g_ref, ln1_b_ref,
                  w_r_ref, b_r_ref,
                  x1_ref, xp_ref, slab_ref, route_ref, counts_ref,
                  xb_ref, ymix_ref, tri_ref, resid_ref, carry_ref, *, ts, nt, n_tiles, alpha):
    t = pl.program_id(0)
    j = jnp.minimum(t, n_tiles - 1) % nt
    bf16 = jnp.bfloat16

    @pl.when(t == 0)
    def _():
        carry_ref[...] = jnp.zeros_like(carry_ref)
        resid_ref[...] = jnp.zeros_like(resid_ref)
        ri = lax.broadcasted_iota(jnp.int32, (ts, ts), 0)
        ci = lax.broadcasted_iota(jnp.int32, (ts, ts), 1)
        tri_ref[...] = jnp.where(ri > ci, 1.0, 0.0).astype(bf16)

    lane = lax.broadcasted_iota(jnp.int32, (ts, LANES), 1)
    lane_f = lane.astype(jnp.float32)
    neg = jnp.float32(-jnp.inf)
    big = jnp.float32(1e9)
    is_c = (lane >= COARSE_OFF) & (lane < COARSE_OFF + N_EXPERT_GROUPS)

    def proj(lhs, lo, hi):
        return _dot(lhs, w_in_ref[:, lo:hi]) + b_in_ref[:, lo:hi]

    def back_norm():
        x1 = _layer_norm(resid_ref[...], ln1_g_ref[...], ln1_b_ref[...])
        x1_ref[...] = x1
        xp_ref[...] = _pack_halves(x1)
        return x1.astype(bf16)

    def back_route(x1b):
        logits = _dot(x1b, w_r_ref[...]) + b_r_ref[...]
        lc = jnp.where(is_c, logits, neg)
        mx = jnp.max(lc, axis=1, keepdims=True)
        grp = jnp.min(jnp.where(lc == mx, lane_f - COARSE_OFF, big), axis=1, keepdims=True)
        p_grp = 1.0 / jnp.sum(jnp.where(is_c, jnp.exp(logits - mx), 0.0), axis=1, keepdims=True)
        grp_lo = grp * EXPERTS_PER_GROUP
        in_grp = (lane_f >= grp_lo) & (lane_f < grp_lo + EXPERTS_PER_GROUP)
        lf = jnp.where(in_grp, logits, neg)
        v1 = jnp.max(lf, axis=1, keepdims=True)
        e1 = jnp.min(jnp.where(lf == v1, lane_f, big), axis=1, keepdims=True)
        lf2 = jnp.where(lane_f == e1, neg, lf)
        v2 = jnp.max(lf2, axis=1, keepdims=True)
        e2 = jnp.min(jnp.where(lf2 == v2, lane_f, big), axis=1, keepdims=True)
        a = jnp.exp(v2 - v1)
        return e1, e2, p_grp / (1.0 + a), p_grp * a / (1.0 + a)

    def back_rank(e1, e2, w1, w2):
        hit1 = lane_f == e1
        hit2 = lane_f == e2
        onehot = jnp.where(hit1 | hit2, 1.0, 0.0)
        carry = carry_ref[...]
        before = _dot(tri_ref[...], onehot.astype(bf16)) + carry
        r1 = jnp.sum(jnp.where(hit1, before, 0.0), axis=1, keepdims=True)
        r2 = jnp.sum(jnp.where(hit2, before, 0.0), axis=1, keepdims=True)
        carry = jnp.where(t > 0, carry + jnp.sum(onehot, axis=0, keepdims=True), carry)
        carry_ref[...] = carry
        counts_ref[...] = jnp.broadcast_to(carry, counts_ref.shape)
        slab = jnp.where(lane == COL_E1, e1, 0.0)
        slab = jnp.where(lane == COL_E2, e2, slab)
        slab = jnp.where(lane == COL_W1, w1, slab)
        slab = jnp.where(lane == COL_W2, w2, slab)
        slab = jnp.where(lane == COL_R1, r1, slab)
        slab = jnp.where(lane == COL_R2, r2, slab)
        slab_ref[...] = slab
        route_ref[...] = slab.T[0:SUBLANES, :]

    def front_load():
        xb_ref[0:HALO, :] = x_prev_ref[...].astype(bf16)
        xb_ref[HALO:HALO + ts, :] = x_ref[...].astype(bf16)
        xb_ref[HALO + ts:HALO + ts + HALO, :] = x_next_ref[...].astype(bf16)

    def front_conv():
        xe = xb_ref[...]
        g_e = proj(xe, 0, D_CONV) * proj(xe, 2 * D_CONV, 3 * D_CONV)
        row_e = lax.broadcasted_iota(jnp.int32, (ts + 2 * HALO, 1), 0)
        has_prev = jnp.where(j > 0, 1.0, 0.0)
        has_next = jnp.where(j < nt - 1, 1.0, 0.0)
        g_e = g_e * jnp.where(row_e < HALO, has_prev, jnp.where(row_e >= HALO + ts, has_next, 1.0))
        conv = (g_e[HALO - 1:HALO - 1 + ts, :] * conv_w_ref[0:1, :] + g_e[HALO:HALO + ts, :] * conv_w_ref[1:2, :]
                + g_e[HALO + 1:HALO + 1 + ts, :] * conv_w_ref[2:3, :] + conv_b_ref[...])
        y_a = proj(xb_ref[HALO:HALO + ts, :], D_CONV, 2 * D_CONV) * conv
        ymix_ref[:, 0:D_CONV] = y_a.astype(bf16)

    def front_gate_proj():
        xm = xb_ref[HALO:HALO + ts, :]
        return proj(xm, 3 * D_CONV, 3 * D_CONV + D_SG), proj(xm, 3 * D_CONV + D_SG, 3 * D_CONV + 2 * D_SG)

    def front_gate(u_pre, v_pre):
        u = _gelu_tanh(u_pre)
        v = _gelu_tanh(v_pre)
        v_ln = _layer_norm(v, sg_g_ref[...], sg_b_ref[...]).astype(bf16)
        first_head = lax.broadcasted_iota(jnp.int32, (CHUNK, LANES), 1) < SG_HEAD_DIM
        for q0 in range(0, ts, 2 * CHUNK):
            q1 = q0 + CHUNK
            for hp in range(N_SG_HEADS // 2):
                c0 = hp * LANES
                rhs = jnp.concatenate([v_ln[q0:q0 + CHUNK, c0:c0 + LANES], v_ln[q1:q1 + CHUNK, c0:c0 + LANES]], axis=1)
                res = _dot(ws_ref[hp], rhs)
                bias = bs_ref[:, c0:c0 + LANES]
                m0 = jnp.where(first_head, res[0:CHUNK, 0:LANES], res[CHUNK:2 * CHUNK, 0:LANES]) + bias
                m1 = jnp.where(first_head, res[0:CHUNK, LANES:2 * LANES], res[CHUNK:2 * CHUNK, LANES:2 * LANES]) + bias
                ymix_ref[q0:q0 + CHUNK, D_CONV + c0:D_CONV + c0 + LANES] = (u[q0:q0 + CHUNK, c0:c0 + LANES] * m0).astype(bf16)
                ymix_ref[q1:q1 + CHUNK, D_CONV + c0:D_CONV + c0 + LANES] = (u[q1:q1 + CHUNK, c0:c0 + LANES] * m1).astype(bf16)

    def front_out_a():
        resid_ref[...] = alpha * x_ref[...] + b_o_ref[...] + _dot(ymix_ref[:, 0:D_CONV], w_o_ref[0:D_CONV, :])

    def front_out_b():
        resid_ref[...] += _dot(ymix_ref[:, D_CONV:D_MODEL], w_o_ref[D_CONV:D_MODEL, :])

    front_load()
    gate_pre = front_gate_proj()
    x1b = back_norm()
    routing = back_route(x1b)
    front_conv()
    back_rank(*routing)
    front_out_a()
    front_gate(*gate_pre)
    front_out_b()


def _mixer(x, p, alpha):
    nb, s, d = x.shape
    ts = SEQ_TILE
    nt = s // ts
    hb = ts // HALO
    n_tiles = nb * nt
    n_total = n_tiles * ts

    front = lambda t: jnp.minimum(t, n_tiles - 1)
    back = lambda t: jnp.maximum(t - 1, 0)
    cur = lambda t: (front(t) // nt, front(t) % nt, 0)
    prv = lambda t: (front(t) // nt, jnp.maximum((front(t) % nt) * hb - 1, 0), 0)
    nxt = lambda t: (front(t) // nt, jnp.minimum((front(t) % nt + 1) * hb, s // HALO - 1), 0)
    full = lambda a: pl.BlockSpec(a.shape, lambda t: (0,) * a.ndim)
    in_specs = [pl.BlockSpec((None, ts, d), cur), pl.BlockSpec((None, HALO, d), prv),
                pl.BlockSpec((None, HALO, d), nxt)] + [full(a) for a in p]
    out_shape = (jax.ShapeDtypeStruct((n_total, d), jnp.float32),
                 jax.ShapeDtypeStruct((n_total, D_PACK), jnp.uint32),
                 jax.ShapeDtypeStruct((n_total, LANES), jnp.float32),
                 jax.ShapeDtypeStruct((SUBLANES, n_total), jnp.float32),
                 jax.ShapeDtypeStruct((SUBLANES, LANES), jnp.float32))
    row_blk = lambda t: (back(t), 0)
    out_specs = (pl.BlockSpec((ts, d), row_blk),
                 pl.BlockSpec((ts, D_PACK), row_blk),
                 pl.BlockSpec((ts, LANES), row_blk),
                 pl.BlockSpec((SUBLANES, ts), lambda t: (0, back(t))),
                 pl.BlockSpec((SUBLANES, LANES), lambda t: (0, 0)))
    return pl.pallas_call(
        functools.partial(_mixer_kernel, ts=ts, nt=nt, n_tiles=n_tiles, alpha=alpha),
        grid=(n_tiles + 1,),
        in_specs=in_specs,
        out_specs=out_specs,
        out_shape=out_shape,
        scratch_shapes=[pltpu.VMEM((ts + 2 * HALO, d), jnp.bfloat16),
                        pltpu.VMEM((ts, d), jnp.bfloat16),
                        pltpu.VMEM((ts, ts), jnp.bfloat16),
                        pltpu.VMEM((ts, d), jnp.float32),
                        pltpu.VMEM((1, LANES), jnp.float32)],
        compiler_params=pltpu.CompilerParams(dimension_semantics=("arbitrary",),
                                             vmem_limit_bytes=VMEM_LIMIT_BYTES),
        cost_estimate=pl.CostEstimate(
            flops=2 * n_total * (d * (3 * D_CONV + 2 * D_SG) + CHUNK * D_SG + d * d + d * LANES + ts * LANES),
            transcendentals=n_total * (2 * D_SG + 2 * LANES),
            bytes_accessed=n_total * (2 * d * 4 + D_PACK * 4 + LANES * 4 + SUBLANES * 4)),
        name="mixer",
    )(x, x, x, *p)


def _sc_mesh():
    return plsc.VectorSubcoreMesh(core_axis_name="c", subcore_axis_name="s",
                                  num_cores=SC_CORES, num_subcores=SC_SUBCORES)


def _sc_worker_base(per_worker):
    return (lax.axis_index("s") * SC_CORES + lax.axis_index("c")) * per_worker


def _sc_dispatch(pos1, pos2, xp, n_rows):
    n, dp = xp.shape
    per_worker = n // SC_WORKERS
    steps = per_worker // SC_ROWS

    def body(pos1_hbm, pos2_hbm, xp_hbm, xs_hbm, idx1_v, idx2_v, rows_v, sem):
        base = _sc_worker_base(per_worker)

        @pl.loop(0, steps)
        def _(k):
            off = pl.multiple_of(base + k * SC_ROWS, SC_ROWS)
            pltpu.sync_copy(pos1_hbm.at[pl.ds(off, SC_ROWS)], idx1_v)
            pltpu.sync_copy(pos2_hbm.at[pl.ds(off, SC_ROWS)], idx2_v)
            pltpu.sync_copy(xp_hbm.at[pl.ds(off, SC_ROWS)], rows_v)
            c1 = pltpu.async_copy(rows_v, xs_hbm.at[idx1_v], sem)
            c2 = pltpu.async_copy(rows_v, xs_hbm.at[idx2_v], sem)
            c1.wait()
            c2.wait()

    return pl.kernel(
        body,
        out_type=jax.ShapeDtypeStruct((n_rows, dp), xp.dtype),
        mesh=_sc_mesh(),
        scratch_types=[pltpu.VMEM((SC_ROWS,), jnp.int32), pltpu.VMEM((SC_ROWS,), jnp.int32),
                       pltpu.VMEM((SC_ROWS, dp), xp.dtype), pltpu.SemaphoreType.DMA],
        compiler_params=pltpu.CompilerParams(use_tc_tiling_on_sc=True),
        cost_estimate=pl.CostEstimate(flops=0, transcendentals=0, bytes_accessed=3 * n * dp * 4 + 2 * n * 4),
        name="sc_dispatch",
    )(pos1, pos2, xp)


def _sc_gather(pos1, pos2, ys):
    n = pos1.shape[0]
    dp = ys.shape[1]
    per_worker = n // SC_WORKERS
    steps = per_worker // SC_ROWS

    def body(pos1_hbm, pos2_hbm, ys_hbm, out_hbm, idx_v, rows_v, sem):
        base = _sc_worker_base(per_worker)

        @pl.loop(0, steps)
        def _(k):
            off = pl.multiple_of(base + k * SC_ROWS, SC_ROWS)
            for slot, pos_hbm in enumerate((pos1_hbm, pos2_hbm)):
                pltpu.sync_copy(pos_hbm.at[pl.ds(off, SC_ROWS)], idx_v)
                pltpu.async_copy(ys_hbm.at[idx_v], rows_v, sem).wait()
                pltpu.sync_copy(rows_v, out_hbm.at[slot, pl.ds(off, SC_ROWS)])

    return pl.kernel(
        body,
        out_type=jax.ShapeDtypeStruct((2, n, dp), ys.dtype),
        mesh=_sc_mesh(),
        scratch_types=[pltpu.VMEM((SC_ROWS,), jnp.int32), pltpu.VMEM((SC_ROWS, dp), ys.dtype),
                       pltpu.SemaphoreType.DMA],
        compiler_params=pltpu.CompilerParams(use_tc_tiling_on_sc=True),
        cost_estimate=pl.CostEstimate(flops=0, transcendentals=0, bytes_accessed=4 * n * dp * 4 + 2 * n * 4),
        name="sc_gather",
    )(pos1, pos2, ys)


def _expert_kernel(te_ref, first_ref, nsub_ref, slot_ref, next_ref, nused_ref, xs_ref, *refs, cast_weights):
    i = pl.program_id(0)
    bf16 = jnp.bfloat16
    sub = EXPERT_SUBTILE
    slot = slot_ref[i]

    if cast_weights:
        (wg_hbm, wu_hbm, wd_hbm, ys_ref, wgu_out, wd_out,
         wg_buf, wu_buf, wd_buf, wgu_bf_ref, wd_bf_ref, sem, out_sem) = refs
        wgu_ref, wd_ref = wgu_bf_ref, wd_bf_ref

        def weight_copies(expert, s):
            return (pltpu.make_async_copy(wg_hbm.at[expert], wg_buf.at[s], sem.at[s]),
                    pltpu.make_async_copy(wu_hbm.at[expert], wu_buf.at[s], sem.at[s]),
                    pltpu.make_async_copy(wd_hbm.at[expert], wd_buf.at[s], sem.at[s]))

        def emit_copies(expert):
            return (pltpu.make_async_copy(wgu_bf_ref, wgu_out.at[expert], out_sem),
                    pltpu.make_async_copy(wd_bf_ref, wd_out.at[expert], out_sem))
    else:
        wgu_hbm, wd_hbm, ys_ref, wgu_buf, wd_buf, sem = refs
        wgu_ref, wd_ref = wgu_buf.at[slot], wd_buf.at[slot]

        def weight_copies(expert, s):
            return (pltpu.make_async_copy(wgu_hbm.at[expert], wgu_buf.at[s], sem.at[s]),
                    pltpu.make_async_copy(wd_hbm.at[expert], wd_buf.at[s], sem.at[s]))

    @pl.when(first_ref[i] == 1)
    def _():
        @pl.when(i == 0)
        def _():
            for c in weight_copies(te_ref[i], slot):
                c.start()

        for c in weight_copies(te_ref[i], slot):
            c.wait()

        @pl.when(next_ref[i] >= 0)
        def _():
            for c in weight_copies(next_ref[i], 1 - slot):
                c.start()

        if cast_weights:
            @pl.when(i > 0)
            def _():
                for c in emit_copies(te_ref[i]):
                    c.wait()

            wgu_bf_ref[:, 0:D_EXPERT] = wg_buf[slot].astype(bf16)
            wgu_bf_ref[:, D_EXPERT:2 * D_EXPERT] = wu_buf[slot].astype(bf16)
            wd_bf_ref[...] = wd_buf[slot].astype(bf16)
            for c in emit_copies(te_ref[i]):
                c.start()

    def swiglu(starts):
        rows = [pl.ds(r0, sub) for r0 in starts]
        gate_up = []
        for rw in rows:
            lo, hi = _unpack_halves(xs_ref[rw, :])
            gate_up.append(_dot(lo.astype(bf16), wgu_ref[0:D_PACK, :])
                           + _dot(hi.astype(bf16), wgu_ref[D_PACK:D_MODEL, :]))
        for rw, gu in zip(rows, gate_up):
            g = gu[:, 0:D_EXPERT]
            h = g * jax.nn.sigmoid(g) * gu[:, D_EXPERT:2 * D_EXPERT]
            ys_ref[rw, :] = _pack_halves(_dot(h.astype(bf16), wd_ref[...]))

    n_sub = nsub_ref[i]

    def pair(k, c):
        r0 = pl.multiple_of(k * (2 * sub), 2 * sub)
        swiglu([r0, r0 + sub])
        return c

    lax.fori_loop(0, n_sub // 2, pair, 0)

    @pl.when(n_sub % 2 == 1)
    def _():
        swiglu([pl.multiple_of((n_sub - 1) * sub, sub)])

    def clear(k, c):
        ys_ref[pl.ds(pl.multiple_of(k * sub, sub), sub), :] = jnp.zeros((sub, D_PACK), ys_ref.dtype)
        return c

    lax.fori_loop(n_sub, jnp.where(i < nused_ref[0], ys_ref.shape[0] // sub, 0), clear, 0)

    if cast_weights:
        @pl.when(i == pl.num_programs(0) - 1)
        def _():
            for c in emit_copies(te_ref[i]):
                c.wait()


def _experts(tiles, xs, weights):
    n_rows, dp = xs.shape
    d = D_MODEL
    cast_weights = len(weights) == 3
    tm = EXPERT_TILE_CAST if cast_weights else EXPERT_TILE
    row_blk = lambda i, te, fi, ns, sl, nx, nu: (jnp.minimum(i, nu[0] - 1), 0)
    any_spec = pl.BlockSpec(memory_space=pl.ANY)
    ys_shape = jax.ShapeDtypeStruct((n_rows, dp), jnp.uint32)
    ys_spec = pl.BlockSpec((tm, dp), row_blk)
    wgu_bf = ((d, 2 * D_EXPERT), jnp.bfloat16)
    wd_bf = ((D_EXPERT, d), jnp.bfloat16)
    if cast_weights:
        out_shape = (ys_shape, jax.ShapeDtypeStruct((N_EXPERTS,) + wgu_bf[0], wgu_bf[1]),
                     jax.ShapeDtypeStruct((N_EXPERTS,) + wd_bf[0], wd_bf[1]))
        out_specs = (ys_spec, any_spec, any_spec)
        scratch = [pltpu.VMEM((2, d, D_EXPERT), jnp.float32), pltpu.VMEM((2, d, D_EXPERT), jnp.float32),
                   pltpu.VMEM((2, D_EXPERT, d), jnp.float32), pltpu.VMEM(*wgu_bf), pltpu.VMEM(*wd_bf),
                   pltpu.SemaphoreType.DMA((2,)), pltpu.SemaphoreType.DMA]
        weight_bytes = N_EXPERTS * 3 * d * D_EXPERT * (4 + 2)
    else:
        out_shape, out_specs = ys_shape, ys_spec
        scratch = [pltpu.VMEM((2,) + wgu_bf[0], wgu_bf[1]), pltpu.VMEM((2,) + wd_bf[0], wd_bf[1]),
                   pltpu.SemaphoreType.DMA((2,))]
        weight_bytes = N_EXPERTS * 3 * d * D_EXPERT * 2
    grid_spec = pltpu.PrefetchScalarGridSpec(
        num_scalar_prefetch=6,
        grid=(n_rows // tm,),
        in_specs=[pl.BlockSpec((tm, dp), row_blk)] + [any_spec] * len(weights),
        out_specs=out_specs,
        scratch_shapes=scratch,
    )
    return pl.pallas_call(
        functools.partial(_expert_kernel, cast_weights=cast_weights),
        grid_spec=grid_spec,
        out_shape=out_shape,
        compiler_params=pltpu.CompilerParams(dimension_semantics=("arbitrary",),
                                             vmem_limit_bytes=VMEM_LIMIT_BYTES),
        cost_estimate=pl.CostEstimate(flops=2 * n_rows * 3 * d * D_EXPERT, transcendentals=n_rows * D_EXPERT,
                                      bytes_accessed=2 * n_rows * dp * 4 + weight_bytes),
        name="experts",
    )(*tiles, xs, *weights)


def _combine_kernel(x1_ref, slab_ref, y1_ref, y2_ref, g_ref, b_ref, out_ref, *, alpha):
    slab = slab_ref[...]
    w1 = slab[:, COL_W1:COL_W1 + 1]
    w2 = slab[:, COL_W2:COL_W2 + 1]
    lo1, hi1 = _unpack_halves(y1_ref[...])
    lo2, hi2 = _unpack_halves(y2_ref[...])
    moe = jnp.concatenate([w1 * lo1 + w2 * lo2, w1 * hi1 + w2 * hi2], axis=1)
    out_ref[...] = _layer_norm(alpha * x1_ref[...] + moe, g_ref[...], b_ref[...])


def _combine(x1, slab, yg, g, b, alpha):
    n, d = x1.shape
    tt = TOKEN_TILE
    return pl.pallas_call(
        functools.partial(_combine_kernel, alpha=alpha),
        grid=(n // tt,),
        in_specs=[pl.BlockSpec((tt, d), lambda i: (i, 0)),
                  pl.BlockSpec((tt, LANES), lambda i: (i, 0)),
                  pl.BlockSpec((None, tt, D_PACK), lambda i: (0, i, 0)),
                  pl.BlockSpec((None, tt, D_PACK), lambda i: (1, i, 0)),
                  pl.BlockSpec((1, d), lambda i: (0, 0)),
                  pl.BlockSpec((1, d), lambda i: (0, 0))],
        out_specs=pl.BlockSpec((tt, d), lambda i: (i, 0)),
        out_shape=jax.ShapeDtypeStruct((n, d), jnp.float32),
        compiler_params=pltpu.CompilerParams(dimension_semantics=("arbitrary",),
                                             vmem_limit_bytes=VMEM_LIMIT_BYTES),
        cost_estimate=pl.CostEstimate(flops=12 * n * d, transcendentals=n,
                                      bytes_accessed=n * (2 * d * 4 + 2 * D_PACK * 4 + LANES * 4)),
        name="combine",
    )(x1, slab, yg, yg, g, b)


def _row(a):
    return a.reshape(1, -1)


def _mixer_params(w_in, b_in, conv_w, conv_b, sg_ln_g, sg_ln_b, w_s, b_s, w_o, b_o, ln1_g, ln1_b,
                  w_rc, b_rc, w_rf, b_rf):
    bf16 = jnp.bfloat16
    ws_pairs = w_s.reshape(N_SG_HEADS // 2, 2 * CHUNK, CHUNK).astype(bf16)
    bs_full = jnp.repeat(b_s.T, SG_HEAD_DIM, axis=1)
    pad = LANES - N_EXPERTS - N_EXPERT_GROUPS
    w_r = jnp.pad(jnp.concatenate([w_rf, w_rc], axis=1), ((0, 0), (0, pad))).astype(bf16)
    b_r = jnp.pad(jnp.concatenate([b_rf, b_rc]), (0, pad)).reshape(1, LANES)
    return (w_in.astype(bf16), _row(b_in), conv_w, _row(conv_b), _row(sg_ln_g), _row(sg_ln_b),
            ws_pairs, bs_full, w_o.astype(bf16), _row(b_o), _row(ln1_g), _row(ln1_b), w_r, b_r)


def _encoder_layer(x, mixer_params, expert_weights, ln2_g, ln2_b, alpha):
    n = x.shape[0] * x.shape[1]
    x1, xp, slab, route, carry = _mixer(x, mixer_params, alpha)

    tm = EXPERT_TILE_CAST if len(expert_weights) == 3 else EXPERT_TILE
    n_tiles = 2 * n // tm + N_EXPERTS
    cnt = carry[0, 0:N_EXPERTS].astype(jnp.int32)
    tiles_e = jnp.maximum((cnt + tm - 1) // tm, 1 if len(expert_weights) == 3 else 0)
    experts = jnp.arange(N_EXPERTS, dtype=jnp.int32)[:, None]
    up_to = experts.T <= experts
    tile_end = jnp.sum(jnp.where(up_to, tiles_e[None, :], 0), axis=1)
    tile_start = tile_end - tiles_e
    row_off = tile_start * tm

    def sorted_pos(e_row, r_row):
        e = route[e_row].astype(jnp.int32)
        return jnp.sum(jnp.where(e[None, :] == experts, row_off[:, None], 0), axis=0) + route[r_row].astype(jnp.int32)

    pos1 = sorted_pos(COL_E1, COL_R1)
    pos2 = sorted_pos(COL_E2, COL_R2)
    tile_ids = jnp.arange(n_tiles, dtype=jnp.int32)
    n_used = tile_end[N_EXPERTS - 1:N_EXPERTS]
    tile_expert = jnp.sum((tile_end[None, :] <= tile_ids[:, None]).astype(jnp.int32), axis=1)
    last_used = jnp.sum((tile_end <= n_used[0] - 1).astype(jnp.int32))
    tile_expert = jnp.minimum(tile_expert, last_used)
    tile_first = (jnp.any(tile_ids[:, None] == tile_start[None, :], axis=1) & (tile_ids < n_used[0])) | (tile_ids == 0)
    tile_first = tile_first.astype(jnp.int32)
    rows_left = jnp.sum(jnp.where(tile_expert[:, None] == experts.T, (row_off + cnt)[None, :], 0), axis=1) - tile_ids * tm
    tile_nsub = jnp.where(tile_ids < n_used[0], (jnp.clip(rows_left, 0, tm) + EXPERT_SUBTILE - 1) // EXPERT_SUBTILE, 0)

    used = tiles_e > 0
    ordinal = jnp.sum((up_to & used[None, :]).astype(jnp.int32), axis=1) - 1
    e_ids = experts[:, 0]
    later_used = used[None, :] & (e_ids[None, :] > e_ids[:, None])
    next_used = jnp.min(jnp.where(later_used, e_ids[None, :], N_EXPERTS), axis=1)
    next_used = jnp.where(next_used < N_EXPERTS, next_used, -1)
    of_tile = tile_expert[:, None] == experts.T
    tile_slot = jnp.sum(jnp.where(of_tile, ordinal[None, :], 0), axis=1) % 2
    tile_next = jnp.sum(jnp.where(of_tile, next_used[None, :], 0), axis=1)

    tiles = (tile_expert, tile_first, tile_nsub.astype(jnp.int32), tile_slot.astype(jnp.int32),
             tile_next.astype(jnp.int32), n_used)
    xs = _sc_dispatch(pos1, pos2, xp, n_tiles * tm)
    if len(expert_weights) == 3:
        ys, *expert_weights = _experts(tiles, xs, expert_weights)
    else:
        ys = _experts(tiles, xs, expert_weights)
    yg = _sc_gather(pos1, pos2, ys)
    return _combine(x1, slab, yg, _row(ln2_g), _row(ln2_b), alpha).reshape(x.shape), tuple(expert_weights)


def kernel(x_prompt, x_sample, w_in, b_in, conv_w, conv_b, sg_ln_g, sg_ln_b, w_s, b_s, w_o, b_o, ln1_g, ln1_b, w_rc, b_rc, w_rf, b_rf, w_gate, w_up, w_down, ln2_g, ln2_b):
    depth = w_in.shape[0]
    alpha = (2.0 * depth) ** 0.25
    xs = (x_prompt, x_sample)
    for l in range(depth):
        mixer_params = _mixer_params(w_in[l], b_in[l], conv_w[l], conv_b[l], sg_ln_g[l], sg_ln_b[l], w_s[l], b_s[l],
                                     w_o[l], b_o[l], ln1_g[l], ln1_b[l], w_rc[l], b_rc[l], w_rf[l], b_rf[l])
        expert_weights = (w_gate[l], w_up[l], w_down[l])
        outs = [None] * len(xs)
        for k in sorted(range(len(xs)), key=lambda k: -xs[k].shape[0] * xs[k].shape[1]):
            outs[k], expert_weights = _encoder_layer(xs[k], mixer_params, expert_weights, ln2_g[l], ln2_b[l], alpha)
        xs = tuple(outs)
    return xs
```

```python
import functools

import jax
import jax.numpy as jnp
from jax import lax
from jax.experimental import pallas as pl
from jax.experimental.pallas import tpu as pltpu
from jax.experimental.pallas import tpu_sc as plsc

D_MODEL = 1024
D_CONV = 512
D_SG = 512
N_SG_HEADS = 8
SG_HEAD_DIM = D_SG // N_SG_HEADS
CHUNK = 128
N_EXPERT_GROUPS = 4
EXPERTS_PER_GROUP = 8
N_EXPERTS = N_EXPERT_GROUPS * EXPERTS_PER_GROUP
D_EXPERT = 512
LN_EPS = 1e-5
D_PACK = D_MODEL // 2

LANES = 128
SUBLANES = 8
HALO = 16
SEQ_TILE = 1024
EXPERT_TILE = 1024
EXPERT_SUBTILE = 256
TOKEN_TILE = 1024
SC_CORES = 2
SC_SUBCORES = 16
SC_WORKERS = SC_CORES * SC_SUBCORES
SC_ROWS = 64
VMEM_LIMIT_BYTES = 56 * 1024 * 1024

COL_E1, COL_E2, COL_W1, COL_W2, COL_R1, COL_R2 = 0, 1, 2, 3, 4, 5
COARSE_OFF = N_EXPERTS


def _dot(a, b):
    return jnp.dot(a, b, preferred_element_type=jnp.float32)


def _gelu_tanh(x):
    return 0.5 * x * (1.0 + jnp.tanh(0.7978845608028654 * (x + 0.044715 * (x * x * x))))


def _layer_norm(x, g, b):
    mu = jnp.mean(x, axis=-1, keepdims=True)
    xc = x - mu
    var = jnp.mean(xc * xc, axis=-1, keepdims=True)
    return xc * lax.rsqrt(var + LN_EPS) * g + b


def _pack_halves(x):
    u32 = jnp.uint32
    lo = lax.bitcast_convert_type(x[:, 0:D_PACK].astype(jnp.bfloat16).astype(jnp.float32), u32)
    hi = lax.bitcast_convert_type(x[:, D_PACK:D_MODEL].astype(jnp.bfloat16).astype(jnp.float32), u32)
    return (hi & u32(0xFFFF0000)) | (lo >> u32(16))


def _unpack_halves(w):
    u32 = jnp.uint32
    lo = lax.bitcast_convert_type(w << u32(16), jnp.float32)
    hi = lax.bitcast_convert_type(w & u32(0xFFFF0000), jnp.float32)
    return lo, hi


def _mixer_kernel(x_ref, x_prev_ref, x_next_ref, w_in_ref, b_in_ref, conv_w_ref, conv_b_ref,
                  sg_g_ref, sg_b_ref, ws_ref, bs_ref, w_o_ref, b_o_ref, ln1_g_ref, ln1_b_ref,
                  w_r_ref, b_r_ref,
                  x1_ref, xp_ref, slab_ref, route_ref, counts_ref,
                  xb_ref, ymix_ref, tri_ref, resid_ref, carry_ref, *, ts, nt, n_tiles, alpha):
    t = pl.program_id(0)
    j = jnp.minimum(t, n_tiles - 1) % nt
    bf16 = jnp.bfloat16

    @pl.when(t == 0)
    def _():
        carry_ref[...] = jnp.zeros_like(carry_ref)
        resid_ref[...] = jnp.zeros_like(resid_ref)
        ri = lax.broadcasted_iota(jnp.int32, (ts, ts), 0)
        ci = lax.broadcasted_iota(jnp.int32, (ts, ts), 1)
        tri_ref[...] = jnp.where(ri > ci, 1.0, 0.0).astype(bf16)

    lane = lax.broadcasted_iota(jnp.int32, (ts, LANES), 1)
    lane_f = lane.astype(jnp.float32)
    neg = jnp.float32(-jnp.inf)
    big = jnp.float32(1e9)
    is_c = (lane >= COARSE_OFF) & (lane < COARSE_OFF + N_EXPERT_GROUPS)

    def proj(lhs, lo, hi):
        return _dot(lhs, w_in_ref[:, lo:hi]) + b_in_ref[:, lo:hi]

    def back_norm():
        x1 = _layer_norm(resid_ref[...], ln1_g_ref[...], ln1_b_ref[...])
        x1_ref[...] = x1
        xp_ref[...] = _pack_halves(x1)
        return x1.astype(bf16)

    def back_route(x1b):
        logits = _dot(x1b, w_r_ref[...]) + b_r_ref[...]
        lc = jnp.where(is_c, logits, neg)
        mx = jnp.max(lc, axis=1, keepdims=True)
        grp = jnp.min(jnp.where(lc == mx, lane_f - COARSE_OFF, big), axis=1, keepdims=True)
        p_grp = 1.0 / jnp.sum(jnp.where(is_c, jnp.exp(logits - mx), 0.0), axis=1, keepdims=True)
        grp_lo = grp * EXPERTS_PER_GROUP
        in_grp = (lane_f >= grp_lo) & (lane_f < grp_lo + EXPERTS_PER_GROUP)
        lf = jnp.where(in_grp, logits, neg)
        v1 = jnp.max(lf, axis=1, keepdims=True)
        e1 = jnp.min(jnp.where(lf == v1, lane_f, big), axis=1, keepdims=True)
        lf2 = jnp.where(lane_f == e1, neg, lf)
        v2 = jnp.max(lf2, axis=1, keepdims=True)
        e2 = jnp.min(jnp.where(lf2 == v2, lane_f, big), axis=1, keepdims=True)
        a = jnp.exp(v2 - v1)
        return e1, e2, p_grp / (1.0 + a), p_grp * a / (1.0 + a)

    def back_rank(e1, e2, w1, w2):
        hit1 = lane_f == e1
        hit2 = lane_f == e2
        onehot = jnp.where(hit1 | hit2, 1.0, 0.0)
        carry = carry_ref[...]
        before = _dot(tri_ref[...], onehot.astype(bf16)) + carry
        r1 = jnp.sum(jnp.where(hit1, before, 0.0), axis=1, keepdims=True)
        r2 = jnp.sum(jnp.where(hit2, before, 0.0), axis=1, keepdims=True)
        carry = jnp.where(t > 0, carry + jnp.sum(onehot, axis=0, keepdims=True), carry)
        carry_ref[...] = carry
        counts_ref[...] = jnp.broadcast_to(carry, counts_ref.shape)
        slab = jnp.where(lane == COL_E1, e1, 0.0)
        slab = jnp.where(lane == COL_E2, e2, slab)
        slab = jnp.where(lane == COL_W1, w1, slab)
        slab = jnp.where(lane == COL_W2, w2, slab)
        slab = jnp.where(lane == COL_R1, r1, slab)
        slab = jnp.where(lane == COL_R2, r2, slab)
        slab_ref[...] = slab
        route_ref[...] = slab.T[0:SUBLANES, :]

    def front_load():
        xb_ref[0:HALO, :] = x_prev_ref[...].astype(bf16)
        xb_ref[HALO:HALO + ts, :] = x_ref[...].astype(bf16)
        xb_ref[HALO + ts:HALO + ts + HALO, :] = x_next_ref[...].astype(bf16)

    def front_conv():
        xe = xb_ref[...]
        g_e = proj(xe, 0, D_CONV) * proj(xe, 2 * D_CONV, 3 * D_CONV)
        row_e = lax.broadcasted_iota(jnp.int32, (ts + 2 * HALO, 1), 0)
        has_prev = jnp.where(j > 0, 1.0, 0.0)
        has_next = jnp.where(j < nt - 1, 1.0, 0.0)
        g_e = g_e * jnp.where(row_e < HALO, has_prev, jnp.where(row_e >= HALO + ts, has_next, 1.0))
        conv = (g_e[HALO - 1:HALO - 1 + ts, :] * conv_w_ref[0:1, :] + g_e[HALO:HALO + ts, :] * conv_w_ref[1:2, :]
                + g_e[HALO + 1:HALO + 1 + ts, :] * conv_w_ref[2:3, :] + conv_b_ref[...])
        y_a = proj(xb_ref[HALO:HALO + ts, :], D_CONV, 2 * D_CONV) * conv
        ymix_ref[:, 0:D_CONV] = y_a.astype(bf16)

    def front_gate_proj():
        xm = xb_ref[HALO:HALO + ts, :]
        return proj(xm, 3 * D_CONV, 3 * D_CONV + D_SG), proj(xm, 3 * D_CONV + D_SG, 3 * D_CONV + 2 * D_SG)

    def front_gate(u_pre, v_pre):
        u = _gelu_tanh(u_pre)
        v = _gelu_tanh(v_pre)
        v_ln = _layer_norm(v, sg_g_ref[...], sg_b_ref[...]).astype(bf16)
        first_head = lax.broadcasted_iota(jnp.int32, (CHUNK, LANES), 1) < SG_HEAD_DIM
        for q0 in range(0, ts, 2 * CHUNK):
            q1 = q0 + CHUNK
            for hp in range(N_SG_HEADS // 2):
                c0 = hp * LANES
                rhs = jnp.concatenate([v_ln[q0:q0 + CHUNK, c0:c0 + LANES], v_ln[q1:q1 + CHUNK, c0:c0 + LANES]], axis=1)
                res = _dot(ws_ref[hp], rhs)
                bias = bs_ref[:, c0:c0 + LANES]
                m0 = jnp.where(first_head, res[0:CHUNK, 0:LANES], res[CHUNK:2 * CHUNK, 0:LANES]) + bias
                m1 = jnp.where(first_head, res[0:CHUNK, LANES:2 * LANES], res[CHUNK:2 * CHUNK, LANES:2 * LANES]) + bias
                ymix_ref[q0:q0 + CHUNK, D_CONV + c0:D_CONV + c0 + LANES] = (u[q0:q0 + CHUNK, c0:c0 + LANES] * m0).astype(bf16)
                ymix_ref[q1:q1 + CHUNK, D_CONV + c0:D_CONV + c0 + LANES] = (u[q1:q1 + CHUNK, c0:c0 + LANES] * m1).astype(bf16)

    def front_out_a():
        resid_ref[...] = alpha * x_ref[...] + b_o_ref[...] + _dot(ymix_ref[:, 0:D_CONV], w_o_ref[0:D_CONV, :])

    def front_out_b():
        resid_ref[...] += _dot(ymix_ref[:, D_CONV:D_MODEL], w_o_ref[D_CONV:D_MODEL, :])

    front_load()
    gate_pre = front_gate_proj()
    x1b = back_norm()
    routing = back_route(x1b)
    front_conv()
    back_rank(*routing)
    front_out_a()
    front_gate(*gate_pre)
    front_out_b()


def _mixer(x, p, alpha):
    nb, s, d = x.shape
    ts = SEQ_TILE
    nt = s // ts
    hb = ts // HALO
    n_tiles = nb * nt
    n_total = n_tiles * ts

    front = lambda t: jnp.minimum(t, n_tiles - 1)
    back = lambda t: jnp.maximum(t - 1, 0)
    cur = lambda t: (front(t) // nt, front(t) % nt, 0)
    prv = lambda t: (front(t) // nt, jnp.maximum((front(t) % nt) * hb - 1, 0), 0)
    nxt = lambda t: (front(t) // nt, jnp.minimum((front(t) % nt + 1) * hb, s // HALO - 1), 0)
    full = lambda a: pl.BlockSpec(a.shape, lambda t: (0,) * a.ndim)
    in_specs = [pl.BlockSpec((None, ts, d), cur), pl.BlockSpec((None, HALO, d), prv),
                pl.BlockSpec((None, HALO, d), nxt)] + [full(a) for a in p]
    out_shape = (jax.ShapeDtypeStruct((n_total, d), jnp.float32),
                 jax.ShapeDtypeStruct((n_total, D_PACK), jnp.uint32),
                 jax.ShapeDtypeStruct((n_total, LANES), jnp.float32),
                 jax.ShapeDtypeStruct((SUBLANES, n_total), jnp.float32),
                 jax.ShapeDtypeStruct((SUBLANES, LANES), jnp.float32))
    row_blk = lambda t: (back(t), 0)
    out_specs = (pl.BlockSpec((ts, d), row_blk),
                 pl.BlockSpec((ts, D_PACK), row_blk),
                 pl.BlockSpec((ts, LANES), row_blk),
                 pl.BlockSpec((SUBLANES, ts), lambda t: (0, back(t))),
                 pl.BlockSpec((SUBLANES, LANES), lambda t: (0, 0)))
    return pl.pallas_call(
        functools.partial(_mixer_kernel, ts=ts, nt=nt, n_tiles=n_tiles, alpha=alpha),
        grid=(n_tiles + 1,),
        in_specs=in_specs,
        out_specs=out_specs,
        out_shape=out_shape,
        scratch_shapes=[pltpu.VMEM((ts + 2 * HALO, d), jnp.bfloat16),
                        pltpu.VMEM((ts, d), jnp.bfloat16),
                        pltpu.VMEM((ts, ts), jnp.bfloat16),
                        pltpu.VMEM((ts, d), jnp.float32),
                        pltpu.VMEM((1, LANES), jnp.float32)],
        compiler_params=pltpu.CompilerParams(dimension_semantics=("arbitrary",),
                                             vmem_limit_bytes=VMEM_LIMIT_BYTES),
        cost_estimate=pl.CostEstimate(
            flops=2 * n_total * (d * (3 * D_CONV + 2 * D_SG) + CHUNK * D_SG + d * d + d * LANES + ts * LANES),
            transcendentals=n_total * (2 * D_SG + 2 * LANES),
            bytes_accessed=n_total * (2 * d * 4 + D_PACK * 4 + LANES * 4 + SUBLANES * 4)),
        name="mixer",
    )(x, x, x, *p)


def _sc_mesh():
    return plsc.VectorSubcoreMesh(core_axis_name="c", subcore_axis_name="s",
                                  num_cores=SC_CORES, num_subcores=SC_SUBCORES)


def _sc_worker_base(per_worker):
    return (lax.axis_index("s") * SC_CORES + lax.axis_index("c")) * per_worker


def _sc_dispatch(pos1, pos2, xp, n_rows):
    n, dp = xp.shape
    per_worker = n // SC_WORKERS
    steps = per_worker // SC_ROWS

    def body(pos1_hbm, pos2_hbm, xp_hbm, xs_hbm, idx1_v, idx2_v, rows_v, sem):
        base = _sc_worker_base(per_worker)

        @pl.loop(0, steps)
        def _(k):
            off = pl.multiple_of(base + k * SC_ROWS, SC_ROWS)
            pltpu.sync_copy(pos1_hbm.at[pl.ds(off, SC_ROWS)], idx1_v)
            pltpu.sync_copy(pos2_hbm.at[pl.ds(off, SC_ROWS)], idx2_v)
            pltpu.sync_copy(xp_hbm.at[pl.ds(off, SC_ROWS)], rows_v)
            c1 = pltpu.async_copy(rows_v, xs_hbm.at[idx1_v], sem)
            c2 = pltpu.async_copy(rows_v, xs_hbm.at[idx2_v], sem)
            c1.wait()
            c2.wait()

    return pl.kernel(
        body,
        out_type=jax.ShapeDtypeStruct((n_rows, dp), xp.dtype),
        mesh=_sc_mesh(),
        scratch_types=[pltpu.VMEM((SC_ROWS,), jnp.int32), pltpu.VMEM((SC_ROWS,), jnp.int32),
                       pltpu.VMEM((SC_ROWS, dp), xp.dtype), pltpu.SemaphoreType.DMA],
        compiler_params=pltpu.CompilerParams(use_tc_tiling_on_sc=True),
        cost_estimate=pl.CostEstimate(flops=0, transcendentals=0, bytes_accessed=3 * n * dp * 4 + 2 * n * 4),
        name="sc_dispatch",
    )(pos1, pos2, xp)


def _sc_gather(pos1, pos2, ys):
    n = pos1.shape[0]
    dp = ys.shape[1]
    per_worker = n // SC_WORKERS
    steps = per_worker // SC_ROWS

    def body(pos1_hbm, pos2_hbm, ys_hbm, out_hbm, idx_v, rows_v, sem):
        base = _sc_worker_base(per_worker)

        @pl.loop(0, steps)
        def _(k):
            off = pl.multiple_of(base + k * SC_ROWS, SC_ROWS)
            for slot, pos_hbm in enumerate((pos1_hbm, pos2_hbm)):
                pltpu.sync_copy(pos_hbm.at[pl.ds(off, SC_ROWS)], idx_v)
                pltpu.async_copy(ys_hbm.at[idx_v], rows_v, sem).wait()
                pltpu.sync_copy(rows_v, out_hbm.at[slot, pl.ds(off, SC_ROWS)])

    return pl.kernel(
        body,
        out_type=jax.ShapeDtypeStruct((2, n, dp), ys.dtype),
        mesh=_sc_mesh(),
        scratch_types=[pltpu.VMEM((SC_ROWS,), jnp.int32), pltpu.VMEM((SC_ROWS, dp), ys.dtype),
                       pltpu.SemaphoreType.DMA],
        compiler_params=pltpu.CompilerParams(use_tc_tiling_on_sc=True),
        cost_estimate=pl.CostEstimate(flops=0, transcendentals=0, bytes_accessed=4 * n * dp * 4 + 2 * n * 4),
        name="sc_gather",
    )(pos1, pos2, ys)


def _expert_kernel(te_ref, first_ref, nsub_ref, slot_ref, next_ref, nused_ref, xs_ref, *refs, cast_weights):
    i = pl.program_id(0)
    bf16 = jnp.bfloat16
    sub = EXPERT_SUBTILE
    slot = slot_ref[i]

    if cast_weights:
        (wg_hbm, wu_hbm, wd_hbm, ys_ref, wgu_out, wd_out,
         wg_buf, wu_buf, wd_buf, wgu_bf_ref, wd_bf_ref, sem, out_sem) = refs
        wgu_ref, wd_ref = wgu_bf_ref, wd_bf_ref

        def weight_copies(expert, s):
            return (pltpu.make_async_copy(wg_hbm.at[expert], wg_buf.at[s], sem.at[s]),
                    pltpu.make_async_copy(wu_hbm.at[expert], wu_buf.at[s], sem.at[s]),
                    pltpu.make_async_copy(wd_hbm.at[expert], wd_buf.at[s], sem.at[s]))

        def emit_copies(expert):
            return (pltpu.make_async_copy(wgu_bf_ref, wgu_out.at[expert], out_sem),
                    pltpu.make_async_copy(wd_bf_ref, wd_out.at[expert], out_sem))
    else:
        wgu_hbm, wd_hbm, ys_ref, wgu_buf, wd_buf, sem = refs
        wgu_ref, wd_ref = wgu_buf.at[slot], wd_buf.at[slot]

        def weight_copies(expert, s):
            return (pltpu.make_async_copy(wgu_hbm.at[expert], wgu_buf.at[s], sem.at[s]),
                    pltpu.make_async_copy(wd_hbm.at[expert], wd_buf.at[s], sem.at[s]))

    @pl.when(first_ref[i] == 1)
    def _():
        @pl.when(i == 0)
        def _():
            for c in weight_copies(te_ref[i], slot):
                c.start()

        for c in weight_copies(te_ref[i], slot):
            c.wait()

        @pl.when(next_ref[i] >= 0)
        def _():
            for c in weight_copies(next_ref[i], 1 - slot):
                c.start()

        if cast_weights:
            @pl.when(i > 0)
            def _():
                for c in emit_copies(te_ref[i]):
                    c.wait()

            wgu_bf_ref[:, 0:D_EXPERT] = wg_buf[slot].astype(bf16)
            wgu_bf_ref[:, D_EXPERT:2 * D_EXPERT] = wu_buf[slot].astype(bf16)
            wd_bf_ref[...] = wd_buf[slot].astype(bf16)
            for c in emit_copies(te_ref[i]):
                c.start()

    def swiglu(starts):
        rows = [pl.ds(r0, sub) for r0 in starts]
        gate_up = []
        for rw in rows:
            lo, hi = _unpack_halves(xs_ref[rw, :])
            gate_up.append(_dot(lo.astype(bf16), wgu_ref[0:D_PACK, :])
                           + _dot(hi.astype(bf16), wgu_ref[D_PACK:D_MODEL, :]))
        for rw, gu in zip(rows, gate_up):
            g = gu[:, 0:D_EXPERT]
            h = g * jax.nn.sigmoid(g) * gu[:, D_EXPERT:2 * D_EXPERT]
            ys_ref[rw, :] = _pack_halves(_dot(h.astype(bf16), wd_ref[...]))

    n_sub = nsub_ref[i]

    def pair(k, c):
        r0 = pl.multiple_of(k * (2 * sub), 2 * sub)
        swiglu([r0, r0 + sub])
        return c

    lax.fori_loop(0, n_sub // 2, pair, 0)

    @pl.when(n_sub % 2 == 1)
    def _():
        swiglu([pl.multiple_of((n_sub - 1) * sub, sub)])

    def clear(k, c):
        ys_ref[pl.ds(pl.multiple_of(k * sub, sub), sub), :] = jnp.zeros((sub, D_PACK), ys_ref.dtype)
        return c

    lax.fori_loop(n_sub, jnp.where(i < nused_ref[0], ys_ref.shape[0] // sub, 0), clear, 0)

    if cast_weights:
        @pl.when(i == pl.num_programs(0) - 1)
        def _():
            for c in emit_copies(te_ref[i]):
                c.wait()


def _experts(tiles, xs, weights):
    n_rows, dp = xs.shape
    d = D_MODEL
    tm = EXPERT_TILE
    cast_weights = len(weights) == 3
    row_blk = lambda i, te, fi, ns, sl, nx, nu: (jnp.minimum(i, nu[0] - 1), 0)
    any_spec = pl.BlockSpec(memory_space=pl.ANY)
    ys_shape = jax.ShapeDtypeStruct((n_rows, dp), jnp.uint32)
    ys_spec = pl.BlockSpec((tm, dp), row_blk)
    wgu_bf = ((d, 2 * D_EXPERT), jnp.bfloat16)
    wd_bf = ((D_EXPERT, d), jnp.bfloat16)
    if cast_weights:
        out_shape = (ys_shape, jax.ShapeDtypeStruct((N_EXPERTS,) + wgu_bf[0], wgu_bf[1]),
                     jax.ShapeDtypeStruct((N_EXPERTS,) + wd_bf[0], wd_bf[1]))
        out_specs = (ys_spec, any_spec, any_spec)
        scratch = [pltpu.VMEM((2, d, D_EXPERT), jnp.float32), pltpu.VMEM((2, d, D_EXPERT), jnp.float32),
                   pltpu.VMEM((2, D_EXPERT, d), jnp.float32), pltpu.VMEM(*wgu_bf), pltpu.VMEM(*wd_bf),
                   pltpu.SemaphoreType.DMA((2,)), pltpu.SemaphoreType.DMA]
        weight_bytes = N_EXPERTS * 3 * d * D_EXPERT * (4 + 2)
    else:
        out_shape, out_specs = ys_shape, ys_spec
        scratch = [pltpu.VMEM((2,) + wgu_bf[0], wgu_bf[1]), pltpu.VMEM((2,) + wd_bf[0], wd_bf[1]),
                   pltpu.SemaphoreType.DMA((2,))]
        weight_bytes = N_EXPERTS * 3 * d * D_EXPERT * 2
    grid_spec = pltpu.PrefetchScalarGridSpec(
        num_scalar_prefetch=6,
        grid=(n_rows // tm,),
        in_specs=[pl.BlockSpec((tm, dp), row_blk)] + [any_spec] * len(weights),
        out_specs=out_specs,
        scratch_shapes=scratch,
    )
    return pl.pallas_call(
        functools.partial(_expert_kernel, cast_weights=cast_weights),
        grid_spec=grid_spec,
        out_shape=out_shape,
        compiler_params=pltpu.CompilerParams(dimension_semantics=("arbitrary",),
                                             vmem_limit_bytes=VMEM_LIMIT_BYTES),
        cost_estimate=pl.CostEstimate(flops=2 * n_rows * 3 * d * D_EXPERT, transcendentals=n_rows * D_EXPERT,
                                      bytes_accessed=2 * n_rows * dp * 4 + weight_bytes),
        name="experts",
    )(*tiles, xs, *weights)


def _combine_kernel(x1_ref, slab_ref, y1_ref, y2_ref, g_ref, b_ref, out_ref, *, alpha):
    slab = slab_ref[...]
    w1 = slab[:, COL_W1:COL_W1 + 1]
    w2 = slab[:, COL_W2:COL_W2 + 1]
    lo1, hi1 = _unpack_halves(y1_ref[...])
    lo2, hi2 = _unpack_halves(y2_ref[...])
    moe = jnp.concatenate([w1 * lo1 + w2 * lo2, w1 * hi1 + w2 * hi2], axis=1)
    out_ref[...] = _layer_norm(alpha * x1_ref[...] + moe, g_ref[...], b_ref[...])


def _combine(x1, slab, yg, g, b, alpha):
    n, d = x1.shape
    tt = TOKEN_TILE
    return pl.pallas_call(
        functools.partial(_combine_kernel, alpha=alpha),
        grid=(n // tt,),
        in_specs=[pl.BlockSpec((tt, d), lambda i: (i, 0)),
                  pl.BlockSpec((tt, LANES), lambda i: (i, 0)),
                  pl.BlockSpec((None, tt, D_PACK), lambda i: (0, i, 0)),
                  pl.BlockSpec((None, tt, D_PACK), lambda i: (1, i, 0)),
                  pl.BlockSpec((1, d), lambda i: (0, 0)),
                  pl.BlockSpec((1, d), lambda i: (0, 0))],
        out_specs=pl.BlockSpec((tt, d), lambda i: (i, 0)),
        out_shape=jax.ShapeDtypeStruct((n, d), jnp.float32),
        compiler_params=pltpu.CompilerParams(dimension_semantics=("arbitrary",),
                                             vmem_limit_bytes=VMEM_LIMIT_BYTES),
        cost_estimate=pl.CostEstimate(flops=12 * n * d, transcendentals=n,
                                      bytes_accessed=n * (2 * d * 4 + 2 * D_PACK * 4 + LANES * 4)),
        name="combine",
    )(x1, slab, yg, yg, g, b)


def _row(a):
    return a.reshape(1, -1)


def _mixer_params(w_in, b_in, conv_w, conv_b, sg_ln_g, sg_ln_b, w_s, b_s, w_o, b_o, ln1_g, ln1_b,
                  w_rc, b_rc, w_rf, b_rf):
    bf16 = jnp.bfloat16
    ws_pairs = w_s.reshape(N_SG_HEADS // 2, 2 * CHUNK, CHUNK).astype(bf16)
    bs_full = jnp.repeat(b_s.T, SG_HEAD_DIM, axis=1)
    pad = LANES - N_EXPERTS - N_EXPERT_GROUPS
    w_r = jnp.pad(jnp.concatenate([w_rf, w_rc], axis=1), ((0, 0), (0, pad))).astype(bf16)
    b_r = jnp.pad(jnp.concatenate([b_rf, b_rc]), (0, pad)).reshape(1, LANES)
    return (w_in.astype(bf16), _row(b_in), conv_w, _row(conv_b), _row(sg_ln_g), _row(sg_ln_b),
            ws_pairs, bs_full, w_o.astype(bf16), _row(b_o), _row(ln1_g), _row(ln1_b), w_r, b_r)


def _encoder_layer(x, mixer_params, expert_weights, ln2_g, ln2_b, alpha):
    n = x.shape[0] * x.shape[1]
    x1, xp, slab, route, carry = _mixer(x, mixer_params, alpha)

    tm = EXPERT_TILE
    n_tiles = 2 * n // tm + N_EXPERTS
    cnt = carry[0, 0:N_EXPERTS].astype(jnp.int32)
    tiles_e = jnp.maximum((cnt + tm - 1) // tm, 1 if len(expert_weights) == 3 else 0)
    experts = jnp.arange(N_EXPERTS, dtype=jnp.int32)[:, None]
    up_to = experts.T <= experts
    tile_end = jnp.sum(jnp.where(up_to, tiles_e[None, :], 0), axis=1)
    tile_start = tile_end - tiles_e
    row_off = tile_start * tm

    def sorted_pos(e_row, r_row):
        e = route[e_row].astype(jnp.int32)
        return jnp.sum(jnp.where(e[None, :] == experts, row_off[:, None], 0), axis=0) + route[r_row].astype(jnp.int32)

    pos1 = sorted_pos(COL_E1, COL_R1)
    pos2 = sorted_pos(COL_E2, COL_R2)
    tile_ids = jnp.arange(n_tiles, dtype=jnp.int32)
    n_used = tile_end[N_EXPERTS - 1:N_EXPERTS]
    tile_expert = jnp.sum((tile_end[None, :] <= tile_ids[:, None]).astype(jnp.int32), axis=1)
    last_used = jnp.sum((tile_end <= n_used[0] - 1).astype(jnp.int32))
    tile_expert = jnp.minimum(tile_expert, last_used)
    tile_first = (jnp.any(tile_ids[:, None] == tile_start[None, :], axis=1) & (tile_ids < n_used[0])) | (tile_ids == 0)
    tile_first = tile_first.astype(jnp.int32)
    rows_left = jnp.sum(jnp.where(tile_expert[:, None] == experts.T, (row_off + cnt)[None, :], 0), axis=1) - tile_ids * tm
    tile_nsub = jnp.where(tile_ids < n_used[0], (jnp.clip(rows_left, 0, tm) + EXPERT_SUBTILE - 1) // EXPERT_SUBTILE, 0)

    used = tiles_e > 0
    ordinal = jnp.sum((up_to & used[None, :]).astype(jnp.int32), axis=1) - 1
    e_ids = experts[:, 0]
    later_used = used[None, :] & (e_ids[None, :] > e_ids[:, None])
    next_used = jnp.min(jnp.where(later_used, e_ids[None, :], N_EXPERTS), axis=1)
    next_used = jnp.where(next_used < N_EXPERTS, next_used, -1)
    of_tile = tile_expert[:, None] == experts.T
    tile_slot = jnp.sum(jnp.where(of_tile, ordinal[None, :], 0), axis=1) % 2
    tile_next = jnp.sum(jnp.where(of_tile, next_used[None, :], 0), axis=1)

    tiles = (tile_expert, tile_first, tile_nsub.astype(jnp.int32), tile_slot.astype(jnp.int32),
             tile_next.astype(jnp.int32), n_used)
    xs = _sc_dispatch(pos1, pos2, xp, n_tiles * tm)
    if len(expert_weights) == 3:
        ys, *expert_weights = _experts(tiles, xs, expert_weights)
    else:
        ys = _experts(tiles, xs, expert_weights)
    yg = _sc_gather(pos1, pos2, ys)
    return _combine(x1, slab, yg, _row(ln2_g), _row(ln2_b), alpha).reshape(x.shape), tuple(expert_weights)


def kernel(x_prompt, x_sample, w_in, b_in, conv_w, conv_b, sg_ln_g, sg_ln_b, w_s, b_s, w_o, b_o, ln1_g, ln1_b, w_rc, b_rc, w_rf, b_rf, w_gate, w_up, w_down, ln2_g, ln2_b):
    depth = w_in.shape[0]
    alpha = (2.0 * depth) ** 0.25
    xs = (x_prompt, x_sample)
    for l in range(depth):
        mixer_params = _mixer_params(w_in[l], b_in[l], conv_w[l], conv_b[l], sg_ln_g[l], sg_ln_b[l], w_s[l], b_s[l],
                                     w_o[l], b_o[l], ln1_g[l], ln1_b[l], w_rc[l], b_rc[l], w_rf[l], b_rf[l])
        expert_weights = (w_gate[l], w_up[l], w_down[l])
        outs = [None] * len(xs)
        for k in sorted(range(len(xs)), key=lambda k: -xs[k].shape[0] * xs[k].shape[1]):
            outs[k], expert_weights = _encoder_layer(xs[k], mixer_params, expert_weights, ln2_g[l], ln2_b[l], alpha)
        xs = tuple(outs)
    return xs
```

```python
import functools

import jax
import jax.numpy as jnp
from jax import lax
from jax.experimental import pallas as pl
from jax.experimental.pallas import tpu as pltpu
from jax.experimental.pallas import tpu_sc as plsc

D_MODEL = 1024
D_CONV = 512
D_SG = 512
N_SG_HEADS = 8
SG_HEAD_DIM = D_SG // N_SG_HEADS
CHUNK = 128
N_EXPERT_GROUPS = 4
EXPERTS_PER_GROUP = 8
N_EXPERTS = N_EXPERT_GROUPS * EXPERTS_PER_GROUP
D_EXPERT = 512
LN_EPS = 1e-5
D_PACK = D_MODEL // 2

LANES = 128
SUBLANES = 8
HALO = 16
SEQ_TILE = 512
EXPERT_TILE = 1024
EXPERT_SUBTILE = 256
TOKEN_TILE = 1024
SC_CORES = 2
SC_SUBCORES = 16
SC_WORKERS = SC_CORES * SC_SUBCORES
SC_ROWS = 64
VMEM_LIMIT_BYTES = 56 * 1024 * 1024

COL_E1, COL_E2, COL_W1, COL_W2, COL_R1, COL_R2 = 0, 1, 2, 3, 4, 5
COARSE_OFF = N_EXPERTS


def _dot(a, b):
    return jnp.dot(a, b, preferred_element_type=jnp.float32)


def _gelu_tanh(x):
    return 0.5 * x * (1.0 + jnp.tanh(0.7978845608028654 * (x + 0.044715 * (x * x * x))))


def _layer_norm(x, g, b):
    mu = jnp.mean(x, axis=-1, keepdims=True)
    xc = x - mu
    var = jnp.mean(xc * xc, axis=-1, keepdims=True)
    return xc * lax.rsqrt(var + LN_EPS) * g + b


def _pack_halves(x):
    u32 = jnp.uint32
    lo = lax.bitcast_convert_type(x[:, 0:D_PACK].astype(jnp.bfloat16).astype(jnp.float32), u32)
    hi = lax.bitcast_convert_type(x[:, D_PACK:D_MODEL].astype(jnp.bfloat16).astype(jnp.float32), u32)
    return (hi & u32(0xFFFF0000)) | (lo >> u32(16))


def _unpack_halves(w):
    u32 = jnp.uint32
    lo = lax.bitcast_convert_type(w << u32(16), jnp.float32)
    hi = lax.bitcast_convert_type(w & u32(0xFFFF0000), jnp.float32)
    return lo, hi


def _mixer_kernel(x_ref, x_prev_ref, x_next_ref, w_in_ref, b_in_ref, conv_w_ref, conv_b_ref,
                  sg_g_ref, sg_b_ref, ws_ref, bs_ref, w_o_ref, b_o_ref, ln1_g_ref, ln1_b_ref,
                  w_r_ref, b_r_ref,
                  x1_ref, xp_ref, slab_ref, route_ref, counts_ref,
                  xb_ref, ymix_ref, tri_ref, resid_ref, carry_ref, *, ts, nt, n_tiles, alpha):
    t = pl.program_id(0)
    j = jnp.minimum(t, n_tiles - 1) % nt
    bf16 = jnp.bfloat16

    @pl.when(t == 0)
    def _():
        carry_ref[...] = jnp.zeros_like(carry_ref)
        resid_ref[...] = jnp.zeros_like(resid_ref)
        ri = lax.broadcasted_iota(jnp.int32, (ts, ts), 0)
        ci = lax.broadcasted_iota(jnp.int32, (ts, ts), 1)
        tri_ref[...] = jnp.where(ri > ci, 1.0, 0.0).astype(bf16)

    lane = lax.broadcasted_iota(jnp.int32, (ts, LANES), 1)
    lane_f = lane.astype(jnp.float32)
    neg = jnp.float32(-jnp.inf)
    big = jnp.float32(1e9)
    is_c = (lane >= COARSE_OFF) & (lane < COARSE_OFF + N_EXPERT_GROUPS)

    def proj(lhs, lo, hi):
        return _dot(lhs, w_in_ref[:, lo:hi]) + b_in_ref[:, lo:hi]

    def back_norm():
        x1 = _layer_norm(resid_ref[...], ln1_g_ref[...], ln1_b_ref[...])
        x1_ref[...] = x1
        xp_ref[...] = _pack_halves(x1)
        return x1.astype(bf16)

    def back_route(x1b):
        logits = _dot(x1b, w_r_ref[...]) + b_r_ref[...]
        lc = jnp.where(is_c, logits, neg)
        mx = jnp.max(lc, axis=1, keepdims=True)
        grp = jnp.min(jnp.where(lc == mx, lane_f - COARSE_OFF, big), axis=1, keepdims=True)
        p_grp = 1.0 / jnp.sum(jnp.where(is_c, jnp.exp(logits - mx), 0.0), axis=1, keepdims=True)
        grp_lo = grp * EXPERTS_PER_GROUP
        in_grp = (lane_f >= grp_lo) & (lane_f < grp_lo + EXPERTS_PER_GROUP)
        lf = jnp.where(in_grp, logits, neg)
        v1 = jnp.max(lf, axis=1, keepdims=True)
        e1 = jnp.min(jnp.where(lf == v1, lane_f, big), axis=1, keepdims=True)
        lf2 = jnp.where(lane_f == e1, neg, lf)
        v2 = jnp.max(lf2, axis=1, keepdims=True)
        e2 = jnp.min(jnp.where(lf2 == v2, lane_f, big), axis=1, keepdims=True)
        a = jnp.exp(v2 - v1)
        return e1, e2, p_grp / (1.0 + a), p_grp * a / (1.0 + a)

    def back_rank(e1, e2, w1, w2):
        hit1 = lane_f == e1
        hit2 = lane_f == e2
        onehot = jnp.where(hit1 | hit2, 1.0, 0.0)
        carry = carry_ref[...]
        before = _dot(tri_ref[...], onehot.astype(bf16)) + carry
        r1 = jnp.sum(jnp.where(hit1, before, 0.0), axis=1, keepdims=True)
        r2 = jnp.sum(jnp.where(hit2, before, 0.0), axis=1, keepdims=True)
        carry = jnp.where(t > 0, carry + jnp.sum(onehot, axis=0, keepdims=True), carry)
        carry_ref[...] = carry
        counts_ref[...] = jnp.broadcast_to(carry, counts_ref.shape)
        slab = jnp.where(lane == COL_E1, e1, 0.0)
        slab = jnp.where(lane == COL_E2, e2, slab)
        slab = jnp.where(lane == COL_W1, w1, slab)
        slab = jnp.where(lane == COL_W2, w2, slab)
        slab = jnp.where(lane == COL_R1, r1, slab)
        slab = jnp.where(lane == COL_R2, r2, slab)
        slab_ref[...] = slab
        route_ref[...] = slab.T[0:SUBLANES, :]

    def front_load():
        xb_ref[0:HALO, :] = x_prev_ref[...].astype(bf16)
        xb_ref[HALO:HALO + ts, :] = x_ref[...].astype(bf16)
        xb_ref[HALO + ts:HALO + ts + HALO, :] = x_next_ref[...].astype(bf16)

    def front_conv():
        xe = xb_ref[...]
        g_e = proj(xe, 0, D_CONV) * proj(xe, 2 * D_CONV, 3 * D_CONV)
        row_e = lax.broadcasted_iota(jnp.int32, (ts + 2 * HALO, 1), 0)
        has_prev = jnp.where(j > 0, 1.0, 0.0)
        has_next = jnp.where(j < nt - 1, 1.0, 0.0)
        g_e = g_e * jnp.where(row_e < HALO, has_prev, jnp.where(row_e >= HALO + ts, has_next, 1.0))
        conv = (g_e[HALO - 1:HALO - 1 + ts, :] * conv_w_ref[0:1, :] + g_e[HALO:HALO + ts, :] * conv_w_ref[1:2, :]
                + g_e[HALO + 1:HALO + 1 + ts, :] * conv_w_ref[2:3, :] + conv_b_ref[...])
        y_a = proj(xb_ref[HALO:HALO + ts, :], D_CONV, 2 * D_CONV) * conv
        ymix_ref[:, 0:D_CONV] = y_a.astype(bf16)

    def front_gate_proj():
        xm = xb_ref[HALO:HALO + ts, :]
        return proj(xm, 3 * D_CONV, 3 * D_CONV + D_SG), proj(xm, 3 * D_CONV + D_SG, 3 * D_CONV + 2 * D_SG)

    def front_gate(u_pre, v_pre):
        u = _gelu_tanh(u_pre)
        v = _gelu_tanh(v_pre)
        v_ln = _layer_norm(v, sg_g_ref[...], sg_b_ref[...]).astype(bf16)
        first_head = lax.broadcasted_iota(jnp.int32, (CHUNK, LANES), 1) < SG_HEAD_DIM
        for q0 in range(0, ts, 2 * CHUNK):
            q1 = q0 + CHUNK
            for hp in range(N_SG_HEADS // 2):
                c0 = hp * LANES
                rhs = jnp.concatenate([v_ln[q0:q0 + CHUNK, c0:c0 + LANES], v_ln[q1:q1 + CHUNK, c0:c0 + LANES]], axis=1)
                res = _dot(ws_ref[hp], rhs)
                bias = bs_ref[:, c0:c0 + LANES]
                m0 = jnp.where(first_head, res[0:CHUNK, 0:LANES], res[CHUNK:2 * CHUNK, 0:LANES]) + bias
                m1 = jnp.where(first_head, res[0:CHUNK, LANES:2 * LANES], res[CHUNK:2 * CHUNK, LANES:2 * LANES]) + bias
                ymix_ref[q0:q0 + CHUNK, D_CONV + c0:D_CONV + c0 + LANES] = (u[q0:q0 + CHUNK, c0:c0 + LANES] * m0).astype(bf16)
                ymix_ref[q1:q1 + CHUNK, D_CONV + c0:D_CONV + c0 + LANES] = (u[q1:q1 + CHUNK, c0:c0 + LANES] * m1).astype(bf16)

    def front_out_a():
        resid_ref[...] = alpha * x_ref[...] + b_o_ref[...] + _dot(ymix_ref[:, 0:D_CONV], w_o_ref[0:D_CONV, :])

    def front_out_b():
        resid_ref[...] += _dot(ymix_ref[:, D_CONV:D_MODEL], w_o_ref[D_CONV:D_MODEL, :])

    front_load()
    gate_pre = front_gate_proj()
    x1b = back_norm()
    routing = back_route(x1b)
    front_conv()
    back_rank(*routing)
    front_out_a()
    front_gate(*gate_pre)
    front_out_b()


def _mixer(x, p, alpha):
    nb, s, d = x.shape
    ts = SEQ_TILE
    nt = s // ts
    hb = ts // HALO
    n_tiles = nb * nt
    n_total = n_tiles * ts

    front = lambda t: jnp.minimum(t, n_tiles - 1)
    back = lambda t: jnp.maximum(t - 1, 0)
    cur = lambda t: (front(t) // nt, front(t) % nt, 0)
    prv = lambda t: (front(t) // nt, jnp.maximum((front(t) % nt) * hb - 1, 0), 0)
    nxt = lambda t: (front(t) // nt, jnp.minimum((front(t) % nt + 1) * hb, s // HALO - 1), 0)
    full = lambda a: pl.BlockSpec(a.shape, lambda t: (0,) * a.ndim)
    in_specs = [pl.BlockSpec((None, ts, d), cur), pl.BlockSpec((None, HALO, d), prv),
                pl.BlockSpec((None, HALO, d), nxt)] + [full(a) for a in p]
    out_shape = (jax.ShapeDtypeStruct((n_total, d), jnp.float32),
                 jax.ShapeDtypeStruct((n_total, D_PACK), jnp.uint32),
                 jax.ShapeDtypeStruct((n_total, LANES), jnp.float32),
                 jax.ShapeDtypeStruct((SUBLANES, n_total), jnp.float32),
                 jax.ShapeDtypeStruct((SUBLANES, LANES), jnp.float32))
    row_blk = lambda t: (back(t), 0)
    out_specs = (pl.BlockSpec((ts, d), row_blk),
                 pl.BlockSpec((ts, D_PACK), row_blk),
                 pl.BlockSpec((ts, LANES), row_blk),
                 pl.BlockSpec((SUBLANES, ts), lambda t: (0, back(t))),
                 pl.BlockSpec((SUBLANES, LANES), lambda t: (0, 0)))
    return pl.pallas_call(
        functools.partial(_mixer_kernel, ts=ts, nt=nt, n_tiles=n_tiles, alpha=alpha),
        grid=(n_tiles + 1,),
        in_specs=in_specs,
        out_specs=out_specs,
        out_shape=out_shape,
        scratch_shapes=[pltpu.VMEM((ts + 2 * HALO, d), jnp.bfloat16),
                        pltpu.VMEM((ts, d), jnp.bfloat16),
                        pltpu.VMEM((ts, ts), jnp.bfloat16),
                        pltpu.VMEM((ts, d), jnp.float32),
                        pltpu.VMEM((1, LANES), jnp.float32)],
        compiler_params=pltpu.CompilerParams(dimension_semantics=("arbitrary",),
                                             vmem_limit_bytes=VMEM_LIMIT_BYTES),
        cost_estimate=pl.CostEstimate(
            flops=2 * n_total * (d * (3 * D_CONV + 2 * D_SG) + CHUNK * D_SG + d * d + d * LANES + ts * LANES),
            transcendentals=n_total * (2 * D_SG + 2 * LANES),
            bytes_accessed=n_total * (2 * d * 4 + D_PACK * 4 + LANES * 4 + SUBLANES * 4)),
        name="mixer",
    )(x, x, x, *p)


def _sc_mesh():
    return plsc.VectorSubcoreMesh(core_axis_name="c", subcore_axis_name="s",
                                  num_cores=SC_CORES, num_subcores=SC_SUBCORES)


def _sc_worker_base(per_worker):
    return (lax.axis_index("s") * SC_CORES + lax.axis_index("c")) * per_worker


def _sc_dispatch(pos1, pos2, xp, n_rows):
    n, dp = xp.shape
    per_worker = n // SC_WORKERS
    steps = per_worker // SC_ROWS

    def body(pos1_hbm, pos2_hbm, xp_hbm, xs_hbm, idx1_v, idx2_v, rows_v, sem):
        base = _sc_worker_base(per_worker)

        @pl.loop(0, steps)
        def _(k):
            off = pl.multiple_of(base + k * SC_ROWS, SC_ROWS)
            pltpu.sync_copy(pos1_hbm.at[pl.ds(off, SC_ROWS)], idx1_v)
            pltpu.sync_copy(pos2_hbm.at[pl.ds(off, SC_ROWS)], idx2_v)
            pltpu.sync_copy(xp_hbm.at[pl.ds(off, SC_ROWS)], rows_v)
            c1 = pltpu.async_copy(rows_v, xs_hbm.at[idx1_v], sem)
            c2 = pltpu.async_copy(rows_v, xs_hbm.at[idx2_v], sem)
            c1.wait()
            c2.wait()

    return pl.kernel(
        body,
        out_type=jax.ShapeDtypeStruct((n_rows, dp), xp.dtype),
        mesh=_sc_mesh(),
        scratch_types=[pltpu.VMEM((SC_ROWS,), jnp.int32), pltpu.VMEM((SC_ROWS,), jnp.int32),
                       pltpu.VMEM((SC_ROWS, dp), xp.dtype), pltpu.SemaphoreType.DMA],
        compiler_params=pltpu.CompilerParams(use_tc_tiling_on_sc=True),
        cost_estimate=pl.CostEstimate(flops=0, transcendentals=0, bytes_accessed=3 * n * dp * 4 + 2 * n * 4),
        name="sc_dispatch",
    )(pos1, pos2, xp)


def _sc_gather(pos1, pos2, ys):
    n = pos1.shape[0]
    dp = ys.shape[1]
    per_worker = n // SC_WORKERS
    steps = per_worker // SC_ROWS

    def body(pos1_hbm, pos2_hbm, ys_hbm, out_hbm, idx_v, rows_v, sem):
        base = _sc_worker_base(per_worker)

        @pl.loop(0, steps)
        def _(k):
            off = pl.multiple_of(base + k * SC_ROWS, SC_ROWS)
            for slot, pos_hbm in enumerate((pos1_hbm, pos2_hbm)):
                pltpu.sync_copy(pos_hbm.at[pl.ds(off, SC_ROWS)], idx_v)
                pltpu.async_copy(ys_hbm.at[idx_v], rows_v, sem).wait()
                pltpu.sync_copy(rows_v, out_hbm.at[slot, pl.ds(off, SC_ROWS)])

    return pl.kernel(
        body,
        out_type=jax.ShapeDtypeStruct((2, n, dp), ys.dtype),
        mesh=_sc_mesh(),
        scratch_types=[pltpu.VMEM((SC_ROWS,), jnp.int32), pltpu.VMEM((SC_ROWS, dp), ys.dtype),
                       pltpu.SemaphoreType.DMA],
        compiler_params=pltpu.CompilerParams(use_tc_tiling_on_sc=True),
        cost_estimate=pl.CostEstimate(flops=0, transcendentals=0, bytes_accessed=4 * n * dp * 4 + 2 * n * 4),
        name="sc_gather",
    )(pos1, pos2, ys)


def _expert_kernel(te_ref, first_ref, nsub_ref, slot_ref, next_ref, nused_ref, xs_ref, *refs, cast_weights):
    i = pl.program_id(0)
    bf16 = jnp.bfloat16
    sub = EXPERT_SUBTILE
    slot = slot_ref[i]

    if cast_weights:
        (wg_hbm, wu_hbm, wd_hbm, ys_ref, wgu_out, wd_out,
         wg_buf, wu_buf, wd_buf, wgu_bf_ref, wd_bf_ref, sem, out_sem) = refs
        wgu_ref, wd_ref = wgu_bf_ref, wd_bf_ref

        def weight_copies(expert, s):
            return (pltpu.make_async_copy(wg_hbm.at[expert], wg_buf.at[s], sem.at[s]),
                    pltpu.make_async_copy(wu_hbm.at[expert], wu_buf.at[s], sem.at[s]),
                    pltpu.make_async_copy(wd_hbm.at[expert], wd_buf.at[s], sem.at[s]))

        def emit_copies(expert):
            return (pltpu.make_async_copy(wgu_bf_ref, wgu_out.at[expert], out_sem),
                    pltpu.make_async_copy(wd_bf_ref, wd_out.at[expert], out_sem))
    else:
        wgu_hbm, wd_hbm, ys_ref, wgu_buf, wd_buf, sem = refs
        wgu_ref, wd_ref = wgu_buf.at[slot], wd_buf.at[slot]

        def weight_copies(expert, s):
            return (pltpu.make_async_copy(wgu_hbm.at[expert], wgu_buf.at[s], sem.at[s]),
                    pltpu.make_async_copy(wd_hbm.at[expert], wd_buf.at[s], sem.at[s]))

    @pl.when(first_ref[i] == 1)
    def _():
        @pl.when(i == 0)
        def _():
            for c in weight_copies(te_ref[i], slot):
                c.start()

        for c in weight_copies(te_ref[i], slot):
            c.wait()

        @pl.when(next_ref[i] >= 0)
        def _():
            for c in weight_copies(next_ref[i], 1 - slot):
                c.start()

        if cast_weights:
            @pl.when(i > 0)
            def _():
                for c in emit_copies(te_ref[i]):
                    c.wait()

            wgu_bf_ref[:, 0:D_EXPERT] = wg_buf[slot].astype(bf16)
            wgu_bf_ref[:, D_EXPERT:2 * D_EXPERT] = wu_buf[slot].astype(bf16)
            wd_bf_ref[...] = wd_buf[slot].astype(bf16)
            for c in emit_copies(te_ref[i]):
                c.start()

    def swiglu(starts):
        rows = [pl.ds(r0, sub) for r0 in starts]
        gate_up = []
        for rw in rows:
            lo, hi = _unpack_halves(xs_ref[rw, :])
            gate_up.append(_dot(lo.astype(bf16), wgu_ref[0:D_PACK, :])
                           + _dot(hi.astype(bf16), wgu_ref[D_PACK:D_MODEL, :]))
        for rw, gu in zip(rows, gate_up):
            g = gu[:, 0:D_EXPERT]
            h = g * jax.nn.sigmoid(g) * gu[:, D_EXPERT:2 * D_EXPERT]
            ys_ref[rw, :] = _pack_halves(_dot(h.astype(bf16), wd_ref[...]))

    n_sub = nsub_ref[i]

    def pair(k, c):
        r0 = pl.multiple_of(k * (2 * sub), 2 * sub)
        swiglu([r0, r0 + sub])
        return c

    lax.fori_loop(0, n_sub // 2, pair, 0)

    @pl.when(n_sub % 2 == 1)
    def _():
        swiglu([pl.multiple_of((n_sub - 1) * sub, sub)])

    def clear(k, c):
        ys_ref[pl.ds(pl.multiple_of(k * sub, sub), sub), :] = jnp.zeros((sub, D_PACK), ys_ref.dtype)
        return c

    lax.fori_loop(n_sub, jnp.where(i < nused_ref[0], ys_ref.shape[0] // sub, 0), clear, 0)

    if cast_weights:
        @pl.when(i == pl.num_programs(0) - 1)
        def _():
            for c in emit_copies(te_ref[i]):
                c.wait()


def _experts(tiles, xs, weights):
    n_rows, dp = xs.shape
    d = D_MODEL
    tm = EXPERT_TILE
    cast_weights = len(weights) == 3
    row_blk = lambda i, te, fi, ns, sl, nx, nu: (jnp.minimum(i, nu[0] - 1), 0)
    any_spec = pl.BlockSpec(memory_space=pl.ANY)
    ys_shape = jax.ShapeDtypeStruct((n_rows, dp), jnp.uint32)
    ys_spec = pl.BlockSpec((tm, dp), row_blk)
    wgu_bf = ((d, 2 * D_EXPERT), jnp.bfloat16)
    wd_bf = ((D_EXPERT, d), jnp.bfloat16)
    if cast_weights:
        out_shape = (ys_shape, jax.ShapeDtypeStruct((N_EXPERTS,) + wgu_bf[0], wgu_bf[1]),
                     jax.ShapeDtypeStruct((N_EXPERTS,) + wd_bf[0], wd_bf[1]))
        out_specs = (ys_spec, any_spec, any_spec)
        scratch = [pltpu.VMEM((2, d, D_EXPERT), jnp.float32), pltpu.VMEM((2, d, D_EXPERT), jnp.float32),
                   pltpu.VMEM((2, D_EXPERT, d), jnp.float32), pltpu.VMEM(*wgu_bf), pltpu.VMEM(*wd_bf),
                   pltpu.SemaphoreType.DMA((2,)), pltpu.SemaphoreType.DMA]
        weight_bytes = N_EXPERTS * 3 * d * D_EXPERT * (4 + 2)
    else:
        out_shape, out_specs = ys_shape, ys_spec
        scratch = [pltpu.VMEM((2,) + wgu_bf[0], wgu_bf[1]), pltpu.VMEM((2,) + wd_bf[0], wd_bf[1]),
                   pltpu.SemaphoreType.DMA((2,))]
        weight_bytes = N_EXPERTS * 3 * d * D_EXPERT * 2
    grid_spec = pltpu.PrefetchScalarGridSpec(
        num_scalar_prefetch=6,
        grid=(n_rows // tm,),
        in_specs=[pl.BlockSpec((tm, dp), row_blk)] + [any_spec] * len(weights),
        out_specs=out_specs,
        scratch_shapes=scratch,
    )
    return pl.pallas_call(
        functools.partial(_expert_kernel, cast_weights=cast_weights),
        grid_spec=grid_spec,
        out_shape=out_shape,
        compiler_params=pltpu.CompilerParams(dimension_semantics=("arbitrary",),
                                             vmem_limit_bytes=VMEM_LIMIT_BYTES),
        cost_estimate=pl.CostEstimate(flops=2 * n_rows * 3 * d * D_EXPERT, transcendentals=n_rows * D_EXPERT,
                                      bytes_accessed=2 * n_rows * dp * 4 + weight_bytes),
        name="experts",
    )(*tiles, xs, *weights)


def _combine_kernel(x1_ref, slab_ref, y1_ref, y2_ref, g_ref, b_ref, out_ref, *, alpha):
    slab = slab_ref[...]
    w1 = slab[:, COL_W1:COL_W1 + 1]
    w2 = slab[:, COL_W2:COL_W2 + 1]
    lo1, hi1 = _unpack_halves(y1_ref[...])
    lo2, hi2 = _unpack_halves(y2_ref[...])
    moe = jnp.concatenate([w1 * lo1 + w2 * lo2, w1 * hi1 + w2 * hi2], axis=1)
    out_ref[...] = _layer_norm(alpha * x1_ref[...] + moe, g_ref[...], b_ref[...])


def _combine(x1, slab, yg, g, b, alpha):
    n, d = x1.shape
    tt = TOKEN_TILE
    return pl.pallas_call(
        functools.partial(_combine_kernel, alpha=alpha),
        grid=(n // tt,),
        in_specs=[pl.BlockSpec((tt, d), lambda i: (i, 0)),
                  pl.BlockSpec((tt, LANES), lambda i: (i, 0)),
                  pl.BlockSpec((None, tt, D_PACK), lambda i: (0, i, 0)),
                  pl.BlockSpec((None, tt, D_PACK), lambda i: (1, i, 0)),
                  pl.BlockSpec((1, d), lambda i: (0, 0)),
                  pl.BlockSpec((1, d), lambda i: (0, 0))],
        out_specs=pl.BlockSpec((tt, d), lambda i: (i, 0)),
        out_shape=jax.ShapeDtypeStruct((n, d), jnp.float32),
        compiler_params=pltpu.CompilerParams(dimension_semantics=("arbitrary",),
                                             vmem_limit_bytes=VMEM_LIMIT_BYTES),
        cost_estimate=pl.CostEstimate(flops=12 * n * d, transcendentals=n,
                                      bytes_accessed=n * (2 * d * 4 + 2 * D_PACK * 4 + LANES * 4)),
        name="combine",
    )(x1, slab, yg, yg, g, b)


def _row(a):
    return a.reshape(1, -1)


def _mixer_params(w_in, b_in, conv_w, conv_b, sg_ln_g, sg_ln_b, w_s, b_s, w_o, b_o, ln1_g, ln1_b,
                  w_rc, b_rc, w_rf, b_rf):
    bf16 = jnp.bfloat16
    ws_pairs = w_s.reshape(N_SG_HEADS // 2, 2 * CHUNK, CHUNK).astype(bf16)
    bs_full = jnp.repeat(b_s.T, SG_HEAD_DIM, axis=1)
    pad = LANES - N_EXPERTS - N_EXPERT_GROUPS
    w_r = jnp.pad(jnp.concatenate([w_rf, w_rc], axis=1), ((0, 0), (0, pad))).astype(bf16)
    b_r = jnp.pad(jnp.concatenate([b_rf, b_rc]), (0, pad)).reshape(1, LANES)
    return (w_in.astype(bf16), _row(b_in), conv_w, _row(conv_b), _row(sg_ln_g), _row(sg_ln_b),
            ws_pairs, bs_full, w_o.astype(bf16), _row(b_o), _row(ln1_g), _row(ln1_b), w_r, b_r)


def _encoder_layer(x, mixer_params, expert_weights, ln2_g, ln2_b, alpha):
    n = x.shape[0] * x.shape[1]
    x1, xp, slab, route, carry = _mixer(x, mixer_params, alpha)

    tm = EXPERT_TILE
    n_tiles = 2 * n // tm + N_EXPERTS
    cnt = carry[0, 0:N_EXPERTS].astype(jnp.int32)
    tiles_e = jnp.maximum((cnt + tm - 1) // tm, 1 if len(expert_weights) == 3 else 0)
    experts = jnp.arange(N_EXPERTS, dtype=jnp.int32)[:, None]
    up_to = experts.T <= experts
    tile_end = jnp.sum(jnp.where(up_to, tiles_e[None, :], 0), axis=1)
    tile_start = tile_end - tiles_e
    row_off = tile_start * tm

    def sorted_pos(e_row, r_row):
        e = route[e_row].astype(jnp.int32)
        return jnp.sum(jnp.where(e[None, :] == experts, row_off[:, None], 0), axis=0) + route[r_row].astype(jnp.int32)

    pos1 = sorted_pos(COL_E1, COL_R1)
    pos2 = sorted_pos(COL_E2, COL_R2)
    tile_ids = jnp.arange(n_tiles, dtype=jnp.int32)
    n_used = tile_end[N_EXPERTS - 1:N_EXPERTS]
    tile_expert = jnp.sum((tile_end[None, :] <= tile_ids[:, None]).astype(jnp.int32), axis=1)
    last_used = jnp.sum((tile_end <= n_used[0] - 1).astype(jnp.int32))
    tile_expert = jnp.minimum(tile_expert, last_used)
    tile_first = (jnp.any(tile_ids[:, None] == tile_start[None, :], axis=1) & (tile_ids < n_used[0])) | (tile_ids == 0)
    tile_first = tile_first.astype(jnp.int32)
    rows_left = jnp.sum(jnp.where(tile_expert[:, None] == experts.T, (row_off + cnt)[None, :], 0), axis=1) - tile_ids * tm
    tile_nsub = jnp.where(tile_ids < n_used[0], (jnp.clip(rows_left, 0, tm) + EXPERT_SUBTILE - 1) // EXPERT_SUBTILE, 0)

    used = tiles_e > 0
    ordinal = jnp.sum((up_to & used[None, :]).astype(jnp.int32), axis=1) - 1
    e_ids = experts[:, 0]
    later_used = used[None, :] & (e_ids[None, :] > e_ids[:, None])
    next_used = jnp.min(jnp.where(later_used, e_ids[None, :], N_EXPERTS), axis=1)
    next_used = jnp.where(next_used < N_EXPERTS, next_used, -1)
    of_tile = tile_expert[:, None] == experts.T
    tile_slot = jnp.sum(jnp.where(of_tile, ordinal[None, :], 0), axis=1) % 2
    tile_next = jnp.sum(jnp.where(of_tile, next_used[None, :], 0), axis=1)

    tiles = (tile_expert, tile_first, tile_nsub.astype(jnp.int32), tile_slot.astype(jnp.int32),
             tile_next.astype(jnp.int32), n_used)
    xs = _sc_dispatch(pos1, pos2, xp, n_tiles * tm)
    if len(expert_weights) == 3:
        ys, *expert_weights = _experts(tiles, xs, expert_weights)
    else:
        ys = _experts(tiles, xs, expert_weights)
    yg = _sc_gather(pos1, pos2, ys)
    return _combine(x1, slab, yg, _row(ln2_g), _row(ln2_b), alpha).reshape(x.shape), tuple(expert_weights)


def kernel(x_prompt, x_sample, w_in, b_in, conv_w, conv_b, sg_ln_g, sg_ln_b, w_s, b_s, w_o, b_o, ln1_g, ln1_b, w_rc, b_rc, w_rf, b_rf, w_gate, w_up, w_down, ln2_g, ln2_b):
    depth = w_in.shape[0]
    alpha = (2.0 * depth) ** 0.25
    xs = (x_prompt, x_sample)
    for l in range(depth):
        mixer_params = _mixer_params(w_in[l], b_in[l], conv_w[l], conv_b[l], sg_ln_g[l], sg_ln_b[l], w_s[l], b_s[l],
                                     w_o[l], b_o[l], ln1_g[l], ln1_b[l], w_rc[l], b_rc[l], w_rf[l], b_rf[l])
        expert_weights = (w_gate[l], w_up[l], w_down[l])
        outs = [None] * len(xs)
        for k in sorted(range(len(xs)), key=lambda k: -xs[k].shape[0] * xs[k].shape[1]):
            outs[k], expert_weights = _encoder_layer(xs[k], mixer_params, expert_weights, ln2_g[l], ln2_b[l], alpha)
        xs = tuple(outs)
    return xs
```

```python
import functools

import jax
import jax.numpy as jnp
from jax import lax
from jax.experimental import pallas as pl
from jax.experimental.pallas import tpu as pltpu
from jax.experimental.pallas import tpu_sc as plsc

D_MODEL = 1024
D_CONV = 512
D_SG = 512
N_SG_HEADS = 8
SG_HEAD_DIM = D_SG // N_SG_HEADS
CHUNK = 128
N_EXPERT_GROUPS = 4
EXPERTS_PER_GROUP = 8
N_EXPERTS = N_EXPERT_GROUPS * EXPERTS_PER_GROUP
D_EXPERT = 512
LN_EPS = 1e-5
D_PACK = D_MODEL // 2

LANES = 128
SUBLANES = 8
HALO = 16
SEQ_TILE = 512
EXPERT_TILE = 1024
EXPERT_SUBTILE = 256
TOKEN_TILE = 1024
SC_CORES = 2
SC_SUBCORES = 16
SC_WORKERS = SC_CORES * SC_SUBCORES
SC_ROWS = 64
VMEM_LIMIT_BYTES = 56 * 1024 * 1024

COL_E1, COL_E2, COL_W1, COL_W2, COL_R1, COL_R2 = 0, 1, 2, 3, 4, 5
COARSE_OFF = N_EXPERTS


def _dot(a, b):
    return jnp.dot(a, b, preferred_element_type=jnp.float32)


def _gelu_tanh(x):
    return 0.5 * x * (1.0 + jnp.tanh(0.7978845608028654 * (x + 0.044715 * (x * x * x))))


def _layer_norm(x, g, b):
    mu = jnp.mean(x, axis=-1, keepdims=True)
    xc = x - mu
    var = jnp.mean(xc * xc, axis=-1, keepdims=True)
    return xc * lax.rsqrt(var + LN_EPS) * g + b


def _pack_halves(x):
    u32 = jnp.uint32
    lo = lax.bitcast_convert_type(x[:, 0:D_PACK].astype(jnp.bfloat16).astype(jnp.float32), u32)
    hi = lax.bitcast_convert_type(x[:, D_PACK:D_MODEL].astype(jnp.bfloat16).astype(jnp.float32), u32)
    return (hi & u32(0xFFFF0000)) | (lo >> u32(16))


def _unpack_halves(w):
    u32 = jnp.uint32
    lo = lax.bitcast_convert_type(w << u32(16), jnp.float32)
    hi = lax.bitcast_convert_type(w & u32(0xFFFF0000), jnp.float32)
    return lo, hi


def _mixer_kernel(x_ref, x_prev_ref, x_next_ref, w_in_ref, b_in_ref, conv_w_ref, conv_b_ref,
                  sg_g_ref, sg_b_ref, ws_ref, bs_ref, w_o_ref, b_o_ref, ln1_g_ref, ln1_b_ref,
                  w_r_ref, b_r_ref,
                  x1_ref, xp_ref, route_ref, counts_ref,
                  xb_ref, ymix_ref, tri_ref, resid_ref, carry_ref, *, ts, nt, n_tiles, alpha):
    t = pl.program_id(0)
    j = jnp.minimum(t, n_tiles - 1) % nt
    bf16 = jnp.bfloat16

    @pl.when(t == 0)
    def _():
        carry_ref[...] = jnp.zeros_like(carry_ref)
        resid_ref[...] = jnp.zeros_like(resid_ref)
        ri = lax.broadcasted_iota(jnp.int32, (ts, ts), 0)
        ci = lax.broadcasted_iota(jnp.int32, (ts, ts), 1)
        tri_ref[...] = jnp.where(ri > ci, 1.0, 0.0).astype(bf16)

    lane = lax.broadcasted_iota(jnp.int32, (ts, LANES), 1)
    lane_f = lane.astype(jnp.float32)
    neg = jnp.float32(-jnp.inf)
    big = jnp.float32(1e9)
    is_c = (lane >= COARSE_OFF) & (lane < COARSE_OFF + N_EXPERT_GROUPS)

    def proj(lhs, lo, hi):
        return _dot(lhs, w_in_ref[:, lo:hi]) + b_in_ref[:, lo:hi]

    def back_norm():
        x1 = _layer_norm(resid_ref[...], ln1_g_ref[...], ln1_b_ref[...])
        x1_ref[...] = x1
        xp_ref[...] = _pack_halves(x1)
        return x1.astype(bf16)

    def back_route(x1b):
        logits = _dot(x1b, w_r_ref[...]) + b_r_ref[...]
        lc = jnp.where(is_c, logits, neg)
        mx = jnp.max(lc, axis=1, keepdims=True)
        grp = jnp.min(jnp.where(lc == mx, lane_f - COARSE_OFF, big), axis=1, keepdims=True)
        p_grp = 1.0 / jnp.sum(jnp.where(is_c, jnp.exp(logits - mx), 0.0), axis=1, keepdims=True)
        grp_lo = grp * EXPERTS_PER_GROUP
        in_grp = (lane_f >= grp_lo) & (lane_f < grp_lo + EXPERTS_PER_GROUP)
        lf = jnp.where(in_grp, logits, neg)
        v1 = jnp.max(lf, axis=1, keepdims=True)
        e1 = jnp.min(jnp.where(lf == v1, lane_f, big), axis=1, keepdims=True)
        lf2 = jnp.where(lane_f == e1, neg, lf)
        v2 = jnp.max(lf2, axis=1, keepdims=True)
        e2 = jnp.min(jnp.where(lf2 == v2, lane_f, big), axis=1, keepdims=True)
        a = jnp.exp(v2 - v1)
        return e1, e2, p_grp / (1.0 + a), p_grp * a / (1.0 + a)

    def back_rank(e1, e2, w1, w2):
        hit1 = lane_f == e1
        hit2 = lane_f == e2
        onehot = jnp.where(hit1 | hit2, 1.0, 0.0)
        carry = carry_ref[...]
        before = _dot(tri_ref[...], onehot.astype(bf16)) + carry
        r1 = jnp.sum(jnp.where(hit1, before, 0.0), axis=1, keepdims=True)
        r2 = jnp.sum(jnp.where(hit2, before, 0.0), axis=1, keepdims=True)
        carry = jnp.where(t > 0, carry + jnp.sum(onehot, axis=0, keepdims=True), carry)
        carry_ref[...] = carry
        counts_ref[...] = jnp.broadcast_to(carry, counts_ref.shape)
        slab = jnp.where(lane == COL_E1, e1, 0.0)
        slab = jnp.where(lane == COL_E2, e2, slab)
        slab = jnp.where(lane == COL_W1, w1, slab)
        slab = jnp.where(lane == COL_W2, w2, slab)
        slab = jnp.where(lane == COL_R1, r1, slab)
        slab = jnp.where(lane == COL_R2, r2, slab)
        route_ref[...] = slab.T[0:SUBLANES, :]

    def front_load():
        xb_ref[0:HALO, :] = x_prev_ref[...].astype(bf16)
        xb_ref[HALO:HALO + ts, :] = x_ref[...].astype(bf16)
        xb_ref[HALO + ts:HALO + ts + HALO, :] = x_next_ref[...].astype(bf16)

    def front_conv():
        xe = xb_ref[...]
        g_e = proj(xe, 0, D_CONV) * proj(xe, 2 * D_CONV, 3 * D_CONV)
        row_e = lax.broadcasted_iota(jnp.int32, (ts + 2 * HALO, 1), 0)
        has_prev = jnp.where(j > 0, 1.0, 0.0)
        has_next = jnp.where(j < nt - 1, 1.0, 0.0)
        g_e = g_e * jnp.where(row_e < HALO, has_prev, jnp.where(row_e >= HALO + ts, has_next, 1.0))
        conv = (g_e[HALO - 1:HALO - 1 + ts, :] * conv_w_ref[0:1, :] + g_e[HALO:HALO + ts, :] * conv_w_ref[1:2, :]
                + g_e[HALO + 1:HALO + 1 + ts, :] * conv_w_ref[2:3, :] + conv_b_ref[...])
        y_a = proj(xb_ref[HALO:HALO + ts, :], D_CONV, 2 * D_CONV) * conv
        ymix_ref[:, 0:D_CONV] = y_a.astype(bf16)

    def front_gate_proj():
        xm = xb_ref[HALO:HALO + ts, :]
        return proj(xm, 3 * D_CONV, 3 * D_CONV + D_SG), proj(xm, 3 * D_CONV + D_SG, 3 * D_CONV + 2 * D_SG)

    def front_gate(u_pre, v_pre):
        u = _gelu_tanh(u_pre)
        v = _gelu_tanh(v_pre)
        v_ln = _layer_norm(v, sg_g_ref[...], sg_b_ref[...]).astype(bf16)
        first_head = lax.broadcasted_iota(jnp.int32, (CHUNK, LANES), 1) < SG_HEAD_DIM
        for q0 in range(0, ts, 2 * CHUNK):
            q1 = q0 + CHUNK
            for hp in range(N_SG_HEADS // 2):
                c0 = hp * LANES
                rhs = jnp.concatenate([v_ln[q0:q0 + CHUNK, c0:c0 + LANES], v_ln[q1:q1 + CHUNK, c0:c0 + LANES]], axis=1)
                res = _dot(ws_ref[hp], rhs)
                bias = bs_ref[:, c0:c0 + LANES]
                m0 = jnp.where(first_head, res[0:CHUNK, 0:LANES], res[CHUNK:2 * CHUNK, 0:LANES]) + bias
                m1 = jnp.where(first_head, res[0:CHUNK, LANES:2 * LANES], res[CHUNK:2 * CHUNK, LANES:2 * LANES]) + bias
                ymix_ref[q0:q0 + CHUNK, D_CONV + c0:D_CONV + c0 + LANES] = (u[q0:q0 + CHUNK, c0:c0 + LANES] * m0).astype(bf16)
                ymix_ref[q1:q1 + CHUNK, D_CONV + c0:D_CONV + c0 + LANES] = (u[q1:q1 + CHUNK, c0:c0 + LANES] * m1).astype(bf16)

    def front_out_a():
        resid_ref[...] = alpha * x_ref[...] + b_o_ref[...] + _dot(ymix_ref[:, 0:D_CONV], w_o_ref[0:D_CONV, :])

    def front_out_b():
        resid_ref[...] += _dot(ymix_ref[:, D_CONV:D_MODEL], w_o_ref[D_CONV:D_MODEL, :])

    front_load()
    gate_pre = front_gate_proj()
    x1b = back_norm()
    routing = back_route(x1b)
    front_conv()
    back_rank(*routing)
    front_out_a()
    front_gate(*gate_pre)
    front_out_b()


def _mixer(x, p, alpha):
    nb, s, d = x.shape
    ts = SEQ_TILE
    nt = s // ts
    hb = ts // HALO
    n_tiles = nb * nt
    n_total = n_tiles * ts

    front = lambda t: jnp.minimum(t, n_tiles - 1)
    back = lambda t: jnp.maximum(t - 1, 0)
    cur = lambda t: (front(t) // nt, front(t) % nt, 0)
    prv = lambda t: (front(t) // nt, jnp.maximum((front(t) % nt) * hb - 1, 0), 0)
    nxt = lambda t: (front(t) // nt, jnp.minimum((front(t) % nt + 1) * hb, s // HALO - 1), 0)
    full = lambda a: pl.BlockSpec(a.shape, lambda t: (0,) * a.ndim)
    in_specs = [pl.BlockSpec((None, ts, d), cur), pl.BlockSpec((None, HALO, d), prv),
                pl.BlockSpec((None, HALO, d), nxt)] + [full(a) for a in p]
    out_shape = (jax.ShapeDtypeStruct((n_total, d), jnp.float32),
                 jax.ShapeDtypeStruct((n_total, D_PACK), jnp.uint32),
                 jax.ShapeDtypeStruct((SUBLANES, n_total), jnp.float32),
                 jax.ShapeDtypeStruct((SUBLANES, LANES), jnp.float32))
    row_blk = lambda t: (back(t), 0)
    out_specs = (pl.BlockSpec((ts, d), row_blk),
                 pl.BlockSpec((ts, D_PACK), row_blk),
                 pl.BlockSpec((SUBLANES, ts), lambda t: (0, back(t))),
                 pl.BlockSpec((SUBLANES, LANES), lambda t: (0, 0)))
    return pl.pallas_call(
        functools.partial(_mixer_kernel, ts=ts, nt=nt, n_tiles=n_tiles, alpha=alpha),
        grid=(n_tiles + 1,),
        in_specs=in_specs,
        out_specs=out_specs,
        out_shape=out_shape,
        scratch_shapes=[pltpu.VMEM((ts + 2 * HALO, d), jnp.bfloat16),
                        pltpu.VMEM((ts, d), jnp.bfloat16),
                        pltpu.VMEM((ts, ts), jnp.bfloat16),
                        pltpu.VMEM((ts, d), jnp.float32),
                        pltpu.VMEM((1, LANES), jnp.float32)],
        compiler_params=pltpu.CompilerParams(dimension_semantics=("arbitrary",),
                                             vmem_limit_bytes=VMEM_LIMIT_BYTES),
        cost_estimate=pl.CostEstimate(
            flops=2 * n_total * (d * (3 * D_CONV + 2 * D_SG) + CHUNK * D_SG + d * d + d * LANES + ts * LANES),
            transcendentals=n_total * (2 * D_SG + 2 * LANES),
            bytes_accessed=n_total * (2 * d * 4 + D_PACK * 4 + SUBLANES * 4)),
        name="mixer",
    )(x, x, x, *p)


def _sc_mesh():
    return plsc.VectorSubcoreMesh(core_axis_name="c", subcore_axis_name="s",
                                  num_cores=SC_CORES, num_subcores=SC_SUBCORES)


def _sc_worker_base(per_worker):
    return (lax.axis_index("s") * SC_CORES + lax.axis_index("c")) * per_worker


def _sc_dispatch(pos1, pos2, xp, n_rows):
    n, dp = xp.shape
    per_worker = n // SC_WORKERS
    steps = per_worker // SC_ROWS

    def body(pos1_hbm, pos2_hbm, xp_hbm, xs_hbm, idx1_v, idx2_v, rows_v, sem):
        base = _sc_worker_base(per_worker)

        @pl.loop(0, steps)
        def _(k):
            off = pl.multiple_of(base + k * SC_ROWS, SC_ROWS)
            pltpu.sync_copy(pos1_hbm.at[pl.ds(off, SC_ROWS)], idx1_v)
            pltpu.sync_copy(pos2_hbm.at[pl.ds(off, SC_ROWS)], idx2_v)
            pltpu.sync_copy(xp_hbm.at[pl.ds(off, SC_ROWS)], rows_v)
            c1 = pltpu.async_copy(rows_v, xs_hbm.at[idx1_v], sem)
            c2 = pltpu.async_copy(rows_v, xs_hbm.at[idx2_v], sem)
            c1.wait()
            c2.wait()

    return pl.kernel(
        body,
        out_type=jax.ShapeDtypeStruct((n_rows, dp), xp.dtype),
        mesh=_sc_mesh(),
        scratch_types=[pltpu.VMEM((SC_ROWS,), jnp.int32), pltpu.VMEM((SC_ROWS,), jnp.int32),
                       pltpu.VMEM((SC_ROWS, dp), xp.dtype), pltpu.SemaphoreType.DMA],
        compiler_params=pltpu.CompilerParams(use_tc_tiling_on_sc=True),
        cost_estimate=pl.CostEstimate(flops=0, transcendentals=0, bytes_accessed=3 * n * dp * 4 + 2 * n * 4),
        name="sc_dispatch",
    )(pos1, pos2, xp)


def _sc_gather(pos1, pos2, ys):
    n = pos1.shape[0]
    dp = ys.shape[1]
    per_worker = n // SC_WORKERS
    steps = per_worker // SC_ROWS

    def body(pos1_hbm, pos2_hbm, ys_hbm, out_hbm, idx_v, rows_v, sem):
        base = _sc_worker_base(per_worker)

        @pl.loop(0, steps)
        def _(k):
            off = pl.multiple_of(base + k * SC_ROWS, SC_ROWS)
            for slot, pos_hbm in enumerate((pos1_hbm, pos2_hbm)):
                pltpu.sync_copy(pos_hbm.at[pl.ds(off, SC_ROWS)], idx_v)
                pltpu.async_copy(ys_hbm.at[idx_v], rows_v, sem).wait()
                pltpu.sync_copy(rows_v, out_hbm.at[slot, pl.ds(off, SC_ROWS)])

    return pl.kernel(
        body,
        out_type=jax.ShapeDtypeStruct((2, n, dp), ys.dtype),
        mesh=_sc_mesh(),
        scratch_types=[pltpu.VMEM((SC_ROWS,), jnp.int32), pltpu.VMEM((SC_ROWS, dp), ys.dtype),
                       pltpu.SemaphoreType.DMA],
        compiler_params=pltpu.CompilerParams(use_tc_tiling_on_sc=True),
        cost_estimate=pl.CostEstimate(flops=0, transcendentals=0, bytes_accessed=4 * n * dp * 4 + 2 * n * 4),
        name="sc_gather",
    )(pos1, pos2, ys)


def _expert_kernel(te_ref, first_ref, nsub_ref, slot_ref, next_ref, nused_ref, xs_ref, *refs, cast_weights):
    i = pl.program_id(0)
    bf16 = jnp.bfloat16
    sub = EXPERT_SUBTILE
    slot = slot_ref[i]

    if cast_weights:
        (wg_hbm, wu_hbm, wd_hbm, ys_ref, wgu_out, wd_out,
         wg_buf, wu_buf, wd_buf, wgu_bf_ref, wd_bf_ref, sem, out_sem) = refs
        wgu_ref, wd_ref = wgu_bf_ref, wd_bf_ref

        def weight_copies(expert, s):
            return (pltpu.make_async_copy(wg_hbm.at[expert], wg_buf.at[s], sem.at[s]),
                    pltpu.make_async_copy(wu_hbm.at[expert], wu_buf.at[s], sem.at[s]),
                    pltpu.make_async_copy(wd_hbm.at[expert], wd_buf.at[s], sem.at[s]))

        def emit_copies(expert):
            return (pltpu.make_async_copy(wgu_bf_ref, wgu_out.at[expert], out_sem),
                    pltpu.make_async_copy(wd_bf_ref, wd_out.at[expert], out_sem))
    else:
        wgu_hbm, wd_hbm, ys_ref, wgu_buf, wd_buf, sem = refs
        wgu_ref, wd_ref = wgu_buf.at[slot], wd_buf.at[slot]

        def weight_copies(expert, s):
            return (pltpu.make_async_copy(wgu_hbm.at[expert], wgu_buf.at[s], sem.at[s]),
                    pltpu.make_async_copy(wd_hbm.at[expert], wd_buf.at[s], sem.at[s]))

    @pl.when(first_ref[i] == 1)
    def _():
        @pl.when(i == 0)
        def _():
            for c in weight_copies(te_ref[i], slot):
                c.start()

        for c in weight_copies(te_ref[i], slot):
            c.wait()

        @pl.when(next_ref[i] >= 0)
        def _():
            for c in weight_copies(next_ref[i], 1 - slot):
                c.start()

        if cast_weights:
            @pl.when(i > 0)
            def _():
                for c in emit_copies(te_ref[i]):
                    c.wait()

            wgu_bf_ref[:, 0:D_EXPERT] = wg_buf[slot].astype(bf16)
            wgu_bf_ref[:, D_EXPERT:2 * D_EXPERT] = wu_buf[slot].astype(bf16)
            wd_bf_ref[...] = wd_buf[slot].astype(bf16)
            for c in emit_copies(te_ref[i]):
                c.start()

    def swiglu(starts):
        rows = [pl.ds(r0, sub) for r0 in starts]
        gate_up = []
        for rw in rows:
            lo, hi = _unpack_halves(xs_ref[rw, :])
            gate_up.append(_dot(lo.astype(bf16), wgu_ref[0:D_PACK, :])
                           + _dot(hi.astype(bf16), wgu_ref[D_PACK:D_MODEL, :]))
        for rw, gu in zip(rows, gate_up):
            g = gu[:, 0:D_EXPERT]
            h = g * jax.nn.sigmoid(g) * gu[:, D_EXPERT:2 * D_EXPERT]
            ys_ref[rw, :] = _pack_halves(_dot(h.astype(bf16), wd_ref[...]))

    n_sub = nsub_ref[i]

    def pair(k, c):
        r0 = pl.multiple_of(k * (2 * sub), 2 * sub)
        swiglu([r0, r0 + sub])
        return c

    lax.fori_loop(0, n_sub // 2, pair, 0)

    @pl.when(n_sub % 2 == 1)
    def _():
        swiglu([pl.multiple_of((n_sub - 1) * sub, sub)])

    def clear(k, c):
        ys_ref[pl.ds(pl.multiple_of(k * sub, sub), sub), :] = jnp.zeros((sub, D_PACK), ys_ref.dtype)
        return c

    lax.fori_loop(n_sub, jnp.where(i < nused_ref[0], ys_ref.shape[0] // sub, 0), clear, 0)

    if cast_weights:
        @pl.when(i == pl.num_programs(0) - 1)
        def _():
            for c in emit_copies(te_ref[i]):
                c.wait()


def _experts(tiles, xs, weights):
    n_rows, dp = xs.shape
    d = D_MODEL
    tm = EXPERT_TILE
    cast_weights = len(weights) == 3
    row_blk = lambda i, te, fi, ns, sl, nx, nu: (jnp.minimum(i, nu[0] - 1), 0)
    any_spec = pl.BlockSpec(memory_space=pl.ANY)
    ys_shape = jax.ShapeDtypeStruct((n_rows, dp), jnp.uint32)
    ys_spec = pl.BlockSpec((tm, dp), row_blk)
    wgu_bf = ((d, 2 * D_EXPERT), jnp.bfloat16)
    wd_bf = ((D_EXPERT, d), jnp.bfloat16)
    if cast_weights:
        out_shape = (ys_shape, jax.ShapeDtypeStruct((N_EXPERTS,) + wgu_bf[0], wgu_bf[1]),
                     jax.ShapeDtypeStruct((N_EXPERTS,) + wd_bf[0], wd_bf[1]))
        out_specs = (ys_spec, any_spec, any_spec)
        scratch = [pltpu.VMEM((2, d, D_EXPERT), jnp.float32), pltpu.VMEM((2, d, D_EXPERT), jnp.float32),
                   pltpu.VMEM((2, D_EXPERT, d), jnp.float32), pltpu.VMEM(*wgu_bf), pltpu.VMEM(*wd_bf),
                   pltpu.SemaphoreType.DMA((2,)), pltpu.SemaphoreType.DMA]
        weight_bytes = N_EXPERTS * 3 * d * D_EXPERT * (4 + 2)
    else:
        out_shape, out_specs = ys_shape, ys_spec
        scratch = [pltpu.VMEM((2,) + wgu_bf[0], wgu_bf[1]), pltpu.VMEM((2,) + wd_bf[0], wd_bf[1]),
                   pltpu.SemaphoreType.DMA((2,))]
        weight_bytes = N_EXPERTS * 3 * d * D_EXPERT * 2
    grid_spec = pltpu.PrefetchScalarGridSpec(
        num_scalar_prefetch=6,
        grid=(n_rows // tm,),
        in_specs=[pl.BlockSpec((tm, dp), row_blk)] + [any_spec] * len(weights),
        out_specs=out_specs,
        scratch_shapes=scratch,
    )
    return pl.pallas_call(
        functools.partial(_expert_kernel, cast_weights=cast_weights),
        grid_spec=grid_spec,
        out_shape=out_shape,
        compiler_params=pltpu.CompilerParams(dimension_semantics=("arbitrary",),
                                             vmem_limit_bytes=VMEM_LIMIT_BYTES),
        cost_estimate=pl.CostEstimate(flops=2 * n_rows * 3 * d * D_EXPERT, transcendentals=n_rows * D_EXPERT,
                                      bytes_accessed=2 * n_rows * dp * 4 + weight_bytes),
        name="experts",
    )(*tiles, xs, *weights)


def _combine_kernel(x1_ref, route_ref, y1_ref, y2_ref, g_ref, b_ref, out_ref, *, alpha):
    tt = route_ref.shape[1]
    cols = jnp.concatenate([route_ref[...], jnp.zeros((LANES - SUBLANES, tt), jnp.float32)], axis=0).T
    w1 = cols[:, COL_W1:COL_W1 + 1]
    w2 = cols[:, COL_W2:COL_W2 + 1]
    lo1, hi1 = _unpack_halves(y1_ref[...])
    lo2, hi2 = _unpack_halves(y2_ref[...])
    moe = jnp.concatenate([w1 * lo1 + w2 * lo2, w1 * hi1 + w2 * hi2], axis=1)
    out_ref[...] = _layer_norm(alpha * x1_ref[...] + moe, g_ref[...], b_ref[...])


def _combine(x1, route, yg, g, b, alpha):
    n, d = x1.shape
    tt = TOKEN_TILE
    return pl.pallas_call(
        functools.partial(_combine_kernel, alpha=alpha),
        grid=(n // tt,),
        in_specs=[pl.BlockSpec((tt, d), lambda i: (i, 0)),
                  pl.BlockSpec((SUBLANES, tt), lambda i: (0, i)),
                  pl.BlockSpec((None, tt, D_PACK), lambda i: (0, i, 0)),
                  pl.BlockSpec((None, tt, D_PACK), lambda i: (1, i, 0)),
                  pl.BlockSpec((1, d), lambda i: (0, 0)),
                  pl.BlockSpec((1, d), lambda i: (0, 0))],
        out_specs=pl.BlockSpec((tt, d), lambda i: (i, 0)),
        out_shape=jax.ShapeDtypeStruct((n, d), jnp.float32),
        compiler_params=pltpu.CompilerParams(dimension_semantics=("arbitrary",),
                                             vmem_limit_bytes=VMEM_LIMIT_BYTES),
        cost_estimate=pl.CostEstimate(flops=12 * n * d, transcendentals=n,
                                      bytes_accessed=n * (2 * d * 4 + 2 * D_PACK * 4 + SUBLANES * 4)),
        name="combine",
    )(x1, route, yg, yg, g, b)


def _row(a):
    return a.reshape(1, -1)


def _mixer_params(w_in, b_in, conv_w, conv_b, sg_ln_g, sg_ln_b, w_s, b_s, w_o, b_o, ln1_g, ln1_b,
                  w_rc, b_rc, w_rf, b_rf):
    bf16 = jnp.bfloat16
    ws_pairs = w_s.reshape(N_SG_HEADS // 2, 2 * CHUNK, CHUNK).astype(bf16)
    bs_full = jnp.repeat(b_s.T, SG_HEAD_DIM, axis=1)
    pad = LANES - N_EXPERTS - N_EXPERT_GROUPS
    w_r = jnp.pad(jnp.concatenate([w_rf, w_rc], axis=1), ((0, 0), (0, pad))).astype(bf16)
    b_r = jnp.pad(jnp.concatenate([b_rf, b_rc]), (0, pad)).reshape(1, LANES)
    return (w_in.astype(bf16), _row(b_in), conv_w, _row(conv_b), _row(sg_ln_g), _row(sg_ln_b),
            ws_pairs, bs_full, w_o.astype(bf16), _row(b_o), _row(ln1_g), _row(ln1_b), w_r, b_r)


def _encoder_layer(x, mixer_params, expert_weights, ln2_g, ln2_b, alpha):
    n = x.shape[0] * x.shape[1]
    x1, xp, route, carry = _mixer(x, mixer_params, alpha)

    tm = EXPERT_TILE
    n_tiles = 2 * n // tm + N_EXPERTS
    cnt = carry[0, 0:N_EXPERTS].astype(jnp.int32)
    tiles_e = jnp.maximum((cnt + tm - 1) // tm, 1 if len(expert_weights) == 3 else 0)
    experts = jnp.arange(N_EXPERTS, dtype=jnp.int32)[:, None]
    up_to = experts.T <= experts
    tile_end = jnp.sum(jnp.where(up_to, tiles_e[None, :], 0), axis=1)
    tile_start = tile_end - tiles_e
    row_off = tile_start * tm

    def sorted_pos(e_row, r_row):
        e = route[e_row].astype(jnp.int32)
        return jnp.sum(jnp.where(e[None, :] == experts, row_off[:, None], 0), axis=0) + route[r_row].astype(jnp.int32)

    pos1 = sorted_pos(COL_E1, COL_R1)
    pos2 = sorted_pos(COL_E2, COL_R2)
    tile_ids = jnp.arange(n_tiles, dtype=jnp.int32)
    n_used = tile_end[N_EXPERTS - 1:N_EXPERTS]
    tile_expert = jnp.sum((tile_end[None, :] <= tile_ids[:, None]).astype(jnp.int32), axis=1)
    last_used = jnp.sum((tile_end <= n_used[0] - 1).astype(jnp.int32))
    tile_expert = jnp.minimum(tile_expert, last_used)
    tile_first = (jnp.any(tile_ids[:, None] == tile_start[None, :], axis=1) & (tile_ids < n_used[0])) | (tile_ids == 0)
    tile_first = tile_first.astype(jnp.int32)
    rows_left = jnp.sum(jnp.where(tile_expert[:, None] == experts.T, (row_off + cnt)[None, :], 0), axis=1) - tile_ids * tm
    tile_nsub = jnp.where(tile_ids < n_used[0], (jnp.clip(rows_left, 0, tm) + EXPERT_SUBTILE - 1) // EXPERT_SUBTILE, 0)

    used = tiles_e > 0
    ordinal = jnp.sum((up_to & used[None, :]).astype(jnp.int32), axis=1) - 1
    e_ids = experts[:, 0]
    later_used = used[None, :] & (e_ids[None, :] > e_ids[:, None])
    next_used = jnp.min(jnp.where(later_used, e_ids[None, :], N_EXPERTS), axis=1)
    next_used = jnp.where(next_used < N_EXPERTS, next_used, -1)
    of_tile = tile_expert[:, None] == experts.T
    tile_slot = jnp.sum(jnp.where(of_tile, ordinal[None, :], 0), axis=1) % 2
    tile_next = jnp.sum(jnp.where(of_tile, next_used[None, :], 0), axis=1)

    tiles = (tile_expert, tile_first, tile_nsub.astype(jnp.int32), tile_slot.astype(jnp.int32),
             tile_next.astype(jnp.int32), n_used)
    xs = _sc_dispatch(pos1, pos2, xp, n_tiles * tm)
    if len(expert_weights) == 3:
        ys, *expert_weights = _experts(tiles, xs, expert_weights)
    else:
        ys = _experts(tiles, xs, expert_weights)
    yg = _sc_gather(pos1, pos2, ys)
    return _combine(x1, route, yg, _row(ln2_g), _row(ln2_b), alpha).reshape(x.shape), tuple(expert_weights)


def kernel(x_prompt, x_sample, w_in, b_in, conv_w, conv_b, sg_ln_g, sg_ln_b, w_s, b_s, w_o, b_o, ln1_g, ln1_b, w_rc, b_rc, w_rf, b_rf, w_gate, w_up, w_down, ln2_g, ln2_b):
    depth = w_in.shape[0]
    alpha = (2.0 * depth) ** 0.25
    xs = (x_prompt, x_sample)
    for l in range(depth):
        mixer_params = _mixer_params(w_in[l], b_in[l], conv_w[l], conv_b[l], sg_ln_g[l], sg_ln_b[l], w_s[l], b_s[l],
                                     w_o[l], b_o[l], ln1_g[l], ln1_b[l], w_rc[l], b_rc[l], w_rf[l], b_rf[l])
        expert_weights = (w_gate[l], w_up[l], w_down[l])
        outs = [None] * len(xs)
        for k in sorted(range(len(xs)), key=lambda k: -xs[k].shape[0] * xs[k].shape[1]):
            outs[k], expert_weights = _encoder_layer(xs[k], mixer_params, expert_weights, ln2_g[l], ln2_b[l], alpha)
        xs = tuple(outs)
    return xs
```

```python
import functools

import jax
import jax.numpy as jnp
from jax import lax
from jax.experimental import pallas as pl
from jax.experimental.pallas import tpu as pltpu
from jax.experimental.pallas import tpu_sc as plsc

D_MODEL = 1024
D_CONV = 512
D_SG = 512
N_SG_HEADS = 8
SG_HEAD_DIM = D_SG // N_SG_HEADS
CHUNK = 128
N_EXPERT_GROUPS = 4
EXPERTS_PER_GROUP = 8
N_EXPERTS = N_EXPERT_GROUPS * EXPERTS_PER_GROUP
D_EXPERT = 512
LN_EPS = 1e-5
D_PACK = D_MODEL // 2

LANES = 128
SUBLANES = 8
HALO = 16
SEQ_TILE = 512
EXPERT_TILE = 1024
EXPERT_SUBTILE = 256
TOKEN_TILE = 1024
SC_CORES = 2
SC_SUBCORES = 16
SC_WORKERS = SC_CORES * SC_SUBCORES
SC_ROWS = 64
VMEM_LIMIT_BYTES = 56 * 1024 * 1024

COL_E1, COL_E2, COL_W1, COL_W2, COL_R1, COL_R2 = 0, 1, 2, 3, 4, 5
COARSE_OFF = N_EXPERTS


def _dot(a, b):
    return jnp.dot(a, b, preferred_element_type=jnp.float32)


def _gelu_tanh(x):
    return 0.5 * x * (1.0 + jnp.tanh(0.7978845608028654 * (x + 0.044715 * (x * x * x))))


def _layer_norm(x, g, b):
    mu = jnp.mean(x, axis=-1, keepdims=True)
    xc = x - mu
    var = jnp.mean(xc * xc, axis=-1, keepdims=True)
    return xc * lax.rsqrt(var + LN_EPS) * g + b


def _pack_halves(x):
    u32 = jnp.uint32
    lo = lax.bitcast_convert_type(x[:, 0:D_PACK].astype(jnp.bfloat16).astype(jnp.float32), u32)
    hi = lax.bitcast_convert_type(x[:, D_PACK:D_MODEL].astype(jnp.bfloat16).astype(jnp.float32), u32)
    return (hi & u32(0xFFFF0000)) | (lo >> u32(16))


def _unpack_halves(w):
    u32 = jnp.uint32
    lo = lax.bitcast_convert_type(w << u32(16), jnp.float32)
    hi = lax.bitcast_convert_type(w & u32(0xFFFF0000), jnp.float32)
    return lo, hi


def _mixer_kernel(x_ref, x_prev_ref, x_next_ref, w_in_ref, b_in_ref, conv_w_ref, conv_b_ref,
                  sg_g_ref, sg_b_ref, ws_ref, bs_ref, w_o_ref, b_o_ref, ln1_g_ref, ln1_b_ref,
                  w_r_ref, b_r_ref,
                  xp_ref, route_ref, counts_ref,
                  xb_ref, ymix_ref, tri_ref, resid_ref, carry_ref, *, ts, nt, n_tiles, alpha):
    t = pl.program_id(0)
    j = jnp.minimum(t, n_tiles - 1) % nt
    bf16 = jnp.bfloat16

    @pl.when(t == 0)
    def _():
        carry_ref[...] = jnp.zeros_like(carry_ref)
        resid_ref[...] = jnp.zeros_like(resid_ref)
        ri = lax.broadcasted_iota(jnp.int32, (ts, ts), 0)
        ci = lax.broadcasted_iota(jnp.int32, (ts, ts), 1)
        tri_ref[...] = jnp.where(ri > ci, 1.0, 0.0).astype(bf16)

    lane = lax.broadcasted_iota(jnp.int32, (ts, LANES), 1)
    lane_f = lane.astype(jnp.float32)
    neg = jnp.float32(-jnp.inf)
    big = jnp.float32(1e9)
    is_c = (lane >= COARSE_OFF) & (lane < COARSE_OFF + N_EXPERT_GROUPS)

    def proj(lhs, lo, hi):
        return _dot(lhs, w_in_ref[:, lo:hi]) + b_in_ref[:, lo:hi]

    def back_norm():
        x1 = _layer_norm(resid_ref[...], ln1_g_ref[...], ln1_b_ref[...])
        xp_ref[...] = _pack_halves(x1)
        return x1.astype(bf16)

    def back_route(x1b):
        logits = _dot(x1b, w_r_ref[...]) + b_r_ref[...]
        lc = jnp.where(is_c, logits, neg)
        mx = jnp.max(lc, axis=1, keepdims=True)
        grp = jnp.min(jnp.where(lc == mx, lane_f - COARSE_OFF, big), axis=1, keepdims=True)
        p_grp = 1.0 / jnp.sum(jnp.where(is_c, jnp.exp(logits - mx), 0.0), axis=1, keepdims=True)
        grp_lo = grp * EXPERTS_PER_GROUP
        in_grp = (lane_f >= grp_lo) & (lane_f < grp_lo + EXPERTS_PER_GROUP)
        lf = jnp.where(in_grp, logits, neg)
        v1 = jnp.max(lf, axis=1, keepdims=True)
        e1 = jnp.min(jnp.where(lf == v1, lane_f, big), axis=1, keepdims=True)
        lf2 = jnp.where(lane_f == e1, neg, lf)
        v2 = jnp.max(lf2, axis=1, keepdims=True)
        e2 = jnp.min(jnp.where(lf2 == v2, lane_f, big), axis=1, keepdims=True)
        a = jnp.exp(v2 - v1)
        return e1, e2, p_grp / (1.0 + a), p_grp * a / (1.0 + a)

    def back_rank(e1, e2, w1, w2):
        hit1 = lane_f == e1
        hit2 = lane_f == e2
        onehot = jnp.where(hit1 | hit2, 1.0, 0.0)
        carry = carry_ref[...]
        before = _dot(tri_ref[...], onehot.astype(bf16)) + carry
        r1 = jnp.sum(jnp.where(hit1, before, 0.0), axis=1, keepdims=True)
        r2 = jnp.sum(jnp.where(hit2, before, 0.0), axis=1, keepdims=True)
        carry = jnp.where(t > 0, carry + jnp.sum(onehot, axis=0, keepdims=True), carry)
        carry_ref[...] = carry
        counts_ref[...] = jnp.broadcast_to(carry, counts_ref.shape)
        slab = jnp.where(lane == COL_E1, e1, 0.0)
        slab = jnp.where(lane == COL_E2, e2, slab)
        slab = jnp.where(lane == COL_W1, w1, slab)
        slab = jnp.where(lane == COL_W2, w2, slab)
        slab = jnp.where(lane == COL_R1, r1, slab)
        slab = jnp.where(lane == COL_R2, r2, slab)
        route_ref[...] = slab.T[0:SUBLANES, :]

    def front_load():
        xb_ref[0:HALO, :] = x_prev_ref[...].astype(bf16)
        xb_ref[HALO:HALO + ts, :] = x_ref[...].astype(bf16)
        xb_ref[HALO + ts:HALO + ts + HALO, :] = x_next_ref[...].astype(bf16)

    def front_conv():
        xe = xb_ref[...]
        g_e = proj(xe, 0, D_CONV) * proj(xe, 2 * D_CONV, 3 * D_CONV)
        row_e = lax.broadcasted_iota(jnp.int32, (ts + 2 * HALO, 1), 0)
        has_prev = jnp.where(j > 0, 1.0, 0.0)
        has_next = jnp.where(j < nt - 1, 1.0, 0.0)
        g_e = g_e * jnp.where(row_e < HALO, has_prev, jnp.where(row_e >= HALO + ts, has_next, 1.0))
        conv = (g_e[HALO - 1:HALO - 1 + ts, :] * conv_w_ref[0:1, :] + g_e[HALO:HALO + ts, :] * conv_w_ref[1:2, :]
                + g_e[HALO + 1:HALO + 1 + ts, :] * conv_w_ref[2:3, :] + conv_b_ref[...])
        y_a = proj(xb_ref[HALO:HALO + ts, :], D_CONV, 2 * D_CONV) * conv
        ymix_ref[:, 0:D_CONV] = y_a.astype(bf16)

    def front_gate_proj():
        xm = xb_ref[HALO:HALO + ts, :]
        return proj(xm, 3 * D_CONV, 3 * D_CONV + D_SG), proj(xm, 3 * D_CONV + D_SG, 3 * D_CONV + 2 * D_SG)

    def front_gate(u_pre, v_pre):
        u = _gelu_tanh(u_pre)
        v = _gelu_tanh(v_pre)
        v_ln = _layer_norm(v, sg_g_ref[...], sg_b_ref[...]).astype(bf16)
        first_head = lax.broadcasted_iota(jnp.int32, (CHUNK, LANES), 1) < SG_HEAD_DIM
        for q0 in range(0, ts, 2 * CHUNK):
            q1 = q0 + CHUNK
            for hp in range(N_SG_HEADS // 2):
                c0 = hp * LANES
                rhs = jnp.concatenate([v_ln[q0:q0 + CHUNK, c0:c0 + LANES], v_ln[q1:q1 + CHUNK, c0:c0 + LANES]], axis=1)
                res = _dot(ws_ref[hp], rhs)
                bias = bs_ref[:, c0:c0 + LANES]
                m0 = jnp.where(first_head, res[0:CHUNK, 0:LANES], res[CHUNK:2 * CHUNK, 0:LANES]) + bias
                m1 = jnp.where(first_head, res[0:CHUNK, LANES:2 * LANES], res[CHUNK:2 * CHUNK, LANES:2 * LANES]) + bias
                ymix_ref[q0:q0 + CHUNK, D_CONV + c0:D_CONV + c0 + LANES] = (u[q0:q0 + CHUNK, c0:c0 + LANES] * m0).astype(bf16)
                ymix_ref[q1:q1 + CHUNK, D_CONV + c0:D_CONV + c0 + LANES] = (u[q1:q1 + CHUNK, c0:c0 + LANES] * m1).astype(bf16)

    def front_out_a():
        resid_ref[...] = alpha * x_ref[...] + b_o_ref[...] + _dot(ymix_ref[:, 0:D_CONV], w_o_ref[0:D_CONV, :])

    def front_out_b():
        resid_ref[...] += _dot(ymix_ref[:, D_CONV:D_MODEL], w_o_ref[D_CONV:D_MODEL, :])

    front_load()
    gate_pre = front_gate_proj()
    x1b = back_norm()
    routing = back_route(x1b)
    front_conv()
    back_rank(*routing)
    front_out_a()
    front_gate(*gate_pre)
    front_out_b()


def _mixer(x, p, alpha):
    nb, s, d = x.shape
    ts = SEQ_TILE
    nt = s // ts
    hb = ts // HALO
    n_tiles = nb * nt
    n_total = n_tiles * ts

    front = lambda t: jnp.minimum(t, n_tiles - 1)
    back = lambda t: jnp.maximum(t - 1, 0)
    cur = lambda t: (front(t) // nt, front(t) % nt, 0)
    prv = lambda t: (front(t) // nt, jnp.maximum((front(t) % nt) * hb - 1, 0), 0)
    nxt = lambda t: (front(t) // nt, jnp.minimum((front(t) % nt + 1) * hb, s // HALO - 1), 0)
    full = lambda a: pl.BlockSpec(a.shape, lambda t: (0,) * a.ndim)
    in_specs = [pl.BlockSpec((None, ts, d), cur), pl.BlockSpec((None, HALO, d), prv),
                pl.BlockSpec((None, HALO, d), nxt)] + [full(a) for a in p]
    out_shape = (jax.ShapeDtypeStruct((n_total, D_PACK), jnp.uint32),
                 jax.ShapeDtypeStruct((SUBLANES, n_total), jnp.float32),
                 jax.ShapeDtypeStruct((SUBLANES, LANES), jnp.float32))
    row_blk = lambda t: (back(t), 0)
    out_specs = (pl.BlockSpec((ts, D_PACK), row_blk),
                 pl.BlockSpec((SUBLANES, ts), lambda t: (0, back(t))),
                 pl.BlockSpec((SUBLANES, LANES), lambda t: (0, 0)))
    return pl.pallas_call(
        functools.partial(_mixer_kernel, ts=ts, nt=nt, n_tiles=n_tiles, alpha=alpha),
        grid=(n_tiles + 1,),
        in_specs=in_specs,
        out_specs=out_specs,
        out_shape=out_shape,
        scratch_shapes=[pltpu.VMEM((ts + 2 * HALO, d), jnp.bfloat16),
                        pltpu.VMEM((ts, d), jnp.bfloat16),
                        pltpu.VMEM((ts, ts), jnp.bfloat16),
                        pltpu.VMEM((ts, d), jnp.float32),
                        pltpu.VMEM((1, LANES), jnp.float32)],
        compiler_params=pltpu.CompilerParams(dimension_semantics=("arbitrary",),
                                             vmem_limit_bytes=VMEM_LIMIT_BYTES),
        cost_estimate=pl.CostEstimate(
            flops=2 * n_total * (d * (3 * D_CONV + 2 * D_SG) + CHUNK * D_SG + d * d + d * LANES + ts * LANES),
            transcendentals=n_total * (2 * D_SG + 2 * LANES),
            bytes_accessed=n_total * (d * 4 + D_PACK * 4 + SUBLANES * 4)),
        name="mixer",
    )(x, x, x, *p)


def _sc_mesh():
    return plsc.VectorSubcoreMesh(core_axis_name="c", subcore_axis_name="s",
                                  num_cores=SC_CORES, num_subcores=SC_SUBCORES)


def _sc_worker_base(per_worker):
    return (lax.axis_index("s") * SC_CORES + lax.axis_index("c")) * per_worker


def _sc_dispatch(pos1, pos2, xp, n_rows):
    n, dp = xp.shape
    per_worker = n // SC_WORKERS
    steps = per_worker // SC_ROWS

    def body(pos1_hbm, pos2_hbm, xp_hbm, xs_hbm, idx1_v, idx2_v, rows_v, sem):
        base = _sc_worker_base(per_worker)

        @pl.loop(0, steps)
        def _(k):
            off = pl.multiple_of(base + k * SC_ROWS, SC_ROWS)
            pltpu.sync_copy(pos1_hbm.at[pl.ds(off, SC_ROWS)], idx1_v)
            pltpu.sync_copy(pos2_hbm.at[pl.ds(off, SC_ROWS)], idx2_v)
            pltpu.sync_copy(xp_hbm.at[pl.ds(off, SC_ROWS)], rows_v)
            c1 = pltpu.async_copy(rows_v, xs_hbm.at[idx1_v], sem)
            c2 = pltpu.async_copy(rows_v, xs_hbm.at[idx2_v], sem)
            c1.wait()
            c2.wait()

    return pl.kernel(
        body,
        out_type=jax.ShapeDtypeStruct((n_rows, dp), xp.dtype),
        mesh=_sc_mesh(),
        scratch_types=[pltpu.VMEM((SC_ROWS,), jnp.int32), pltpu.VMEM((SC_ROWS,), jnp.int32),
                       pltpu.VMEM((SC_ROWS, dp), xp.dtype), pltpu.SemaphoreType.DMA],
        compiler_params=pltpu.CompilerParams(use_tc_tiling_on_sc=True),
        cost_estimate=pl.CostEstimate(flops=0, transcendentals=0, bytes_accessed=3 * n * dp * 4 + 2 * n * 4),
        name="sc_dispatch",
    )(pos1, pos2, xp)


def _sc_gather(pos1, pos2, ys):
    n = pos1.shape[0]
    dp = ys.shape[1]
    per_worker = n // SC_WORKERS
    steps = per_worker // SC_ROWS

    def body(pos1_hbm, pos2_hbm, ys_hbm, out_hbm, idx_v, rows_v, sem):
        base = _sc_worker_base(per_worker)

        @pl.loop(0, steps)
        def _(k):
            off = pl.multiple_of(base + k * SC_ROWS, SC_ROWS)
            for slot, pos_hbm in enumerate((pos1_hbm, pos2_hbm)):
                pltpu.sync_copy(pos_hbm.at[pl.ds(off, SC_ROWS)], idx_v)
                pltpu.async_copy(ys_hbm.at[idx_v], rows_v, sem).wait()
                pltpu.sync_copy(rows_v, out_hbm.at[slot, pl.ds(off, SC_ROWS)])

    return pl.kernel(
        body,
        out_type=jax.ShapeDtypeStruct((2, n, dp), ys.dtype),
        mesh=_sc_mesh(),
        scratch_types=[pltpu.VMEM((SC_ROWS,), jnp.int32), pltpu.VMEM((SC_ROWS, dp), ys.dtype),
                       pltpu.SemaphoreType.DMA],
        compiler_params=pltpu.CompilerParams(use_tc_tiling_on_sc=True),
        cost_estimate=pl.CostEstimate(flops=0, transcendentals=0, bytes_accessed=4 * n * dp * 4 + 2 * n * 4),
        name="sc_gather",
    )(pos1, pos2, ys)


def _expert_kernel(te_ref, first_ref, nsub_ref, slot_ref, next_ref, nused_ref, xs_ref, *refs, cast_weights):
    i = pl.program_id(0)
    bf16 = jnp.bfloat16
    sub = EXPERT_SUBTILE
    slot = slot_ref[i]

    if cast_weights:
        (wg_hbm, wu_hbm, wd_hbm, ys_ref, wgu_out, wd_out,
         wg_buf, wu_buf, wd_buf, wgu_bf_ref, wd_bf_ref, sem, out_sem) = refs
        wgu_ref, wd_ref = wgu_bf_ref, wd_bf_ref

        def weight_copies(expert, s):
            return (pltpu.make_async_copy(wg_hbm.at[expert], wg_buf.at[s], sem.at[s]),
                    pltpu.make_async_copy(wu_hbm.at[expert], wu_buf.at[s], sem.at[s]),
                    pltpu.make_async_copy(wd_hbm.at[expert], wd_buf.at[s], sem.at[s]))

        def emit_copies(expert):
            return (pltpu.make_async_copy(wgu_bf_ref, wgu_out.at[expert], out_sem),
                    pltpu.make_async_copy(wd_bf_ref, wd_out.at[expert], out_sem))
    else:
        wgu_hbm, wd_hbm, ys_ref, wgu_buf, wd_buf, sem = refs
        wgu_ref, wd_ref = wgu_buf.at[slot], wd_buf.at[slot]

        def weight_copies(expert, s):
            return (pltpu.make_async_copy(wgu_hbm.at[expert], wgu_buf.at[s], sem.at[s]),
                    pltpu.make_async_copy(wd_hbm.at[expert], wd_buf.at[s], sem.at[s]))

    @pl.when(first_ref[i] == 1)
    def _():
        @pl.when(i == 0)
        def _():
            for c in weight_copies(te_ref[i], slot):
                c.start()

        for c in weight_copies(te_ref[i], slot):
            c.wait()

        @pl.when(next_ref[i] >= 0)
        def _():
            for c in weight_copies(next_ref[i], 1 - slot):
                c.start()

        if cast_weights:
            @pl.when(i > 0)
            def _():
                for c in emit_copies(te_ref[i]):
                    c.wait()

            wgu_bf_ref[:, 0:D_EXPERT] = wg_buf[slot].astype(bf16)
            wgu_bf_ref[:, D_EXPERT:2 * D_EXPERT] = wu_buf[slot].astype(bf16)
            wd_bf_ref[...] = wd_buf[slot].astype(bf16)
            for c in emit_copies(te_ref[i]):
                c.start()

    def swiglu(starts):
        rows = [pl.ds(r0, sub) for r0 in starts]
        gate_up = []
        for rw in rows:
            lo, hi = _unpack_halves(xs_ref[rw, :])
            gate_up.append(_dot(lo.astype(bf16), wgu_ref[0:D_PACK, :])
                           + _dot(hi.astype(bf16), wgu_ref[D_PACK:D_MODEL, :]))
        for rw, gu in zip(rows, gate_up):
            g = gu[:, 0:D_EXPERT]
            h = g * jax.nn.sigmoid(g) * gu[:, D_EXPERT:2 * D_EXPERT]
            ys_ref[rw, :] = _pack_halves(_dot(h.astype(bf16), wd_ref[...]))

    n_sub = nsub_ref[i]

    def pair(k, c):
        r0 = pl.multiple_of(k * (2 * sub), 2 * sub)
        swiglu([r0, r0 + sub])
        return c

    lax.fori_loop(0, n_sub // 2, pair, 0)

    @pl.when(n_sub % 2 == 1)
    def _():
        swiglu([pl.multiple_of((n_sub - 1) * sub, sub)])

    def clear(k, c):
        ys_ref[pl.ds(pl.multiple_of(k * sub, sub), sub), :] = jnp.zeros((sub, D_PACK), ys_ref.dtype)
        return c

    lax.fori_loop(n_sub, jnp.where(i < nused_ref[0], ys_ref.shape[0] // sub, 0), clear, 0)

    if cast_weights:
        @pl.when(i == pl.num_programs(0) - 1)
        def _():
            for c in emit_copies(te_ref[i]):
                c.wait()


def _experts(tiles, xs, weights):
    n_rows, dp = xs.shape
    d = D_MODEL
    tm = EXPERT_TILE
    cast_weights = len(weights) == 3
    row_blk = lambda i, te, fi, ns, sl, nx, nu: (jnp.minimum(i, nu[0] - 1), 0)
    any_spec = pl.BlockSpec(memory_space=pl.ANY)
    ys_shape = jax.ShapeDtypeStruct((n_rows, dp), jnp.uint32)
    ys_spec = pl.BlockSpec((tm, dp), row_blk)
    wgu_bf = ((d, 2 * D_EXPERT), jnp.bfloat16)
    wd_bf = ((D_EXPERT, d), jnp.bfloat16)
    if cast_weights:
        out_shape = (ys_shape, jax.ShapeDtypeStruct((N_EXPERTS,) + wgu_bf[0], wgu_bf[1]),
                     jax.ShapeDtypeStruct((N_EXPERTS,) + wd_bf[0], wd_bf[1]))
        out_specs = (ys_spec, any_spec, any_spec)
        scratch = [pltpu.VMEM((2, d, D_EXPERT), jnp.float32), pltpu.VMEM((2, d, D_EXPERT), jnp.float32),
                   pltpu.VMEM((2, D_EXPERT, d), jnp.float32), pltpu.VMEM(*wgu_bf), pltpu.VMEM(*wd_bf),
                   pltpu.SemaphoreType.DMA((2,)), pltpu.SemaphoreType.DMA]
        weight_bytes = N_EXPERTS * 3 * d * D_EXPERT * (4 + 2)
    else:
        out_shape, out_specs = ys_shape, ys_spec
        scratch = [pltpu.VMEM((2,) + wgu_bf[0], wgu_bf[1]), pltpu.VMEM((2,) + wd_bf[0], wd_bf[1]),
                   pltpu.SemaphoreType.DMA((2,))]
        weight_bytes = N_EXPERTS * 3 * d * D_EXPERT * 2
    grid_spec = pltpu.PrefetchScalarGridSpec(
        num_scalar_prefetch=6,
        grid=(n_rows // tm,),
        in_specs=[pl.BlockSpec((tm, dp), row_blk)] + [any_spec] * len(weights),
        out_specs=out_specs,
        scratch_shapes=scratch,
    )
    return pl.pallas_call(
        functools.partial(_expert_kernel, cast_weights=cast_weights),
        grid_spec=grid_spec,
        out_shape=out_shape,
        compiler_params=pltpu.CompilerParams(dimension_semantics=("arbitrary",),
                                             vmem_limit_bytes=VMEM_LIMIT_BYTES),
        cost_estimate=pl.CostEstimate(flops=2 * n_rows * 3 * d * D_EXPERT, transcendentals=n_rows * D_EXPERT,
                                      bytes_accessed=2 * n_rows * dp * 4 + weight_bytes),
        name="experts",
    )(*tiles, xs, *weights)


def _combine_kernel(xp_ref, route_ref, y1_ref, y2_ref, g_ref, b_ref, out_ref, *, alpha):
    tt = route_ref.shape[1]
    cols = jnp.concatenate([route_ref[...], jnp.zeros((LANES - SUBLANES, tt), jnp.float32)], axis=0).T
    w1 = cols[:, COL_W1:COL_W1 + 1]
    w2 = cols[:, COL_W2:COL_W2 + 1]
    lo1, hi1 = _unpack_halves(y1_ref[...])
    lo2, hi2 = _unpack_halves(y2_ref[...])
    lo, hi = _unpack_halves(xp_ref[...])
    y = jnp.concatenate([alpha * lo + (w1 * lo1 + w2 * lo2), alpha * hi + (w1 * hi1 + w2 * hi2)], axis=1)
    out_ref[...] = _layer_norm(y, g_ref[...], b_ref[...])


def _combine(xp, route, yg, g, b, alpha):
    n, d = xp.shape[0], D_MODEL
    tt = TOKEN_TILE
    return pl.pallas_call(
        functools.partial(_combine_kernel, alpha=alpha),
        grid=(n // tt,),
        in_specs=[pl.BlockSpec((tt, D_PACK), lambda i: (i, 0)),
                  pl.BlockSpec((SUBLANES, tt), lambda i: (0, i)),
                  pl.BlockSpec((None, tt, D_PACK), lambda i: (0, i, 0)),
                  pl.BlockSpec((None, tt, D_PACK), lambda i: (1, i, 0)),
                  pl.BlockSpec((1, d), lambda i: (0, 0)),
                  pl.BlockSpec((1, d), lambda i: (0, 0))],
        out_specs=pl.BlockSpec((tt, d), lambda i: (i, 0)),
        out_shape=jax.ShapeDtypeStruct((n, d), jnp.float32),
        compiler_params=pltpu.CompilerParams(dimension_semantics=("arbitrary",),
                                             vmem_limit_bytes=VMEM_LIMIT_BYTES),
        cost_estimate=pl.CostEstimate(flops=12 * n * d, transcendentals=n,
                                      bytes_accessed=n * (d * 4 + 3 * D_PACK * 4 + SUBLANES * 4)),
        name="combine",
    )(xp, route, yg, yg, g, b)


def _row(a):
    return a.reshape(1, -1)


def _mixer_params(w_in, b_in, conv_w, conv_b, sg_ln_g, sg_ln_b, w_s, b_s, w_o, b_o, ln1_g, ln1_b,
                  w_rc, b_rc, w_rf, b_rf):
    bf16 = jnp.bfloat16
    ws_pairs = w_s.reshape(N_SG_HEADS // 2, 2 * CHUNK, CHUNK).astype(bf16)
    bs_full = jnp.repeat(b_s.T, SG_HEAD_DIM, axis=1)
    pad = LANES - N_EXPERTS - N_EXPERT_GROUPS
    w_r = jnp.pad(jnp.concatenate([w_rf, w_rc], axis=1), ((0, 0), (0, pad))).astype(bf16)
    b_r = jnp.pad(jnp.concatenate([b_rf, b_rc]), (0, pad)).reshape(1, LANES)
    return (w_in.astype(bf16), _row(b_in), conv_w, _row(conv_b), _row(sg_ln_g), _row(sg_ln_b),
            ws_pairs, bs_full, w_o.astype(bf16), _row(b_o), _row(ln1_g), _row(ln1_b), w_r, b_r)


def _encoder_layer(x, mixer_params, expert_weights, ln2_g, ln2_b, alpha):
    n = x.shape[0] * x.shape[1]
    xp, route, carry = _mixer(x, mixer_params, alpha)

    tm = EXPERT_TILE
    n_tiles = 2 * n // tm + N_EXPERTS
    cnt = carry[0, 0:N_EXPERTS].astype(jnp.int32)
    tiles_e = jnp.maximum((cnt + tm - 1) // tm, 1 if len(expert_weights) == 3 else 0)
    experts = jnp.arange(N_EXPERTS, dtype=jnp.int32)[:, None]
    up_to = experts.T <= experts
    tile_end = jnp.sum(jnp.where(up_to, tiles_e[None, :], 0), axis=1)
    tile_start = tile_end - tiles_e
    row_off = tile_start * tm

    def sorted_pos(e_row, r_row):
        e = route[e_row].astype(jnp.int32)
        return jnp.sum(jnp.where(e[None, :] == experts, row_off[:, None], 0), axis=0) + route[r_row].astype(jnp.int32)

    pos1 = sorted_pos(COL_E1, COL_R1)
    pos2 = sorted_pos(COL_E2, COL_R2)
    tile_ids = jnp.arange(n_tiles, dtype=jnp.int32)
    n_used = tile_end[N_EXPERTS - 1:N_EXPERTS]
    tile_expert = jnp.sum((tile_end[None, :] <= tile_ids[:, None]).astype(jnp.int32), axis=1)
    last_used = jnp.sum((tile_end <= n_used[0] - 1).astype(jnp.int32))
    tile_expert = jnp.minimum(tile_expert, last_used)
    tile_first = (jnp.any(tile_ids[:, None] == tile_start[None, :], axis=1) & (tile_ids < n_used[0])) | (tile_ids == 0)
    tile_first = tile_first.astype(jnp.int32)
    rows_left = jnp.sum(jnp.where(tile_expert[:, None] == experts.T, (row_off + cnt)[None, :], 0), axis=1) - tile_ids * tm
    tile_nsub = jnp.where(tile_ids < n_used[0], (jnp.clip(rows_left, 0, tm) + EXPERT_SUBTILE - 1) // EXPERT_SUBTILE, 0)

    used = tiles_e > 0
    ordinal = jnp.sum((up_to & used[None, :]).astype(jnp.int32), axis=1) - 1
    e_ids = experts[:, 0]
    later_used = used[None, :] & (e_ids[None, :] > e_ids[:, None])
    next_used = jnp.min(jnp.where(later_used, e_ids[None, :], N_EXPERTS), axis=1)
    next_used = jnp.where(next_used < N_EXPERTS, next_used, -1)
    of_tile = tile_expert[:, None] == experts.T
    tile_slot = jnp.sum(jnp.where(of_tile, ordinal[None, :], 0), axis=1) % 2
    tile_next = jnp.sum(jnp.where(of_tile, next_used[None, :], 0), axis=1)

    tiles = (tile_expert, tile_first, tile_nsub.astype(jnp.int32), tile_slot.astype(jnp.int32),
             tile_next.astype(jnp.int32), n_used)
    xs = _sc_dispatch(pos1, pos2, xp, n_tiles * tm)
    if len(expert_weights) == 3:
        ys, *expert_weights = _experts(tiles, xs, expert_weights)
    else:
        ys = _experts(tiles, xs, expert_weights)
    yg = _sc_gather(pos1, pos2, ys)
    return _combine(xp, route, yg, _row(ln2_g), _row(ln2_b), alpha).reshape(x.shape), tuple(expert_weights)


def kernel(x_prompt, x_sample, w_in, b_in, conv_w, conv_b, sg_ln_g, sg_ln_b, w_s, b_s, w_o, b_o, ln1_g, ln1_b, w_rc, b_rc, w_rf, b_rf, w_gate, w_up, w_down, ln2_g, ln2_b):
    depth = w_in.shape[0]
    alpha = (2.0 * depth) ** 0.25
    xs = (x_prompt, x_sample)
    for l in range(depth):
        mixer_params = _mixer_params(w_in[l], b_in[l], conv_w[l], conv_b[l], sg_ln_g[l], sg_ln_b[l], w_s[l], b_s[l],
                                     w_o[l], b_o[l], ln1_g[l], ln1_b[l], w_rc[l], b_rc[l], w_rf[l], b_rf[l])
        expert_weights = (w_gate[l], w_up[l], w_down[l])
        outs = [None] * len(xs)
        for k in sorted(range(len(xs)), key=lambda k: -xs[k].shape[0] * xs[k].shape[1]):
            outs[k], expert_weights = _encoder_layer(xs[k], mixer_params, expert_weights, ln2_g[l], ln2_b[l], alpha)
        xs = tuple(outs)
    return xs
```

```python
import functools

import jax
import jax.numpy as jnp
from jax import lax
from jax.experimental import pallas as pl
from jax.experimental.pallas import tpu as pltpu
from jax.experimental.pallas import tpu_sc as plsc

D_MODEL = 1024
D_CONV = 512
D_SG = 512
N_SG_HEADS = 8
SG_HEAD_DIM = D_SG // N_SG_HEADS
CHUNK = 128
N_EXPERT_GROUPS = 4
EXPERTS_PER_GROUP = 8
N_EXPERTS = N_EXPERT_GROUPS * EXPERTS_PER_GROUP
D_EXPERT = 512
LN_EPS = 1e-5
D_PACK = D_MODEL // 2

LANES = 128
SUBLANES = 8
HALO = 16
SEQ_TILE = 512
EXPERT_TILE = 1024
EXPERT_SUBTILE = 256
TOKEN_TILE = 1024
COMBINE_SLOTS = 3
SC_CORES = 2
SC_SUBCORES = 16
SC_WORKERS = SC_CORES * SC_SUBCORES
SC_ROWS = 64
VMEM_LIMIT_BYTES = 56 * 1024 * 1024

COL_E1, COL_E2, COL_W1, COL_W2, COL_R1, COL_R2 = 0, 1, 2, 3, 4, 5
COARSE_OFF = N_EXPERTS


def _dot(a, b):
    return jnp.dot(a, b, preferred_element_type=jnp.float32)


def _gelu_tanh(x):
    return 0.5 * x * (1.0 + jnp.tanh(0.7978845608028654 * (x + 0.044715 * (x * x * x))))


def _layer_norm(x, g, b):
    mu = jnp.mean(x, axis=-1, keepdims=True)
    xc = x - mu
    var = jnp.mean(xc * xc, axis=-1, keepdims=True)
    return xc * lax.rsqrt(var + LN_EPS) * g + b


def _pack_halves(x):
    u32 = jnp.uint32
    lo = lax.bitcast_convert_type(x[:, 0:D_PACK].astype(jnp.bfloat16).astype(jnp.float32), u32)
    hi = lax.bitcast_convert_type(x[:, D_PACK:D_MODEL].astype(jnp.bfloat16).astype(jnp.float32), u32)
    return (hi & u32(0xFFFF0000)) | (lo >> u32(16))


def _unpack_halves(w):
    u32 = jnp.uint32
    lo = lax.bitcast_convert_type(w << u32(16), jnp.float32)
    hi = lax.bitcast_convert_type(w & u32(0xFFFF0000), jnp.float32)
    return lo, hi


def _mixer_kernel(x_ref, x_prev_ref, x_next_ref, w_in_ref, b_in_ref, conv_w_ref, conv_b_ref,
                  sg_g_ref, sg_b_ref, ws_ref, bs_ref, w_o_ref, b_o_ref, ln1_g_ref, ln1_b_ref,
                  w_r_ref, b_r_ref,
                  xp_ref, route_ref, counts_ref,
                  xb_ref, ymix_ref, tri_ref, resid_ref, carry_ref, *, ts, nt, n_tiles, alpha):
    t = pl.program_id(0)
    j = jnp.minimum(t, n_tiles - 1) % nt
    bf16 = jnp.bfloat16

    @pl.when(t == 0)
    def _():
        carry_ref[...] = jnp.zeros_like(carry_ref)
        resid_ref[...] = jnp.zeros_like(resid_ref)
        ri = lax.broadcasted_iota(jnp.int32, (ts, ts), 0)
        ci = lax.broadcasted_iota(jnp.int32, (ts, ts), 1)
        tri_ref[...] = jnp.where(ri > ci, 1.0, 0.0).astype(bf16)

    lane = lax.broadcasted_iota(jnp.int32, (ts, LANES), 1)
    lane_f = lane.astype(jnp.float32)
    neg = jnp.float32(-jnp.inf)
    big = jnp.float32(1e9)
    is_c = (lane >= COARSE_OFF) & (lane < COARSE_OFF + N_EXPERT_GROUPS)

    def proj(lhs, lo, hi):
        return _dot(lhs, w_in_ref[:, lo:hi]) + b_in_ref[:, lo:hi]

    def back_norm():
        x1 = _layer_norm(resid_ref[...], ln1_g_ref[...], ln1_b_ref[...])
        xp_ref[...] = _pack_halves(x1)
        return x1.astype(bf16)

    def back_route(x1b):
        logits = _dot(x1b, w_r_ref[...]) + b_r_ref[...]
        lc = jnp.where(is_c, logits, neg)
        mx = jnp.max(lc, axis=1, keepdims=True)
        grp = jnp.min(jnp.where(lc == mx, lane_f - COARSE_OFF, big), axis=1, keepdims=True)
        p_grp = 1.0 / jnp.sum(jnp.where(is_c, jnp.exp(logits - mx), 0.0), axis=1, keepdims=True)
        grp_lo = grp * EXPERTS_PER_GROUP
        in_grp = (lane_f >= grp_lo) & (lane_f < grp_lo + EXPERTS_PER_GROUP)
        lf = jnp.where(in_grp, logits, neg)
        v1 = jnp.max(lf, axis=1, keepdims=True)
        e1 = jnp.min(jnp.where(lf == v1, lane_f, big), axis=1, keepdims=True)
        lf2 = jnp.where(lane_f == e1, neg, lf)
        v2 = jnp.max(lf2, axis=1, keepdims=True)
        e2 = jnp.min(jnp.where(lf2 == v2, lane_f, big), axis=1, keepdims=True)
        a = jnp.exp(v2 - v1)
        return e1, e2, p_grp / (1.0 + a), p_grp * a / (1.0 + a)

    def back_rank(e1, e2, w1, w2):
        hit1 = lane_f == e1
        hit2 = lane_f == e2
        onehot = jnp.where(hit1 | hit2, 1.0, 0.0)
        carry = carry_ref[...]
        before = _dot(tri_ref[...], onehot.astype(bf16)) + carry
        r1 = jnp.sum(jnp.where(hit1, before, 0.0), axis=1, keepdims=True)
        r2 = jnp.sum(jnp.where(hit2, before, 0.0), axis=1, keepdims=True)
        carry = jnp.where(t > 0, carry + jnp.sum(onehot, axis=0, keepdims=True), carry)
        carry_ref[...] = carry
        counts_ref[...] = jnp.broadcast_to(carry, counts_ref.shape)
        slab = jnp.where(lane == COL_E1, e1, 0.0)
        slab = jnp.where(lane == COL_E2, e2, slab)
        slab = jnp.where(lane == COL_W1, w1, slab)
        slab = jnp.where(lane == COL_W2, w2, slab)
        slab = jnp.where(lane == COL_R1, r1, slab)
        slab = jnp.where(lane == COL_R2, r2, slab)
        route_ref[...] = slab.T[0:SUBLANES, :]

    def front_load():
        xb_ref[0:HALO, :] = x_prev_ref[...].astype(bf16)
        xb_ref[HALO:HALO + ts, :] = x_ref[...].astype(bf16)
        xb_ref[HALO + ts:HALO + ts + HALO, :] = x_next_ref[...].astype(bf16)

    def front_conv():
        xe = xb_ref[...]
        g_e = proj(xe, 0, D_CONV) * proj(xe, 2 * D_CONV, 3 * D_CONV)
        row_e = lax.broadcasted_iota(jnp.int32, (ts + 2 * HALO, 1), 0)
        has_prev = jnp.where(j > 0, 1.0, 0.0)
        has_next = jnp.where(j < nt - 1, 1.0, 0.0)
        g_e = g_e * jnp.where(row_e < HALO, has_prev, jnp.where(row_e >= HALO + ts, has_next, 1.0))
        conv = (g_e[HALO - 1:HALO - 1 + ts, :] * conv_w_ref[0:1, :] + g_e[HALO:HALO + ts, :] * conv_w_ref[1:2, :]
                + g_e[HALO + 1:HALO + 1 + ts, :] * conv_w_ref[2:3, :] + conv_b_ref[...])
        y_a = proj(xb_ref[HALO:HALO + ts, :], D_CONV, 2 * D_CONV) * conv
        ymix_ref[:, 0:D_CONV] = y_a.astype(bf16)

    def front_gate_proj():
        xm = xb_ref[HALO:HALO + ts, :]
        return proj(xm, 3 * D_CONV, 3 * D_CONV + D_SG), proj(xm, 3 * D_CONV + D_SG, 3 * D_CONV + 2 * D_SG)

    def front_gate(u_pre, v_pre):
        u = _gelu_tanh(u_pre)
        v = _gelu_tanh(v_pre)
        v_ln = _layer_norm(v, sg_g_ref[...], sg_b_ref[...]).astype(bf16)
        first_head = lax.broadcasted_iota(jnp.int32, (CHUNK, LANES), 1) < SG_HEAD_DIM
        for q0 in range(0, ts, 2 * CHUNK):
            q1 = q0 + CHUNK
            for hp in range(N_SG_HEADS // 2):
                c0 = hp * LANES
                rhs = jnp.concatenate([v_ln[q0:q0 + CHUNK, c0:c0 + LANES], v_ln[q1:q1 + CHUNK, c0:c0 + LANES]], axis=1)
                res = _dot(ws_ref[hp], rhs)
                bias = bs_ref[:, c0:c0 + LANES]
                m0 = jnp.where(first_head, res[0:CHUNK, 0:LANES], res[CHUNK:2 * CHUNK, 0:LANES]) + bias
                m1 = jnp.where(first_head, res[0:CHUNK, LANES:2 * LANES], res[CHUNK:2 * CHUNK, LANES:2 * LANES]) + bias
                ymix_ref[q0:q0 + CHUNK, D_CONV + c0:D_CONV + c0 + LANES] = (u[q0:q0 + CHUNK, c0:c0 + LANES] * m0).astype(bf16)
                ymix_ref[q1:q1 + CHUNK, D_CONV + c0:D_CONV + c0 + LANES] = (u[q1:q1 + CHUNK, c0:c0 + LANES] * m1).astype(bf16)

    def front_out_a():
        resid_ref[...] = alpha * x_ref[...] + b_o_ref[...] + _dot(ymix_ref[:, 0:D_CONV], w_o_ref[0:D_CONV, :])

    def front_out_b():
        resid_ref[...] += _dot(ymix_ref[:, D_CONV:D_MODEL], w_o_ref[D_CONV:D_MODEL, :])

    front_load()
    gate_pre = front_gate_proj()
    x1b = back_norm()
    routing = back_route(x1b)
    front_conv()
    back_rank(*routing)
    front_out_a()
    front_gate(*gate_pre)
    front_out_b()


def _mixer(x, p, alpha):
    nb, s, d = x.shape
    ts = SEQ_TILE
    nt = s // ts
    hb = ts // HALO
    n_tiles = nb * nt
    n_total = n_tiles * ts

    front = lambda t: jnp.minimum(t, n_tiles - 1)
    back = lambda t: jnp.maximum(t - 1, 0)
    cur = lambda t: (front(t) // nt, front(t) % nt, 0)
    prv = lambda t: (front(t) // nt, jnp.maximum((front(t) % nt) * hb - 1, 0), 0)
    nxt = lambda t: (front(t) // nt, jnp.minimum((front(t) % nt + 1) * hb, s // HALO - 1), 0)
    full = lambda a: pl.BlockSpec(a.shape, lambda t: (0,) * a.ndim)
    in_specs = [pl.BlockSpec((None, ts, d), cur), pl.BlockSpec((None, HALO, d), prv),
                pl.BlockSpec((None, HALO, d), nxt)] + [full(a) for a in p]
    out_shape = (jax.ShapeDtypeStruct((n_total, D_PACK), jnp.uint32),
                 jax.ShapeDtypeStruct((SUBLANES, n_total), jnp.float32),
                 jax.ShapeDtypeStruct((SUBLANES, LANES), jnp.float32))
    row_blk = lambda t: (back(t), 0)
    out_specs = (pl.BlockSpec((ts, D_PACK), row_blk),
                 pl.BlockSpec((SUBLANES, ts), lambda t: (0, back(t))),
                 pl.BlockSpec((SUBLANES, LANES), lambda t: (0, 0)))
    return pl.pallas_call(
        functools.partial(_mixer_kernel, ts=ts, nt=nt, n_tiles=n_tiles, alpha=alpha),
        grid=(n_tiles + 1,),
        in_specs=in_specs,
        out_specs=out_specs,
        out_shape=out_shape,
        scratch_shapes=[pltpu.VMEM((ts + 2 * HALO, d), jnp.bfloat16),
                        pltpu.VMEM((ts, d), jnp.bfloat16),
                        pltpu.VMEM((ts, ts), jnp.bfloat16),
                        pltpu.VMEM((ts, d), jnp.float32),
                        pltpu.VMEM((1, LANES), jnp.float32)],
        compiler_params=pltpu.CompilerParams(dimension_semantics=("arbitrary",),
                                             vmem_limit_bytes=VMEM_LIMIT_BYTES),
        cost_estimate=pl.CostEstimate(
            flops=2 * n_total * (d * (3 * D_CONV + 2 * D_SG) + CHUNK * D_SG + d * d + d * LANES + ts * LANES),
            transcendentals=n_total * (2 * D_SG + 2 * LANES),
            bytes_accessed=n_total * (d * 4 + D_PACK * 4 + SUBLANES * 4)),
        name="mixer",
    )(x, x, x, *p)


def _sc_mesh():
    return plsc.VectorSubcoreMesh(core_axis_name="c", subcore_axis_name="s",
                                  num_cores=SC_CORES, num_subcores=SC_SUBCORES)


def _sc_worker_base(per_worker):
    return (lax.axis_index("s") * SC_CORES + lax.axis_index("c")) * per_worker


def _sc_dispatch(pos1, pos2, xp, n_rows):
    n, dp = xp.shape
    per_worker = n // SC_WORKERS
    steps = per_worker // SC_ROWS

    def body(pos1_hbm, pos2_hbm, xp_hbm, xs_hbm, idx1_v, idx2_v, rows_v, sem):
        base = _sc_worker_base(per_worker)

        @pl.loop(0, steps)
        def _(k):
            off = pl.multiple_of(base + k * SC_ROWS, SC_ROWS)
            pltpu.sync_copy(pos1_hbm.at[pl.ds(off, SC_ROWS)], idx1_v)
            pltpu.sync_copy(pos2_hbm.at[pl.ds(off, SC_ROWS)], idx2_v)
            pltpu.sync_copy(xp_hbm.at[pl.ds(off, SC_ROWS)], rows_v)
            c1 = pltpu.async_copy(rows_v, xs_hbm.at[idx1_v], sem)
            c2 = pltpu.async_copy(rows_v, xs_hbm.at[idx2_v], sem)
            c1.wait()
            c2.wait()

    return pl.kernel(
        body,
        out_type=jax.ShapeDtypeStruct((n_rows, dp), xp.dtype),
        mesh=_sc_mesh(),
        scratch_types=[pltpu.VMEM((SC_ROWS,), jnp.int32), pltpu.VMEM((SC_ROWS,), jnp.int32),
                       pltpu.VMEM((SC_ROWS, dp), xp.dtype), pltpu.SemaphoreType.DMA],
        compiler_params=pltpu.CompilerParams(use_tc_tiling_on_sc=True),
        cost_estimate=pl.CostEstimate(flops=0, transcendentals=0, bytes_accessed=3 * n * dp * 4 + 2 * n * 4),
        name="sc_dispatch",
    )(pos1, pos2, xp)


def _sc_gather(pos1, pos2, ys):
    n = pos1.shape[0]
    dp = ys.shape[1]
    per_worker = n // SC_WORKERS
    steps = per_worker // SC_ROWS

    def body(pos1_hbm, pos2_hbm, ys_hbm, out_hbm, idx_v, rows_v, sem):
        base = _sc_worker_base(per_worker)

        @pl.loop(0, steps)
        def _(k):
            off = pl.multiple_of(base + k * SC_ROWS, SC_ROWS)
            for slot, pos_hbm in enumerate((pos1_hbm, pos2_hbm)):
                pltpu.sync_copy(pos_hbm.at[pl.ds(off, SC_ROWS)], idx_v)
                pltpu.async_copy(ys_hbm.at[idx_v], rows_v, sem).wait()
                pltpu.sync_copy(rows_v, out_hbm.at[slot, pl.ds(off, SC_ROWS)])

    return pl.kernel(
        body,
        out_type=jax.ShapeDtypeStruct((2, n, dp), ys.dtype),
        mesh=_sc_mesh(),
        scratch_types=[pltpu.VMEM((SC_ROWS,), jnp.int32), pltpu.VMEM((SC_ROWS, dp), ys.dtype),
                       pltpu.SemaphoreType.DMA],
        compiler_params=pltpu.CompilerParams(use_tc_tiling_on_sc=True),
        cost_estimate=pl.CostEstimate(flops=0, transcendentals=0, bytes_accessed=4 * n * dp * 4 + 2 * n * 4),
        name="sc_gather",
    )(pos1, pos2, ys)


def _expert_kernel(te_ref, first_ref, nsub_ref, slot_ref, next_ref, nused_ref, xs_ref, *refs, cast_weights):
    i = pl.program_id(0)
    bf16 = jnp.bfloat16
    sub = EXPERT_SUBTILE
    slot = slot_ref[i]

    if cast_weights:
        (wg_hbm, wu_hbm, wd_hbm, ys_ref, wgu_out, wd_out,
         wg_buf, wu_buf, wd_buf, wgu_bf_ref, wd_bf_ref, sem, out_sem) = refs
        wgu_ref, wd_ref = wgu_bf_ref, wd_bf_ref

        def weight_copies(expert, s):
            return (pltpu.make_async_copy(wg_hbm.at[expert], wg_buf.at[s], sem.at[s]),
                    pltpu.make_async_copy(wu_hbm.at[expert], wu_buf.at[s], sem.at[s]),
                    pltpu.make_async_copy(wd_hbm.at[expert], wd_buf.at[s], sem.at[s]))

        def emit_copies(expert):
            return (pltpu.make_async_copy(wgu_bf_ref, wgu_out.at[expert], out_sem),
                    pltpu.make_async_copy(wd_bf_ref, wd_out.at[expert], out_sem))
    else:
        wgu_hbm, wd_hbm, ys_ref, wgu_buf, wd_buf, sem = refs
        wgu_ref, wd_ref = wgu_buf.at[slot], wd_buf.at[slot]

        def weight_copies(expert, s):
            return (pltpu.make_async_copy(wgu_hbm.at[expert], wgu_buf.at[s], sem.at[s]),
                    pltpu.make_async_copy(wd_hbm.at[expert], wd_buf.at[s], sem.at[s]))

    @pl.when(first_ref[i] == 1)
    def _():
        @pl.when(i == 0)
        def _():
            for c in weight_copies(te_ref[i], slot):
                c.start()

        for c in weight_copies(te_ref[i], slot):
            c.wait()

        @pl.when(next_ref[i] >= 0)
        def _():
            for c in weight_copies(next_ref[i], 1 - slot):
                c.start()

        if cast_weights:
            @pl.when(i > 0)
            def _():
                for c in emit_copies(te_ref[i]):
                    c.wait()

            wgu_bf_ref[:, 0:D_EXPERT] = wg_buf[slot].astype(bf16)
            wgu_bf_ref[:, D_EXPERT:2 * D_EXPERT] = wu_buf[slot].astype(bf16)
            wd_bf_ref[...] = wd_buf[slot].astype(bf16)
            for c in emit_copies(te_ref[i]):
                c.start()

    def swiglu(starts):
        rows = [pl.ds(r0, sub) for r0 in starts]
        gate_up = []
        for rw in rows:
            lo, hi = _unpack_halves(xs_ref[rw, :])
            gate_up.append(_dot(lo.astype(bf16), wgu_ref[0:D_PACK, :])
                           + _dot(hi.astype(bf16), wgu_ref[D_PACK:D_MODEL, :]))
        for rw, gu in zip(rows, gate_up):
            g = gu[:, 0:D_EXPERT]
            h = g * jax.nn.sigmoid(g) * gu[:, D_EXPERT:2 * D_EXPERT]
            ys_ref[rw, :] = _pack_halves(_dot(h.astype(bf16), wd_ref[...]))

    n_sub = nsub_ref[i]

    def pair(k, c):
        r0 = pl.multiple_of(k * (2 * sub), 2 * sub)
        swiglu([r0, r0 + sub])
        return c

    lax.fori_loop(0, n_sub // 2, pair, 0)

    @pl.when(n_sub % 2 == 1)
    def _():
        swiglu([pl.multiple_of((n_sub - 1) * sub, sub)])

    def clear(k, c):
        ys_ref[pl.ds(pl.multiple_of(k * sub, sub), sub), :] = jnp.zeros((sub, D_PACK), ys_ref.dtype)
        return c

    lax.fori_loop(n_sub, jnp.where(i < nused_ref[0], ys_ref.shape[0] // sub, 0), clear, 0)

    if cast_weights:
        @pl.when(i == pl.num_programs(0) - 1)
        def _():
            for c in emit_copies(te_ref[i]):
                c.wait()


def _experts(tiles, xs, weights):
    n_rows, dp = xs.shape
    d = D_MODEL
    tm = EXPERT_TILE
    cast_weights = len(weights) == 3
    row_blk = lambda i, te, fi, ns, sl, nx, nu: (jnp.minimum(i, nu[0] - 1), 0)
    any_spec = pl.BlockSpec(memory_space=pl.ANY)
    ys_shape = jax.ShapeDtypeStruct((n_rows, dp), jnp.uint32)
    ys_spec = pl.BlockSpec((tm, dp), row_blk)
    wgu_bf = ((d, 2 * D_EXPERT), jnp.bfloat16)
    wd_bf = ((D_EXPERT, d), jnp.bfloat16)
    if cast_weights:
        out_shape = (ys_shape, jax.ShapeDtypeStruct((N_EXPERTS,) + wgu_bf[0], wgu_bf[1]),
                     jax.ShapeDtypeStruct((N_EXPERTS,) + wd_bf[0], wd_bf[1]))
        out_specs = (ys_spec, any_spec, any_spec)
        scratch = [pltpu.VMEM((2, d, D_EXPERT), jnp.float32), pltpu.VMEM((2, d, D_EXPERT), jnp.float32),
                   pltpu.VMEM((2, D_EXPERT, d), jnp.float32), pltpu.VMEM(*wgu_bf), pltpu.VMEM(*wd_bf),
                   pltpu.SemaphoreType.DMA((2,)), pltpu.SemaphoreType.DMA]
        weight_bytes = N_EXPERTS * 3 * d * D_EXPERT * (4 + 2)
    else:
        out_shape, out_specs = ys_shape, ys_spec
        scratch = [pltpu.VMEM((2,) + wgu_bf[0], wgu_bf[1]), pltpu.VMEM((2,) + wd_bf[0], wd_bf[1]),
                   pltpu.SemaphoreType.DMA((2,))]
        weight_bytes = N_EXPERTS * 3 * d * D_EXPERT * 2
    grid_spec = pltpu.PrefetchScalarGridSpec(
        num_scalar_prefetch=6,
        grid=(n_rows // tm,),
        in_specs=[pl.BlockSpec((tm, dp), row_blk)] + [any_spec] * len(weights),
        out_specs=out_specs,
        scratch_shapes=scratch,
    )
    return pl.pallas_call(
        functools.partial(_expert_kernel, cast_weights=cast_weights),
        grid_spec=grid_spec,
        out_shape=out_shape,
        compiler_params=pltpu.CompilerParams(dimension_semantics=("arbitrary",),
                                             vmem_limit_bytes=VMEM_LIMIT_BYTES),
        cost_estimate=pl.CostEstimate(flops=2 * n_rows * 3 * d * D_EXPERT, transcendentals=n_rows * D_EXPERT,
                                      bytes_accessed=2 * n_rows * dp * 4 + weight_bytes),
        name="experts",
    )(*tiles, xs, *weights)


def _combine_kernel(route_ref, g_ref, b_ref, xp_hbm, yg_hbm, out_ref, xp_buf, y_buf, sem, *, alpha, n_steps):
    i = pl.program_id(0)
    tt = out_ref.shape[0]
    ahead = COMBINE_SLOTS - 1

    def row_copies(step):
        start = step * tt
        if not isinstance(step, int):
            start = pl.multiple_of(start, tt)
        rows, s = pl.ds(start, tt), step % COMBINE_SLOTS
        return (pltpu.make_async_copy(xp_hbm.at[rows], xp_buf.at[s], sem.at[s]),
                pltpu.make_async_copy(yg_hbm.at[:, rows], y_buf.at[s], sem.at[s]))

    @pl.when(i == 0)
    def _():
        for step in range(min(ahead, n_steps)):
            for c in row_copies(step):
                c.start()

    @pl.when(i + ahead < n_steps)
    def _():
        for c in row_copies(i + ahead):
            c.start()

    for c in row_copies(i):
        c.wait()

    slot = i % COMBINE_SLOTS
    cols = jnp.concatenate([route_ref[...], jnp.zeros((LANES - SUBLANES, tt), jnp.float32)], axis=0).T
    w1 = cols[:, COL_W1:COL_W1 + 1]
    w2 = cols[:, COL_W2:COL_W2 + 1]
    lo1, hi1 = _unpack_halves(y_buf[slot, 0])
    lo2, hi2 = _unpack_halves(y_buf[slot, 1])
    lo, hi = _unpack_halves(xp_buf[slot])
    y = jnp.concatenate([alpha * lo + (w1 * lo1 + w2 * lo2), alpha * hi + (w1 * hi1 + w2 * hi2)], axis=1)
    out_ref[...] = _layer_norm(y, g_ref[...], b_ref[...])


def _combine(xp, route, yg, g, b, alpha):
    n, d = xp.shape[0], D_MODEL
    tt = TOKEN_TILE
    any_spec = pl.BlockSpec(memory_space=pl.ANY)
    return pl.pallas_call(
        functools.partial(_combine_kernel, alpha=alpha, n_steps=n // tt),
        grid=(n // tt,),
        in_specs=[pl.BlockSpec((SUBLANES, tt), lambda i: (0, i)),
                  pl.BlockSpec((1, d), lambda i: (0, 0)),
                  pl.BlockSpec((1, d), lambda i: (0, 0)),
                  any_spec, any_spec],
        out_specs=pl.BlockSpec((tt, d), lambda i: (i, 0)),
        out_shape=jax.ShapeDtypeStruct((n, d), jnp.float32),
        scratch_shapes=[pltpu.VMEM((COMBINE_SLOTS, tt, D_PACK), jnp.uint32),
                        pltpu.VMEM((COMBINE_SLOTS, 2, tt, D_PACK), jnp.uint32),
                        pltpu.SemaphoreType.DMA((COMBINE_SLOTS,))],
        compiler_params=pltpu.CompilerParams(dimension_semantics=("arbitrary",),
                                             vmem_limit_bytes=VMEM_LIMIT_BYTES),
        cost_estimate=pl.CostEstimate(flops=12 * n * d, transcendentals=n,
                                      bytes_accessed=n * (d * 4 + 3 * D_PACK * 4 + SUBLANES * 4)),
        name="combine",
    )(route, g, b, xp, yg)


def _row(a):
    return a.reshape(1, -1)


def _mixer_params(w_in, b_in, conv_w, conv_b, sg_ln_g, sg_ln_b, w_s, b_s, w_o, b_o, ln1_g, ln1_b,
                  w_rc, b_rc, w_rf, b_rf):
    bf16 = jnp.bfloat16
    ws_pairs = w_s.reshape(N_SG_HEADS // 2, 2 * CHUNK, CHUNK).astype(bf16)
    bs_full = jnp.repeat(b_s.T, SG_HEAD_DIM, axis=1)
    pad = LANES - N_EXPERTS - N_EXPERT_GROUPS
    w_r = jnp.pad(jnp.concatenate([w_rf, w_rc], axis=1), ((0, 0), (0, pad))).astype(bf16)
    b_r = jnp.pad(jnp.concatenate([b_rf, b_rc]), (0, pad)).reshape(1, LANES)
    return (w_in.astype(bf16), _row(b_in), conv_w, _row(conv_b), _row(sg_ln_g), _row(sg_ln_b),
            ws_pairs, bs_full, w_o.astype(bf16), _row(b_o), _row(ln1_g), _row(ln1_b), w_r, b_r)


def _encoder_layer(x, mixer_params, expert_weights, ln2_g, ln2_b, alpha):
    n = x.shape[0] * x.shape[1]
    xp, route, carry = _mixer(x, mixer_params, alpha)

    tm = EXPERT_TILE
    n_tiles = 2 * n // tm + N_EXPERTS
    cnt = carry[0, 0:N_EXPERTS].astype(jnp.int32)
    tiles_e = jnp.maximum((cnt + tm - 1) // tm, 1 if len(expert_weights) == 3 else 0)
    experts = jnp.arange(N_EXPERTS, dtype=jnp.int32)[:, None]
    up_to = experts.T <= experts
    tile_end = jnp.sum(jnp.where(up_to, tiles_e[None, :], 0), axis=1)
    tile_start = tile_end - tiles_e
    row_off = tile_start * tm

    def sorted_pos(e_row, r_row):
        e = route[e_row].astype(jnp.int32)
        return jnp.sum(jnp.where(e[None, :] == experts, row_off[:, None], 0), axis=0) + route[r_row].astype(jnp.int32)

    pos1 = sorted_pos(COL_E1, COL_R1)
    pos2 = sorted_pos(COL_E2, COL_R2)
    tile_ids = jnp.arange(n_tiles, dtype=jnp.int32)
    n_used = tile_end[N_EXPERTS - 1:N_EXPERTS]
    tile_expert = jnp.sum((tile_end[None, :] <= tile_ids[:, None]).astype(jnp.int32), axis=1)
    last_used = jnp.sum((tile_end <= n_used[0] - 1).astype(jnp.int32))
    tile_expert = jnp.minimum(tile_expert, last_used)
    tile_first = (jnp.any(tile_ids[:, None] == tile_start[None, :], axis=1) & (tile_ids < n_used[0])) | (tile_ids == 0)
    tile_first = tile_first.astype(jnp.int32)
    rows_left = jnp.sum(jnp.where(tile_expert[:, None] == experts.T, (row_off + cnt)[None, :], 0), axis=1) - tile_ids * tm
    tile_nsub = jnp.where(tile_ids < n_used[0], (jnp.clip(rows_left, 0, tm) + EXPERT_SUBTILE - 1) // EXPERT_SUBTILE, 0)

    used = tiles_e > 0
    ordinal = jnp.sum((up_to & used[None, :]).astype(jnp.int32), axis=1) - 1
    e_ids = experts[:, 0]
    later_used = used[None, :] & (e_ids[None, :] > e_ids[:, None])
    next_used = jnp.min(jnp.where(later_used, e_ids[None, :], N_EXPERTS), axis=1)
    next_used = jnp.where(next_used < N_EXPERTS, next_used, -1)
    of_tile = tile_expert[:, None] == experts.T
    tile_slot = jnp.sum(jnp.where(of_tile, ordinal[None, :], 0), axis=1) % 2
    tile_next = jnp.sum(jnp.where(of_tile, next_used[None, :], 0), axis=1)

    tiles = (tile_expert, tile_first, tile_nsub.astype(jnp.int32), tile_slot.astype(jnp.int32),
             tile_next.astype(jnp.int32), n_used)
    xs = _sc_dispatch(pos1, pos2, xp, n_tiles * tm)
    if len(expert_weights) == 3:
        ys, *expert_weights = _experts(tiles, xs, expert_weights)
    else:
        ys = _experts(tiles, xs, expert_weights)
    yg = _sc_gather(pos1, pos2, ys)
    return _combine(xp, route, yg, _row(ln2_g), _row(ln2_b), alpha).reshape(x.shape), tuple(expert_weights)


def kernel(x_prompt, x_sample, w_in, b_in, conv_w, conv_b, sg_ln_g, sg_ln_b, w_s, b_s, w_o, b_o, ln1_g, ln1_b, w_rc, b_rc, w_rf, b_rf, w_gate, w_up, w_down, ln2_g, ln2_b):
    depth = w_in.shape[0]
    alpha = (2.0 * depth) ** 0.25
    xs = (x_prompt, x_sample)
    for l in range(depth):
        mixer_params = _mixer_params(w_in[l], b_in[l], conv_w[l], conv_b[l], sg_ln_g[l], sg_ln_b[l], w_s[l], b_s[l],
                                     w_o[l], b_o[l], ln1_g[l], ln1_b[l], w_rc[l], b_rc[l], w_rf[l], b_rf[l])
        expert_weights = (w_gate[l], w_up[l], w_down[l])
        outs = [None] * len(xs)
        for k in sorted(range(len(xs)), key=lambda k: -xs[k].shape[0] * xs[k].shape[1]):
            outs[k], expert_weights = _encoder_layer(xs[k], mixer_params, expert_weights, ln2_g[l], ln2_b[l], alpha)
        xs = tuple(outs)
    return xs
```

```python
import functools

import jax
import jax.numpy as jnp
from jax import lax
from jax.experimental import pallas as pl
from jax.experimental.pallas import tpu as pltpu
from jax.experimental.pallas import tpu_sc as plsc

D_MODEL = 1024
D_CONV = 512
D_SG = 512
N_SG_HEADS = 8
SG_HEAD_DIM = D_SG // N_SG_HEADS
CHUNK = 128
N_EXPERT_GROUPS = 4
EXPERTS_PER_GROUP = 8
N_EXPERTS = N_EXPERT_GROUPS * EXPERTS_PER_GROUP
D_EXPERT = 512
LN_EPS = 1e-5
D_PACK = D_MODEL // 2

LANES = 128
SUBLANES = 8
HALO = 16
SEQ_TILE = 512
EXPERT_TILE = 1024
EXPERT_SUBTILE = 256
TOKEN_TILE = 1024
COMBINE_SLOTS = 3
EXPERT_ROW_SLOTS = 3
SC_CORES = 2
SC_SUBCORES = 16
SC_WORKERS = SC_CORES * SC_SUBCORES
SC_ROWS = 64
VMEM_LIMIT_BYTES = 56 * 1024 * 1024

COL_E1, COL_E2, COL_W1, COL_W2, COL_R1, COL_R2 = 0, 1, 2, 3, 4, 5
COARSE_OFF = N_EXPERTS


def _dot(a, b):
    return jnp.dot(a, b, preferred_element_type=jnp.float32)


def _gelu_tanh(x):
    return 0.5 * x * (1.0 + jnp.tanh(0.7978845608028654 * (x + 0.044715 * (x * x * x))))


def _layer_norm(x, g, b):
    mu = jnp.mean(x, axis=-1, keepdims=True)
    xc = x - mu
    var = jnp.mean(xc * xc, axis=-1, keepdims=True)
    return xc * lax.rsqrt(var + LN_EPS) * g + b


def _pack_halves(x):
    u32 = jnp.uint32
    lo = lax.bitcast_convert_type(x[:, 0:D_PACK].astype(jnp.bfloat16).astype(jnp.float32), u32)
    hi = lax.bitcast_convert_type(x[:, D_PACK:D_MODEL].astype(jnp.bfloat16).astype(jnp.float32), u32)
    return (hi & u32(0xFFFF0000)) | (lo >> u32(16))


def _unpack_halves(w):
    u32 = jnp.uint32
    lo = lax.bitcast_convert_type(w << u32(16), jnp.float32)
    hi = lax.bitcast_convert_type(w & u32(0xFFFF0000), jnp.float32)
    return lo, hi


def _mixer_kernel(x_ref, x_prev_ref, x_next_ref, w_in_ref, b_in_ref, conv_w_ref, conv_b_ref,
                  sg_g_ref, sg_b_ref, ws_ref, bs_ref, w_o_ref, b_o_ref, ln1_g_ref, ln1_b_ref,
                  w_r_ref, b_r_ref,
                  xp_ref, route_ref, counts_ref,
                  xb_ref, ymix_ref, tri_ref, resid_ref, carry_ref, *, ts, nt, n_tiles, alpha):
    t = pl.program_id(0)
    j = jnp.minimum(t, n_tiles - 1) % nt
    bf16 = jnp.bfloat16

    @pl.when(t == 0)
    def _():
        carry_ref[...] = jnp.zeros_like(carry_ref)
        resid_ref[...] = jnp.zeros_like(resid_ref)
        ri = lax.broadcasted_iota(jnp.int32, (ts, ts), 0)
        ci = lax.broadcasted_iota(jnp.int32, (ts, ts), 1)
        tri_ref[...] = jnp.where(ri > ci, 1.0, 0.0).astype(bf16)

    lane = lax.broadcasted_iota(jnp.int32, (ts, LANES), 1)
    lane_f = lane.astype(jnp.float32)
    neg = jnp.float32(-jnp.inf)
    big = jnp.float32(1e9)
    is_c = (lane >= COARSE_OFF) & (lane < COARSE_OFF + N_EXPERT_GROUPS)

    def proj(lhs, lo, hi):
        return _dot(lhs, w_in_ref[:, lo:hi]) + b_in_ref[:, lo:hi]

    def back_norm():
        x1 = _layer_norm(resid_ref[...], ln1_g_ref[...], ln1_b_ref[...])
        xp_ref[...] = _pack_halves(x1)
        return x1.astype(bf16)

    def back_route(x1b):
        logits = _dot(x1b, w_r_ref[...]) + b_r_ref[...]
        lc = jnp.where(is_c, logits, neg)
        mx = jnp.max(lc, axis=1, keepdims=True)
        grp = jnp.min(jnp.where(lc == mx, lane_f - COARSE_OFF, big), axis=1, keepdims=True)
        p_grp = 1.0 / jnp.sum(jnp.where(is_c, jnp.exp(logits - mx), 0.0), axis=1, keepdims=True)
        grp_lo = grp * EXPERTS_PER_GROUP
        in_grp = (lane_f >= grp_lo) & (lane_f < grp_lo + EXPERTS_PER_GROUP)
        lf = jnp.where(in_grp, logits, neg)
        v1 = jnp.max(lf, axis=1, keepdims=True)
        e1 = jnp.min(jnp.where(lf == v1, lane_f, big), axis=1, keepdims=True)
        lf2 = jnp.where(lane_f == e1, neg, lf)
        v2 = jnp.max(lf2, axis=1, keepdims=True)
        e2 = jnp.min(jnp.where(lf2 == v2, lane_f, big), axis=1, keepdims=True)
        a = jnp.exp(v2 - v1)
        return e1, e2, p_grp / (1.0 + a), p_grp * a / (1.0 + a)

    def back_rank(e1, e2, w1, w2):
        hit1 = lane_f == e1
        hit2 = lane_f == e2
        onehot = jnp.where(hit1 | hit2, 1.0, 0.0)
        carry = carry_ref[...]
        before = _dot(tri_ref[...], onehot.astype(bf16)) + carry
        r1 = jnp.sum(jnp.where(hit1, before, 0.0), axis=1, keepdims=True)
        r2 = jnp.sum(jnp.where(hit2, before, 0.0), axis=1, keepdims=True)
        carry = jnp.where(t > 0, carry + jnp.sum(onehot, axis=0, keepdims=True), carry)
        carry_ref[...] = carry
        counts_ref[...] = jnp.broadcast_to(carry, counts_ref.shape)
        slab = jnp.where(lane == COL_E1, e1, 0.0)
        slab = jnp.where(lane == COL_E2, e2, slab)
        slab = jnp.where(lane == COL_W1, w1, slab)
        slab = jnp.where(lane == COL_W2, w2, slab)
        slab = jnp.where(lane == COL_R1, r1, slab)
        slab = jnp.where(lane == COL_R2, r2, slab)
        route_ref[...] = slab.T[0:SUBLANES, :]

    def front_load():
        xb_ref[0:HALO, :] = x_prev_ref[...].astype(bf16)
        xb_ref[HALO:HALO + ts, :] = x_ref[...].astype(bf16)
        xb_ref[HALO + ts:HALO + ts + HALO, :] = x_next_ref[...].astype(bf16)

    def front_conv():
        xe = xb_ref[...]
        g_e = proj(xe, 0, D_CONV) * proj(xe, 2 * D_CONV, 3 * D_CONV)
        row_e = lax.broadcasted_iota(jnp.int32, (ts + 2 * HALO, 1), 0)
        has_prev = jnp.where(j > 0, 1.0, 0.0)
        has_next = jnp.where(j < nt - 1, 1.0, 0.0)
        g_e = g_e * jnp.where(row_e < HALO, has_prev, jnp.where(row_e >= HALO + ts, has_next, 1.0))
        conv = (g_e[HALO - 1:HALO - 1 + ts, :] * conv_w_ref[0:1, :] + g_e[HALO:HALO + ts, :] * conv_w_ref[1:2, :]
                + g_e[HALO + 1:HALO + 1 + ts, :] * conv_w_ref[2:3, :] + conv_b_ref[...])
        y_a = proj(xb_ref[HALO:HALO + ts, :], D_CONV, 2 * D_CONV) * conv
        ymix_ref[:, 0:D_CONV] = y_a.astype(bf16)

    def front_gate_proj():
        xm = xb_ref[HALO:HALO + ts, :]
        return proj(xm, 3 * D_CONV, 3 * D_CONV + D_SG), proj(xm, 3 * D_CONV + D_SG, 3 * D_CONV + 2 * D_SG)

    def front_gate(u_pre, v_pre):
        u = _gelu_tanh(u_pre)
        v = _gelu_tanh(v_pre)
        v_ln = _layer_norm(v, sg_g_ref[...], sg_b_ref[...]).astype(bf16)
        first_head = lax.broadcasted_iota(jnp.int32, (CHUNK, LANES), 1) < SG_HEAD_DIM
        for q0 in range(0, ts, 2 * CHUNK):
            q1 = q0 + CHUNK
            for hp in range(N_SG_HEADS // 2):
                c0 = hp * LANES
                rhs = jnp.concatenate([v_ln[q0:q0 + CHUNK, c0:c0 + LANES], v_ln[q1:q1 + CHUNK, c0:c0 + LANES]], axis=1)
                res = _dot(ws_ref[hp], rhs)
                bias = bs_ref[:, c0:c0 + LANES]
                m0 = jnp.where(first_head, res[0:CHUNK, 0:LANES], res[CHUNK:2 * CHUNK, 0:LANES]) + bias
                m1 = jnp.where(first_head, res[0:CHUNK, LANES:2 * LANES], res[CHUNK:2 * CHUNK, LANES:2 * LANES]) + bias
                ymix_ref[q0:q0 + CHUNK, D_CONV + c0:D_CONV + c0 + LANES] = (u[q0:q0 + CHUNK, c0:c0 + LANES] * m0).astype(bf16)
                ymix_ref[q1:q1 + CHUNK, D_CONV + c0:D_CONV + c0 + LANES] = (u[q1:q1 + CHUNK, c0:c0 + LANES] * m1).astype(bf16)

    def front_out_a():
        resid_ref[...] = alpha * x_ref[...] + b_o_ref[...] + _dot(ymix_ref[:, 0:D_CONV], w_o_ref[0:D_CONV, :])

    def front_out_b():
        resid_ref[...] += _dot(ymix_ref[:, D_CONV:D_MODEL], w_o_ref[D_CONV:D_MODEL, :])

    front_load()
    gate_pre = front_gate_proj()
    x1b = back_norm()
    routing = back_route(x1b)
    front_conv()
    back_rank(*routing)
    front_out_a()
    front_gate(*gate_pre)
    front_out_b()


def _mixer(x, p, alpha):
    nb, s, d = x.shape
    ts = SEQ_TILE
    nt = s // ts
    hb = ts // HALO
    n_tiles = nb * nt
    n_total = n_tiles * ts

    front = lambda t: jnp.minimum(t, n_tiles - 1)
    back = lambda t: jnp.maximum(t - 1, 0)
    cur = lambda t: (front(t) // nt, front(t) % nt, 0)
    prv = lambda t: (front(t) // nt, jnp.maximum((front(t) % nt) * hb - 1, 0), 0)
    nxt = lambda t: (front(t) // nt, jnp.minimum((front(t) % nt + 1) * hb, s // HALO - 1), 0)
    full = lambda a: pl.BlockSpec(a.shape, lambda t: (0,) * a.ndim)
    in_specs = [pl.BlockSpec((None, ts, d), cur), pl.BlockSpec((None, HALO, d), prv),
                pl.BlockSpec((None, HALO, d), nxt)] + [full(a) for a in p]
    out_shape = (jax.ShapeDtypeStruct((n_total, D_PACK), jnp.uint32),
                 jax.ShapeDtypeStruct((SUBLANES, n_total), jnp.float32),
                 jax.ShapeDtypeStruct((SUBLANES, LANES), jnp.float32))
    row_blk = lambda t: (back(t), 0)
    out_specs = (pl.BlockSpec((ts, D_PACK), row_blk),
                 pl.BlockSpec((SUBLANES, ts), lambda t: (0, back(t))),
                 pl.BlockSpec((SUBLANES, LANES), lambda t: (0, 0)))
    return pl.pallas_call(
        functools.partial(_mixer_kernel, ts=ts, nt=nt, n_tiles=n_tiles, alpha=alpha),
        grid=(n_tiles + 1,),
        in_specs=in_specs,
        out_specs=out_specs,
        out_shape=out_shape,
        scratch_shapes=[pltpu.VMEM((ts + 2 * HALO, d), jnp.bfloat16),
                        pltpu.VMEM((ts, d), jnp.bfloat16),
                        pltpu.VMEM((ts, ts), jnp.bfloat16),
                        pltpu.VMEM((ts, d), jnp.float32),
                        pltpu.VMEM((1, LANES), jnp.float32)],
        compiler_params=pltpu.CompilerParams(dimension_semantics=("arbitrary",),
                                             vmem_limit_bytes=VMEM_LIMIT_BYTES),
        cost_estimate=pl.CostEstimate(
            flops=2 * n_total * (d * (3 * D_CONV + 2 * D_SG) + CHUNK * D_SG + d * d + d * LANES + ts * LANES),
            transcendentals=n_total * (2 * D_SG + 2 * LANES),
            bytes_accessed=n_total * (d * 4 + D_PACK * 4 + SUBLANES * 4)),
        name="mixer",
    )(x, x, x, *p)


def _sc_mesh():
    return plsc.VectorSubcoreMesh(core_axis_name="c", subcore_axis_name="s",
                                  num_cores=SC_CORES, num_subcores=SC_SUBCORES)


def _sc_worker_base(per_worker):
    return (lax.axis_index("s") * SC_CORES + lax.axis_index("c")) * per_worker


def _sc_dispatch(pos1, pos2, xp, n_rows):
    n, dp = xp.shape
    per_worker = n // SC_WORKERS
    steps = per_worker // SC_ROWS

    def body(pos1_hbm, pos2_hbm, xp_hbm, xs_hbm, idx1_v, idx2_v, rows_v, sem):
        base = _sc_worker_base(per_worker)

        @pl.loop(0, steps)
        def _(k):
            off = pl.multiple_of(base + k * SC_ROWS, SC_ROWS)
            pltpu.sync_copy(pos1_hbm.at[pl.ds(off, SC_ROWS)], idx1_v)
            pltpu.sync_copy(pos2_hbm.at[pl.ds(off, SC_ROWS)], idx2_v)
            pltpu.sync_copy(xp_hbm.at[pl.ds(off, SC_ROWS)], rows_v)
            c1 = pltpu.async_copy(rows_v, xs_hbm.at[idx1_v], sem)
            c2 = pltpu.async_copy(rows_v, xs_hbm.at[idx2_v], sem)
            c1.wait()
            c2.wait()

    return pl.kernel(
        body,
        out_type=jax.ShapeDtypeStruct((n_rows, dp), xp.dtype),
        mesh=_sc_mesh(),
        scratch_types=[pltpu.VMEM((SC_ROWS,), jnp.int32), pltpu.VMEM((SC_ROWS,), jnp.int32),
                       pltpu.VMEM((SC_ROWS, dp), xp.dtype), pltpu.SemaphoreType.DMA],
        compiler_params=pltpu.CompilerParams(use_tc_tiling_on_sc=True),
        cost_estimate=pl.CostEstimate(flops=0, transcendentals=0, bytes_accessed=3 * n * dp * 4 + 2 * n * 4),
        name="sc_dispatch",
    )(pos1, pos2, xp)


def _sc_gather(pos1, pos2, ys):
    n = pos1.shape[0]
    dp = ys.shape[1]
    per_worker = n // SC_WORKERS
    steps = per_worker // SC_ROWS

    def body(pos1_hbm, pos2_hbm, ys_hbm, out_hbm, idx_v, rows_v, sem):
        base = _sc_worker_base(per_worker)

        @pl.loop(0, steps)
        def _(k):
            off = pl.multiple_of(base + k * SC_ROWS, SC_ROWS)
            for slot, pos_hbm in enumerate((pos1_hbm, pos2_hbm)):
                pltpu.sync_copy(pos_hbm.at[pl.ds(off, SC_ROWS)], idx_v)
                pltpu.async_copy(ys_hbm.at[idx_v], rows_v, sem).wait()
                pltpu.sync_copy(rows_v, out_hbm.at[slot, pl.ds(off, SC_ROWS)])

    return pl.kernel(
        body,
        out_type=jax.ShapeDtypeStruct((2, n, dp), ys.dtype),
        mesh=_sc_mesh(),
        scratch_types=[pltpu.VMEM((SC_ROWS,), jnp.int32), pltpu.VMEM((SC_ROWS, dp), ys.dtype),
                       pltpu.SemaphoreType.DMA],
        compiler_params=pltpu.CompilerParams(use_tc_tiling_on_sc=True),
        cost_estimate=pl.CostEstimate(flops=0, transcendentals=0, bytes_accessed=4 * n * dp * 4 + 2 * n * 4),
        name="sc_gather",
    )(pos1, pos2, ys)


def _expert_kernel(te_ref, first_ref, nsub_ref, slot_ref, next_ref, nused_ref, xs_hbm, *refs, cast_weights):
    i = pl.program_id(0)
    bf16 = jnp.bfloat16
    sub = EXPERT_SUBTILE
    slot = slot_ref[i]

    if cast_weights:
        (wg_hbm, wu_hbm, wd_hbm, ys_ref, wgu_out, wd_out,
         wg_buf, wu_buf, wd_buf, wgu_bf_ref, wd_bf_ref, sem, out_sem, xs_buf, xs_sem) = refs
        wgu_ref, wd_ref = wgu_bf_ref, wd_bf_ref

        def weight_copies(expert, s):
            return (pltpu.make_async_copy(wg_hbm.at[expert], wg_buf.at[s], sem.at[s]),
                    pltpu.make_async_copy(wu_hbm.at[expert], wu_buf.at[s], sem.at[s]),
                    pltpu.make_async_copy(wd_hbm.at[expert], wd_buf.at[s], sem.at[s]))

        def emit_copies(expert):
            return (pltpu.make_async_copy(wgu_bf_ref, wgu_out.at[expert], out_sem),
                    pltpu.make_async_copy(wd_bf_ref, wd_out.at[expert], out_sem))
    else:
        wgu_hbm, wd_hbm, ys_ref, wgu_buf, wd_buf, sem, xs_buf, xs_sem = refs
        wgu_ref, wd_ref = wgu_buf.at[slot], wd_buf.at[slot]

        def weight_copies(expert, s):
            return (pltpu.make_async_copy(wgu_hbm.at[expert], wgu_buf.at[s], sem.at[s]),
                    pltpu.make_async_copy(wd_hbm.at[expert], wd_buf.at[s], sem.at[s]))

    tm = ys_ref.shape[0]
    ahead = EXPERT_ROW_SLOTS - 1
    n_used = nused_ref[0]

    def row_copy(step):
        start = step * tm
        if not isinstance(step, int):
            start = pl.multiple_of(start, tm)
        s = step % EXPERT_ROW_SLOTS
        return pltpu.make_async_copy(xs_hbm.at[pl.ds(start, tm)], xs_buf.at[s], xs_sem.at[s])

    @pl.when(i == 0)
    def _():
        for step in range(ahead):
            @pl.when(step < n_used)
            def _():
                row_copy(step).start()

    @pl.when(i + ahead < n_used)
    def _():
        row_copy(i + ahead).start()

    @pl.when(i < n_used)
    def _():
        row_copy(i).wait()

    xs_ref = xs_buf.at[i % EXPERT_ROW_SLOTS]

    @pl.when(first_ref[i] == 1)
    def _():
        @pl.when(i == 0)
        def _():
            for c in weight_copies(te_ref[i], slot):
                c.start()

        for c in weight_copies(te_ref[i], slot):
            c.wait()

        @pl.when(next_ref[i] >= 0)
        def _():
            for c in weight_copies(next_ref[i], 1 - slot):
                c.start()

        if cast_weights:
            @pl.when(i > 0)
            def _():
                for c in emit_copies(te_ref[i]):
                    c.wait()

            wgu_bf_ref[:, 0:D_EXPERT] = wg_buf[slot].astype(bf16)
            wgu_bf_ref[:, D_EXPERT:2 * D_EXPERT] = wu_buf[slot].astype(bf16)
            wd_bf_ref[...] = wd_buf[slot].astype(bf16)
            for c in emit_copies(te_ref[i]):
                c.start()

    def swiglu(starts):
        rows = [pl.ds(r0, sub) for r0 in starts]
        gate_up = []
        for rw in rows:
            lo, hi = _unpack_halves(xs_ref[rw, :])
            gate_up.append(_dot(lo.astype(bf16), wgu_ref[0:D_PACK, :])
                           + _dot(hi.astype(bf16), wgu_ref[D_PACK:D_MODEL, :]))
        for rw, gu in zip(rows, gate_up):
            g = gu[:, 0:D_EXPERT]
            h = g * jax.nn.sigmoid(g) * gu[:, D_EXPERT:2 * D_EXPERT]
            ys_ref[rw, :] = _pack_halves(_dot(h.astype(bf16), wd_ref[...]))

    n_sub = nsub_ref[i]

    def pair(k, c):
        r0 = pl.multiple_of(k * (2 * sub), 2 * sub)
        swiglu([r0, r0 + sub])
        return c

    lax.fori_loop(0, n_sub // 2, pair, 0)

    @pl.when(n_sub % 2 == 1)
    def _():
        swiglu([pl.multiple_of((n_sub - 1) * sub, sub)])

    def clear(k, c):
        ys_ref[pl.ds(pl.multiple_of(k * sub, sub), sub), :] = jnp.zeros((sub, D_PACK), ys_ref.dtype)
        return c

    lax.fori_loop(n_sub, jnp.where(i < nused_ref[0], ys_ref.shape[0] // sub, 0), clear, 0)

    if cast_weights:
        @pl.when(i == pl.num_programs(0) - 1)
        def _():
            for c in emit_copies(te_ref[i]):
                c.wait()


def _experts(tiles, xs, weights):
    n_rows, dp = xs.shape
    d = D_MODEL
    tm = EXPERT_TILE
    cast_weights = len(weights) == 3
    row_blk = lambda i, te, fi, ns, sl, nx, nu: (jnp.minimum(i, nu[0] - 1), 0)
    any_spec = pl.BlockSpec(memory_space=pl.ANY)
    ys_shape = jax.ShapeDtypeStruct((n_rows, dp), jnp.uint32)
    ys_spec = pl.BlockSpec((tm, dp), row_blk)
    wgu_bf = ((d, 2 * D_EXPERT), jnp.bfloat16)
    wd_bf = ((D_EXPERT, d), jnp.bfloat16)
    if cast_weights:
        out_shape = (ys_shape, jax.ShapeDtypeStruct((N_EXPERTS,) + wgu_bf[0], wgu_bf[1]),
                     jax.ShapeDtypeStruct((N_EXPERTS,) + wd_bf[0], wd_bf[1]))
        out_specs = (ys_spec, any_spec, any_spec)
        scratch = [pltpu.VMEM((2, d, D_EXPERT), jnp.float32), pltpu.VMEM((2, d, D_EXPERT), jnp.float32),
                   pltpu.VMEM((2, D_EXPERT, d), jnp.float32), pltpu.VMEM(*wgu_bf), pltpu.VMEM(*wd_bf),
                   pltpu.SemaphoreType.DMA((2,)), pltpu.SemaphoreType.DMA]
        weight_bytes = N_EXPERTS * 3 * d * D_EXPERT * (4 + 2)
    else:
        out_shape, out_specs = ys_shape, ys_spec
        scratch = [pltpu.VMEM((2,) + wgu_bf[0], wgu_bf[1]), pltpu.VMEM((2,) + wd_bf[0], wd_bf[1]),
                   pltpu.SemaphoreType.DMA((2,))]
        weight_bytes = N_EXPERTS * 3 * d * D_EXPERT * 2
    scratch += [pltpu.VMEM((EXPERT_ROW_SLOTS, tm, dp), jnp.uint32), pltpu.SemaphoreType.DMA((EXPERT_ROW_SLOTS,))]
    grid_spec = pltpu.PrefetchScalarGridSpec(
        num_scalar_prefetch=6,
        grid=(n_rows // tm,),
        in_specs=[any_spec] * (1 + len(weights)),
        out_specs=out_specs,
        scratch_shapes=scratch,
    )
    return pl.pallas_call(
        functools.partial(_expert_kernel, cast_weights=cast_weights),
        grid_spec=grid_spec,
        out_shape=out_shape,
        compiler_params=pltpu.CompilerParams(dimension_semantics=("arbitrary",),
                                             vmem_limit_bytes=VMEM_LIMIT_BYTES),
        cost_estimate=pl.CostEstimate(flops=2 * n_rows * 3 * d * D_EXPERT, transcendentals=n_rows * D_EXPERT,
                                      bytes_accessed=2 * n_rows * dp * 4 + weight_bytes),
        name="experts",
    )(*tiles, xs, *weights)


def _combine_kernel(route_ref, g_ref, b_ref, xp_hbm, yg_hbm, out_ref, xp_buf, y_buf, sem, *, alpha, n_steps):
    i = pl.program_id(0)
    tt = out_ref.shape[0]
    ahead = COMBINE_SLOTS - 1

    def row_copies(step):
        start = step * tt
        if not isinstance(step, int):
            start = pl.multiple_of(start, tt)
        rows, s = pl.ds(start, tt), step % COMBINE_SLOTS
        return (pltpu.make_async_copy(xp_hbm.at[rows], xp_buf.at[s], sem.at[s]),
                pltpu.make_async_copy(yg_hbm.at[:, rows], y_buf.at[s], sem.at[s]))

    @pl.when(i == 0)
    def _():
        for step in range(min(ahead, n_steps)):
            for c in row_copies(step):
                c.start()

    @pl.when(i + ahead < n_steps)
    def _():
        for c in row_copies(i + ahead):
            c.start()

    for c in row_copies(i):
        c.wait()

    slot = i % COMBINE_SLOTS
    cols = jnp.concatenate([route_ref[...], jnp.zeros((LANES - SUBLANES, tt), jnp.float32)], axis=0).T
    w1 = cols[:, COL_W1:COL_W1 + 1]
    w2 = cols[:, COL_W2:COL_W2 + 1]
    lo1, hi1 = _unpack_halves(y_buf[slot, 0])
    lo2, hi2 = _unpack_halves(y_buf[slot, 1])
    lo, hi = _unpack_halves(xp_buf[slot])
    y = jnp.concatenate([alpha * lo + (w1 * lo1 + w2 * lo2), alpha * hi + (w1 * hi1 + w2 * hi2)], axis=1)
    out_ref[...] = _layer_norm(y, g_ref[...], b_ref[...])


def _combine(xp, route, yg, g, b, alpha):
    n, d = xp.shape[0], D_MODEL
    tt = TOKEN_TILE
    any_spec = pl.BlockSpec(memory_space=pl.ANY)
    return pl.pallas_call(
        functools.partial(_combine_kernel, alpha=alpha, n_steps=n // tt),
        grid=(n // tt,),
        in_specs=[pl.BlockSpec((SUBLANES, tt), lambda i: (0, i)),
                  pl.BlockSpec((1, d), lambda i: (0, 0)),
                  pl.BlockSpec((1, d), lambda i: (0, 0)),
                  any_spec, any_spec],
        out_specs=pl.BlockSpec((tt, d), lambda i: (i, 0)),
        out_shape=jax.ShapeDtypeStruct((n, d), jnp.float32),
        scratch_shapes=[pltpu.VMEM((COMBINE_SLOTS, tt, D_PACK), jnp.uint32),
                        pltpu.VMEM((COMBINE_SLOTS, 2, tt, D_PACK), jnp.uint32),
                        pltpu.SemaphoreType.DMA((COMBINE_SLOTS,))],
        compiler_params=pltpu.CompilerParams(dimension_semantics=("arbitrary",),
                                             vmem_limit_bytes=VMEM_LIMIT_BYTES),
        cost_estimate=pl.CostEstimate(flops=12 * n * d, transcendentals=n,
                                      bytes_accessed=n * (d * 4 + 3 * D_PACK * 4 + SUBLANES * 4)),
        name="combine",
    )(route, g, b, xp, yg)


def _row(a):
    return a.reshape(1, -1)


def _mixer_params(w_in, b_in, conv_w, conv_b, sg_ln_g, sg_ln_b, w_s, b_s, w_o, b_o, ln1_g, ln1_b,
                  w_rc, b_rc, w_rf, b_rf):
    bf16 = jnp.bfloat16
    ws_pairs = w_s.reshape(N_SG_HEADS // 2, 2 * CHUNK, CHUNK).astype(bf16)
    bs_full = jnp.repeat(b_s.T, SG_HEAD_DIM, axis=1)
    pad = LANES - N_EXPERTS - N_EXPERT_GROUPS
    w_r = jnp.pad(jnp.concatenate([w_rf, w_rc], axis=1), ((0, 0), (0, pad))).astype(bf16)
    b_r = jnp.pad(jnp.concatenate([b_rf, b_rc]), (0, pad)).reshape(1, LANES)
    return (w_in.astype(bf16), _row(b_in), conv_w, _row(conv_b), _row(sg_ln_g), _row(sg_ln_b),
            ws_pairs, bs_full, w_o.astype(bf16), _row(b_o), _row(ln1_g), _row(ln1_b), w_r, b_r)


def _encoder_layer(x, mixer_params, expert_weights, ln2_g, ln2_b, alpha):
    n = x.shape[0] * x.shape[1]
    xp, route, carry = _mixer(x, mixer_params, alpha)

    tm = EXPERT_TILE
    n_tiles = 2 * n // tm + N_EXPERTS
    cnt = carry[0, 0:N_EXPERTS].astype(jnp.int32)
    tiles_e = jnp.maximum((cnt + tm - 1) // tm, 1 if len(expert_weights) == 3 else 0)
    experts = jnp.arange(N_EXPERTS, dtype=jnp.int32)[:, None]
    up_to = experts.T <= experts
    tile_end = jnp.sum(jnp.where(up_to, tiles_e[None, :], 0), axis=1)
    tile_start = tile_end - tiles_e
    row_off = tile_start * tm

    def sorted_pos(e_row, r_row):
        e = route[e_row].astype(jnp.int32)
        return jnp.sum(jnp.where(e[None, :] == experts, row_off[:, None], 0), axis=0) + route[r_row].astype(jnp.int32)

    pos1 = sorted_pos(COL_E1, COL_R1)
    pos2 = sorted_pos(COL_E2, COL_R2)
    tile_ids = jnp.arange(n_tiles, dtype=jnp.int32)
    n_used = tile_end[N_EXPERTS - 1:N_EXPERTS]
    tile_expert = jnp.sum((tile_end[None, :] <= tile_ids[:, None]).astype(jnp.int32), axis=1)
    last_used = jnp.sum((tile_end <= n_used[0] - 1).astype(jnp.int32))
    tile_expert = jnp.minimum(tile_expert, last_used)
    tile_first = (jnp.any(tile_ids[:, None] == tile_start[None, :], axis=1) & (tile_ids < n_used[0])) | (tile_ids == 0)
    tile_first = tile_first.astype(jnp.int32)
    rows_left = jnp.sum(jnp.where(tile_expert[:, None] == experts.T, (row_off + cnt)[None, :], 0), axis=1) - tile_ids * tm
    tile_nsub = jnp.where(tile_ids < n_used[0], (jnp.clip(rows_left, 0, tm) + EXPERT_SUBTILE - 1) // EXPERT_SUBTILE, 0)

    used = tiles_e > 0
    ordinal = jnp.sum((up_to & used[None, :]).astype(jnp.int32), axis=1) - 1
    e_ids = experts[:, 0]
    later_used = used[None, :] & (e_ids[None, :] > e_ids[:, None])
    next_used = jnp.min(jnp.where(later_used, e_ids[None, :], N_EXPERTS), axis=1)
    next_used = jnp.where(next_used < N_EXPERTS, next_used, -1)
    of_tile = tile_expert[:, None] == experts.T
    tile_slot = jnp.sum(jnp.where(of_tile, ordinal[None, :], 0), axis=1) % 2
    tile_next = jnp.sum(jnp.where(of_tile, next_used[None, :], 0), axis=1)

    tiles = (tile_expert, tile_first, tile_nsub.astype(jnp.int32), tile_slot.astype(jnp.int32),
             tile_next.astype(jnp.int32), n_used)
    xs = _sc_dispatch(pos1, pos2, xp, n_tiles * tm)
    if len(expert_weights) == 3:
        ys, *expert_weights = _experts(tiles, xs, expert_weights)
    else:
        ys = _experts(tiles, xs, expert_weights)
    yg = _sc_gather(pos1, pos2, ys)
    return _combine(xp, route, yg, _row(ln2_g), _row(ln2_b), alpha).reshape(x.shape), tuple(expert_weights)


def kernel(x_prompt, x_sample, w_in, b_in, conv_w, conv_b, sg_ln_g, sg_ln_b, w_s, b_s, w_o, b_o, ln1_g, ln1_b, w_rc, b_rc, w_rf, b_rf, w_gate, w_up, w_down, ln2_g, ln2_b):
    depth = w_in.shape[0]
    alpha = (2.0 * depth) ** 0.25
    xs = (x_prompt, x_sample)
    for l in range(depth):
        mixer_params = _mixer_params(w_in[l], b_in[l], conv_w[l], conv_b[l], sg_ln_g[l], sg_ln_b[l], w_s[l], b_s[l],
                                     w_o[l], b_o[l], ln1_g[l], ln1_b[l], w_rc[l], b_rc[l], w_rf[l], b_rf[l])
        expert_weights = (w_gate[l], w_up[l], w_down[l])
        outs = [None] * len(xs)
        for k in sorted(range(len(xs)), key=lambda k: -xs[k].shape[0] * xs[k].shape[1]):
            outs[k], expert_weights = _encoder_layer(xs[k], mixer_params, expert_weights, ln2_g[l], ln2_b[l], alpha)
        xs = tuple(outs)
    return xs
```

```python
import functools

import jax
import jax.numpy as jnp
from jax import lax
from jax.experimental import pallas as pl
from jax.experimental.pallas import tpu as pltpu
from jax.experimental.pallas import tpu_sc as plsc

D_MODEL = 1024
D_CONV = 512
D_SG = 512
N_SG_HEADS = 8
SG_HEAD_DIM = D_SG // N_SG_HEADS
CHUNK = 128
N_EXPERT_GROUPS = 4
EXPERTS_PER_GROUP = 8
N_EXPERTS = N_EXPERT_GROUPS * EXPERTS_PER_GROUP
D_EXPERT = 512
LN_EPS = 1e-5
D_PACK = D_MODEL // 2

LANES = 128
SUBLANES = 8
HALO = 16
SEQ_TILE = 512
EXPERT_TILE = 1024
EXPERT_SUBTILE = 256
TOKEN_TILE = 1024
COMBINE_SLOTS = 3
EXPERT_ROW_SLOTS = 3
SC_CORES = 2
SC_SUBCORES = 16
SC_WORKERS = SC_CORES * SC_SUBCORES
SC_ROWS = 64
VMEM_LIMIT_BYTES = 56 * 1024 * 1024

COL_E1, COL_E2, COL_W1, COL_W2, COL_R1, COL_R2 = 0, 1, 2, 3, 4, 5
COARSE_OFF = N_EXPERTS


def _dot(a, b):
    return jnp.dot(a, b, preferred_element_type=jnp.float32)


def _gelu_tanh(x):
    return 0.5 * x * (1.0 + jnp.tanh(0.7978845608028654 * (x + 0.044715 * (x * x * x))))


def _layer_norm(x, g, b):
    mu = jnp.mean(x, axis=-1, keepdims=True)
    xc = x - mu
    var = jnp.mean(xc * xc, axis=-1, keepdims=True)
    return xc * lax.rsqrt(var + LN_EPS) * g + b


def _pack_halves(x):
    u32 = jnp.uint32
    lo = lax.bitcast_convert_type(x[:, 0:D_PACK].astype(jnp.bfloat16).astype(jnp.float32), u32)
    hi = lax.bitcast_convert_type(x[:, D_PACK:D_MODEL].astype(jnp.bfloat16).astype(jnp.float32), u32)
    return (hi & u32(0xFFFF0000)) | (lo >> u32(16))


def _unpack_halves(w):
    u32 = jnp.uint32
    lo = lax.bitcast_convert_type(w << u32(16), jnp.float32)
    hi = lax.bitcast_convert_type(w & u32(0xFFFF0000), jnp.float32)
    return lo, hi


def _mixer_kernel(x_ref, x_prev_ref, x_next_ref, w_in_ref, b_in_ref, conv_w_ref, conv_b_ref,
                  sg_g_ref, sg_b_ref, ws_ref, bs_ref, w_o_ref, b_o_ref, ln1_g_ref, ln1_b_ref,
                  w_r_ref, b_r_ref,
                  xp_ref, route_ref, counts_ref,
                  xb_ref, ymix_ref, tri_ref, resid_ref, carry_ref, *, ts, nt, n_tiles, alpha):
    t = pl.program_id(0)
    j = jnp.minimum(t, n_tiles - 1) % nt
    bf16 = jnp.bfloat16

    @pl.when(t == 0)
    def _():
        carry_ref[...] = jnp.zeros_like(carry_ref)
        resid_ref[...] = jnp.zeros_like(resid_ref)
        ri = lax.broadcasted_iota(jnp.int32, (ts, ts), 0)
        ci = lax.broadcasted_iota(jnp.int32, (ts, ts), 1)
        tri_ref[...] = jnp.where(ri > ci, 1.0, 0.0).astype(bf16)

    lane = lax.broadcasted_iota(jnp.int32, (ts, LANES), 1)
    lane_f = lane.astype(jnp.float32)
    neg = jnp.float32(-jnp.inf)
    big = jnp.float32(1e9)
    is_c = (lane >= COARSE_OFF) & (lane < COARSE_OFF + N_EXPERT_GROUPS)

    def proj(lhs, lo, hi):
        return _dot(lhs, w_in_ref[:, lo:hi]) + b_in_ref[:, lo:hi]

    def back_norm():
        x1 = _layer_norm(resid_ref[...], ln1_g_ref[...], ln1_b_ref[...])
        xp_ref[...] = _pack_halves(x1)
        return x1.astype(bf16)

    def back_route(x1b):
        logits = _dot(x1b, w_r_ref[...]) + b_r_ref[...]
        lc = jnp.where(is_c, logits, neg)
        mx = jnp.max(lc, axis=1, keepdims=True)
        grp = jnp.min(jnp.where(lc == mx, lane_f - COARSE_OFF, big), axis=1, keepdims=True)
        p_grp = 1.0 / jnp.sum(jnp.where(is_c, jnp.exp(logits - mx), 0.0), axis=1, keepdims=True)
        grp_lo = grp * EXPERTS_PER_GROUP
        in_grp = (lane_f >= grp_lo) & (lane_f < grp_lo + EXPERTS_PER_GROUP)
        lf = jnp.where(in_grp, logits, neg)
        v1 = jnp.max(lf, axis=1, keepdims=True)
        e1 = jnp.min(jnp.where(lf == v1, lane_f, big), axis=1, keepdims=True)
        lf2 = jnp.where(lane_f == e1, neg, lf)
        v2 = jnp.max(lf2, axis=1, keepdims=True)
        e2 = jnp.min(jnp.where(lf2 == v2, lane_f, big), axis=1, keepdims=True)
        a = jnp.exp(v2 - v1)
        return e1, e2, p_grp / (1.0 + a), p_grp * a / (1.0 + a)

    def back_rank(e1, e2, w1, w2):
        hit1 = lane_f == e1
        hit2 = lane_f == e2
        onehot = jnp.where(hit1 | hit2, 1.0, 0.0)
        carry = carry_ref[...]
        before = _dot(tri_ref[...], onehot.astype(bf16)) + carry
        r1 = jnp.sum(jnp.where(hit1, before, 0.0), axis=1, keepdims=True)
        r2 = jnp.sum(jnp.where(hit2, before, 0.0), axis=1, keepdims=True)
        carry = jnp.where(t > 0, carry + jnp.sum(onehot, axis=0, keepdims=True), carry)
        carry_ref[...] = carry
        counts_ref[...] = jnp.broadcast_to(carry, counts_ref.shape)
        slab = jnp.where(lane == COL_E1, e1, 0.0)
        slab = jnp.where(lane == COL_E2, e2, slab)
        slab = jnp.where(lane == COL_W1, w1, slab)
        slab = jnp.where(lane == COL_W2, w2, slab)
        slab = jnp.where(lane == COL_R1, r1, slab)
        slab = jnp.where(lane == COL_R2, r2, slab)
        route_ref[...] = slab.T[0:SUBLANES, :]

    def front_load():
        xb_ref[0:HALO, :] = x_prev_ref[...].astype(bf16)
        xb_ref[HALO:HALO + ts, :] = x_ref[...].astype(bf16)
        xb_ref[HALO + ts:HALO + ts + HALO, :] = x_next_ref[...].astype(bf16)

    def front_conv():
        xe = xb_ref[...]
        g_e = proj(xe, 0, D_CONV) * proj(xe, 2 * D_CONV, 3 * D_CONV)
        row_e = lax.broadcasted_iota(jnp.int32, (ts + 2 * HALO, 1), 0)
        has_prev = jnp.where(j > 0, 1.0, 0.0)
        has_next = jnp.where(j < nt - 1, 1.0, 0.0)
        g_e = g_e * jnp.where(row_e < HALO, has_prev, jnp.where(row_e >= HALO + ts, has_next, 1.0))
        conv = (g_e[HALO - 1:HALO - 1 + ts, :] * conv_w_ref[0:1, :] + g_e[HALO:HALO + ts, :] * conv_w_ref[1:2, :]
                + g_e[HALO + 1:HALO + 1 + ts, :] * conv_w_ref[2:3, :] + conv_b_ref[...])
        y_a = proj(xb_ref[HALO:HALO + ts, :], D_CONV, 2 * D_CONV) * conv
        ymix_ref[:, 0:D_CONV] = y_a.astype(bf16)

    def front_gate_proj():
        xm = xb_ref[HALO:HALO + ts, :]
        return proj(xm, 3 * D_CONV, 3 * D_CONV + D_SG), proj(xm, 3 * D_CONV + D_SG, 3 * D_CONV + 2 * D_SG)

    def front_gate(u_pre, v_pre):
        u = _gelu_tanh(u_pre)
        v = _gelu_tanh(v_pre)
        v_ln = _layer_norm(v, sg_g_ref[...], sg_b_ref[...]).astype(bf16)
        first_head = lax.broadcasted_iota(jnp.int32, (CHUNK, LANES), 1) < SG_HEAD_DIM
        for q0 in range(0, ts, 2 * CHUNK):
            q1 = q0 + CHUNK
            for hp in range(N_SG_HEADS // 2):
                c0 = hp * LANES
                rhs = jnp.concatenate([v_ln[q0:q0 + CHUNK, c0:c0 + LANES], v_ln[q1:q1 + CHUNK, c0:c0 + LANES]], axis=1)
                res = _dot(ws_ref[hp], rhs)
                bias = bs_ref[:, c0:c0 + LANES]
                m0 = jnp.where(first_head, res[0:CHUNK, 0:LANES], res[CHUNK:2 * CHUNK, 0:LANES]) + bias
                m1 = jnp.where(first_head, res[0:CHUNK, LANES:2 * LANES], res[CHUNK:2 * CHUNK, LANES:2 * LANES]) + bias
                ymix_ref[q0:q0 + CHUNK, D_CONV + c0:D_CONV + c0 + LANES] = (u[q0:q0 + CHUNK, c0:c0 + LANES] * m0).astype(bf16)
                ymix_ref[q1:q1 + CHUNK, D_CONV + c0:D_CONV + c0 + LANES] = (u[q1:q1 + CHUNK, c0:c0 + LANES] * m1).astype(bf16)

    def front_out_a():
        resid_ref[...] = alpha * x_ref[...] + b_o_ref[...] + _dot(ymix_ref[:, 0:D_CONV], w_o_ref[0:D_CONV, :])

    def front_out_b():
        resid_ref[...] += _dot(ymix_ref[:, D_CONV:D_MODEL], w_o_ref[D_CONV:D_MODEL, :])

    front_load()
    gate_pre = front_gate_proj()
    x1b = back_norm()
    routing = back_route(x1b)
    front_conv()
    back_rank(*routing)
    front_out_a()
    front_gate(*gate_pre)
    front_out_b()


def _mixer(x, p, alpha):
    nb, s, d = x.shape
    ts = SEQ_TILE
    nt = s // ts
    hb = ts // HALO
    n_tiles = nb * nt
    n_total = n_tiles * ts

    front = lambda t: jnp.minimum(t, n_tiles - 1)
    back = lambda t: jnp.maximum(t - 1, 0)
    cur = lambda t: (front(t) // nt, front(t) % nt, 0)
    prv = lambda t: (front(t) // nt, jnp.maximum((front(t) % nt) * hb - 1, 0), 0)
    nxt = lambda t: (front(t) // nt, jnp.minimum((front(t) % nt + 1) * hb, s // HALO - 1), 0)
    full = lambda a: pl.BlockSpec(a.shape, lambda t: (0,) * a.ndim)
    in_specs = [pl.BlockSpec((None, ts, d), cur), pl.BlockSpec((None, HALO, d), prv),
                pl.BlockSpec((None, HALO, d), nxt)] + [full(a) for a in p]
    out_shape = (jax.ShapeDtypeStruct((n_total, D_PACK), jnp.uint32),
                 jax.ShapeDtypeStruct((SUBLANES, n_total), jnp.float32),
                 jax.ShapeDtypeStruct((SUBLANES, LANES), jnp.float32))
    row_blk = lambda t: (back(t), 0)
    out_specs = (pl.BlockSpec((ts, D_PACK), row_blk),
                 pl.BlockSpec((SUBLANES, ts), lambda t: (0, back(t))),
                 pl.BlockSpec((SUBLANES, LANES), lambda t: (0, 0)))
    return pl.pallas_call(
        functools.partial(_mixer_kernel, ts=ts, nt=nt, n_tiles=n_tiles, alpha=alpha),
        grid=(n_tiles + 1,),
        in_specs=in_specs,
        out_specs=out_specs,
        out_shape=out_shape,
        scratch_shapes=[pltpu.VMEM((ts + 2 * HALO, d), jnp.bfloat16),
                        pltpu.VMEM((ts, d), jnp.bfloat16),
                        pltpu.VMEM((ts, ts), jnp.bfloat16),
                        pltpu.VMEM((ts, d), jnp.float32),
                        pltpu.VMEM((1, LANES), jnp.float32)],
        compiler_params=pltpu.CompilerParams(dimension_semantics=("arbitrary",),
                                             vmem_limit_bytes=VMEM_LIMIT_BYTES),
        cost_estimate=pl.CostEstimate(
            flops=2 * n_total * (d * (3 * D_CONV + 2 * D_SG) + CHUNK * D_SG + d * d + d * LANES + ts * LANES),
            transcendentals=n_total * (2 * D_SG + 2 * LANES),
            bytes_accessed=n_total * (d * 4 + D_PACK * 4 + SUBLANES * 4)),
        name="mixer",
    )(x, x, x, *p)


def _sc_mesh():
    return plsc.VectorSubcoreMesh(core_axis_name="c", subcore_axis_name="s",
                                  num_cores=SC_CORES, num_subcores=SC_SUBCORES)


def _sc_worker_base(per_worker):
    return (lax.axis_index("s") * SC_CORES + lax.axis_index("c")) * per_worker


def _sc_dispatch(pos1, pos2, xp, n_rows):
    n, dp = xp.shape
    per_worker = n // SC_WORKERS
    steps = per_worker // SC_ROWS

    def body(pos1_hbm, pos2_hbm, xp_hbm, xs_hbm, idx1_v, idx2_v, rows_v, sem):
        base = _sc_worker_base(per_worker)

        @pl.loop(0, steps)
        def _(k):
            off = pl.multiple_of(base + k * SC_ROWS, SC_ROWS)
            pltpu.sync_copy(pos1_hbm.at[pl.ds(off, SC_ROWS)], idx1_v)
            pltpu.sync_copy(pos2_hbm.at[pl.ds(off, SC_ROWS)], idx2_v)
            pltpu.sync_copy(xp_hbm.at[pl.ds(off, SC_ROWS)], rows_v)
            c1 = pltpu.async_copy(rows_v, xs_hbm.at[idx1_v], sem)
            c2 = pltpu.async_copy(rows_v, xs_hbm.at[idx2_v], sem)
            c1.wait()
            c2.wait()

    return pl.kernel(
        body,
        out_type=jax.ShapeDtypeStruct((n_rows, dp), xp.dtype),
        mesh=_sc_mesh(),
        scratch_types=[pltpu.VMEM((SC_ROWS,), jnp.int32), pltpu.VMEM((SC_ROWS,), jnp.int32),
                       pltpu.VMEM((SC_ROWS, dp), xp.dtype), pltpu.SemaphoreType.DMA],
        compiler_params=pltpu.CompilerParams(use_tc_tiling_on_sc=True),
        cost_estimate=pl.CostEstimate(flops=0, transcendentals=0, bytes_accessed=3 * n * dp * 4 + 2 * n * 4),
        name="sc_dispatch",
    )(pos1, pos2, xp)


def _sc_gather(pos1, pos2, ys):
    n = pos1.shape[0]
    dp = ys.shape[1]
    per_worker = n // SC_WORKERS
    steps = per_worker // SC_ROWS

    def body(pos1_hbm, pos2_hbm, ys_hbm, out_hbm, idx_v, rows_v, sem):
        base = _sc_worker_base(per_worker)

        @pl.loop(0, steps)
        def _(k):
            off = pl.multiple_of(base + k * SC_ROWS, SC_ROWS)
            for slot, pos_hbm in enumerate((pos1_hbm, pos2_hbm)):
                pltpu.sync_copy(pos_hbm.at[pl.ds(off, SC_ROWS)], idx_v)
                pltpu.async_copy(ys_hbm.at[idx_v], rows_v, sem).wait()
                pltpu.sync_copy(rows_v, out_hbm.at[slot, pl.ds(off, SC_ROWS)])

    return pl.kernel(
        body,
        out_type=jax.ShapeDtypeStruct((2, n, dp), ys.dtype),
        mesh=_sc_mesh(),
        scratch_types=[pltpu.VMEM((SC_ROWS,), jnp.int32), pltpu.VMEM((SC_ROWS, dp), ys.dtype),
                       pltpu.SemaphoreType.DMA],
        compiler_params=pltpu.CompilerParams(use_tc_tiling_on_sc=True),
        cost_estimate=pl.CostEstimate(flops=0, transcendentals=0, bytes_accessed=4 * n * dp * 4 + 2 * n * 4),
        name="sc_gather",
    )(pos1, pos2, ys)


def _expert_kernel(te_ref, first_ref, nsub_ref, slot_ref, next_ref, nused_ref, xs_hbm, *refs, cast_weights):
    i = pl.program_id(0)
    bf16 = jnp.bfloat16
    sub = EXPERT_SUBTILE
    slot = slot_ref[i]

    if cast_weights:
        (wg_hbm, wu_hbm, wd_hbm, ys_ref, wgu_out, wd_out,
         wg_buf, wu_buf, wd_buf, wgu_bf_ref, wd_bf_ref, sem, out_sem, xs_buf, xs_sem) = refs
        wgu_ref, wd_ref = wgu_bf_ref, wd_bf_ref

        def weight_copies(expert, s):
            return (pltpu.make_async_copy(wg_hbm.at[expert], wg_buf.at[s], sem.at[s]),
                    pltpu.make_async_copy(wu_hbm.at[expert], wu_buf.at[s], sem.at[s]),
                    pltpu.make_async_copy(wd_hbm.at[expert], wd_buf.at[s], sem.at[s]))

        def emit_copies(expert):
            return (pltpu.make_async_copy(wgu_bf_ref, wgu_out.at[expert], out_sem),
                    pltpu.make_async_copy(wd_bf_ref, wd_out.at[expert], out_sem))
    else:
        wgu_hbm, wd_hbm, ys_ref, wgu_buf, wd_buf, sem, xs_buf, xs_sem = refs
        wgu_ref, wd_ref = wgu_buf.at[slot], wd_buf.at[slot]

        def weight_copies(expert, s):
            return (pltpu.make_async_copy(wgu_hbm.at[expert], wgu_buf.at[s], sem.at[s]),
                    pltpu.make_async_copy(wd_hbm.at[expert], wd_buf.at[s], sem.at[s]))

    tm = ys_ref.shape[0]
    ahead = EXPERT_ROW_SLOTS - 1
    n_used = nused_ref[0]

    def row_copies(step, act):
        start = step * tm
        if not isinstance(step, int):
            start = pl.multiple_of(start, tm)
        s = step % EXPERT_ROW_SLOTS
        for k in range(tm // sub):
            @pl.when(k < nsub_ref[step])
            def _():
                act(pltpu.make_async_copy(xs_hbm.at[pl.ds(start + k * sub, sub)],
                                          xs_buf.at[s, pl.ds(k * sub, sub)], xs_sem.at[s]))

    @pl.when(i == 0)
    def _():
        for step in range(ahead):
            @pl.when(step < n_used)
            def _():
                row_copies(step, lambda c: c.start())

    @pl.when(i + ahead < n_used)
    def _():
        row_copies(i + ahead, lambda c: c.start())

    @pl.when(i < n_used)
    def _():
        row_copies(i, lambda c: c.wait())

    xs_ref = xs_buf.at[i % EXPERT_ROW_SLOTS]

    @pl.when(first_ref[i] == 1)
    def _():
        @pl.when(i == 0)
        def _():
            for c in weight_copies(te_ref[i], slot):
                c.start()

        for c in weight_copies(te_ref[i], slot):
            c.wait()

        @pl.when(next_ref[i] >= 0)
        def _():
            for c in weight_copies(next_ref[i], 1 - slot):
                c.start()

        if cast_weights:
            @pl.when(i > 0)
            def _():
                for c in emit_copies(te_ref[i]):
                    c.wait()

            wgu_bf_ref[:, 0:D_EXPERT] = wg_buf[slot].astype(bf16)
            wgu_bf_ref[:, D_EXPERT:2 * D_EXPERT] = wu_buf[slot].astype(bf16)
            wd_bf_ref[...] = wd_buf[slot].astype(bf16)
            for c in emit_copies(te_ref[i]):
                c.start()

    def swiglu(starts):
        rows = [pl.ds(r0, sub) for r0 in starts]
        gate_up = []
        for rw in rows:
            lo, hi = _unpack_halves(xs_ref[rw, :])
            gate_up.append(_dot(lo.astype(bf16), wgu_ref[0:D_PACK, :])
                           + _dot(hi.astype(bf16), wgu_ref[D_PACK:D_MODEL, :]))
        for rw, gu in zip(rows, gate_up):
            g = gu[:, 0:D_EXPERT]
            h = g * jax.nn.sigmoid(g) * gu[:, D_EXPERT:2 * D_EXPERT]
            ys_ref[rw, :] = _pack_halves(_dot(h.astype(bf16), wd_ref[...]))

    n_sub = nsub_ref[i]

    def pair(k, c):
        r0 = pl.multiple_of(k * (2 * sub), 2 * sub)
        swiglu([r0, r0 + sub])
        return c

    lax.fori_loop(0, n_sub // 2, pair, 0)

    @pl.when(n_sub % 2 == 1)
    def _():
        swiglu([pl.multiple_of((n_sub - 1) * sub, sub)])

    def clear(k, c):
        ys_ref[pl.ds(pl.multiple_of(k * sub, sub), sub), :] = jnp.zeros((sub, D_PACK), ys_ref.dtype)
        return c

    lax.fori_loop(n_sub, jnp.where(i < nused_ref[0], ys_ref.shape[0] // sub, 0), clear, 0)

    if cast_weights:
        @pl.when(i == pl.num_programs(0) - 1)
        def _():
            for c in emit_copies(te_ref[i]):
                c.wait()


def _experts(tiles, xs, weights):
    n_rows, dp = xs.shape
    d = D_MODEL
    tm = EXPERT_TILE
    cast_weights = len(weights) == 3
    row_blk = lambda i, te, fi, ns, sl, nx, nu: (jnp.minimum(i, nu[0] - 1), 0)
    any_spec = pl.BlockSpec(memory_space=pl.ANY)
    ys_shape = jax.ShapeDtypeStruct((n_rows, dp), jnp.uint32)
    ys_spec = pl.BlockSpec((tm, dp), row_blk)
    wgu_bf = ((d, 2 * D_EXPERT), jnp.bfloat16)
    wd_bf = ((D_EXPERT, d), jnp.bfloat16)
    if cast_weights:
        out_shape = (ys_shape, jax.ShapeDtypeStruct((N_EXPERTS,) + wgu_bf[0], wgu_bf[1]),
                     jax.ShapeDtypeStruct((N_EXPERTS,) + wd_bf[0], wd_bf[1]))
        out_specs = (ys_spec, any_spec, any_spec)
        scratch = [pltpu.VMEM((2, d, D_EXPERT), jnp.float32), pltpu.VMEM((2, d, D_EXPERT), jnp.float32),
                   pltpu.VMEM((2, D_EXPERT, d), jnp.float32), pltpu.VMEM(*wgu_bf), pltpu.VMEM(*wd_bf),
                   pltpu.SemaphoreType.DMA((2,)), pltpu.SemaphoreType.DMA]
        weight_bytes = N_EXPERTS * 3 * d * D_EXPERT * (4 + 2)
    else:
        out_shape, out_specs = ys_shape, ys_spec
        scratch = [pltpu.VMEM((2,) + wgu_bf[0], wgu_bf[1]), pltpu.VMEM((2,) + wd_bf[0], wd_bf[1]),
                   pltpu.SemaphoreType.DMA((2,))]
        weight_bytes = N_EXPERTS * 3 * d * D_EXPERT * 2
    scratch += [pltpu.VMEM((EXPERT_ROW_SLOTS, tm, dp), jnp.uint32), pltpu.SemaphoreType.DMA((EXPERT_ROW_SLOTS,))]
    grid_spec = pltpu.PrefetchScalarGridSpec(
        num_scalar_prefetch=6,
        grid=(n_rows // tm,),
        in_specs=[any_spec] * (1 + len(weights)),
        out_specs=out_specs,
        scratch_shapes=scratch,
    )
    return pl.pallas_call(
        functools.partial(_expert_kernel, cast_weights=cast_weights),
        grid_spec=grid_spec,
        out_shape=out_shape,
        compiler_params=pltpu.CompilerParams(dimension_semantics=("arbitrary",),
                                             vmem_limit_bytes=VMEM_LIMIT_BYTES),
        cost_estimate=pl.CostEstimate(flops=2 * n_rows * 3 * d * D_EXPERT, transcendentals=n_rows * D_EXPERT,
                                      bytes_accessed=2 * n_rows * dp * 4 + weight_bytes),
        name="experts",
    )(*tiles, xs, *weights)


def _combine_kernel(route_ref, g_ref, b_ref, xp_hbm, yg_hbm, out_ref, xp_buf, y_buf, sem, *, alpha, n_steps):
    i = pl.program_id(0)
    tt = out_ref.shape[0]
    ahead = COMBINE_SLOTS - 1

    def row_copies(step):
        start = step * tt
        if not isinstance(step, int):
            start = pl.multiple_of(start, tt)
        rows, s = pl.ds(start, tt), step % COMBINE_SLOTS
        return (pltpu.make_async_copy(xp_hbm.at[rows], xp_buf.at[s], sem.at[s]),
                pltpu.make_async_copy(yg_hbm.at[:, rows], y_buf.at[s], sem.at[s]))

    @pl.when(i == 0)
    def _():
        for step in range(min(ahead, n_steps)):
            for c in row_copies(step):
                c.start()

    @pl.when(i + ahead < n_steps)
    def _():
        for c in row_copies(i + ahead):
            c.start()

    for c in row_copies(i):
        c.wait()

    slot = i % COMBINE_SLOTS
    cols = jnp.concatenate([route_ref[...], jnp.zeros((LANES - SUBLANES, tt), jnp.float32)], axis=0).T
    w1 = cols[:, COL_W1:COL_W1 + 1]
    w2 = cols[:, COL_W2:COL_W2 + 1]
    lo1, hi1 = _unpack_halves(y_buf[slot, 0])
    lo2, hi2 = _unpack_halves(y_buf[slot, 1])
    lo, hi = _unpack_halves(xp_buf[slot])
    y = jnp.concatenate([alpha * lo + (w1 * lo1 + w2 * lo2), alpha * hi + (w1 * hi1 + w2 * hi2)], axis=1)
    out_ref[...] = _layer_norm(y, g_ref[...], b_ref[...])


def _combine(xp, route, yg, g, b, alpha):
    n, d = xp.shape[0], D_MODEL
    tt = TOKEN_TILE
    any_spec = pl.BlockSpec(memory_space=pl.ANY)
    return pl.pallas_call(
        functools.partial(_combine_kernel, alpha=alpha, n_steps=n // tt),
        grid=(n // tt,),
        in_specs=[pl.BlockSpec((SUBLANES, tt), lambda i: (0, i)),
                  pl.BlockSpec((1, d), lambda i: (0, 0)),
                  pl.BlockSpec((1, d), lambda i: (0, 0)),
                  any_spec, any_spec],
        out_specs=pl.BlockSpec((tt, d), lambda i: (i, 0)),
        out_shape=jax.ShapeDtypeStruct((n, d), jnp.float32),
        scratch_shapes=[pltpu.VMEM((COMBINE_SLOTS, tt, D_PACK), jnp.uint32),
                        pltpu.VMEM((COMBINE_SLOTS, 2, tt, D_PACK), jnp.uint32),
                        pltpu.SemaphoreType.DMA((COMBINE_SLOTS,))],
        compiler_params=pltpu.CompilerParams(dimension_semantics=("arbitrary",),
                                             vmem_limit_bytes=VMEM_LIMIT_BYTES),
        cost_estimate=pl.CostEstimate(flops=12 * n * d, transcendentals=n,
                                      bytes_accessed=n * (d * 4 + 3 * D_PACK * 4 + SUBLANES * 4)),
        name="combine",
    )(route, g, b, xp, yg)


def _row(a):
    return a.reshape(1, -1)


def _mixer_params(w_in, b_in, conv_w, conv_b, sg_ln_g, sg_ln_b, w_s, b_s, w_o, b_o, ln1_g, ln1_b,
                  w_rc, b_rc, w_rf, b_rf):
    bf16 = jnp.bfloat16
    ws_pairs = w_s.reshape(N_SG_HEADS // 2, 2 * CHUNK, CHUNK).astype(bf16)
    bs_full = jnp.repeat(b_s.T, SG_HEAD_DIM, axis=1)
    pad = LANES - N_EXPERTS - N_EXPERT_GROUPS
    w_r = jnp.pad(jnp.concatenate([w_rf, w_rc], axis=1), ((0, 0), (0, pad))).astype(bf16)
    b_r = jnp.pad(jnp.concatenate([b_rf, b_rc]), (0, pad)).reshape(1, LANES)
    return (w_in.astype(bf16), _row(b_in), conv_w, _row(conv_b), _row(sg_ln_g), _row(sg_ln_b),
            ws_pairs, bs_full, w_o.astype(bf16), _row(b_o), _row(ln1_g), _row(ln1_b), w_r, b_r)


def _encoder_layer(x, mixer_params, expert_weights, ln2_g, ln2_b, alpha):
    n = x.shape[0] * x.shape[1]
    xp, route, carry = _mixer(x, mixer_params, alpha)

    tm = EXPERT_TILE
    n_tiles = 2 * n // tm + N_EXPERTS
    cnt = carry[0, 0:N_EXPERTS].astype(jnp.int32)
    tiles_e = jnp.maximum((cnt + tm - 1) // tm, 1 if len(expert_weights) == 3 else 0)
    experts = jnp.arange(N_EXPERTS, dtype=jnp.int32)[:, None]
    up_to = experts.T <= experts
    tile_end = jnp.sum(jnp.where(up_to, tiles_e[None, :], 0), axis=1)
    tile_start = tile_end - tiles_e
    row_off = tile_start * tm

    def sorted_pos(e_row, r_row):
        e = route[e_row].astype(jnp.int32)
        return jnp.sum(jnp.where(e[None, :] == experts, row_off[:, None], 0), axis=0) + route[r_row].astype(jnp.int32)

    pos1 = sorted_pos(COL_E1, COL_R1)
    pos2 = sorted_pos(COL_E2, COL_R2)
    tile_ids = jnp.arange(n_tiles, dtype=jnp.int32)
    n_used = tile_end[N_EXPERTS - 1:N_EXPERTS]
    tile_expert = jnp.sum((tile_end[None, :] <= tile_ids[:, None]).astype(jnp.int32), axis=1)
    last_used = jnp.sum((tile_end <= n_used[0] - 1).astype(jnp.int32))
    tile_expert = jnp.minimum(tile_expert, last_used)
    tile_first = (jnp.any(tile_ids[:, None] == tile_start[None, :], axis=1) & (tile_ids < n_used[0])) | (tile_ids == 0)
    tile_first = tile_first.astype(jnp.int32)
    rows_left = jnp.sum(jnp.where(tile_expert[:, None] == experts.T, (row_off + cnt)[None, :], 0), axis=1) - tile_ids * tm
    tile_nsub = jnp.where(tile_ids < n_used[0], (jnp.clip(rows_left, 0, tm) + EXPERT_SUBTILE - 1) // EXPERT_SUBTILE, 0)

    used = tiles_e > 0
    ordinal = jnp.sum((up_to & used[None, :]).astype(jnp.int32), axis=1) - 1
    e_ids = experts[:, 0]
    later_used = used[None, :] & (e_ids[None, :] > e_ids[:, None])
    next_used = jnp.min(jnp.where(later_used, e_ids[None, :], N_EXPERTS), axis=1)
    next_used = jnp.where(next_used < N_EXPERTS, next_used, -1)
    of_tile = tile_expert[:, None] == experts.T
    tile_slot = jnp.sum(jnp.where(of_tile, ordinal[None, :], 0), axis=1) % 2
    tile_next = jnp.sum(jnp.where(of_tile, next_used[None, :], 0), axis=1)

    tiles = (tile_expert, tile_first, tile_nsub.astype(jnp.int32), tile_slot.astype(jnp.int32),
             tile_next.astype(jnp.int32), n_used)
    xs = _sc_dispatch(pos1, pos2, xp, n_tiles * tm)
    if len(expert_weights) == 3:
        ys, *expert_weights = _experts(tiles, xs, expert_weights)
    else:
        ys = _experts(tiles, xs, expert_weights)
    yg = _sc_gather(pos1, pos2, ys)
    return _combine(xp, route, yg, _row(ln2_g), _row(ln2_b), alpha).reshape(x.shape), tuple(expert_weights)


def kernel(x_prompt, x_sample, w_in, b_in, conv_w, conv_b, sg_ln_g, sg_ln_b, w_s, b_s, w_o, b_o, ln1_g, ln1_b, w_rc, b_rc, w_rf, b_rf, w_gate, w_up, w_down, ln2_g, ln2_b):
    depth = w_in.shape[0]
    alpha = (2.0 * depth) ** 0.25
    xs = (x_prompt, x_sample)
    for l in range(depth):
        mixer_params = _mixer_params(w_in[l], b_in[l], conv_w[l], conv_b[l], sg_ln_g[l], sg_ln_b[l], w_s[l], b_s[l],
                                     w_o[l], b_o[l], ln1_g[l], ln1_b[l], w_rc[l], b_rc[l], w_rf[l], b_rf[l])
        expert_weights = (w_gate[l], w_up[l], w_down[l])
        outs = [None] * len(xs)
        for k in sorted(range(len(xs)), key=lambda k: -xs[k].shape[0] * xs[k].shape[1]):
            outs[k], expert_weights = _encoder_layer(xs[k], mixer_params, expert_weights, ln2_g[l], ln2_b[l], alpha)
        xs = tuple(outs)
    return xs
```
